```python
import jax, jax.numpy as jnp
from jax import lax
import numpy as np

D_MODEL = 1024
BATCH = 8
SEQ = 4096
DEPTH = 1

D_RNN = 1024
RNN_HEADS = 16
RNN_HEAD_DIM = D_RNN // RNN_HEADS
CONV_WIDTH = 4
LRU_C = 8.0
HEAD_DIM = 64
HEADS_PER_GROUP = 8
DILATED_GROUPS = ((128, 1), (512, 4), (2048, 16))
N_ATT_HEADS = HEADS_PER_GROUP * len(DILATED_GROUPS)
D_ATT = N_ATT_HEADS * HEAD_DIM
D_ATT_OUT = HEADS_PER_GROUP * HEAD_DIM
NUM_BUCKETS = 32
MAX_DISTANCE = 2048
N_EXPERTS = 256
TOP_K = 8
N_EXPERT_GROUPS = 8
TOPK_GROUPS = 4
EXPERTS_PER_GROUP = N_EXPERTS // N_EXPERT_GROUPS
D_EXPERT = 256
D_SHARED = 256
ROUTED_SCALE = 2.5
EXPERT_BLOCK = 128
DEEPNORM_ALPHA = (2 * DEPTH) ** 0.25
DEEPNORM_BETA = (8 * DEPTH) ** -0.25
LN_EPS = 1e-5
IN_SPLITS = (D_RNN, D_RNN, D_ATT, D_ATT, D_ATT, D_MODEL, D_MODEL)
D_IN = sum(IN_SPLITS)

kernel_name = 'hybrid_rglru_dilated_attn_moe_deepnorm'


def _layer_norm(x, g, b):
    xf = x.astype(jnp.float32)
    mu = xf.mean(-1, keepdims=True)
    var = jnp.square(xf - mu).mean(-1, keepdims=True)
    return ((xf - mu) * lax.rsqrt(var + LN_EPS) * g.astype(jnp.float32) + b.astype(jnp.float32)).astype(x.dtype)


def _causal_depthwise_conv(x, w, b):
    y = lax.conv_general_dilated(
        x, w[:, None, :].astype(x.dtype), window_strides=(1,),
        padding=((CONV_WIDTH - 1, 0),), dimension_numbers=('NWC', 'WIO', 'NWC'),
        feature_group_count=x.shape[-1])
    return y + b.astype(x.dtype)


def _rg_lru(xc, w_a, b_a, w_i, b_i, lam):
    B, S, _ = xc.shape
    xh = xc.astype(jnp.float32).reshape(B, S, RNN_HEADS, RNN_HEAD_DIM)
    r = jax.nn.sigmoid(jnp.einsum('bshi,hij->bshj', xh, w_a.astype(jnp.float32)) + b_a.astype(jnp.float32))
    i = jax.nn.sigmoid(jnp.einsum('bshi,hij->bshj', xh, w_i.astype(jnp.float32)) + b_i.astype(jnp.float32))
    log_a = -LRU_C * r * jax.nn.softplus(-lam.astype(jnp.float32)).reshape(RNN_HEADS, RNN_HEAD_DIM)
    a = jnp.exp(log_a)
    u = jnp.sqrt(-jnp.expm1(2.0 * log_a)) * (i * xh)

    def combine(c1, c2):
        a1, b1 = c1
        a2, b2 = c2
        return a1 * a2, a2 * b1 + b2

    _, h = lax.associative_scan(combine, (a, u), axis=1)
    return h.reshape(B, S, D_RNN)


def _t5_bucket(dist):
    max_exact = NUM_BUCKETS // 2
    d = np.maximum(dist, 1).astype(np.float64)
    large = max_exact + (np.log(d / max_exact) / np.log(MAX_DISTANCE / max_exact)
                         * (NUM_BUCKETS - max_exact)).astype(np.int64)
    large = np.minimum(large, NUM_BUCKETS - 1)
    return np.where(dist < max_exact, dist, large).astype(np.int32)


def _dilated_group(q, k, v, table, window, dilation):
    B, S, H, hd = q.shape
    n_steps = window // dilation
    blk = n_steps
    L = S // dilation
    nb = -(-L // blk)
    Lp = nb * blk

    def to_blocks(t):
        t = t.reshape(B, L, dilation, H, hd).transpose(0, 2, 1, 3, 4)
        t = jnp.pad(t, ((0, 0), (0, 0), (0, Lp - L), (0, 0), (0, 0)))
        return t.reshape(B, dilation, nb, blk, H, hd).astype(jnp.float32)

    def with_prev(t):
        prev = jnp.pad(t[:, :, :-1], ((0, 0), (0, 0), (1, 0), (0, 0), (0, 0), (0, 0)))
        return jnp.concatenate([prev, t], axis=3)

    qb = to_blocks(q) * (hd ** -0.5)
    kb = with_prev(to_blocks(k))
    vb = with_prev(to_blocks(v))

    qi = np.arange(blk)[:, None]
    ki = np.arange(2 * blk)[None, :]
    rel = qi + blk - ki
    in_window = (rel >= 0) & (rel <= n_steps)
    key_exists = (np.arange(nb)[:, None, None] * blk + ki[None] - blk) >= 0
    mask = in_window[None] & key_exists
    bucket = _t5_bucket(np.clip(rel, 0, None) * dilation)
    bias = jnp.transpose(table.astype(jnp.float32)[bucket], (2, 0, 1))

    s = jnp.einsum('brnqhd,brnkhd->brnhqk', qb, kb) + bias
    s = jnp.where(mask[None, None, :, None], s, -jnp.inf)
    m = s.max(-1)
    p = jnp.exp(s - m[..., None])
    l = p.sum(-1)
    o = jnp.einsum('brnhqk,brnkhd->brnqhd', p, vb)

    def from_blocks(t):
        rest = t.shape[4:]
        t = t.reshape((B, dilation, Lp) + rest)[:, :, :L]
        return jnp.swapaxes(t, 1, 2).reshape((B, S) + rest)

    m = from_blocks(jnp.swapaxes(m, 3, 4))
    l = from_blocks(jnp.swapaxes(l, 3, 4))
    return from_blocks(o), m, l


def _mixer(x, w_in, b_in, conv_w, conv_b, w_rg_a, b_rg_a, w_rg_i, b_rg_i, lru_lambda,
           w_proj_rnn, w_proj_att, rel_bias, w_out):
    B, S, _ = x.shape
    proj = jnp.einsum('bsd,de->bse', x, w_in) + b_in
    split_at = [int(c) for c in np.cumsum(IN_SPLITS)[:-1]]
    xr, gr, q, k, v, g_rnn, g_att = jnp.split(proj, split_at, axis=-1)

    xc = _causal_depthwise_conv(xr, conv_w, conv_b)
    h = _rg_lru(xc, w_rg_a, b_rg_a, w_rg_i, b_rg_i, lru_lambda)
    y_rnn = (h * jax.nn.gelu(gr.astype(jnp.float32))).astype(x.dtype)

    q = q.reshape(B, S, N_ATT_HEADS, HEAD_DIM)
    k = k.reshape(B, S, N_ATT_HEADS, HEAD_DIM)
    v = v.reshape(B, S, N_ATT_HEADS, HEAD_DIM)
    outs, maxes, dens = [], [], []
    for g, (window, dilation) in enumerate(DILATED_GROUPS):
        hs = slice(g * HEADS_PER_GROUP, (g + 1) * HEADS_PER_GROUP)
        o_g, m_g, l_g = _dilated_group(q[:, :, hs], k[:, :, hs], v[:, :, hs], rel_bias[:, hs], window, dilation)
        outs.append(o_g)
        maxes.append(m_g)
        dens.append(l_g)
    o = jnp.stack(outs)
    m = jnp.stack(maxes)
    l = jnp.stack(dens)
    w_g = jnp.exp(m - m.max(0))
    y_att = (w_g[..., None] * o).sum(0) / (w_g * l).sum(0)[..., None]
    y_att = y_att.reshape(B, S, D_ATT_OUT).astype(x.dtype)

    merged = (jax.nn.sigmoid(g_rnn) * jnp.einsum('bsr,rd->bsd', y_rnn, w_proj_rnn)
              + jax.nn.sigmoid(g_att) * jnp.einsum('bsa,ad->bsd', y_att, w_proj_att))
    return jnp.einsum('bsd,de->bse', merged, w_out)


def _moe(x2d, w_router, router_bias, w_exp_gate, w_exp_up, w_exp_down, w_sh_gate, w_sh_up, w_sh_down):
    T, D = x2d.shape
    scores = jax.nn.sigmoid(jnp.einsum('td,de->te', x2d.astype(jnp.float32), w_router.astype(jnp.float32)))
    sel = scores + router_bias.astype(jnp.float32)
    g_score = lax.top_k(sel.reshape(T, N_EXPERT_GROUPS, EXPERTS_PER_GROUP), 2)[0].sum(-1)
    _, g_idx = lax.top_k(g_score, TOPK_GROUPS)
    g_keep = (g_idx[:, :, None] == jnp.arange(N_EXPERT_GROUPS)).any(axis=1)
    sel = jnp.where(jnp.repeat(g_keep, EXPERTS_PER_GROUP, axis=1), sel, -jnp.inf)
    _, e_idx = lax.top_k(sel, TOP_K)
    gate = jnp.take_along_axis(scores, e_idx, axis=1)
    gate = gate / gate.sum(-1, keepdims=True) * ROUTED_SCALE

    A = T * TOP_K
    flat_e = e_idx.reshape(A)
    flat_tok = jnp.repeat(jnp.arange(T, dtype=jnp.int32), TOP_K)
    flat_w = gate.reshape(A)
    order = jnp.argsort(flat_e)
    se, stok, sw = flat_e[order], flat_tok[order], flat_w[order]
    counts = jnp.bincount(flat_e, length=N_EXPERTS)
    padded = (counts + EXPERT_BLOCK - 1) // EXPERT_BLOCK * EXPERT_BLOCK
    pad_end = jnp.cumsum(padded)
    start_sorted = jnp.cumsum(counts) - counts
    dest = (pad_end - padded)[se] + jnp.arange(A) - start_sorted[se]
    NB = -(-A // EXPERT_BLOCK) + N_EXPERTS
    tok_buf = jnp.zeros((NB * EXPERT_BLOCK,), jnp.int32).at[dest].set(stok)
    w_buf = jnp.zeros((NB * EXPERT_BLOCK,), jnp.float32).at[dest].set(sw)
    block_expert = jnp.minimum(
        jnp.searchsorted(pad_end // EXPERT_BLOCK, jnp.arange(NB), side='right'), N_EXPERTS - 1)

    def expert_block(args):
        tok, wt, e = args
        xb = x2d[tok]
        hdn = jax.nn.silu(xb @ w_exp_gate[e]) * (xb @ w_exp_up[e])
        return (hdn @ w_exp_down[e]) * wt[:, None].astype(xb.dtype)

    ys = lax.map(expert_block, (tok_buf.reshape(NB, EXPERT_BLOCK), w_buf.reshape(NB, EXPERT_BLOCK), block_expert))
    routed = jax.ops.segment_sum(ys.reshape(-1, D), tok_buf, num_segments=T)
    shared = (jax.nn.silu(x2d @ w_sh_gate) * (x2d @ w_sh_up)) @ w_sh_down
    return routed + shared


def setup_inputs(seed: int = 0) -> dict:
    key = jax.random.key(seed)
    ks = jax.random.split(key, 28)
    f32 = jnp.float32

    def nrm(k, shape, scale):
        return jax.random.normal(k, shape, f32) * scale

    x = nrm(ks[0], (BATCH, SEQ, D_MODEL), 1.0)
    w_in = nrm(ks[1], (DEPTH, D_MODEL, D_IN), D_MODEL ** -0.5)
    b_in = nrm(ks[2], (DEPTH, D_IN), 0.02)
    conv_w = nrm(ks[3], (DEPTH, CONV_WIDTH, D_RNN), CONV_WIDTH ** -0.5)
    conv_b = nrm(ks[4], (DEPTH, D_RNN), 0.02)
    w_rg_a = nrm(ks[5], (DEPTH, RNN_HEADS, RNN_HEAD_DIM, RNN_HEAD_DIM), RNN_HEAD_DIM ** -0.5)
    b_rg_a = nrm(ks[6], (DEPTH, RNN_HEADS, RNN_HEAD_DIM), 0.02)
    w_rg_i = nrm(ks[7], (DEPTH, RNN_HEADS, RNN_HEAD_DIM, RNN_HEAD_DIM), RNN_HEAD_DIM ** -0.5)
    b_rg_i = nrm(ks[8], (DEPTH, RNN_HEADS, RNN_HEAD_DIM), 0.02)
    a_c = jax.random.uniform(ks[9], (DEPTH, D_RNN), f32, 0.9, 0.999)
    base = a_c ** (1.0 / LRU_C)
    lru_lambda = jnp.log(base) - jnp.log1p(-base)
    w_proj_rnn = nrm(ks[10], (DEPTH, D_RNN, D_MODEL), D_RNN ** -0.5)
    w_proj_att = nrm(ks[11], (DEPTH, D_ATT_OUT, D_MODEL), D_ATT_OUT ** -0.5)
    rel_bias = nrm(ks[12], (NUM_BUCKETS, N_ATT_HEADS), 0.1)
    w_out = nrm(ks[13], (DEPTH, D_MODEL, D_MODEL), D_MODEL ** -0.5 * DEEPNORM_BETA)
    ln1_g = 1.0 + nrm(ks[14], (DEPTH, D_MODEL), 0.01)
    ln1_b = nrm(ks[15], (DEPTH, D_MODEL), 0.01)
    w_router = nrm(ks[16], (DEPTH, D_MODEL, N_EXPERTS), D_MODEL ** -0.5)
    router_bias = nrm(ks[17], (DEPTH, N_EXPERTS), 0.01)
    w_exp_gate = nrm(ks[18], (DEPTH, N_EXPERTS, D_MODEL, D_EXPERT), D_MODEL ** -0.5)
    w_exp_up = nrm(ks[19], (DEPTH, N_EXPERTS, D_MODEL, D_EXPERT), D_MODEL ** -0.5)
    w_exp_down = nrm(ks[20], (DEPTH, N_EXPERTS, D_EXPERT, D_MODEL), D_EXPERT ** -0.5 * DEEPNORM_BETA)
    w_sh_gate = nrm(ks[21], (DEPTH, D_MODEL, D_SHARED), D_MODEL ** -0.5)
    w_sh_up = nrm(ks[22], (DEPTH, D_MODEL, D_SHARED), D_MODEL ** -0.5)
    w_sh_down = nrm(ks[23], (DEPTH, D_SHARED, D_MODEL), D_SHARED ** -0.5 * DEEPNORM_BETA)
    ln2_g = 1.0 + nrm(ks[24], (DEPTH, D_MODEL), 0.01)
    ln2_b = nrm(ks[25], (DEPTH, D_MODEL), 0.01)
    return {'x': x, 'w_in': w_in, 'b_in': b_in, 'conv_w': conv_w, 'conv_b': conv_b,
            'w_rg_a': w_rg_a, 'b_rg_a': b_rg_a, 'w_rg_i': w_rg_i, 'b_rg_i': b_rg_i,
            'lru_lambda': lru_lambda, 'w_proj_rnn': w_proj_rnn, 'w_proj_att': w_proj_att,
            'rel_bias': rel_bias, 'w_out': w_out, 'ln1_g': ln1_g, 'ln1_b': ln1_b,
            'w_router': w_router, 'router_bias': router_bias, 'w_exp_gate': w_exp_gate,
            'w_exp_up': w_exp_up, 'w_exp_down': w_exp_down, 'w_sh_gate': w_sh_gate,
            'w_sh_up': w_sh_up, 'w_sh_down': w_sh_down, 'ln2_g': ln2_g, 'ln2_b': ln2_b}


def reference(x, w_in, b_in, conv_w, conv_b, w_rg_a, b_rg_a, w_rg_i, b_rg_i, lru_lambda,
              w_proj_rnn, w_proj_att, rel_bias, w_out, ln1_g, ln1_b, w_router, router_bias,
              w_exp_gate, w_exp_up, w_exp_down, w_sh_gate, w_sh_up, w_sh_down, ln2_g, ln2_b):
    B, S, D = x.shape
    for i in range(DEPTH):
        mix = _mixer(x, w_in[i], b_in[i], conv_w[i], conv_b[i], w_rg_a[i], b_rg_a[i], w_rg_i[i],
                     b_rg_i[i], lru_lambda[i], w_proj_rnn[i], w_proj_att[i], rel_bias, w_out[i])
        x = _layer_norm(DEEPNORM_ALPHA * x + mix, ln1_g[i], ln1_b[i])
        ffn = _moe(x.reshape(B * S, D), w_router[i], router_bias[i], w_exp_gate[i], w_exp_up[i],
                   w_exp_down[i], w_sh_gate[i], w_sh_up[i], w_sh_down[i])
        x = _layer_norm(DEEPNORM_ALPHA * x + ffn.reshape(B, S, D), ln2_g[i], ln2_b[i])
    return x
```

```python
import functools

import numpy as np
import jax
import jax.numpy as jnp
from jax import lax
from jax.experimental import pallas as pl
from jax.experimental.pallas import tpu as pltpu

F32 = jnp.float32
BF16 = jnp.bfloat16
I32 = jnp.int32

RNN_HEADS = 16
LRU_C = 8.0
HEAD_DIM = 64
HEADS_PER_GROUP = 8
DILATED_GROUPS = ((128, 1), (512, 4), (2048, 16))
NUM_BUCKETS = 32
MAX_DISTANCE = 2048
TOP_K = 8
N_EXPERT_GROUPS = 8
TOPK_GROUPS = 4
ROUTED_SCALE = 2.5
LN_EPS = 1e-5

LANES = 128
SUBLANES = 8
VMEM_LIMIT_BYTES = 56 * 1024 * 1024

MASK_VALUE = -1e30
EXPERT_ROWS = 256
ROW_SLAB = (SUBLANES, LANES)


def _cparams(*sem):
    return pltpu.CompilerParams(dimension_semantics=sem, vmem_limit_bytes=VMEM_LIMIT_BYTES)


def _sigmoid(v):
    return 0.5 * (jnp.tanh(0.5 * v) + 1.0)


def _in_proj_body(x_ref, w_ref, b_ref, o_ref, *, tn):
    xb = x_ref[...].astype(BF16)
    for j in range(o_ref.shape[1] // tn):
        sl = slice(j * tn, (j + 1) * tn)
        acc = jnp.dot(xb, w_ref[:, sl], preferred_element_type=F32)
        o_ref[:, sl] = (acc + b_ref[:, sl]).astype(o_ref.dtype)


def _in_proj(x2, w, b, *, tm=512, tn=512):
    T, D = x2.shape
    N = w.shape[1]
    return pl.pallas_call(
        functools.partial(_in_proj_body, tn=tn),
        grid=(T // tm,),
        in_specs=[pl.BlockSpec((tm, D), lambda i: (i, 0)),
                  pl.BlockSpec((D, N), lambda i: (0, 0), pipeline_mode=pl.Buffered(1)),
                  pl.BlockSpec((1, N), lambda i: (0, 0))],
        out_specs=pl.BlockSpec((tm, N), lambda i: (i, 0)),
        out_shape=jax.ShapeDtypeStruct((T, N), BF16),
        compiler_params=_cparams("parallel"),
        name="in_proj",
    )(x2, w, b)


def _rnn_body(xr_ref, gr_ref, cw_ref, cb_ref, wg_ref, bg_ref, lam_ref, y_ref,
              xext, a_s, b_s, hc, *, ts, cb):
    s = pl.program_id(2)

    @pl.when(s == 0)
    def _():
        xext[0:SUBLANES, :] = jnp.zeros((SUBLANES, cb), F32)
        hc[...] = jnp.zeros_like(hc)

    xr = xr_ref[0].astype(F32)
    xext[SUBLANES:SUBLANES + ts, :] = xr
    nw = cw_ref.shape[0]
    xc = cw_ref[nw - 1:nw, :] * xr + cb_ref[...]
    for j in range(nw - 1):
        off = SUBLANES - (nw - 1 - j)
        xc = xc + cw_ref[j:j + 1, :] * xext[off:off + ts, :]
    xext[0:SUBLANES, :] = xext[ts:ts + SUBLANES, :]

    gates = jnp.dot(xc.astype(BF16), wg_ref[0], preferred_element_type=F32) + bg_ref[...]
    r = _sigmoid(gates[:, :cb])
    ig = _sigmoid(gates[:, cb:])
    nl = -lam_ref[...]
    sp = jnp.maximum(nl, 0.0) + jnp.log1p(jnp.exp(-jnp.abs(nl)))
    log_a = (-LRU_C) * r * sp
    a = jnp.exp(log_a)
    u = jnp.sqrt(1.0 - a * a) * (ig * xc)

    row = lax.broadcasted_iota(I32, (ts, cb), 0) & (SUBLANES - 1)
    av, bv = a, u
    for sft in (1, 2, 4):
        a_sh = pltpu.roll(av, sft, 0)
        b_sh = pltpu.roll(bv, sft, 0)
        m = row >= sft
        bv = jnp.where(m, av * b_sh + bv, bv)
        av = jnp.where(m, av * a_sh, av)
    a_s[...] = av
    b_s[...] = bv

    def carry(g, h):
        i0 = pl.multiple_of(g * SUBLANES, SUBLANES)
        h8 = b_s[pl.ds(i0, SUBLANES), :] + a_s[pl.ds(i0, SUBLANES), :] * h
        b_s[pl.ds(i0, SUBLANES), :] = h8
        return h8[SUBLANES - 1:SUBLANES, :]

    hc[0:1, :] = lax.fori_loop(0, ts // SUBLANES, carry, hc[0:1, :], unroll=8)
    gr = gr_ref[0].astype(F32)
    y_ref[0] = (b_s[...] * jax.nn.gelu(gr)).astype(y_ref.dtype)


def _block_diag(w, per):
    H, d, _ = w.shape
    w4 = w.reshape(H // per, per, d, d)
    out = jnp.einsum('gpij,pq->gpiqj', w4, jnp.eye(per, dtype=w.dtype))
    return out.reshape(H // per, per * d, per * d)


def _rnn(proj3, conv_w, conv_b, w_rg_a, b_rg_a, w_rg_i, b_rg_i, lam, *, ts=512, cb=256):
    B, S, _ = proj3.shape
    d_rnn = conv_w.shape[-1]
    nc = d_rnn // cb
    per = cb // (d_rnn // RNN_HEADS)
    wg = jnp.concatenate([_block_diag(w_rg_a, per), _block_diag(w_rg_i, per)], axis=-1).astype(BF16)
    bg = jnp.concatenate([b_rg_a.reshape(nc, 1, cb), b_rg_i.reshape(nc, 1, cb)], axis=-1)
    return pl.pallas_call(
        functools.partial(_rnn_body, ts=ts, cb=cb),
        grid=(B, nc, S // ts),
        in_specs=[pl.BlockSpec((1, ts, cb), lambda b, c, s: (b, s, c)),
                  pl.BlockSpec((1, ts, cb), lambda b, c, s: (b, s, nc + c)),
                  pl.BlockSpec((conv_w.shape[0], cb), lambda b, c, s: (0, c)),
                  pl.BlockSpec((1, cb), lambda b, c, s: (0, c)),
                  pl.BlockSpec((1, cb, 2 * cb), lambda b, c, s: (c, 0, 0)),
                  pl.BlockSpec((None, 1, 2 * cb), lambda b, c, s: (c, 0, 0)),
                  pl.BlockSpec((1, cb), lambda b, c, s: (0, c))],
        out_specs=pl.BlockSpec((1, ts, cb), lambda b, c, s: (b, s, c)),
        out_shape=jax.ShapeDtypeStruct((B, S, d_rnn), BF16),
        scratch_shapes=[pltpu.VMEM((ts + SUBLANES, cb), F32), pltpu.VMEM((ts, cb), F32),
                        pltpu.VMEM((ts, cb), F32), pltpu.VMEM((SUBLANES, cb), F32)],
        compiler_params=_cparams("parallel", "parallel", "arbitrary"),
        name="rnn",
    )(proj3, proj3, conv_w, conv_b.reshape(1, d_rnn), wg, bg, lam.reshape(1, d_rnn))


def _t5_bucket(dist):
    max_exact = NUM_BUCKETS // 2
    d = np.maximum(dist, 1).astype(np.float64)
    large = max_exact + (np.log(d / max_exact) / np.log(MAX_DISTANCE / max_exact)
                         * (NUM_BUCKETS - max_exact)).astype(np.int64)
    large = np.minimum(large, NUM_BUCKETS - 1)
    return np.where(dist < max_exact, dist, large).astype(np.int32)


def _attn_bias(table, window, dilation):
    blk = window // dilation
    qi = np.arange(blk)[:, None]
    ki = np.arange(2 * blk)[None, :]
    rel = qi + blk - ki
    in_window = (rel >= 0) & (rel <= blk)
    bucket = _t5_bucket(np.clip(rel, 0, None) * dilation)
    bias = jnp.transpose(table.astype(F32)[bucket], (2, 0, 1))
    first = in_window & (ki >= blk)
    return jnp.stack([jnp.where(first[None], bias, MASK_VALUE),
                      jnp.where(in_window[None], bias, MASK_VALUE)])


def _attn_body(q_ref, kp_ref, kc_ref, vp_ref, vc_ref, bias_ref, o_ref, lse_ref):
    blk = q_ref.shape[1]
    lo = lax.broadcasted_iota(I32, (blk, LANES), 1) < HEAD_DIM
    scale = HEAD_DIM ** -0.5
    for p in range(q_ref.shape[2] // LANES):
        sl = slice(p * LANES, (p + 1) * LANES)
        q2 = q_ref[0, :, sl] * scale
        k2 = jnp.concatenate([kp_ref[0, :, sl], kc_ref[0, :, sl]], axis=0)
        v2 = jnp.concatenate([vp_ref[0, :, sl], vc_ref[0, :, sl]], axis=0)
        outs, lses = [], []
        for hh in range(2):
            qm = jnp.where(lo if hh == 0 else jnp.logical_not(lo), q2, jnp.zeros_like(q2))
            sc = lax.dot_general(qm, k2, (((1,), (1,)), ((), ())), preferred_element_type=F32)
            sc = sc + bias_ref[0, 2 * p + hh]
            m = jnp.max(sc, axis=-1, keepdims=True)
            e = jnp.exp(sc - m)
            l = jnp.sum(e, axis=-1, keepdims=True)
            o = jnp.dot(e.astype(BF16), v2, preferred_element_type=F32)
            outs.append(o / l)
            lses.append(jnp.broadcast_to(m + jnp.log(l), (blk, LANES)))
        o_ref[0, :, sl] = jnp.where(lo, outs[0], outs[1]).astype(o_ref.dtype)
        lse_ref[0, :, sl] = jnp.where(lo, lses[0], lses[1])


def _attn_group(qkv, bias, *, blk, d, gw, q_blk, k_blk, v_blk, row_blk):
    B, L, _ = qkv.shape
    nb = L // blk

    def cur(col):
        return pl.BlockSpec((1, blk, gw), lambda b, r, n: (b, n, r * row_blk + col))

    def prev(col):
        return pl.BlockSpec((1, blk, gw), lambda b, r, n: (b, jnp.maximum(n - 1, 0), r * row_blk + col))

    return pl.pallas_call(
        _attn_body,
        grid=(B, d, nb),
        in_specs=[cur(q_blk), prev(k_blk), cur(k_blk), prev(v_blk), cur(v_blk),
                  pl.BlockSpec((1,) + bias.shape[1:], lambda b, r, n: (jnp.minimum(n, 1), 0, 0, 0))],
        out_specs=[pl.BlockSpec((1, blk, gw), lambda b, r, n: (b, n, r)),
                   pl.BlockSpec((1, blk, gw), lambda b, r, n: (b, n, r))],
        out_shape=[jax.ShapeDtypeStruct((B, L, d * gw), BF16),
                   jax.ShapeDtypeStruct((B, L, d * gw), F32)],
        compiler_params=_cparams("parallel", "parallel", "parallel"),
        name=f"attn_d{d}",
    )(qkv, qkv, qkv, qkv, qkv, bias)


def _merge_body(yr_ref, o1_ref, o2_ref, o3_ref, l1_ref, l2_ref, l3_ref, g_ref, x_ref,
                wr_ref, wa_ref, wo_ref, lg_ref, lb_ref, wrt_ref,
                x1_ref, x1b_ref, lt_ref, *, alpha):
    l1, l2, l3 = l1_ref[...], l2_ref[...], l3_ref[...]
    mx = jnp.maximum(jnp.maximum(l1, l2), l3)
    w1, w2, w3 = jnp.exp(l1 - mx), jnp.exp(l2 - mx), jnp.exp(l3 - mx)
    y_att = (w1 * o1_ref[...].astype(F32) + w2 * o2_ref[...].astype(F32)
             + w3 * o3_ref[...].astype(F32)) / (w1 + w2 + w3)
    pr = jnp.dot(yr_ref[...], wr_ref[...], preferred_element_type=F32)
    pa = jnp.dot(y_att.astype(BF16), wa_ref[...], preferred_element_type=F32)
    dm = pr.shape[1]
    g = g_ref[...].astype(F32)
    merged = _sigmoid(g[:, :dm]) * pr + _sigmoid(g[:, dm:]) * pa
    mix = jnp.dot(merged.astype(BF16), wo_ref[...], preferred_element_type=F32)
    z = alpha * x_ref[...] + mix
    mu = jnp.mean(z, axis=-1, keepdims=True)
    zc = z - mu
    var = jnp.mean(zc * zc, axis=-1, keepdims=True)
    x1 = zc * lax.rsqrt(var + LN_EPS) * lg_ref[...] + lb_ref[...]
    x1_ref[...] = x1
    x1b_ref[...] = x1.astype(BF16)
    lt_ref[...] = lax.dot_general(wrt_ref[...], x1, (((1,), (1,)), ((), ())),
                                  precision=lax.Precision.HIGHEST, preferred_element_type=F32)


def _merge(y_rnn, os_, lses, proj, gate_blk, x2, wr, wa, wo, ln_g, ln_b, w_router_t, *, alpha, tm=256):
    T, D = x2.shape
    da = os_[0].shape[1]
    E = w_router_t.shape[0]
    row = lambda w: pl.BlockSpec((tm, w), lambda i: (i, 0))
    full = lambda a: pl.BlockSpec(a.shape, lambda i: (0,) * a.ndim)
    return pl.pallas_call(
        functools.partial(_merge_body, alpha=alpha),
        grid=(T // tm,),
        in_specs=[row(D), row(da), row(da), row(da), row(da), row(da), row(da),
                  pl.BlockSpec((tm, 2 * D), lambda i: (i, gate_blk)), row(D),
                  full(wr), full(wa), full(wo), full(ln_g), full(ln_b), full(w_router_t)],
        out_specs=[row(D), row(D), pl.BlockSpec((E, tm), lambda i: (0, i))],
        out_shape=[jax.ShapeDtypeStruct((T, D), F32), jax.ShapeDtypeStruct((T, D), BF16),
                   jax.ShapeDtypeStruct((E, T), F32)],
        compiler_params=_cparams("parallel"),
        name="merge",
    )(y_rnn, *os_, *lses, proj, x2, wr, wa, wo, ln_g, ln_b, w_router_t)


def _first_max(vals, idx, big):
    m = jnp.max(vals, axis=0, keepdims=True)
    i = jnp.min(jnp.where(vals == m, idx, big), axis=0, keepdims=True)
    return m, i


def _route_body(lt_ref, rb_ref, e_ref, g_ref, r_ref, cnt_ref, *, tl):
    E = lt_ref.shape[0]
    per = E // N_EXPERT_GROUPS
    neg = -jnp.inf

    @pl.when(pl.program_id(0) == 0)
    def _():
        cnt_ref[...] = jnp.zeros_like(cnt_ref)

    scores = jax.nn.sigmoid(lt_ref[...])
    sel = scores + rb_ref[...]
    rowi = lax.broadcasted_iota(I32, (E, tl), 0)

    gi = lax.broadcasted_iota(I32, (N_EXPERT_GROUPS, tl), 0)
    gsc = jnp.zeros((N_EXPERT_GROUPS, tl), F32)
    for g in range(N_EXPERT_GROUPS):
        blk = sel[g * per:(g + 1) * per]
        ri = lax.broadcasted_iota(I32, (per, tl), 0) + g * per
        m1, i1 = _first_max(blk, ri, E)
        m2 = jnp.max(jnp.where(ri == i1, neg, blk), axis=0, keepdims=True)
        gsc = jnp.where(gi == g, m1 + m2, gsc)
    keep = jnp.zeros((N_EXPERT_GROUPS, tl), F32)
    for _ in range(TOPK_GROUPS):
        _, ig = _first_max(gsc, gi, N_EXPERT_GROUPS)
        hit = gi == ig
        keep = jnp.where(hit, 1.0, keep)
        gsc = jnp.where(hit, neg, gsc)
    cur = jnp.concatenate(
        [jnp.where(keep[g:g + 1] > 0.5, sel[g * per:(g + 1) * per], neg) for g in range(N_EXPERT_GROUPS)],
        axis=0)

    ti = lax.broadcasted_iota(I32, (tl, tl), 0)
    tj = lax.broadcasted_iota(I32, (tl, tl), 1)
    earlier = jnp.where(ti < tj, 1.0, 0.0).astype(BF16)

    es, gv = [], []
    onehot = jnp.zeros((E, tl), F32)
    for k in range(TOP_K):
        _, ie = _first_max(cur, rowi, E)
        hit = rowi == ie
        es.append(ie)
        gv.append(jnp.sum(jnp.where(hit, scores, 0.0), axis=0, keepdims=True))
        onehot = jnp.where(hit, 1.0, onehot)
        cur = jnp.where(hit, neg, cur)
    gsum = gv[0]
    for k in range(1, TOP_K):
        gsum = gsum + gv[k]
    ranks = jnp.dot(onehot.astype(BF16), earlier, preferred_element_type=F32) + cnt_ref[...]
    for k in range(TOP_K):
        e_ref[k:k + 1, :] = es[k]
        g_ref[k:k + 1, :] = gv[k] / gsum * ROUTED_SCALE
        r_ref[k:k + 1, :] = jnp.sum(jnp.where(rowi == es[k], ranks, 0.0), axis=0, keepdims=True).astype(I32)
    cnt_ref[...] += jnp.sum(onehot, axis=1, keepdims=True)


def _route(logits_t, router_bias, *, tl=256):
    E, T = logits_t.shape
    kt = pl.BlockSpec((TOP_K, tl), lambda i: (0, i))
    return pl.pallas_call(
        functools.partial(_route_body, tl=tl),
        grid=(T // tl,),
        in_specs=[pl.BlockSpec((E, tl), lambda i: (0, i)), pl.BlockSpec((E, 1), lambda i: (0, 0))],
        out_specs=[kt, kt, kt, pl.BlockSpec((E, 1), lambda i: (0, 0))],
        out_shape=[jax.ShapeDtypeStruct((TOP_K, T), I32), jax.ShapeDtypeStruct((TOP_K, T), F32),
                   jax.ShapeDtypeStruct((TOP_K, T), I32), jax.ShapeDtypeStruct((E, 1), F32)],
        compiler_params=_cparams("arbitrary"),
        name="route",
    )(logits_t, router_bias.reshape(E, 1))


def _dest_body(e_ref, r_ref, ss_ref, d_ref):
    E = ss_ref.shape[0]
    tl = e_ref.shape[1]
    rowi = lax.broadcasted_iota(I32, (E, tl), 0)
    e = e_ref[...]
    rows = [jnp.sum(jnp.where(rowi == e[k:k + 1], ss_ref[...], 0), axis=0, keepdims=True)
            for k in range(e.shape[0])]
    d_ref[...] = jnp.concatenate(rows, axis=0) + r_ref[...]


def _dest(e_idx, rank, seg_start, *, tl=512):
    K, T = e_idx.shape
    E = seg_start.shape[0]
    kt = pl.BlockSpec((K, tl), lambda i: (0, i))
    return pl.pallas_call(
        _dest_body, grid=(T // tl,),
        in_specs=[kt, kt, pl.BlockSpec((E, 1), lambda i: (0, 0))],
        out_specs=kt, out_shape=jax.ShapeDtypeStruct((K, T), I32),
        compiler_params=_cparams("parallel"), name="dest",
    )(e_idx, rank, seg_start.reshape(E, 1))


def _dispatch_body(ps_ref, pc_ref, dest_hbm, x_hbm, zero_hbm, xs_hbm, dsm, sem_i, sem_r, sem_z, *, tl, K):
    i = pl.program_id(0)
    n = pl.num_programs(0)

    def row_copy(t, slot):
        return pltpu.make_async_copy(x_hbm.at[i * tl + t], xs_hbm.at[dsm[slot]], sem_r)

    def pad_copy(row):
        return pltpu.make_async_copy(zero_hbm.at[0], xs_hbm.at[row], sem_z)

    idx_cp = pltpu.make_async_copy(dest_hbm.at[i], dsm, sem_i)
    idx_cp.start()
    idx_cp.wait()

    def issue(t, c):
        for k in range(K):
            row_copy(t, k * tl + t).start()
        return c

    lax.fori_loop(0, tl, issue, 0)

    E = ps_ref.shape[0]
    per_step = -(-E // n)

    def pads(j, c, wait):
        e = jnp.minimum(i * per_step + j, E - 1)
        cnt = jnp.where(i * per_step + j < E, pc_ref[e], 0)

        def one(r, c2):
            cp = pad_copy(ps_ref[e] + r)
            cp.wait() if wait else cp.start()
            return c2

        return lax.fori_loop(0, cnt, one, c)

    lax.fori_loop(0, per_step, functools.partial(pads, wait=False), 0)

    def drain(t, c):
        for k in range(K):
            row_copy(t, k * tl + t).wait()
        return c

    lax.fori_loop(0, tl, drain, 0)
    lax.fori_loop(0, per_step, functools.partial(pads, wait=True), 0)


def _dispatch(x3, dest, pad_start, pad_cnt, n_rows, *, tl=256):
    T = x3.shape[0]
    K = dest.shape[0]
    dest_t = dest.reshape(K, T // tl, tl).transpose(1, 0, 2).reshape(T // tl, K * tl)
    zero = jnp.zeros((1,) + ROW_SLAB, x3.dtype)
    any_spec = pl.BlockSpec(memory_space=pl.ANY)
    return pl.pallas_call(
        functools.partial(_dispatch_body, tl=tl, K=K),
        grid_spec=pltpu.PrefetchScalarGridSpec(
            num_scalar_prefetch=2, grid=(T // tl,),
            in_specs=[any_spec, any_spec, any_spec],
            out_specs=any_spec,
            scratch_shapes=[pltpu.SMEM((K * tl,), I32), pltpu.SemaphoreType.DMA,
                            pltpu.SemaphoreType.DMA, pltpu.SemaphoreType.DMA]),
        out_shape=jax.ShapeDtypeStruct((n_rows,) + ROW_SLAB, x3.dtype),
        compiler_params=pltpu.CompilerParams(dimension_semantics=("arbitrary",), has_side_effects=True),
        name="dispatch",
    )(pad_start, pad_cnt, dest_t, x3, zero)


def _experts_body(be_ref, nu_ref, xs_ref, wg_ref, wu_ref, wd_ref, ys_ref, wgu_s, wd_s):
    j = pl.program_id(0)
    de = wg_ref.shape[2]
    changed = jnp.logical_or(j == 0, be_ref[j] != be_ref[jnp.maximum(j - 1, 0)])

    @pl.when(changed)
    def _():
        wgu_s[:, :de] = wg_ref[0].astype(BF16)
        wgu_s[:, de:] = wu_ref[0].astype(BF16)
        wd_s[...] = wd_ref[0].astype(BF16)

    @pl.when(j < nu_ref[0])
    def _():
        h = jnp.dot(xs_ref[...], wgu_s[...], preferred_element_type=F32)
        hg = h[:, :de]
        act = (hg * _sigmoid(hg) * h[:, de:]).astype(BF16)
        ys_ref[...] = jnp.dot(act, wd_s[...], preferred_element_type=F32).astype(ys_ref.dtype)

    @pl.when(j >= nu_ref[0])
    def _():
        ys_ref[...] = jnp.zeros_like(ys_ref)


def _experts(xs2, blk_e, n_used, wg, wu, wd):
    n_rows, D = xs2.shape
    R = EXPERT_ROWS
    de = wg.shape[2]
    xmap = lambda j, be, nu: (jnp.minimum(j, nu[0] - 1), 0)
    wmap = lambda j, be, nu: (be[j], 0, 0)
    return pl.pallas_call(
        _experts_body,
        grid_spec=pltpu.PrefetchScalarGridSpec(
            num_scalar_prefetch=2, grid=(n_rows // R,),
            in_specs=[pl.BlockSpec((R, D), xmap),
                      pl.BlockSpec((1, D, de), wmap), pl.BlockSpec((1, D, de), wmap),
                      pl.BlockSpec((1, de, D), wmap)],
            out_specs=pl.BlockSpec((R, D), lambda j, be, nu: (j, 0)),
            scratch_shapes=[pltpu.VMEM((D, 2 * de), BF16), pltpu.VMEM((de, D), BF16)]),
        out_shape=jax.ShapeDtypeStruct((n_rows, D), BF16),
        compiler_params=_cparams("arbitrary"),
        name="experts",
    )(blk_e, n_used, xs2, wg, wu, wd)


def _combine_body(dest_hbm, gate_hbm, ys_hbm, o_ref, dsm, gsm, buf, sem_i, sem_r, *, tc, K):
    i = pl.program_id(0)
    cp_d = pltpu.make_async_copy(dest_hbm.at[i], dsm, sem_i)
    cp_g = pltpu.make_async_copy(gate_hbm.at[i], gsm, sem_i)
    cp_d.start()
    cp_g.start()
    cp_d.wait()
    cp_g.wait()

    def row_copy(t, k):
        return pltpu.make_async_copy(ys_hbm.at[dsm[k * tc + t]], buf.at[k * tc + t], sem_r)

    def issue(t, c):
        for k in range(K):
            row_copy(t, k).start()
        return c

    lax.fori_loop(0, tc, issue, 0)

    def drain(t, c):
        for k in range(K):
            row_copy(t, k).wait()
        return c

    lax.fori_loop(0, tc, drain, 0)

    def mix(t, c):
        acc = gsm[t] * buf[t].astype(F32)
        for k in range(1, K):
            acc = acc + gsm[k * tc + t] * buf[k * tc + t].astype(F32)
        o_ref[t] = acc
        return c

    lax.fori_loop(0, tc, mix, 0)


def _combine(ys3, dest, gate, *, tc=256):
    K, T = dest.shape
    tiled = lambda a: a.reshape(K, T // tc, tc).transpose(1, 0, 2).reshape(T // tc, K * tc)
    any_spec = pl.BlockSpec(memory_space=pl.ANY)
    return pl.pallas_call(
        functools.partial(_combine_body, tc=tc, K=K),
        grid=(T // tc,),
        in_specs=[any_spec, any_spec, any_spec],
        out_specs=pl.BlockSpec((tc,) + ROW_SLAB, lambda i: (i, 0, 0)),
        out_shape=jax.ShapeDtypeStruct((T,) + ROW_SLAB, F32),
        scratch_shapes=[pltpu.SMEM((K * tc,), I32), pltpu.SMEM((K * tc,), F32),
                        pltpu.VMEM((K * tc,) + ROW_SLAB, ys3.dtype),
                        pltpu.SemaphoreType.DMA, pltpu.SemaphoreType.DMA],
        compiler_params=_cparams("arbitrary"),
        name="combine",
    )(tiled(dest), tiled(gate), ys3)


def _final_body(x1_ref, rt_ref, wgu_ref, wd_ref, lg_ref, lb_ref, o_ref, *, alpha):
    x1 = x1_ref[...]
    ds_ = wd_ref.shape[0]
    h = jnp.dot(x1.astype(BF16), wgu_ref[...], preferred_element_type=F32)
    hg = h[:, :ds_]
    act = (hg * _sigmoid(hg) * h[:, ds_:]).astype(BF16)
    shared = jnp.dot(act, wd_ref[...], preferred_element_type=F32)
    z = alpha * x1 + (rt_ref[...] + shared)
    mu = jnp.mean(z, axis=-1, keepdims=True)
    zc = z - mu
    var = jnp.mean(zc * zc, axis=-1, keepdims=True)
    o_ref[...] = zc * lax.rsqrt(var + LN_EPS) * lg_ref[...] + lb_ref[...]


def _final(x1, routed, wgu, wd, ln_g, ln_b, *, alpha, tm=512):
    T, D = x1.shape
    row = pl.BlockSpec((tm, D), lambda i: (i, 0))
    full = lambda a: pl.BlockSpec(a.shape, lambda i: (0,) * a.ndim)
    return pl.pallas_call(
        functools.partial(_final_body, alpha=alpha),
        grid=(T // tm,),
        in_specs=[row, row, full(wgu), full(wd), full(ln_g), full(ln_b)],
        out_specs=row, out_shape=jax.ShapeDtypeStruct((T, D), F32),
        compiler_params=_cparams("parallel"), name="final",
    )(x1, routed, wgu, wd, ln_g, ln_b)


def _mixer_ln1(x, w_in, b_in, conv_w, conv_b, w_rg_a, b_rg_a, w_rg_i, b_rg_i, lru_lambda,
               w_proj_rnn, w_proj_att, rel_bias, w_out, ln1_g, ln1_b, w_router, alpha):
    B, S, D = x.shape
    T = B * S
    d_rnn = conv_w.shape[-1]
    gw = HEADS_PER_GROUP * HEAD_DIM
    d_att = gw * len(DILATED_GROUPS)
    a0 = 2 * d_rnn
    a1 = a0 + 3 * d_att
    perm = lambda w: jnp.concatenate([w[..., :a0], w[..., a1:], w[..., a0:a1]], axis=-1)
    w_p = perm(w_in).astype(BF16)
    b_p = perm(b_in).reshape(1, -1)
    N = w_p.shape[1]
    x2 = x.reshape(T, D)

    proj = _in_proj(x2, w_p, b_p)
    y_rnn = _rnn(proj.reshape(B, S, N), conv_w, conv_b, w_rg_a, b_rg_a, w_rg_i, b_rg_i, lru_lambda)

    qkv0 = a0 + 2 * D
    os_, lses = [], []
    for g, (window, d) in enumerate(DILATED_GROUPS):
        blk = window // d
        bias = _attn_bias(rel_bias[:, g * HEADS_PER_GROUP:(g + 1) * HEADS_PER_GROUP], window, d)
        if d == 1:
            o, lse = _attn_group(proj.reshape(B, S, N), bias, blk=blk, d=1, gw=gw,
                                 q_blk=(qkv0 + g * gw) // gw, k_blk=(qkv0 + d_att + g * gw) // gw,
                                 v_blk=(qkv0 + 2 * d_att + g * gw) // gw, row_blk=N // gw)
        else:
            cols = [proj[:, qkv0 + j * d_att + g * gw: qkv0 + j * d_att + (g + 1) * gw] for j in range(3)]
            qkv = jnp.concatenate(cols, axis=-1).reshape(B, S // d, d * 3 * gw)
            o, lse = _attn_group(qkv, bias, blk=blk, d=d, gw=gw, q_blk=0, k_blk=1, v_blk=2, row_blk=3)
        os_.append(o.reshape(T, gw))
        lses.append(lse.reshape(T, gw))

    return _merge(y_rnn.reshape(T, d_rnn), os_, lses, proj, a0 // (2 * D), x2,
                  w_proj_rnn.astype(BF16), w_proj_att.astype(BF16), w_out.astype(BF16),
                  ln1_g.reshape(1, D), ln1_b.reshape(1, D), w_router.T, alpha=alpha)


def _moe_ln2(x1, x1b, logits_t, router_bias, w_exp_gate, w_exp_up, w_exp_down,
             w_sh_gate, w_sh_up, w_sh_down, ln2_g, ln2_b, alpha):
    T, D = x1.shape
    E = logits_t.shape[0]
    R = EXPERT_ROWS
    n_blk = T * TOP_K // R + E
    n_rows = n_blk * R

    e_idx, gate, rank, counts = _route(logits_t, router_bias)
    counts = counts.reshape(E).astype(I32)
    padded = (counts + R - 1) // R * R
    pad_end = jnp.cumsum(padded)
    seg_start = pad_end - padded
    n_used = (pad_end[-1:] // R).astype(I32)
    blk_e = jnp.minimum(jnp.sum(jnp.arange(n_blk, dtype=I32)[:, None] >= (pad_end // R)[None, :], axis=1),
                        E - 1).astype(I32)
    dest = _dest(e_idx, rank, seg_start)

    xs3 = _dispatch(x1b.reshape((T,) + ROW_SLAB), dest, seg_start + counts, padded - counts, n_rows)
    ys = _experts(xs3.reshape(n_rows, D), blk_e, n_used, w_exp_gate, w_exp_up, w_exp_down)
    routed = _combine(ys.reshape((n_rows,) + ROW_SLAB), dest, gate)
    wgu = jnp.concatenate([w_sh_gate, w_sh_up], axis=-1).astype(BF16)
    return _final(x1, routed.reshape(T, D), wgu, w_sh_down.astype(BF16),
                  ln2_g.reshape(1, D), ln2_b.reshape(1, D), alpha=alpha)


def kernel(x, w_in, b_in, conv_w, conv_b, w_rg_a, b_rg_a, w_rg_i, b_rg_i, lru_lambda,
           w_proj_rnn, w_proj_att, rel_bias, w_out, ln1_g, ln1_b, w_router, router_bias,
           w_exp_gate, w_exp_up, w_exp_down, w_sh_gate, w_sh_up, w_sh_down, ln2_g, ln2_b):
    B, S, D = x.shape
    depth = w_in.shape[0]
    alpha = (2 * depth) ** 0.25
    for i in range(depth):
        x1, x1b, logits_t = _mixer_ln1(
            x, w_in[i], b_in[i], conv_w[i], conv_b[i], w_rg_a[i], b_rg_a[i], w_rg_i[i], b_rg_i[i],
            lru_lambda[i], w_proj_rnn[i], w_proj_att[i], rel_bias, w_out[i], ln1_g[i], ln1_b[i],
            w_router[i], alpha)
        out = _moe_ln2(x1, x1b, logits_t, router_bias[i], w_exp_gate[i], w_exp_up[i], w_exp_down[i],
                       w_sh_gate[i], w_sh_up[i], w_sh_down[i], ln2_g[i], ln2_b[i], alpha)
        x = out.reshape(B, S, D)
    return x
```

```python
import functools

import numpy as np
import jax
import jax.numpy as jnp
from jax import lax
from jax.experimental import pallas as pl
from jax.experimental.pallas import tpu as pltpu

F32 = jnp.float32
BF16 = jnp.bfloat16
I32 = jnp.int32
U32 = jnp.uint32

RNN_HEADS = 16
LRU_C = 8.0
HEAD_DIM = 64
HEADS_PER_GROUP = 8
DILATED_GROUPS = ((128, 1), (512, 4), (2048, 16))
NUM_BUCKETS = 32
MAX_DISTANCE = 2048
TOP_K = 8
N_EXPERT_GROUPS = 8
TOPK_GROUPS = 4
ROUTED_SCALE = 2.5
LN_EPS = 1e-5

LANES = 128
SUBLANES = 8
VMEM_LIMIT_BYTES = 56 * 1024 * 1024

MASK_VALUE = -1e30
EXPERT_ROWS = 256
ROW_TILE = 256


def _cparams(*sem):
    return pltpu.CompilerParams(dimension_semantics=sem, vmem_limit_bytes=VMEM_LIMIT_BYTES)


def _sigmoid(v):
    return 0.5 * (jnp.tanh(0.5 * v) + 1.0)


def _in_proj_body(x_ref, w_ref, b_ref, o_ref, *, tn):
    xb = x_ref[...].astype(BF16)
    for j in range(o_ref.shape[1] // tn):
        sl = slice(j * tn, (j + 1) * tn)
        acc = jnp.dot(xb, w_ref[:, sl], preferred_element_type=F32)
        o_ref[:, sl] = (acc + b_ref[:, sl]).astype(o_ref.dtype)


def _in_proj(x2, w, b, *, tm=512, tn=512):
    T, D = x2.shape
    N = w.shape[1]
    return pl.pallas_call(
        functools.partial(_in_proj_body, tn=tn),
        grid=(T // tm,),
        in_specs=[pl.BlockSpec((tm, D), lambda i: (i, 0)),
                  pl.BlockSpec((D, N), lambda i: (0, 0), pipeline_mode=pl.Buffered(1)),
                  pl.BlockSpec((1, N), lambda i: (0, 0))],
        out_specs=pl.BlockSpec((tm, N), lambda i: (i, 0)),
        out_shape=jax.ShapeDtypeStruct((T, N), BF16),
        compiler_params=_cparams("parallel"),
        name="in_proj",
    )(x2, w, b)


def _rnn_body(xr_ref, gr_ref, cw_ref, cb_ref, wg_ref, bg_ref, lam_ref, y_ref,
              xext, a_s, b_s, hc, *, ts, cb):
    s = pl.program_id(2)

    @pl.when(s == 0)
    def _():
        xext[0:SUBLANES, :] = jnp.zeros((SUBLANES, cb), F32)
        hc[...] = jnp.zeros_like(hc)

    xr = xr_ref[0].astype(F32)
    xext[SUBLANES:SUBLANES + ts, :] = xr
    nw = cw_ref.shape[0]
    xc = cw_ref[nw - 1:nw, :] * xr + cb_ref[...]
    for j in range(nw - 1):
        off = SUBLANES - (nw - 1 - j)
        xc = xc + cw_ref[j:j + 1, :] * xext[off:off + ts, :]
    xext[0:SUBLANES, :] = xext[ts:ts + SUBLANES, :]

    gates = jnp.dot(xc.astype(BF16), wg_ref[0], preferred_element_type=F32) + bg_ref[...]
    r = _sigmoid(gates[:, :cb])
    ig = _sigmoid(gates[:, cb:])
    nl = -lam_ref[...]
    sp = jnp.maximum(nl, 0.0) + jnp.log1p(jnp.exp(-jnp.abs(nl)))
    log_a = (-LRU_C) * r * sp
    a = jnp.exp(log_a)
    u = jnp.sqrt(1.0 - a * a) * (ig * xc)

    row = lax.broadcasted_iota(I32, (ts, cb), 0) & (SUBLANES - 1)
    av, bv = a, u
    for sft in (1, 2, 4):
        a_sh = pltpu.roll(av, sft, 0)
        b_sh = pltpu.roll(bv, sft, 0)
        m = row >= sft
        bv = jnp.where(m, av * b_sh + bv, bv)
        av = jnp.where(m, av * a_sh, av)
    a_s[...] = av
    b_s[...] = bv

    def carry(g, h):
        i0 = pl.multiple_of(g * SUBLANES, SUBLANES)
        h8 = b_s[pl.ds(i0, SUBLANES), :] + a_s[pl.ds(i0, SUBLANES), :] * h
        b_s[pl.ds(i0, SUBLANES), :] = h8
        return h8[SUBLANES - 1:SUBLANES, :]

    hc[0:1, :] = lax.fori_loop(0, ts // SUBLANES, carry, hc[0:1, :], unroll=8)
    gr = gr_ref[0].astype(F32)
    y_ref[0] = (b_s[...] * jax.nn.gelu(gr)).astype(y_ref.dtype)


def _block_diag(w, per):
    H, d, _ = w.shape
    w4 = w.reshape(H // per, per, d, d)
    out = jnp.einsum('gpij,pq->gpiqj', w4, jnp.eye(per, dtype=w.dtype))
    return out.reshape(H // per, per * d, per * d)


def _rnn(proj3, conv_w, conv_b, w_rg_a, b_rg_a, w_rg_i, b_rg_i, lam, *, ts=512, cb=256):
    B, S, _ = proj3.shape
    d_rnn = conv_w.shape[-1]
    nc = d_rnn // cb
    per = cb // (d_rnn // RNN_HEADS)
    wg = jnp.concatenate([_block_diag(w_rg_a, per), _block_diag(w_rg_i, per)], axis=-1).astype(BF16)
    bg = jnp.concatenate([b_rg_a.reshape(nc, 1, cb), b_rg_i.reshape(nc, 1, cb)], axis=-1)
    return pl.pallas_call(
        functools.partial(_rnn_body, ts=ts, cb=cb),
        grid=(B, nc, S // ts),
        in_specs=[pl.BlockSpec((1, ts, cb), lambda b, c, s: (b, s, c)),
                  pl.BlockSpec((1, ts, cb), lambda b, c, s: (b, s, nc + c)),
                  pl.BlockSpec((conv_w.shape[0], cb), lambda b, c, s: (0, c)),
                  pl.BlockSpec((1, cb), lambda b, c, s: (0, c)),
                  pl.BlockSpec((1, cb, 2 * cb), lambda b, c, s: (c, 0, 0)),
                  pl.BlockSpec((None, 1, 2 * cb), lambda b, c, s: (c, 0, 0)),
                  pl.BlockSpec((1, cb), lambda b, c, s: (0, c))],
        out_specs=pl.BlockSpec((1, ts, cb), lambda b, c, s: (b, s, c)),
        out_shape=jax.ShapeDtypeStruct((B, S, d_rnn), BF16),
        scratch_shapes=[pltpu.VMEM((ts + SUBLANES, cb), F32), pltpu.VMEM((ts, cb), F32),
                        pltpu.VMEM((ts, cb), F32), pltpu.VMEM((SUBLANES, cb), F32)],
        compiler_params=_cparams("parallel", "parallel", "arbitrary"),
        name="rnn",
    )(proj3, proj3, conv_w, conv_b.reshape(1, d_rnn), wg, bg, lam.reshape(1, d_rnn))


def _t5_bucket(dist):
    max_exact = NUM_BUCKETS // 2
    d = np.maximum(dist, 1).astype(np.float64)
    large = max_exact + (np.log(d / max_exact) / np.log(MAX_DISTANCE / max_exact)
                         * (NUM_BUCKETS - max_exact)).astype(np.int64)
    large = np.minimum(large, NUM_BUCKETS - 1)
    return np.where(dist < max_exact, dist, large).astype(np.int32)


def _attn_bias(table, window, dilation):
    blk = window // dilation
    qi = np.arange(blk)[:, None]
    ki = np.arange(2 * blk)[None, :]
    rel = qi + blk - ki
    in_window = (rel >= 0) & (rel <= blk)
    bucket = _t5_bucket(np.clip(rel, 0, None) * dilation)
    onehot = (bucket[..., None] == np.arange(NUM_BUCKETS)).astype(np.float32)
    bias = jnp.einsum('qkn,nh->hqk', onehot, table.astype(F32), precision=lax.Precision.HIGHEST)
    first = in_window & (ki >= blk)
    return jnp.stack([jnp.where(first[None], bias, MASK_VALUE),
                      jnp.where(in_window[None], bias, MASK_VALUE)])


def _attn_body(q_ref, kp_ref, kc_ref, vp_ref, vc_ref, bias_ref, o_ref, lse_ref):
    blk = q_ref.shape[1]
    lo = lax.broadcasted_iota(I32, (blk, LANES), 1) < HEAD_DIM
    scale = HEAD_DIM ** -0.5
    for p in range(q_ref.shape[2] // LANES):
        sl = slice(p * LANES, (p + 1) * LANES)
        q2 = q_ref[0, :, sl] * scale
        k2 = jnp.concatenate([kp_ref[0, :, sl], kc_ref[0, :, sl]], axis=0)
        v2 = jnp.concatenate([vp_ref[0, :, sl], vc_ref[0, :, sl]], axis=0)
        outs, lses = [], []
        for hh in range(2):
            qm = jnp.where(lo if hh == 0 else jnp.logical_not(lo), q2, jnp.zeros_like(q2))
            sc = lax.dot_general(qm, k2, (((1,), (1,)), ((), ())), preferred_element_type=F32)
            sc = sc + bias_ref[0, 2 * p + hh]
            m = jnp.max(sc, axis=-1, keepdims=True)
            e = jnp.exp(sc - m)
            l = jnp.sum(e, axis=-1, keepdims=True)
            o = jnp.dot(e.astype(BF16), v2, preferred_element_type=F32)
            outs.append(o / l)
            lses.append(jnp.broadcast_to(m + jnp.log(l), (blk, LANES)))
        o_ref[0, :, sl] = jnp.where(lo, outs[0], outs[1]).astype(o_ref.dtype)
        lse_ref[0, :, sl] = jnp.where(lo, lses[0], lses[1])


def _attn_group(qkv, bias, *, blk, d, gw, q_blk, k_blk, v_blk, row_blk):
    B, L, _ = qkv.shape
    nb = L // blk

    def cur(col):
        return pl.BlockSpec((1, blk, gw), lambda b, r, n: (b, n, r * row_blk + col))

    def prev(col):
        return pl.BlockSpec((1, blk, gw), lambda b, r, n: (b, jnp.maximum(n - 1, 0), r * row_blk + col))

    return pl.pallas_call(
        _attn_body,
        grid=(B, d, nb),
        in_specs=[cur(q_blk), prev(k_blk), cur(k_blk), prev(v_blk), cur(v_blk),
                  pl.BlockSpec((1,) + bias.shape[1:], lambda b, r, n: (jnp.minimum(n, 1), 0, 0, 0))],
        out_specs=[pl.BlockSpec((1, blk, gw), lambda b, r, n: (b, n, r)),
                   pl.BlockSpec((1, blk, gw), lambda b, r, n: (b, n, r))],
        out_shape=[jax.ShapeDtypeStruct((B, L, d * gw), BF16),
                   jax.ShapeDtypeStruct((B, L, d * gw), F32)],
        compiler_params=_cparams("parallel", "parallel", "parallel"),
        name=f"attn_d{d}",
    )(qkv, qkv, qkv, qkv, qkv, bias)


def _merge_body(yr_ref, o1_ref, o2_ref, o3_ref, l1_ref, l2_ref, l3_ref, g_ref, x_ref,
                wr_ref, wa_ref, wo_ref, lg_ref, lb_ref, wrt_ref,
                x1_ref, x1p_ref, lt_ref, *, alpha):
    l1, l2, l3 = l1_ref[...], l2_ref[...], l3_ref[...]
    mx = jnp.maximum(jnp.maximum(l1, l2), l3)
    w1, w2, w3 = jnp.exp(l1 - mx), jnp.exp(l2 - mx), jnp.exp(l3 - mx)
    y_att = (w1 * o1_ref[...].astype(F32) + w2 * o2_ref[...].astype(F32)
             + w3 * o3_ref[...].astype(F32)) / (w1 + w2 + w3)
    pr = jnp.dot(yr_ref[...], wr_ref[...], preferred_element_type=F32)
    pa = jnp.dot(y_att.astype(BF16), wa_ref[...], preferred_element_type=F32)
    dm = pr.shape[1]
    g = g_ref[...].astype(F32)
    merged = _sigmoid(g[:, :dm]) * pr + _sigmoid(g[:, dm:]) * pa
    mix = jnp.dot(merged.astype(BF16), wo_ref[...], preferred_element_type=F32)
    z = alpha * x_ref[...] + mix
    mu = jnp.mean(z, axis=-1, keepdims=True)
    zc = z - mu
    var = jnp.mean(zc * zc, axis=-1, keepdims=True)
    x1 = zc * lax.rsqrt(var + LN_EPS) * lg_ref[...] + lb_ref[...]
    x1_ref[...] = x1
    x1p_ref[...] = _pack_rows(x1)
    lt_ref[...] = lax.dot_general(wrt_ref[...], x1, (((1,), (1,)), ((), ())),
                                  precision=lax.Precision.HIGHEST, preferred_element_type=F32)


def _merge(y_rnn, os_, lses, proj, gate_blk, x2, wr, wa, wo, ln_g, ln_b, w_router_t, *, alpha, tm=256):
    T, D = x2.shape
    da = os_[0].shape[1]
    E = w_router_t.shape[0]
    row = lambda w: pl.BlockSpec((tm, w), lambda i: (i, 0))
    full = lambda a: pl.BlockSpec(a.shape, lambda i: (0,) * a.ndim)
    return pl.pallas_call(
        functools.partial(_merge_body, alpha=alpha),
        grid=(T // tm,),
        in_specs=[row(D), row(da), row(da), row(da), row(da), row(da), row(da),
                  pl.BlockSpec((tm, 2 * D), lambda i: (i, gate_blk)), row(D),
                  full(wr), full(wa), full(wo), full(ln_g), full(ln_b), full(w_router_t)],
        out_specs=[row(D), row(D // 2), pl.BlockSpec((E, tm), lambda i: (0, i))],
        out_shape=[jax.ShapeDtypeStruct((T, D), F32), jax.ShapeDtypeStruct((T, D // 2), U32),
                   jax.ShapeDtypeStruct((E, T), F32)],
        compiler_params=_cparams("parallel"),
        name="merge",
    )(y_rnn, *os_, *lses, proj, x2, wr, wa, wo, ln_g, ln_b, w_router_t)


def _first_max(vals, idx, big):
    m = jnp.max(vals, axis=0, keepdims=True)
    i = jnp.min(jnp.where(vals == m, idx, big), axis=0, keepdims=True)
    return m, i


def _route_body(lt_ref, rb_ref, e_ref, g_ref, r_ref, cnt_ref, *, tl):
    E = lt_ref.shape[0]
    per = E // N_EXPERT_GROUPS
    neg = -jnp.inf

    @pl.when(pl.program_id(0) == 0)
    def _():
        cnt_ref[...] = jnp.zeros_like(cnt_ref)

    scores = jax.nn.sigmoid(lt_ref[...])
    sel = scores + rb_ref[...]
    rowi = lax.broadcasted_iota(I32, (E, tl), 0)

    gi = lax.broadcasted_iota(I32, (N_EXPERT_GROUPS, tl), 0)
    gsc = jnp.zeros((N_EXPERT_GROUPS, tl), F32)
    for g in range(N_EXPERT_GROUPS):
        blk = sel[g * per:(g + 1) * per]
        ri = lax.broadcasted_iota(I32, (per, tl), 0) + g * per
        m1, i1 = _first_max(blk, ri, E)
        m2 = jnp.max(jnp.where(ri == i1, neg, blk), axis=0, keepdims=True)
        gsc = jnp.where(gi == g, m1 + m2, gsc)
    keep = jnp.zeros((N_EXPERT_GROUPS, tl), F32)
    for _ in range(TOPK_GROUPS):
        _, ig = _first_max(gsc, gi, N_EXPERT_GROUPS)
        hit = gi == ig
        keep = jnp.where(hit, 1.0, keep)
        gsc = jnp.where(hit, neg, gsc)
    cur = jnp.concatenate(
        [jnp.where(keep[g:g + 1] > 0.5, sel[g * per:(g + 1) * per], neg) for g in range(N_EXPERT_GROUPS)],
        axis=0)

    ti = lax.broadcasted_iota(I32, (tl, tl), 0)
    tj = lax.broadcasted_iota(I32, (tl, tl), 1)
    earlier = jnp.where(ti < tj, 1.0, 0.0).astype(BF16)

    es, gv = [], []
    onehot = jnp.zeros((E, tl), F32)
    for k in range(TOP_K):
        _, ie = _first_max(cur, rowi, E)
        hit = rowi == ie
        es.append(ie)
        gv.append(jnp.sum(jnp.where(hit, scores, 0.0), axis=0, keepdims=True))
        onehot = jnp.where(hit, 1.0, onehot)
        cur = jnp.where(hit, neg, cur)
    gsum = gv[0]
    for k in range(1, TOP_K):
        gsum = gsum + gv[k]
    ranks = jnp.dot(onehot.astype(BF16), earlier, preferred_element_type=F32) + cnt_ref[...]
    for k in range(TOP_K):
        e_ref[k:k + 1, :] = es[k]
        g_ref[k:k + 1, :] = gv[k] / gsum * ROUTED_SCALE
        r_ref[k:k + 1, :] = jnp.sum(jnp.where(rowi == es[k], ranks, 0.0), axis=0, keepdims=True).astype(I32)
    cnt_ref[...] += jnp.sum(onehot, axis=1, keepdims=True)


def _route(logits_t, router_bias, *, tl=256):
    E, T = logits_t.shape
    kt = pl.BlockSpec((TOP_K, tl), lambda i: (0, i))
    return pl.pallas_call(
        functools.partial(_route_body, tl=tl),
        grid=(T // tl,),
        in_specs=[pl.BlockSpec((E, tl), lambda i: (0, i)), pl.BlockSpec((E, 1), lambda i: (0, 0))],
        out_specs=[kt, kt, kt, pl.BlockSpec((E, 1), lambda i: (0, 0))],
        out_shape=[jax.ShapeDtypeStruct((TOP_K, T), I32), jax.ShapeDtypeStruct((TOP_K, T), F32),
                   jax.ShapeDtypeStruct((TOP_K, T), I32), jax.ShapeDtypeStruct((E, 1), F32)],
        compiler_params=_cparams("arbitrary"),
        name="route",
    )(logits_t, router_bias.reshape(E, 1))


def _dest_body(e_ref, r_ref, ss_ref, d_ref):
    E = ss_ref.shape[0]
    tl = e_ref.shape[1]
    rowi = lax.broadcasted_iota(I32, (E, tl), 0)
    e = e_ref[...]
    rows = [jnp.sum(jnp.where(rowi == e[k:k + 1], ss_ref[...], 0), axis=0, keepdims=True)
            for k in range(e.shape[0])]
    d_ref[...] = jnp.concatenate(rows, axis=0) + r_ref[...]


def _dest(e_idx, rank, seg_start, *, tl=512):
    K, T = e_idx.shape
    E = seg_start.shape[0]
    kt = pl.BlockSpec((K, tl), lambda i: (0, i))
    return pl.pallas_call(
        _dest_body, grid=(T // tl,),
        in_specs=[kt, kt, pl.BlockSpec((E, 1), lambda i: (0, 0))],
        out_specs=kt, out_shape=jax.ShapeDtypeStruct((K, T), I32),
        compiler_params=_cparams("parallel"), name="dest",
    )(e_idx, rank, seg_start.reshape(E, 1))


def _pack_rows(v):
    w = v.shape[1] // 2
    lo = pltpu.bitcast(v[:, :w].astype(BF16).astype(F32), U32) >> 16
    hi = pltpu.bitcast(v[:, w:].astype(BF16).astype(F32), U32) & jnp.uint32(0xFFFF0000)
    return lo | hi


def _unpack_rows(p):
    lo = pltpu.bitcast(p << 16, F32)
    hi = pltpu.bitcast(p & jnp.uint32(0xFFFF0000), F32)
    return lo, hi


def _dispatch_body(ps_ref, pc_ref, nu_ref, dest_hbm, x_ref, xs_hbm, dsm, zrow, sem_i, sem_r, sem_z, sem_t,
                   *, tl, K):
    i = pl.program_id(0)
    n = pl.num_programs(0)
    R = zrow.shape[0]

    def pad_copy(row):
        return pltpu.make_async_copy(zrow.at[pl.ds(0, 1), :], xs_hbm.at[pl.ds(row, 1), :], sem_z)

    idx_cp = pltpu.make_async_copy(dest_hbm.at[i], dsm, sem_i)
    idx_cp.start()
    zrow[...] = jnp.zeros_like(zrow)
    idx_cp.wait()

    n_blk = xs_hbm.shape[0] // R
    tail_per_step = -(-n_blk // n)

    def tail(j, c, wait):
        blk = i * tail_per_step + j

        @pl.when(jnp.logical_and(blk >= nu_ref[0], blk < n_blk))
        def _():
            cp = pltpu.make_async_copy(zrow, xs_hbm.at[pl.ds(pl.multiple_of(blk * R, R), R), :], sem_t)
            cp.wait() if wait else cp.start()

        return c

    lax.fori_loop(0, tail_per_step, functools.partial(tail, wait=False), 0)

    def issue(t, c):
        src = x_ref.at[pl.ds(t, 1), :]
        for k in range(K):
            pltpu.make_async_copy(src, xs_hbm.at[pl.ds(dsm[k * tl + t], 1), :], sem_r).start()
        return c

    lax.fori_loop(0, tl, issue, 0)

    E = ps_ref.shape[0]
    per_step = -(-E // n)

    def pads(j, c, wait):
        e = jnp.minimum(i * per_step + j, E - 1)
        cnt = jnp.where(i * per_step + j < E, pc_ref[e], 0)

        def one(r, c2):
            cp = pad_copy(ps_ref[e] + r)
            cp.wait() if wait else cp.start()
            return c2

        return lax.fori_loop(0, cnt, one, c)

    lax.fori_loop(0, per_step, functools.partial(pads, wait=False), 0)
    for k in range(K):
        pltpu.make_async_copy(x_ref, xs_hbm.at[pl.ds(0, tl), :], sem_r).wait()
    lax.fori_loop(0, per_step, functools.partial(pads, wait=True), 0)
    lax.fori_loop(0, tail_per_step, functools.partial(tail, wait=True), 0)


def _dispatch(xp, dest_t, pad_start, pad_cnt, n_used, n_rows, *, tl):
    T, W = xp.shape
    K = dest_t.shape[1] // tl
    any_spec = pl.BlockSpec(memory_space=pl.ANY)
    return pl.pallas_call(
        functools.partial(_dispatch_body, tl=tl, K=K),
        grid_spec=pltpu.PrefetchScalarGridSpec(
            num_scalar_prefetch=3, grid=(T // tl,),
            in_specs=[any_spec, pl.BlockSpec((tl, W), lambda i, ps, pc, nu: (i, 0))],
            out_specs=any_spec,
            scratch_shapes=[pltpu.SMEM((K * tl,), I32), pltpu.VMEM((EXPERT_ROWS, W), U32),
                            pltpu.SemaphoreType.DMA, pltpu.SemaphoreType.DMA, pltpu.SemaphoreType.DMA,
                            pltpu.SemaphoreType.DMA]),
        out_shape=jax.ShapeDtypeStruct((n_rows, W), U32),
        compiler_params=_cparams("arbitrary"),
        name="dispatch",
    )(pad_start, pad_cnt, n_used, dest_t, xp)


def _experts_body(be_ref, nu_ref, xs_ref, wg_ref, wu_ref, wd_ref, ys_ref, wgu_s, wd_s):
    j = pl.program_id(0)
    de = wg_ref.shape[2]
    changed = jnp.logical_or(j == 0, be_ref[j] != be_ref[jnp.maximum(j - 1, 0)])

    @pl.when(changed)
    def _():
        wgu_s[:, :de] = wg_ref[0].astype(BF16)
        wgu_s[:, de:] = wu_ref[0].astype(BF16)
        wd_s[...] = wd_ref[0].astype(BF16)

    @pl.when(j < nu_ref[0])
    def _():
        lo, hi = _unpack_rows(xs_ref[...])
        xb = jnp.concatenate([lo.astype(BF16), hi.astype(BF16)], axis=1)
        h = jnp.dot(xb, wgu_s[...], preferred_element_type=F32)
        hg = h[:, :de]
        act = (hg * _sigmoid(hg) * h[:, de:]).astype(BF16)
        ys_ref[...] = _pack_rows(jnp.dot(act, wd_s[...], preferred_element_type=F32))

    @pl.when(j >= nu_ref[0])
    def _():
        ys_ref[...] = jnp.zeros_like(ys_ref)


def _experts(xs, blk_e, n_used, wg, wu, wd):
    n_rows, W = xs.shape
    R = EXPERT_ROWS
    D, de = wg.shape[1], wg.shape[2]
    xmap = lambda j, be, nu: (jnp.minimum(j, nu[0] - 1), 0)
    wmap = lambda j, be, nu: (be[j], 0, 0)
    return pl.pallas_call(
        _experts_body,
        grid_spec=pltpu.PrefetchScalarGridSpec(
            num_scalar_prefetch=2, grid=(n_rows // R,),
            in_specs=[pl.BlockSpec((R, W), xmap),
                      pl.BlockSpec((1, D, de), wmap), pl.BlockSpec((1, D, de), wmap),
                      pl.BlockSpec((1, de, D), wmap)],
            out_specs=pl.BlockSpec((R, W), lambda j, be, nu: (j, 0)),
            scratch_shapes=[pltpu.VMEM((D, 2 * de), BF16), pltpu.VMEM((de, D), BF16)]),
        out_shape=jax.ShapeDtypeStruct((n_rows, W), U32),
        compiler_params=_cparams("arbitrary"),
        name="experts",
    )(blk_e, n_used, xs, wg, wu, wd)


def _combine_body(dest_hbm, ys_hbm, gate_ref, x1_ref, wgu_ref, wd_ref, lg_ref, lb_ref, o_ref,
                  dsm, buf, sem_i, sem_r, *, tc, K, alpha):
    i = pl.program_id(0)
    idx_cp = pltpu.make_async_copy(dest_hbm.at[i], dsm, sem_i)
    idx_cp.start()
    idx_cp.wait()

    def issue(t, c):
        for k in range(K):
            r = k * tc + t
            pltpu.make_async_copy(ys_hbm.at[pl.ds(dsm[r], 1), :], buf.at[pl.ds(r, 1), :], sem_r).start()
        return c

    lax.fori_loop(0, tc, issue, 0)

    x1 = x1_ref[...]
    ds_ = wd_ref.shape[0]
    h = jnp.dot(x1.astype(BF16), wgu_ref[...], preferred_element_type=F32)
    hg = h[:, :ds_]
    act = (hg * _sigmoid(hg) * h[:, ds_:]).astype(BF16)
    shared = jnp.dot(act, wd_ref[...], preferred_element_type=F32)

    pltpu.make_async_copy(ys_hbm.at[pl.ds(0, K * tc), :], buf, sem_r).wait()
    g = gate_ref[...]
    lo_acc = hi_acc = None
    for k in range(K):
        lo, hi = _unpack_rows(buf[k * tc:(k + 1) * tc, :])
        gk = g[:, k:k + 1]
        lo_acc = gk * lo if k == 0 else lo_acc + gk * lo
        hi_acc = gk * hi if k == 0 else hi_acc + gk * hi
    routed = jnp.concatenate([lo_acc, hi_acc], axis=1)

    z = alpha * x1 + (routed + shared)
    mu = jnp.mean(z, axis=-1, keepdims=True)
    zc = z - mu
    var = jnp.mean(zc * zc, axis=-1, keepdims=True)
    o_ref[...] = zc * lax.rsqrt(var + LN_EPS) * lg_ref[...] + lb_ref[...]


def _combine(ys, dest_t, gate_tk, x1, wgu, wd, ln_g, ln_b, *, alpha, tc):
    T, D = x1.shape
    K = gate_tk.shape[1]
    W = ys.shape[1]
    any_spec = pl.BlockSpec(memory_space=pl.ANY)
    full = lambda a: pl.BlockSpec(a.shape, lambda i: (0,) * a.ndim)
    return pl.pallas_call(
        functools.partial(_combine_body, tc=tc, K=K, alpha=alpha),
        grid=(T // tc,),
        in_specs=[any_spec, any_spec, pl.BlockSpec((tc, K), lambda i: (i, 0)),
                  pl.BlockSpec((tc, D), lambda i: (i, 0)), full(wgu), full(wd), full(ln_g), full(ln_b)],
        out_specs=pl.BlockSpec((tc, D), lambda i: (i, 0)),
        out_shape=jax.ShapeDtypeStruct((T, D), F32),
        scratch_shapes=[pltpu.SMEM((K * tc,), I32), pltpu.VMEM((K * tc, W), U32),
                        pltpu.SemaphoreType.DMA, pltpu.SemaphoreType.DMA],
        compiler_params=_cparams("arbitrary"),
        name="combine",
    )(dest_t, ys, gate_tk, x1, wgu, wd, ln_g, ln_b)


def _mixer_ln1(x, w_in, b_in, conv_w, conv_b, w_rg_a, b_rg_a, w_rg_i, b_rg_i, lru_lambda,
               w_proj_rnn, w_proj_att, rel_bias, w_out, ln1_g, ln1_b, w_router, alpha):
    B, S, D = x.shape
    T = B * S
    d_rnn = conv_w.shape[-1]
    gw = HEADS_PER_GROUP * HEAD_DIM
    d_att = gw * len(DILATED_GROUPS)
    a0 = 2 * d_rnn
    a1 = a0 + 3 * d_att
    perm = lambda w: jnp.concatenate([w[..., :a0], w[..., a1:], w[..., a0:a1]], axis=-1)
    w_p = perm(w_in).astype(BF16)
    b_p = perm(b_in).reshape(1, -1)
    N = w_p.shape[1]
    x2 = x.reshape(T, D)

    proj = _in_proj(x2, w_p, b_p)
    y_rnn = _rnn(proj.reshape(B, S, N), conv_w, conv_b, w_rg_a, b_rg_a, w_rg_i, b_rg_i, lru_lambda)

    qkv0 = a0 + 2 * D
    os_, lses = [], []
    for g, (window, d) in enumerate(DILATED_GROUPS):
        blk = window // d
        bias = _attn_bias(rel_bias[:, g * HEADS_PER_GROUP:(g + 1) * HEADS_PER_GROUP], window, d)
        if d == 1:
            o, lse = _attn_group(proj.reshape(B, S, N), bias, blk=blk, d=1, gw=gw,
                                 q_blk=(qkv0 + g * gw) // gw, k_blk=(qkv0 + d_att + g * gw) // gw,
                                 v_blk=(qkv0 + 2 * d_att + g * gw) // gw, row_blk=N // gw)
        else:
            cols = [proj[:, qkv0 + j * d_att + g * gw: qkv0 + j * d_att + (g + 1) * gw] for j in range(3)]
            qkv = jnp.concatenate(cols, axis=-1).reshape(B, S // d, d * 3 * gw)
            o, lse = _attn_group(qkv, bias, blk=blk, d=d, gw=gw, q_blk=0, k_blk=1, v_blk=2, row_blk=3)
        os_.append(o.reshape(T, gw))
        lses.append(lse.reshape(T, gw))

    return _merge(y_rnn.reshape(T, d_rnn), os_, lses, proj, a0 // (2 * D), x2,
                  w_proj_rnn.astype(BF16), w_proj_att.astype(BF16), w_out.astype(BF16),
                  ln1_g.reshape(1, D), ln1_b.reshape(1, D), w_router.T, alpha=alpha)


def _moe_ln2(x1, x1p, logits_t, router_bias, w_exp_gate, w_exp_up, w_exp_down,
             w_sh_gate, w_sh_up, w_sh_down, ln2_g, ln2_b, alpha):
    T, D = x1.shape
    E = logits_t.shape[0]
    R = EXPERT_ROWS
    n_blk = T * TOP_K // R + E
    n_rows = n_blk * R

    e_idx, gate, rank, counts = _route(logits_t, router_bias)
    counts = counts.reshape(E).astype(I32)
    padded = (counts + R - 1) // R * R
    pad_end = jnp.cumsum(padded)
    seg_start = pad_end - padded
    n_used = (pad_end[-1:] // R).astype(I32)
    blk_e = jnp.minimum(jnp.sum(jnp.arange(n_blk, dtype=I32)[:, None] >= (pad_end // R)[None, :], axis=1),
                        E - 1).astype(I32)
    dest = _dest(e_idx, rank, seg_start)
    tl = ROW_TILE
    dest_t = dest.reshape(TOP_K, T // tl, tl).transpose(1, 0, 2).reshape(T // tl, TOP_K * tl)

    xs = _dispatch(x1p, dest_t, seg_start + counts, padded - counts, n_used, n_rows, tl=tl)
    ys = _experts(xs, blk_e, n_used, w_exp_gate, w_exp_up, w_exp_down)
    wgu = jnp.concatenate([w_sh_gate, w_sh_up], axis=-1).astype(BF16)
    return _combine(ys, dest_t, gate.T, x1, wgu, w_sh_down.astype(BF16),
                    ln2_g.reshape(1, D), ln2_b.reshape(1, D), alpha=alpha, tc=tl)


def kernel(x, w_in, b_in, conv_w, conv_b, w_rg_a, b_rg_a, w_rg_i, b_rg_i, lru_lambda,
           w_proj_rnn, w_proj_att, rel_bias, w_out, ln1_g, ln1_b, w_router, router_bias,
           w_exp_gate, w_exp_up, w_exp_down, w_sh_gate, w_sh_up, w_sh_down, ln2_g, ln2_b):
    B, S, D = x.shape
    depth = w_in.shape[0]
    alpha = (2 * depth) ** 0.25
    for i in range(depth):
        x1, x1p, logits_t = _mixer_ln1(
            x, w_in[i], b_in[i], conv_w[i], conv_b[i], w_rg_a[i], b_rg_a[i], w_rg_i[i], b_rg_i[i],
            lru_lambda[i], w_proj_rnn[i], w_proj_att[i], rel_bias, w_out[i], ln1_g[i], ln1_b[i],
            w_router[i], alpha)
        out = _moe_ln2(x1, x1p, logits_t, router_bias[i], w_exp_gate[i], w_exp_up[i], w_exp_down[i],
                       w_sh_gate[i], w_sh_up[i], w_sh_down[i], ln2_g[i], ln2_b[i], alpha)
        x = out.reshape(B, S, D)
    return x
```

```python
import functools

import numpy as np
import jax
import jax.numpy as jnp
from jax import lax
from jax.experimental import pallas as pl
from jax.experimental.pallas import tpu as pltpu

F32 = jnp.float32
BF16 = jnp.bfloat16
I32 = jnp.int32
U32 = jnp.uint32

RNN_HEADS = 16
LRU_C = 8.0
HEAD_DIM = 64
HEADS_PER_GROUP = 8
DILATED_GROUPS = ((128, 1), (512, 4), (2048, 16))
NUM_BUCKETS = 32
MAX_DISTANCE = 2048
TOP_K = 8
N_EXPERT_GROUPS = 8
TOPK_GROUPS = 4
ROUTED_SCALE = 2.5
LN_EPS = 1e-5

LANES = 128
SUBLANES = 8
VMEM_LIMIT_BYTES = 56 * 1024 * 1024

MASK_VALUE = -1e30
EXPERT_ROWS = 256
ROW_TILE = 256


def _cparams(*sem):
    return pltpu.CompilerParams(dimension_semantics=sem, vmem_limit_bytes=VMEM_LIMIT_BYTES)


def _sigmoid(v):
    return 0.5 * (jnp.tanh(0.5 * v) + 1.0)


def _in_proj_body(x_ref, w_ref, b_ref, main_ref, *rest, tn, dils):
    dil_refs, scr = rest[:-1], rest[-1]
    xb = x_ref[...].astype(BF16)
    tm = xb.shape[0]

    def chunk(j):
        sl = slice(j * tn, (j + 1) * tn)
        return jnp.dot(xb, w_ref[:, sl], preferred_element_type=F32) + b_ref[:, sl]

    n_main = main_ref.shape[1] // tn
    for j in range(n_main):
        main_ref[:, j * tn:(j + 1) * tn] = chunk(j).astype(main_ref.dtype)
    j = n_main
    for ref, d in zip(dil_refs, dils):
        for c in range(ref.shape[3] // tn):
            acc = chunk(j)
            j += 1
            for q in range(tn // LANES):
                scr[q] = acc[:, q * LANES:(q + 1) * LANES]
            for r in range(d):
                part = jnp.concatenate([scr[q, pl.ds(r, tm // d, stride=d), :] for q in range(tn // LANES)], axis=1)
                ref[0, r, :, c * tn:(c + 1) * tn] = part.astype(ref.dtype)


def _in_proj(x2, w, b, n_main, dils, *, tm=512, tn=512):
    T, D = x2.shape
    N = w.shape[1]
    wd = (N - n_main) // len(dils)
    out_specs = [pl.BlockSpec((tm, n_main), lambda i: (i, 0))]
    out_shape = [jax.ShapeDtypeStruct((T, n_main), BF16)]
    for d in dils:
        out_specs.append(pl.BlockSpec((1, d, tm // d, wd), lambda i: (i, 0, 0, 0)))
        out_shape.append(jax.ShapeDtypeStruct((T // tm, d, tm // d, wd), BF16))
    return pl.pallas_call(
        functools.partial(_in_proj_body, tn=tn, dils=dils),
        grid=(T // tm,),
        in_specs=[pl.BlockSpec((tm, D), lambda i: (i, 0)),
                  pl.BlockSpec((D, N), lambda i: (0, 0), pipeline_mode=pl.Buffered(1)),
                  pl.BlockSpec((1, N), lambda i: (0, 0))],
        out_specs=out_specs,
        out_shape=out_shape,
        scratch_shapes=[pltpu.VMEM((tn // LANES, tm, LANES), F32)],
        compiler_params=_cparams("parallel"),
        name="in_proj",
    )(x2, w, b)


def _rnn_body(xr_ref, gr_ref, cw_ref, cb_ref, wg_ref, bg_ref, lam_ref, y_ref,
              xext, a_s, b_s, hc, *, ts, cb):
    s = pl.program_id(2)

    @pl.when(s == 0)
    def _():
        xext[0:SUBLANES, :] = jnp.zeros((SUBLANES, cb), F32)
        hc[...] = jnp.zeros_like(hc)

    xr = xr_ref[0].astype(F32)
    xext[SUBLANES:SUBLANES + ts, :] = xr
    nw = cw_ref.shape[0]
    xc = cw_ref[nw - 1:nw, :] * xr + cb_ref[...]
    for j in range(nw - 1):
        off = SUBLANES - (nw - 1 - j)
        xc = xc + cw_ref[j:j + 1, :] * xext[off:off + ts, :]
    xext[0:SUBLANES, :] = xext[ts:ts + SUBLANES, :]

    gates = jnp.dot(xc.astype(BF16), wg_ref[0], preferred_element_type=F32) + bg_ref[...]
    r = _sigmoid(gates[:, :cb])
    ig = _sigmoid(gates[:, cb:])
    nl = -lam_ref[...]
    sp = jnp.maximum(nl, 0.0) + jnp.log1p(jnp.exp(-jnp.abs(nl)))
    log_a = (-LRU_C) * r * sp
    a = jnp.exp(log_a)
    u = jnp.sqrt(1.0 - a * a) * (ig * xc)

    row = lax.broadcasted_iota(I32, (ts, cb), 0) & (SUBLANES - 1)
    av, bv = a, u
    for sft in (1, 2, 4):
        a_sh = pltpu.roll(av, sft, 0)
        b_sh = pltpu.roll(bv, sft, 0)
        m = row >= sft
        bv = jnp.where(m, av * b_sh + bv, bv)
        av = jnp.where(m, av * a_sh, av)
    a_s[...] = av
    b_s[...] = bv

    def carry(g, h):
        i0 = pl.multiple_of(g * SUBLANES, SUBLANES)
        h8 = b_s[pl.ds(i0, SUBLANES), :] + a_s[pl.ds(i0, SUBLANES), :] * h
        b_s[pl.ds(i0, SUBLANES), :] = h8
        return h8[SUBLANES - 1:SUBLANES, :]

    hc[0:1, :] = lax.fori_loop(0, ts // SUBLANES, carry, hc[0:1, :], unroll=8)
    gr = gr_ref[0].astype(F32)
    y_ref[0] = (b_s[...] * jax.nn.gelu(gr)).astype(y_ref.dtype)


def _block_diag(w, per):
    H, d, _ = w.shape
    w4 = w.reshape(H // per, per, d, d)
    out = jnp.einsum('gpij,pq->gpiqj', w4, jnp.eye(per, dtype=w.dtype))
    return out.reshape(H // per, per * d, per * d)


def _rnn(proj3, conv_w, conv_b, w_rg_a, b_rg_a, w_rg_i, b_rg_i, lam, *, ts=512, cb=256):
    B, S, _ = proj3.shape
    d_rnn = conv_w.shape[-1]
    nc = d_rnn // cb
    per = cb // (d_rnn // RNN_HEADS)
    wg = jnp.concatenate([_block_diag(w_rg_a, per), _block_diag(w_rg_i, per)], axis=-1).astype(BF16)
    bg = jnp.concatenate([b_rg_a.reshape(nc, 1, cb), b_rg_i.reshape(nc, 1, cb)], axis=-1)
    return pl.pallas_call(
        functools.partial(_rnn_body, ts=ts, cb=cb),
        grid=(B, nc, S // ts),
        in_specs=[pl.BlockSpec((1, ts, cb), lambda b, c, s: (b, s, c)),
                  pl.BlockSpec((1, ts, cb), lambda b, c, s: (b, s, nc + c)),
                  pl.BlockSpec((conv_w.shape[0], cb), lambda b, c, s: (0, c)),
                  pl.BlockSpec((1, cb), lambda b, c, s: (0, c)),
                  pl.BlockSpec((1, cb, 2 * cb), lambda b, c, s: (c, 0, 0)),
                  pl.BlockSpec((None, 1, 2 * cb), lambda b, c, s: (c, 0, 0)),
                  pl.BlockSpec((1, cb), lambda b, c, s: (0, c))],
        out_specs=pl.BlockSpec((1, ts, cb), lambda b, c, s: (b, s, c)),
        out_shape=jax.ShapeDtypeStruct((B, S, d_rnn), BF16),
        scratch_shapes=[pltpu.VMEM((ts + SUBLANES, cb), F32), pltpu.VMEM((ts, cb), F32),
                        pltpu.VMEM((ts, cb), F32), pltpu.VMEM((SUBLANES, cb), F32)],
        compiler_params=_cparams("parallel", "parallel", "arbitrary"),
        name="rnn",
    )(proj3, proj3, conv_w, conv_b.reshape(1, d_rnn), wg, bg, lam.reshape(1, d_rnn))


def _t5_bucket(dist):
    max_exact = NUM_BUCKETS // 2
    d = np.maximum(dist, 1).astype(np.float64)
    large = max_exact + (np.log(d / max_exact) / np.log(MAX_DISTANCE / max_exact)
                         * (NUM_BUCKETS - max_exact)).astype(np.int64)
    large = np.minimum(large, NUM_BUCKETS - 1)
    return np.where(dist < max_exact, dist, large).astype(np.int32)


def _attn_bias(table, window, dilation):
    blk = window // dilation
    qi = np.arange(blk)[:, None]
    ki = np.arange(2 * blk)[None, :]
    rel = qi + blk - ki
    in_window = (rel >= 0) & (rel <= blk)
    bucket = _t5_bucket(np.clip(rel, 0, None) * dilation)
    onehot = (bucket[..., None] == np.arange(NUM_BUCKETS)).astype(np.float32)
    bias = jnp.einsum('qkn,nh->hqk', onehot, table.astype(F32), precision=lax.Precision.HIGHEST)
    first = in_window & (ki >= blk)
    return jnp.stack([jnp.where(first[None], bias, MASK_VALUE),
                      jnp.where(in_window[None], bias, MASK_VALUE)])


def _attn_heads(q, kp, kc, vp, vc, bias_ref):
    blk, gw = q.shape
    lo = lax.broadcasted_iota(I32, (blk, LANES), 1) < HEAD_DIM
    scale = HEAD_DIM ** -0.5
    o_parts, l_parts = [], []
    for p in range(gw // LANES):
        sl = slice(p * LANES, (p + 1) * LANES)
        q2 = q[:, sl] * scale
        k2 = jnp.concatenate([kp[:, sl], kc[:, sl]], axis=0)
        v2 = jnp.concatenate([vp[:, sl], vc[:, sl]], axis=0)
        outs, lses = [], []
        for hh in range(2):
            qm = jnp.where(lo if hh == 0 else jnp.logical_not(lo), q2, jnp.zeros_like(q2))
            sc = lax.dot_general(qm, k2, (((1,), (1,)), ((), ())), preferred_element_type=F32)
            sc = sc + bias_ref[0, 2 * p + hh]
            m = jnp.max(sc, axis=-1, keepdims=True)
            e = jnp.exp(sc - m)
            l = jnp.sum(e, axis=-1, keepdims=True)
            o = jnp.dot(e.astype(BF16), v2, preferred_element_type=F32)
            outs.append(o / l)
            lses.append(jnp.broadcast_to(m + jnp.log(l), (blk, LANES)))
        o_parts.append(jnp.where(lo, outs[0], outs[1]))
        l_parts.append(jnp.where(lo, lses[0], lses[1]))
    return jnp.concatenate(o_parts, axis=1), jnp.concatenate(l_parts, axis=1)


def _attn_body(q_ref, kp_ref, kc_ref, vp_ref, vc_ref, bias_ref, o_ref, lse_ref):
    o, lse = _attn_heads(q_ref[0], kp_ref[0], kc_ref[0], vp_ref[0], vc_ref[0], bias_ref)
    o_ref[0] = o.astype(o_ref.dtype)
    lse_ref[0] = lse


def _attn_dil_body(cur_ref, prev_ref, bias_ref, o_ref, lse_ref, oscr, lscr, *, d, blk, gw):
    nt = cur_ref.shape[0]
    nq = gw // LANES

    def rows(ref, r, c):
        parts = [ref[j, r, :, c * gw:(c + 1) * gw] for j in range(nt)]
        return parts[0] if nt == 1 else jnp.concatenate(parts, axis=0)

    def residue(r, carry):
        o, lse = _attn_heads(rows(cur_ref, r, 2), rows(prev_ref, r, 0), rows(cur_ref, r, 0),
                             rows(prev_ref, r, 1), rows(cur_ref, r, 1), bias_ref)
        for q in range(nq):
            oscr[q, pl.ds(r, blk, stride=d), :] = o[:, q * LANES:(q + 1) * LANES]
            lscr[q, pl.ds(r, blk, stride=d), :] = lse[:, q * LANES:(q + 1) * LANES]
        return carry

    lax.fori_loop(0, d, residue, 0)
    for q in range(nq):
        o_ref[0, :, q * LANES:(q + 1) * LANES] = oscr[q].astype(o_ref.dtype)
        lse_ref[0, :, q * LANES:(q + 1) * LANES] = lscr[q]


def _attn_dilated(qkv_t, bias, B, *, blk, d, gw):
    n_tiles, _, rows_t, _ = qkv_t.shape
    tm = rows_t * d
    nt = blk * d // tm
    nb = n_tiles // (B * nt)
    S = n_tiles * tm // B
    return pl.pallas_call(
        functools.partial(_attn_dil_body, d=d, blk=blk, gw=gw),
        grid=(B, nb),
        in_specs=[pl.BlockSpec((nt, d, rows_t, 3 * gw), lambda b, n: (b * nb + n, 0, 0, 0)),
                  pl.BlockSpec((nt, d, rows_t, 2 * gw), lambda b, n: (b * nb + jnp.maximum(n - 1, 0), 0, 0, 0)),
                  pl.BlockSpec((1,) + bias.shape[1:], lambda b, n: (jnp.minimum(n, 1), 0, 0, 0))],
        out_specs=[pl.BlockSpec((1, blk * d, gw), lambda b, n: (b, n, 0)),
                   pl.BlockSpec((1, blk * d, gw), lambda b, n: (b, n, 0))],
        out_shape=[jax.ShapeDtypeStruct((B, S, gw), BF16), jax.ShapeDtypeStruct((B, S, gw), F32)],
        scratch_shapes=[pltpu.VMEM((gw // LANES, blk * d, LANES), F32),
                        pltpu.VMEM((gw // LANES, blk * d, LANES), F32)],
        compiler_params=_cparams("parallel", "parallel"),
        name=f"attn_d{d}",
    )(qkv_t, qkv_t, bias)


def _attn_group(qkv, bias, *, blk, d, gw, q_blk, k_blk, v_blk, row_blk):
    B, L, _ = qkv.shape
    nb = L // blk

    def cur(col):
        return pl.BlockSpec((1, blk, gw), lambda b, r, n: (b, n, r * row_blk + col))

    def prev(col):
        return pl.BlockSpec((1, blk, gw), lambda b, r, n: (b, jnp.maximum(n - 1, 0), r * row_blk + col))

    return pl.pallas_call(
        _attn_body,
        grid=(B, d, nb),
        in_specs=[cur(q_blk), prev(k_blk), cur(k_blk), prev(v_blk), cur(v_blk),
                  pl.BlockSpec((1,) + bias.shape[1:], lambda b, r, n: (jnp.minimum(n, 1), 0, 0, 0))],
        out_specs=[pl.BlockSpec((1, blk, gw), lambda b, r, n: (b, n, r)),
                   pl.BlockSpec((1, blk, gw), lambda b, r, n: (b, n, r))],
        out_shape=[jax.ShapeDtypeStruct((B, L, d * gw), BF16),
                   jax.ShapeDtypeStruct((B, L, d * gw), F32)],
        compiler_params=_cparams("parallel", "parallel", "parallel"),
        name=f"attn_d{d}",
    )(qkv, qkv, qkv, qkv, qkv, bias)


def _merge_body(yr_ref, o1_ref, o2_ref, o3_ref, l1_ref, l2_ref, l3_ref, g_ref, x_ref,
                wr_ref, wa_ref, wo_ref, lg_ref, lb_ref, wrt_ref,
                x1_ref, x1p_ref, lt_ref, *, alpha):
    l1, l2, l3 = l1_ref[...], l2_ref[...], l3_ref[...]
    mx = jnp.maximum(jnp.maximum(l1, l2), l3)
    w1, w2, w3 = jnp.exp(l1 - mx), jnp.exp(l2 - mx), jnp.exp(l3 - mx)
    y_att = (w1 * o1_ref[...].astype(F32) + w2 * o2_ref[...].astype(F32)
             + w3 * o3_ref[...].astype(F32)) / (w1 + w2 + w3)
    pr = jnp.dot(yr_ref[...], wr_ref[...], preferred_element_type=F32)
    pa = jnp.dot(y_att.astype(BF16), wa_ref[...], preferred_element_type=F32)
    dm = pr.shape[1]
    g = g_ref[...].astype(F32)
    merged = _sigmoid(g[:, :dm]) * pr + _sigmoid(g[:, dm:]) * pa
    mix = jnp.dot(merged.astype(BF16), wo_ref[...], preferred_element_type=F32)
    z = alpha * x_ref[...] + mix
    mu = jnp.mean(z, axis=-1, keepdims=True)
    zc = z - mu
    var = jnp.mean(zc * zc, axis=-1, keepdims=True)
    x1 = zc * lax.rsqrt(var + LN_EPS) * lg_ref[...] + lb_ref[...]
    x1_ref[...] = x1
    x1p_ref[...] = _pack_rows(x1)
    lt_ref[...] = lax.dot_general(wrt_ref[...], x1, (((1,), (1,)), ((), ())),
                                  precision=lax.Precision.HIGHEST, preferred_element_type=F32)


def _merge(y_rnn, os_, lses, proj, gate_blk, x2, wr, wa, wo, ln_g, ln_b, w_router_t, *, alpha, tm=512):
    T, D = x2.shape
    da = os_[0].shape[1]
    E = w_router_t.shape[0]
    row = lambda w: pl.BlockSpec((tm, w), lambda i: (i, 0))
    full = lambda a: pl.BlockSpec(a.shape, lambda i: (0,) * a.ndim)
    return pl.pallas_call(
        functools.partial(_merge_body, alpha=alpha),
        grid=(T // tm,),
        in_specs=[row(D), row(da), row(da), row(da), row(da), row(da), row(da),
                  pl.BlockSpec((tm, 2 * D), lambda i: (i, gate_blk)), row(D),
                  full(wr), full(wa), full(wo), full(ln_g), full(ln_b), full(w_router_t)],
        out_specs=[row(D), row(D // 2), pl.BlockSpec((E, tm), lambda i: (0, i))],
        out_shape=[jax.ShapeDtypeStruct((T, D), F32), jax.ShapeDtypeStruct((T, D // 2), U32),
                   jax.ShapeDtypeStruct((E, T), F32)],
        compiler_params=_cparams("parallel"),
        name="merge",
    )(y_rnn, *os_, *lses, proj, x2, wr, wa, wo, ln_g, ln_b, w_router_t)


def _first_max(vals, idx, big):
    m = jnp.max(vals, axis=0, keepdims=True)
    i = jnp.min(jnp.where(vals == m, idx, big), axis=0, keepdims=True)
    return m, i


def _route_body(lt_ref, rb_ref, e_ref, g_ref, r_ref, cnt_ref, *, tl):
    E = lt_ref.shape[0]
    per = E // N_EXPERT_GROUPS
    neg = -jnp.inf

    @pl.when(pl.program_id(0) == 0)
    def _():
        cnt_ref[...] = jnp.zeros_like(cnt_ref)

    scores = jax.nn.sigmoid(lt_ref[...])
    sel = scores + rb_ref[...]
    rowi = lax.broadcasted_iota(I32, (E, tl), 0)

    gi = lax.broadcasted_iota(I32, (N_EXPERT_GROUPS, tl), 0)
    gsc = jnp.zeros((N_EXPERT_GROUPS, tl), F32)
    for g in range(N_EXPERT_GROUPS):
        blk = sel[g * per:(g + 1) * per]
        ri = lax.broadcasted_iota(I32, (per, tl), 0) + g * per
        m1, i1 = _first_max(blk, ri, E)
        m2 = jnp.max(jnp.where(ri == i1, neg, blk), axis=0, keepdims=True)
        gsc = jnp.where(gi == g, m1 + m2, gsc)
    keep = jnp.zeros((N_EXPERT_GROUPS, tl), F32)
    for _ in range(TOPK_GROUPS):
        _, ig = _first_max(gsc, gi, N_EXPERT_GROUPS)
        hit = gi == ig
        keep = jnp.where(hit, 1.0, keep)
        gsc = jnp.where(hit, neg, gsc)
    cur = jnp.concatenate(
        [jnp.where(keep[g:g + 1] > 0.5, sel[g * per:(g + 1) * per], neg) for g in range(N_EXPERT_GROUPS)],
        axis=0)

    ti = lax.broadcasted_iota(I32, (tl, tl), 0)
    tj = lax.broadcasted_iota(I32, (tl, tl), 1)
    earlier = jnp.where(ti < tj, 1.0, 0.0).astype(BF16)

    es, gv = [], []
    onehot = jnp.zeros((E, tl), F32)
    for k in range(TOP_K):
        _, ie = _first_max(cur, rowi, E)
        hit = rowi == ie
        es.append(ie)
        gv.append(jnp.sum(jnp.where(hit, scores, 0.0), axis=0, keepdims=True))
        onehot = jnp.where(hit, 1.0, onehot)
        cur = jnp.where(hit, neg, cur)
    gsum = gv[0]
    for k in range(1, TOP_K):
        gsum = gsum + gv[k]
    ranks = jnp.dot(onehot.astype(BF16), earlier, preferred_element_type=F32) + cnt_ref[...]
    for k in range(TOP_K):
        e_ref[k:k + 1, :] = es[k]
        g_ref[k:k + 1, :] = gv[k] / gsum * ROUTED_SCALE
        r_ref[k:k + 1, :] = jnp.sum(jnp.where(rowi == es[k], ranks, 0.0), axis=0, keepdims=True).astype(I32)
    cnt_ref[...] += jnp.sum(onehot, axis=1, keepdims=True)


def _route(logits_t, router_bias, *, tl=256):
    E, T = logits_t.shape
    kt = pl.BlockSpec((TOP_K, tl), lambda i: (0, i))
    return pl.pallas_call(
        functools.partial(_route_body, tl=tl),
        grid=(T // tl,),
        in_specs=[pl.BlockSpec((E, tl), lambda i: (0, i)), pl.BlockSpec((E, 1), lambda i: (0, 0))],
        out_specs=[kt, kt, kt, pl.BlockSpec((E, 1), lambda i: (0, 0))],
        out_shape=[jax.ShapeDtypeStruct((TOP_K, T), I32), jax.ShapeDtypeStruct((TOP_K, T), F32),
                   jax.ShapeDtypeStruct((TOP_K, T), I32), jax.ShapeDtypeStruct((E, 1), F32)],
        compiler_params=_cparams("arbitrary"),
        name="route",
    )(logits_t, router_bias.reshape(E, 1))


def _dest_body(e_ref, r_ref, ss_ref, d_ref):
    E = ss_ref.shape[0]
    tl = e_ref.shape[1]
    rowi = lax.broadcasted_iota(I32, (E, tl), 0)
    e = e_ref[...]
    rows = [jnp.sum(jnp.where(rowi == e[k:k + 1], ss_ref[...], 0), axis=0, keepdims=True)
            for k in range(e.shape[0])]
    d_ref[...] = jnp.concatenate(rows, axis=0) + r_ref[...]


def _dest(e_idx, rank, seg_start, *, tl=512):
    K, T = e_idx.shape
    E = seg_start.shape[0]
    kt = pl.BlockSpec((K, tl), lambda i: (0, i))
    return pl.pallas_call(
        _dest_body, grid=(T // tl,),
        in_specs=[kt, kt, pl.BlockSpec((E, 1), lambda i: (0, 0))],
        out_specs=kt, out_shape=jax.ShapeDtypeStruct((K, T), I32),
        compiler_params=_cparams("parallel"), name="dest",
    )(e_idx, rank, seg_start.reshape(E, 1))


def _pack_rows(v):
    w = v.shape[1] // 2
    lo = pltpu.bitcast(v[:, :w].astype(BF16).astype(F32), U32) >> 16
    hi = pltpu.bitcast(v[:, w:].astype(BF16).astype(F32), U32) & jnp.uint32(0xFFFF0000)
    return lo | hi


def _unpack_rows(p):
    lo = pltpu.bitcast(p << 16, F32)
    hi = pltpu.bitcast(p & jnp.uint32(0xFFFF0000), F32)
    return lo, hi


def _dispatch_body(ps_ref, pc_ref, nu_ref, dest_hbm, x_ref, xs_hbm, dsm, zrow, sem_i, sem_r, sem_z, sem_t,
                   *, tl, K):
    i = pl.program_id(0)
    n = pl.num_programs(0)
    R = zrow.shape[0]

    def pad_copy(row):
        return pltpu.make_async_copy(zrow.at[pl.ds(0, 1), :], xs_hbm.at[pl.ds(row, 1), :], sem_z)

    idx_cp = pltpu.make_async_copy(dest_hbm.at[i], dsm, sem_i)
    idx_cp.start()
    zrow[...] = jnp.zeros_like(zrow)
    idx_cp.wait()

    n_blk = xs_hbm.shape[0] // R
    tail_per_step = -(-n_blk // n)

    def tail(j, c, wait):
        blk = i * tail_per_step + j

        @pl.when(jnp.logical_and(blk >= nu_ref[0], blk < n_blk))
        def _():
            cp = pltpu.make_async_copy(zrow, xs_hbm.at[pl.ds(pl.multiple_of(blk * R, R), R), :], sem_t)
            cp.wait() if wait else cp.start()

        return c

    lax.fori_loop(0, tail_per_step, functools.partial(tail, wait=False), 0)

    def issue(t, c):
        src = x_ref.at[pl.ds(t, 1), :]
        for k in range(K):
            pltpu.make_async_copy(src, xs_hbm.at[pl.ds(dsm[k * tl + t], 1), :], sem_r).start(priority=k % 2)
        return c

    lax.fori_loop(0, tl, issue, 0, unroll=2)

    E = ps_ref.shape[0]
    per_step = -(-E // n)

    def pads(j, c, wait):
        e = jnp.minimum(i * per_step + j, E - 1)
        cnt = jnp.where(i * per_step + j < E, pc_ref[e], 0)

        def one(r, c2):
            cp = pad_copy(ps_ref[e] + r)
            cp.wait() if wait else cp.start()
            return c2

        return lax.fori_loop(0, cnt, one, c)

    lax.fori_loop(0, per_step, functools.partial(pads, wait=False), 0)
    for k in range(K):
        pltpu.make_async_copy(x_ref, xs_hbm.at[pl.ds(0, tl), :], sem_r).wait()
    lax.fori_loop(0, per_step, functools.partial(pads, wait=True), 0)
    lax.fori_loop(0, tail_per_step, functools.partial(tail, wait=True), 0)


def _dispatch(xp, dest_t, pad_start, pad_cnt, n_used, n_rows, *, tl):
    T, W = xp.shape
    K = dest_t.shape[1] // tl
    any_spec = pl.BlockSpec(memory_space=pl.ANY)
    return pl.pallas_call(
        functools.partial(_dispatch_body, tl=tl, K=K),
        grid_spec=pltpu.PrefetchScalarGridSpec(
            num_scalar_prefetch=3, grid=(T // tl,),
            in_specs=[any_spec, pl.BlockSpec((tl, W), lambda i, ps, pc, nu: (i, 0))],
            out_specs=any_spec,
            scratch_shapes=[pltpu.SMEM((K * tl,), I32), pltpu.VMEM((EXPERT_ROWS, W), U32),
                            pltpu.SemaphoreType.DMA, pltpu.SemaphoreType.DMA, pltpu.SemaphoreType.DMA,
                            pltpu.SemaphoreType.DMA]),
        out_shape=jax.ShapeDtypeStruct((n_rows, W), U32),
        compiler_params=_cparams("arbitrary"),
        name="dispatch",
    )(pad_start, pad_cnt, n_used, dest_t, xp)


def _experts_body(be_ref, nu_ref, xs_ref, wg_ref, wu_ref, wd_ref, ys_ref, wgu_s, wd_s):
    j = pl.program_id(0)
    de = wg_ref.shape[2]
    changed = jnp.logical_or(j == 0, be_ref[j] != be_ref[jnp.maximum(j - 1, 0)])

    @pl.when(changed)
    def _():
        wgu_s[:, :de] = wg_ref[0].astype(BF16)
        wgu_s[:, de:] = wu_ref[0].astype(BF16)
        wd_s[...] = wd_ref[0].astype(BF16)

    @pl.when(j < nu_ref[0])
    def _():
        lo, hi = _unpack_rows(xs_ref[...])
        xb = jnp.concatenate([lo.astype(BF16), hi.astype(BF16)], axis=1)
        h = jnp.dot(xb, wgu_s[...], preferred_element_type=F32)
        hg = h[:, :de]
        act = (hg * _sigmoid(hg) * h[:, de:]).astype(BF16)
        ys_ref[...] = _pack_rows(jnp.dot(act, wd_s[...], preferred_element_type=F32))

    @pl.when(j >= nu_ref[0])
    def _():
        ys_ref[...] = jnp.zeros_like(ys_ref)


def _experts(xs, blk_e, n_used, wg, wu, wd):
    n_rows, W = xs.shape
    R = EXPERT_ROWS
    D, de = wg.shape[1], wg.shape[2]
    xmap = lambda j, be, nu: (jnp.minimum(j, nu[0] - 1), 0)
    wmap = lambda j, be, nu: (be[j], 0, 0)
    return pl.pallas_call(
        _experts_body,
        grid_spec=pltpu.PrefetchScalarGridSpec(
            num_scalar_prefetch=2, grid=(n_rows // R,),
            in_specs=[pl.BlockSpec((R, W), xmap),
                      pl.BlockSpec((1, D, de), wmap), pl.BlockSpec((1, D, de), wmap),
                      pl.BlockSpec((1, de, D), wmap)],
            out_specs=pl.BlockSpec((R, W), lambda j, be, nu: (j, 0)),
            scratch_shapes=[pltpu.VMEM((D, 2 * de), BF16), pltpu.VMEM((de, D), BF16)]),
        out_shape=jax.ShapeDtypeStruct((n_rows, W), U32),
        compiler_params=_cparams("arbitrary"),
        name="experts",
    )(blk_e, n_used, xs, wg, wu, wd)


def _combine_body(dest_hbm, ys_hbm, gate_ref, x1_ref, wgu_ref, wd_ref, lg_ref, lb_ref, o_ref,
                  dsm, buf, sem_i, sem_r, *, tc, K, alpha):
    i = pl.program_id(0)
    idx_cp = pltpu.make_async_copy(dest_hbm.at[i], dsm, sem_i)
    idx_cp.start()
    idx_cp.wait()

    def issue(t, c):
        for k in range(K):
            r = k * tc + t
            pltpu.make_async_copy(ys_hbm.at[pl.ds(dsm[r], 1), :], buf.at[pl.ds(r, 1), :],
                                  sem_r).start(priority=k % 2)
        return c

    lax.fori_loop(0, tc, issue, 0, unroll=2)

    x1 = x1_ref[...]
    ds_ = wd_ref.shape[0]
    h = jnp.dot(x1.astype(BF16), wgu_ref[...], preferred_element_type=F32)
    hg = h[:, :ds_]
    act = (hg * _sigmoid(hg) * h[:, ds_:]).astype(BF16)
    shared = jnp.dot(act, wd_ref[...], preferred_element_type=F32)

    pltpu.make_async_copy(ys_hbm.at[pl.ds(0, K * tc), :], buf, sem_r).wait()
    g = gate_ref[...]
    lo_acc = hi_acc = None
    for k in range(K):
        lo, hi = _unpack_rows(buf[k * tc:(k + 1) * tc, :])
        gk = g[:, k:k + 1]
        lo_acc = gk * lo if k == 0 else lo_acc + gk * lo
        hi_acc = gk * hi if k == 0 else hi_acc + gk * hi
    routed = jnp.concatenate([lo_acc, hi_acc], axis=1)

    z = alpha * x1 + (routed + shared)
    mu = jnp.mean(z, axis=-1, keepdims=True)
    zc = z - mu
    var = jnp.mean(zc * zc, axis=-1, keepdims=True)
    o_ref[...] = zc * lax.rsqrt(var + LN_EPS) * lg_ref[...] + lb_ref[...]


def _combine(ys, dest_t, gate_tk, x1, wgu, wd, ln_g, ln_b, *, alpha, tc):
    T, D = x1.shape
    K = gate_tk.shape[1]
    W = ys.shape[1]
    any_spec = pl.BlockSpec(memory_space=pl.ANY)
    full = lambda a: pl.BlockSpec(a.shape, lambda i: (0,) * a.ndim)
    return pl.pallas_call(
        functools.partial(_combine_body, tc=tc, K=K, alpha=alpha),
        grid=(T // tc,),
        in_specs=[any_spec, any_spec, pl.BlockSpec((tc, K), lambda i: (i, 0)),
                  pl.BlockSpec((tc, D), lambda i: (i, 0)), full(wgu), full(wd), full(ln_g), full(ln_b)],
        out_specs=pl.BlockSpec((tc, D), lambda i: (i, 0)),
        out_shape=jax.ShapeDtypeStruct((T, D), F32),
        scratch_shapes=[pltpu.SMEM((K * tc,), I32), pltpu.VMEM((K * tc, W), U32),
                        pltpu.SemaphoreType.DMA, pltpu.SemaphoreType.DMA],
        compiler_params=_cparams("arbitrary"),
        name="combine",
    )(dest_t, ys, gate_tk, x1, wgu, wd, ln_g, ln_b)


def _mixer_ln1(x, w_in, b_in, conv_w, conv_b, w_rg_a, b_rg_a, w_rg_i, b_rg_i, lru_lambda,
               w_proj_rnn, w_proj_att, rel_bias, w_out, ln1_g, ln1_b, w_router, alpha):
    B, S, D = x.shape
    T = B * S
    d_rnn = conv_w.shape[-1]
    gw = HEADS_PER_GROUP * HEAD_DIM
    d_att = gw * len(DILATED_GROUPS)
    a0 = 2 * d_rnn
    a1 = a0 + 3 * d_att
    head = lambda j, g: slice(a0 + j * d_att + g * gw, a0 + j * d_att + (g + 1) * gw)
    plain = [g for g, (_, d) in enumerate(DILATED_GROUPS) if d == 1]
    dilated = [g for g, (_, d) in enumerate(DILATED_GROUPS) if d > 1]
    order = ([slice(0, a0), slice(a1, None)] + [head(j, g) for g in plain for j in range(3)]
             + [head(j, g) for g in dilated for j in (1, 2, 0)])
    perm = lambda w: jnp.concatenate([w[..., s] for s in order], axis=-1)
    w_p = perm(w_in).astype(BF16)
    b_p = perm(b_in).reshape(1, -1)
    qkv0 = a0 + 2 * D
    n_main = qkv0 + 3 * gw * len(plain)
    x2 = x.reshape(T, D)

    proj, *qkv_dil = _in_proj(x2, w_p, b_p, n_main, tuple(DILATED_GROUPS[g][1] for g in dilated))
    proj3 = proj.reshape(B, S, n_main)
    y_rnn = _rnn(proj3, conv_w, conv_b, w_rg_a, b_rg_a, w_rg_i, b_rg_i, lru_lambda)

    os_, lses = [], []
    for g, (window, d) in enumerate(DILATED_GROUPS):
        blk = window // d
        bias = _attn_bias(rel_bias[:, g * HEADS_PER_GROUP:(g + 1) * HEADS_PER_GROUP], window, d)
        if d == 1:
            c0 = (qkv0 + 3 * gw * plain.index(g)) // gw
            o, lse = _attn_group(proj3, bias, blk=blk, d=1, gw=gw, q_blk=c0, k_blk=c0 + 1, v_blk=c0 + 2,
                                 row_blk=n_main // gw)
        else:
            o, lse = _attn_dilated(qkv_dil[dilated.index(g)], bias, B, blk=blk, d=d, gw=gw)
        os_.append(o.reshape(T, gw))
        lses.append(lse.reshape(T, gw))

    return _merge(y_rnn.reshape(T, d_rnn), os_, lses, proj, a0 // (2 * D), x2,
                  w_proj_rnn.astype(BF16), w_proj_att.astype(BF16), w_out.astype(BF16),
                  ln1_g.reshape(1, D), ln1_b.reshape(1, D), w_router.T, alpha=alpha)


def _moe_ln2(x1, x1p, logits_t, router_bias, w_exp_gate, w_exp_up, w_exp_down,
             w_sh_gate, w_sh_up, w_sh_down, ln2_g, ln2_b, alpha):
    T, D = x1.shape
    E = logits_t.shape[0]
    R = EXPERT_ROWS
    n_blk = T * TOP_K // R + E
    n_rows = n_blk * R

    e_idx, gate, rank, counts = _route(logits_t, router_bias)
    counts = counts.reshape(E).astype(I32)
    padded = (counts + R - 1) // R * R
    pad_end = jnp.cumsum(padded)
    seg_start = pad_end - padded
    n_used = (pad_end[-1:] // R).astype(I32)
    blk_e = jnp.minimum(jnp.sum(jnp.arange(n_blk, dtype=I32)[:, None] >= (pad_end // R)[None, :], axis=1),
                        E - 1).astype(I32)
    dest = _dest(e_idx, rank, seg_start)
    tl = ROW_TILE
    dest_t = dest.reshape(TOP_K, T // tl, tl).transpose(1, 0, 2).reshape(T // tl, TOP_K * tl)

    xs = _dispatch(x1p, dest_t, seg_start + counts, padded - counts, n_used, n_rows, tl=tl)
    ys = _experts(xs, blk_e, n_used, w_exp_gate, w_exp_up, w_exp_down)
    wgu = jnp.concatenate([w_sh_gate, w_sh_up], axis=-1).astype(BF16)
    return _combine(ys, dest_t, gate.T, x1, wgu, w_sh_down.astype(BF16),
                    ln2_g.reshape(1, D), ln2_b.reshape(1, D), alpha=alpha, tc=tl)


def kernel(x, w_in, b_in, conv_w, conv_b, w_rg_a, b_rg_a, w_rg_i, b_rg_i, lru_lambda,
           w_proj_rnn, w_proj_att, rel_bias, w_out, ln1_g, ln1_b, w_router, router_bias,
           w_exp_gate, w_exp_up, w_exp_down, w_sh_gate, w_sh_up, w_sh_down, ln2_g, ln2_b):
    B, S, D = x.shape
    depth = w_in.shape[0]
    alpha = (2 * depth) ** 0.25
    for i in range(depth):
        x1, x1p, logits_t = _mixer_ln1(
            x, w_in[i], b_in[i], conv_w[i], conv_b[i], w_rg_a[i], b_rg_a[i], w_rg_i[i], b_rg_i[i],
            lru_lambda[i], w_proj_rnn[i], w_proj_att[i], rel_bias, w_out[i], ln1_g[i], ln1_b[i],
            w_router[i], alpha)
        out = _moe_ln2(x1, x1p, logits_t, router_bias[i], w_exp_gate[i], w_exp_up[i], w_exp_down[i],
                       w_sh_gate[i], w_sh_up[i], w_sh_down[i], ln2_g[i], ln2_b[i], alpha)
        x = out.reshape(B, S, D)
    return x
```

```python
import functools

import numpy as np
import jax
import jax.numpy as jnp
from jax import lax
from jax.experimental import pallas as pl
from jax.experimental.pallas import tpu as pltpu

F32 = jnp.float32
BF16 = jnp.bfloat16
I32 = jnp.int32
U32 = jnp.uint32

RNN_HEADS = 16
LRU_C = 8.0
HEAD_DIM = 64
HEADS_PER_GROUP = 8
DILATED_GROUPS = ((128, 1), (512, 4), (2048, 16))
NUM_BUCKETS = 32
MAX_DISTANCE = 2048
TOP_K = 8
N_EXPERT_GROUPS = 8
TOPK_GROUPS = 4
ROUTED_SCALE = 2.5
LN_EPS = 1e-5

LANES = 128
SUBLANES = 8
VMEM_LIMIT_BYTES = 56 * 1024 * 1024

MASK_VALUE = -1e30
EXPERT_ROWS = 256
ROW_TILE = 256
COMBINE_WINDOW = 24
COMBINE_EXPERT_CHUNK = 64


def _cparams(*sem):
    return pltpu.CompilerParams(dimension_semantics=sem, vmem_limit_bytes=VMEM_LIMIT_BYTES)


def _sigmoid(v):
    return 0.5 * (jnp.tanh(0.5 * v) + 1.0)


def _in_proj_body(x_ref, w_ref, b_ref, main_ref, *rest, tn, dils):
    dil_refs, scr = rest[:-1], rest[-1]
    xb = x_ref[...].astype(BF16)
    tm = xb.shape[0]

    def chunk(j):
        sl = slice(j * tn, (j + 1) * tn)
        return jnp.dot(xb, w_ref[:, sl], preferred_element_type=F32) + b_ref[:, sl]

    n_main = main_ref.shape[1] // tn
    for j in range(n_main):
        main_ref[:, j * tn:(j + 1) * tn] = chunk(j).astype(main_ref.dtype)
    j = n_main
    for ref, d in zip(dil_refs, dils):
        for c in range(ref.shape[3] // tn):
            acc = chunk(j)
            j += 1
            for q in range(tn // LANES):
                scr[q] = acc[:, q * LANES:(q + 1) * LANES]
            for r in range(d):
                part = jnp.concatenate([scr[q, pl.ds(r, tm // d, stride=d), :] for q in range(tn // LANES)], axis=1)
                ref[0, r, :, c * tn:(c + 1) * tn] = part.astype(ref.dtype)


def _in_proj(x2, w, b, n_main, dils, *, tm=512, tn=512):
    T, D = x2.shape
    N = w.shape[1]
    wd = (N - n_main) // len(dils)
    out_specs = [pl.BlockSpec((tm, n_main), lambda i: (i, 0))]
    out_shape = [jax.ShapeDtypeStruct((T, n_main), BF16)]
    for d in dils:
        out_specs.append(pl.BlockSpec((1, d, tm // d, wd), lambda i: (i, 0, 0, 0)))
        out_shape.append(jax.ShapeDtypeStruct((T // tm, d, tm // d, wd), BF16))
    return pl.pallas_call(
        functools.partial(_in_proj_body, tn=tn, dils=dils),
        grid=(T // tm,),
        in_specs=[pl.BlockSpec((tm, D), lambda i: (i, 0)),
                  pl.BlockSpec((D, N), lambda i: (0, 0), pipeline_mode=pl.Buffered(1)),
                  pl.BlockSpec((1, N), lambda i: (0, 0))],
        out_specs=out_specs,
        out_shape=out_shape,
        scratch_shapes=[pltpu.VMEM((tn // LANES, tm, LANES), F32)],
        compiler_params=_cparams("parallel"),
        name="in_proj",
    )(x2, w, b)


def _rnn_body(xr_ref, gr_ref, cw_ref, cb_ref, wg_ref, bg_ref, lam_ref, y_ref,
              xext, a_s, b_s, hc, *, ts, cb):
    s = pl.program_id(2)

    @pl.when(s == 0)
    def _():
        xext[0:SUBLANES, :] = jnp.zeros((SUBLANES, cb), F32)
        hc[...] = jnp.zeros_like(hc)

    xr = xr_ref[0].astype(F32)
    xext[SUBLANES:SUBLANES + ts, :] = xr
    nw = cw_ref.shape[0]
    xc = cw_ref[nw - 1:nw, :] * xr + cb_ref[...]
    for j in range(nw - 1):
        off = SUBLANES - (nw - 1 - j)
        xc = xc + cw_ref[j:j + 1, :] * xext[off:off + ts, :]
    xext[0:SUBLANES, :] = xext[ts:ts + SUBLANES, :]

    gates = jnp.dot(xc.astype(BF16), wg_ref[0], preferred_element_type=F32) + bg_ref[...]
    r = _sigmoid(gates[:, :cb])
    ig = _sigmoid(gates[:, cb:])
    nl = -lam_ref[...]
    sp = jnp.maximum(nl, 0.0) + jnp.log1p(jnp.exp(-jnp.abs(nl)))
    log_a = (-LRU_C) * r * sp
    a = jnp.exp(log_a)
    u = jnp.sqrt(1.0 - a * a) * (ig * xc)

    row = lax.broadcasted_iota(I32, (ts, cb), 0) & (SUBLANES - 1)
    av, bv = a, u
    for sft in (1, 2, 4):
        a_sh = pltpu.roll(av, sft, 0)
        b_sh = pltpu.roll(bv, sft, 0)
        m = row >= sft
        bv = jnp.where(m, av * b_sh + bv, bv)
        av = jnp.where(m, av * a_sh, av)
    a_s[...] = av
    b_s[...] = bv

    def carry(g, h):
        i0 = pl.multiple_of(g * SUBLANES, SUBLANES)
        h8 = b_s[pl.ds(i0, SUBLANES), :] + a_s[pl.ds(i0, SUBLANES), :] * h
        b_s[pl.ds(i0, SUBLANES), :] = h8
        return h8[SUBLANES - 1:SUBLANES, :]

    hc[0:1, :] = lax.fori_loop(0, ts // SUBLANES, carry, hc[0:1, :], unroll=8)
    gr = gr_ref[0].astype(F32)
    y_ref[0] = (b_s[...] * jax.nn.gelu(gr)).astype(y_ref.dtype)


def _block_diag(w, per):
    H, d, _ = w.shape
    w4 = w.reshape(H // per, per, d, d)
    out = jnp.einsum('gpij,pq->gpiqj', w4, jnp.eye(per, dtype=w.dtype))
    return out.reshape(H // per, per * d, per * d)


def _rnn(proj3, conv_w, conv_b, w_rg_a, b_rg_a, w_rg_i, b_rg_i, lam, *, ts=512, cb=256):
    B, S, _ = proj3.shape
    d_rnn = conv_w.shape[-1]
    nc = d_rnn // cb
    per = cb // (d_rnn // RNN_HEADS)
    wg = jnp.concatenate([_block_diag(w_rg_a, per), _block_diag(w_rg_i, per)], axis=-1).astype(BF16)
    bg = jnp.concatenate([b_rg_a.reshape(nc, 1, cb), b_rg_i.reshape(nc, 1, cb)], axis=-1)
    return pl.pallas_call(
        functools.partial(_rnn_body, ts=ts, cb=cb),
        grid=(B, nc, S // ts),
        in_specs=[pl.BlockSpec((1, ts, cb), lambda b, c, s: (b, s, c)),
                  pl.BlockSpec((1, ts, cb), lambda b, c, s: (b, s, nc + c)),
                  pl.BlockSpec((conv_w.shape[0], cb), lambda b, c, s: (0, c)),
                  pl.BlockSpec((1, cb), lambda b, c, s: (0, c)),
                  pl.BlockSpec((1, cb, 2 * cb), lambda b, c, s: (c, 0, 0)),
                  pl.BlockSpec((None, 1, 2 * cb), lambda b, c, s: (c, 0, 0)),
                  pl.BlockSpec((1, cb), lambda b, c, s: (0, c))],
        out_specs=pl.BlockSpec((1, ts, cb), lambda b, c, s: (b, s, c)),
        out_shape=jax.ShapeDtypeStruct((B, S, d_rnn), BF16),
        scratch_shapes=[pltpu.VMEM((ts + SUBLANES, cb), F32), pltpu.VMEM((ts, cb), F32),
                        pltpu.VMEM((ts, cb), F32), pltpu.VMEM((SUBLANES, cb), F32)],
        compiler_params=_cparams("parallel", "parallel", "arbitrary"),
        name="rnn",
    )(proj3, proj3, conv_w, conv_b.reshape(1, d_rnn), wg, bg, lam.reshape(1, d_rnn))


def _t5_bucket(dist):
    max_exact = NUM_BUCKETS // 2
    d = np.maximum(dist, 1).astype(np.float64)
    large = max_exact + (np.log(d / max_exact) / np.log(MAX_DISTANCE / max_exact)
                         * (NUM_BUCKETS - max_exact)).astype(np.int64)
    large = np.minimum(large, NUM_BUCKETS - 1)
    return np.where(dist < max_exact, dist, large).astype(np.int32)


def _attn_bias(table, window, dilation):
    blk = window // dilation
    qi = np.arange(blk)[:, None]
    ki = np.arange(2 * blk)[None, :]
    rel = qi + blk - ki
    in_window = (rel >= 0) & (rel <= blk)
    bucket = _t5_bucket(np.clip(rel, 0, None) * dilation)
    onehot = (bucket[..., None] == np.arange(NUM_BUCKETS)).astype(np.float32)
    bias = jnp.einsum('qkn,nh->hqk', onehot, table.astype(F32), precision=lax.Precision.HIGHEST)
    first = in_window & (ki >= blk)
    return jnp.stack([jnp.where(first[None], bias, MASK_VALUE),
                      jnp.where(in_window[None], bias, MASK_VALUE)])


def _attn_heads(q, kp, kc, vp, vc, bias_ref):
    blk, gw = q.shape
    lo = lax.broadcasted_iota(I32, (blk, LANES), 1) < HEAD_DIM
    scale = HEAD_DIM ** -0.5
    o_parts, l_parts = [], []
    for p in range(gw // LANES):
        sl = slice(p * LANES, (p + 1) * LANES)
        q2 = q[:, sl] * scale
        k2 = jnp.concatenate([kp[:, sl], kc[:, sl]], axis=0)
        v2 = jnp.concatenate([vp[:, sl], vc[:, sl]], axis=0)
        outs, lses = [], []
        for hh in range(2):
            qm = jnp.where(lo if hh == 0 else jnp.logical_not(lo), q2, jnp.zeros_like(q2))
            sc = lax.dot_general(qm, k2, (((1,), (1,)), ((), ())), preferred_element_type=F32)
            sc = sc + bias_ref[0, 2 * p + hh]
            m = jnp.max(sc, axis=-1, keepdims=True)
            e = jnp.exp(sc - m)
            l = jnp.sum(e, axis=-1, keepdims=True)
            o = jnp.dot(e.astype(BF16), v2, preferred_element_type=F32)
            outs.append(o / l)
            lses.append(jnp.broadcast_to(m + jnp.log(l), (blk, LANES)))
        o_parts.append(jnp.where(lo, outs[0], outs[1]))
        l_parts.append(jnp.where(lo, lses[0], lses[1]))
    return jnp.concatenate(o_parts, axis=1), jnp.concatenate(l_parts, axis=1)


def _attn_body(q_ref, kp_ref, kc_ref, vp_ref, vc_ref, bias_ref, o_ref, lse_ref):
    o, lse = _attn_heads(q_ref[0], kp_ref[0], kc_ref[0], vp_ref[0], vc_ref[0], bias_ref)
    o_ref[0] = o.astype(o_ref.dtype)
    lse_ref[0] = lse


def _attn_dil_body(cur_ref, prev_ref, bias_ref, o_ref, lse_ref, oscr, lscr, *, d, blk, gw):
    nt = cur_ref.shape[0]
    nq = gw // LANES

    def rows(ref, r, c):
        parts = [ref[j, r, :, c * gw:(c + 1) * gw] for j in range(nt)]
        return parts[0] if nt == 1 else jnp.concatenate(parts, axis=0)

    def residue(r, carry):
        o, lse = _attn_heads(rows(cur_ref, r, 2), rows(prev_ref, r, 0), rows(cur_ref, r, 0),
                             rows(prev_ref, r, 1), rows(cur_ref, r, 1), bias_ref)
        for q in range(nq):
            oscr[q, pl.ds(r, blk, stride=d), :] = o[:, q * LANES:(q + 1) * LANES]
            lscr[q, pl.ds(r, blk, stride=d), :] = lse[:, q * LANES:(q + 1) * LANES]
        return carry

    lax.fori_loop(0, d, residue, 0)
    for q in range(nq):
        o_ref[0, :, q * LANES:(q + 1) * LANES] = oscr[q].astype(o_ref.dtype)
        lse_ref[0, :, q * LANES:(q + 1) * LANES] = lscr[q]


def _attn_dilated(qkv_t, bias, B, *, blk, d, gw):
    n_tiles, _, rows_t, _ = qkv_t.shape
    tm = rows_t * d
    nt = blk * d // tm
    nb = n_tiles // (B * nt)
    S = n_tiles * tm // B
    return pl.pallas_call(
        functools.partial(_attn_dil_body, d=d, blk=blk, gw=gw),
        grid=(B, nb),
        in_specs=[pl.BlockSpec((nt, d, rows_t, 3 * gw), lambda b, n: (b * nb + n, 0, 0, 0)),
                  pl.BlockSpec((nt, d, rows_t, 2 * gw), lambda b, n: (b * nb + jnp.maximum(n - 1, 0), 0, 0, 0)),
                  pl.BlockSpec((1,) + bias.shape[1:], lambda b, n: (jnp.minimum(n, 1), 0, 0, 0))],
        out_specs=[pl.BlockSpec((1, blk * d, gw), lambda b, n: (b, n, 0)),
                   pl.BlockSpec((1, blk * d, gw), lambda b, n: (b, n, 0))],
        out_shape=[jax.ShapeDtypeStruct((B, S, gw), BF16), jax.ShapeDtypeStruct((B, S, gw), F32)],
        scratch_shapes=[pltpu.VMEM((gw // LANES, blk * d, LANES), F32),
                        pltpu.VMEM((gw // LANES, blk * d, LANES), F32)],
        compiler_params=_cparams("parallel", "parallel"),
        name=f"attn_d{d}",
    )(qkv_t, qkv_t, bias)


def _attn_group(qkv, bias, *, blk, d, gw, q_blk, k_blk, v_blk, row_blk):
    B, L, _ = qkv.shape
    nb = L // blk

    def cur(col):
        return pl.BlockSpec((1, blk, gw), lambda b, r, n: (b, n, r * row_blk + col))

    def prev(col):
        return pl.BlockSpec((1, blk, gw), lambda b, r, n: (b, jnp.maximum(n - 1, 0), r * row_blk + col))

    return pl.pallas_call(
        _attn_body,
        grid=(B, d, nb),
        in_specs=[cur(q_blk), prev(k_blk), cur(k_blk), prev(v_blk), cur(v_blk),
                  pl.BlockSpec((1,) + bias.shape[1:], lambda b, r, n: (jnp.minimum(n, 1), 0, 0, 0))],
        out_specs=[pl.BlockSpec((1, blk, gw), lambda b, r, n: (b, n, r)),
                   pl.BlockSpec((1, blk, gw), lambda b, r, n: (b, n, r))],
        out_shape=[jax.ShapeDtypeStruct((B, L, d * gw), BF16),
                   jax.ShapeDtypeStruct((B, L, d * gw), F32)],
        compiler_params=_cparams("parallel", "parallel", "parallel"),
        name=f"attn_d{d}",
    )(qkv, qkv, qkv, qkv, qkv, bias)


def _merge_body(yr_ref, o1_ref, o2_ref, o3_ref, l1_ref, l2_ref, l3_ref, g_ref, x_ref,
                wr_ref, wa_ref, wo_ref, lg_ref, lb_ref, wrt_ref,
                x1_ref, x1p_ref, lt_ref, *, alpha):
    l1, l2, l3 = l1_ref[...], l2_ref[...], l3_ref[...]
    mx = jnp.maximum(jnp.maximum(l1, l2), l3)
    w1, w2, w3 = jnp.exp(l1 - mx), jnp.exp(l2 - mx), jnp.exp(l3 - mx)
    y_att = (w1 * o1_ref[...].astype(F32) + w2 * o2_ref[...].astype(F32)
             + w3 * o3_ref[...].astype(F32)) / (w1 + w2 + w3)
    pr = jnp.dot(yr_ref[...], wr_ref[...], preferred_element_type=F32)
    pa = jnp.dot(y_att.astype(BF16), wa_ref[...], preferred_element_type=F32)
    dm = pr.shape[1]
    g = g_ref[...].astype(F32)
    merged = _sigmoid(g[:, :dm]) * pr + _sigmoid(g[:, dm:]) * pa
    mix = jnp.dot(merged.astype(BF16), wo_ref[...], preferred_element_type=F32)
    z = alpha * x_ref[...] + mix
    mu = jnp.mean(z, axis=-1, keepdims=True)
    zc = z - mu
    var = jnp.mean(zc * zc, axis=-1, keepdims=True)
    x1 = zc * lax.rsqrt(var + LN_EPS) * lg_ref[...] + lb_ref[...]
    x1_ref[...] = x1
    x1p_ref[...] = _pack_rows(x1)
    lt_ref[...] = lax.dot_general(wrt_ref[...], x1, (((1,), (1,)), ((), ())),
                                  precision=lax.Precision.HIGHEST, preferred_element_type=F32)


def _merge(y_rnn, os_, lses, proj, gate_blk, x2, wr, wa, wo, ln_g, ln_b, w_router_t, *, alpha, tm=512):
    T, D = x2.shape
    da = os_[0].shape[1]
    E = w_router_t.shape[0]
    row = lambda w: pl.BlockSpec((tm, w), lambda i: (i, 0))
    full = lambda a: pl.BlockSpec(a.shape, lambda i: (0,) * a.ndim)
    return pl.pallas_call(
        functools.partial(_merge_body, alpha=alpha),
        grid=(T // tm,),
        in_specs=[row(D), row(da), row(da), row(da), row(da), row(da), row(da),
                  pl.BlockSpec((tm, 2 * D), lambda i: (i, gate_blk)), row(D),
                  full(wr), full(wa), full(wo), full(ln_g), full(ln_b), full(w_router_t)],
        out_specs=[row(D), row(D // 2), pl.BlockSpec((E, tm), lambda i: (0, i))],
        out_shape=[jax.ShapeDtypeStruct((T, D), F32), jax.ShapeDtypeStruct((T, D // 2), U32),
                   jax.ShapeDtypeStruct((E, T), F32)],
        compiler_params=_cparams("parallel"),
        name="merge",
    )(y_rnn, *os_, *lses, proj, x2, wr, wa, wo, ln_g, ln_b, w_router_t)


def _first_max(vals, idx, big):
    m = jnp.max(vals, axis=0, keepdims=True)
    i = jnp.min(jnp.where(vals == m, idx, big), axis=0, keepdims=True)
    return m, i


def _route_body(lt_ref, rb_ref, e_ref, g_ref, r_ref, f_ref, cb_ref, cnt_ref, *, tl):
    E = lt_ref.shape[0]
    per = E // N_EXPERT_GROUPS
    neg = -jnp.inf

    @pl.when(pl.program_id(0) == 0)
    def _():
        cnt_ref[...] = jnp.zeros_like(cnt_ref)

    scores = jax.nn.sigmoid(lt_ref[...])
    sel = scores + rb_ref[...]
    rowi = lax.broadcasted_iota(I32, (E, tl), 0)

    gi = lax.broadcasted_iota(I32, (N_EXPERT_GROUPS, tl), 0)
    gsc = jnp.zeros((N_EXPERT_GROUPS, tl), F32)
    for g in range(N_EXPERT_GROUPS):
        blk = sel[g * per:(g + 1) * per]
        ri = lax.broadcasted_iota(I32, (per, tl), 0) + g * per
        m1, i1 = _first_max(blk, ri, E)
        m2 = jnp.max(jnp.where(ri == i1, neg, blk), axis=0, keepdims=True)
        gsc = jnp.where(gi == g, m1 + m2, gsc)
    keep = jnp.zeros((N_EXPERT_GROUPS, tl), F32)
    for _ in range(TOPK_GROUPS):
        _, ig = _first_max(gsc, gi, N_EXPERT_GROUPS)
        hit = gi == ig
        keep = jnp.where(hit, 1.0, keep)
        gsc = jnp.where(hit, neg, gsc)
    cur = jnp.concatenate(
        [jnp.where(keep[g:g + 1] > 0.5, sel[g * per:(g + 1) * per], neg) for g in range(N_EXPERT_GROUPS)],
        axis=0)

    ti = lax.broadcasted_iota(I32, (tl, tl), 0)
    tj = lax.broadcasted_iota(I32, (tl, tl), 1)
    earlier = jnp.where(ti < tj, 1.0, 0.0).astype(BF16)

    es, gv = [], []
    onehot = jnp.zeros((E, tl), F32)
    for k in range(TOP_K):
        _, ie = _first_max(cur, rowi, E)
        hit = rowi == ie
        es.append(ie)
        gv.append(jnp.sum(jnp.where(hit, scores, 0.0), axis=0, keepdims=True))
        onehot = jnp.where(hit, 1.0, onehot)
        cur = jnp.where(hit, neg, cur)
    gsum = gv[0]
    for k in range(1, TOP_K):
        gsum = gsum + gv[k]
    before = cnt_ref[...]
    cb_ref[0] = before
    local = jnp.dot(onehot.astype(BF16), earlier, preferred_element_type=F32)
    ranks = local + before
    offs = local + (before - SUBLANES * jnp.floor(before * (1.0 / SUBLANES)))
    for k in range(TOP_K):
        hit = rowi == es[k]
        e_ref[k:k + 1, :] = es[k]
        g_ref[k:k + 1, :] = gv[k] / gsum * ROUTED_SCALE
        r_ref[k:k + 1, :] = jnp.sum(jnp.where(hit, ranks, 0.0), axis=0, keepdims=True).astype(I32)
        f_ref[k:k + 1, :] = jnp.sum(jnp.where(hit, offs, 0.0), axis=0, keepdims=True).astype(I32)
    cnt_ref[...] = before + jnp.sum(onehot, axis=1, keepdims=True)


def _route(logits_t, router_bias, *, tl):
    E, T = logits_t.shape
    kt = pl.BlockSpec((TOP_K, tl), lambda i: (0, i))
    kt_i32 = jax.ShapeDtypeStruct((TOP_K, T), I32)
    return pl.pallas_call(
        functools.partial(_route_body, tl=tl),
        grid=(T // tl,),
        in_specs=[pl.BlockSpec((E, tl), lambda i: (0, i)), pl.BlockSpec((E, 1), lambda i: (0, 0))],
        out_specs=[kt, kt, kt, kt, pl.BlockSpec((1, E, 1), lambda i: (i, 0, 0)),
                   pl.BlockSpec((E, 1), lambda i: (0, 0))],
        out_shape=[kt_i32, jax.ShapeDtypeStruct((TOP_K, T), F32), kt_i32, kt_i32,
                   jax.ShapeDtypeStruct((T // tl, E, 1), F32), jax.ShapeDtypeStruct((E, 1), F32)],
        compiler_params=_cparams("arbitrary"),
        name="route",
    )(logits_t, router_bias.reshape(E, 1))


def _dest_body(e_ref, r_ref, ss_ref, d_ref):
    E = ss_ref.shape[0]
    tl = e_ref.shape[1]
    rowi = lax.broadcasted_iota(I32, (E, tl), 0)
    e = e_ref[...]
    rows = [jnp.sum(jnp.where(rowi == e[k:k + 1], ss_ref[...], 0), axis=0, keepdims=True)
            for k in range(e.shape[0])]
    d_ref[...] = jnp.concatenate(rows, axis=0) + r_ref[...]


def _dest(e_idx, rank, seg_start, *, tl=512):
    K, T = e_idx.shape
    E = seg_start.shape[0]
    kt = pl.BlockSpec((K, tl), lambda i: (0, i))
    return pl.pallas_call(
        _dest_body, grid=(T // tl,),
        in_specs=[kt, kt, pl.BlockSpec((E, 1), lambda i: (0, 0))],
        out_specs=kt, out_shape=jax.ShapeDtypeStruct((K, T), I32),
        compiler_params=_cparams("parallel"), name="dest",
    )(e_idx, rank, seg_start.reshape(E, 1))


def _pack_rows(v):
    w = v.shape[1] // 2
    lo = pltpu.bitcast(v[:, :w].astype(BF16).astype(F32), U32) >> 16
    hi = pltpu.bitcast(v[:, w:].astype(BF16).astype(F32), U32) & jnp.uint32(0xFFFF0000)
    return lo | hi


def _unpack_rows(p):
    lo = pltpu.bitcast(p << 16, F32)
    hi = pltpu.bitcast(p & jnp.uint32(0xFFFF0000), F32)
    return lo, hi


def _dispatch_body(ps_ref, pc_ref, nu_ref, dest_hbm, x_ref, xs_hbm, dsm, zrow, sem_i, sem_r, sem_z, sem_t,
                   *, tl, K):
    i = pl.program_id(0)
    n = pl.num_programs(0)
    R = zrow.shape[0]

    def pad_copy(row):
        return pltpu.make_async_copy(zrow.at[pl.ds(0, 1), :], xs_hbm.at[pl.ds(row, 1), :], sem_z)

    idx_cp = pltpu.make_async_copy(dest_hbm.at[i], dsm, sem_i)
    idx_cp.start()
    zrow[...] = jnp.zeros_like(zrow)
    idx_cp.wait()

    n_blk = xs_hbm.shape[0] // R
    tail_per_step = -(-n_blk // n)

    def tail(j, c, wait):
        blk = i * tail_per_step + j

        @pl.when(jnp.logical_and(blk >= nu_ref[0], blk < n_blk))
        def _():
            cp = pltpu.make_async_copy(zrow, xs_hbm.at[pl.ds(pl.multiple_of(blk * R, R), R), :], sem_t)
            cp.wait() if wait else cp.start()

        return c

    lax.fori_loop(0, tail_per_step, functools.partial(tail, wait=False), 0)

    def issue(t, c):
        src = x_ref.at[pl.ds(t, 1), :]
        for k in range(K):
            pltpu.make_async_copy(src, xs_hbm.at[pl.ds(dsm[k * tl + t], 1), :], sem_r).start(priority=k % 2)
        return c

    lax.fori_loop(0, tl, issue, 0, unroll=2)

    E = ps_ref.shape[0]
    per_step = -(-E // n)

    def pads(j, c, wait):
        e = jnp.minimum(i * per_step + j, E - 1)
        cnt = jnp.where(i * per_step + j < E, pc_ref[e], 0)

        def one(r, c2):
            cp = pad_copy(ps_ref[e] + r)
            cp.wait() if wait else cp.start()
            return c2

        return lax.fori_loop(0, cnt, one, c)

    lax.fori_loop(0, per_step, functools.partial(pads, wait=False), 0)
    for k in range(K):
        pltpu.make_async_copy(x_ref, xs_hbm.at[pl.ds(0, tl), :], sem_r).wait()
    lax.fori_loop(0, per_step, functools.partial(pads, wait=True), 0)
    lax.fori_loop(0, tail_per_step, functools.partial(tail, wait=True), 0)


def _dispatch(xp, dest_t, pad_start, pad_cnt, n_used, n_rows, *, tl):
    T, W = xp.shape
    K = dest_t.shape[1] // tl
    any_spec = pl.BlockSpec(memory_space=pl.ANY)
    return pl.pallas_call(
        functools.partial(_dispatch_body, tl=tl, K=K),
        grid_spec=pltpu.PrefetchScalarGridSpec(
            num_scalar_prefetch=3, grid=(T // tl,),
            in_specs=[any_spec, pl.BlockSpec((tl, W), lambda i, ps, pc, nu: (i, 0))],
            out_specs=any_spec,
            scratch_shapes=[pltpu.SMEM((K * tl,), I32), pltpu.VMEM((EXPERT_ROWS, W), U32),
                            pltpu.SemaphoreType.DMA, pltpu.SemaphoreType.DMA, pltpu.SemaphoreType.DMA,
                            pltpu.SemaphoreType.DMA]),
        out_shape=jax.ShapeDtypeStruct((n_rows, W), U32),
        compiler_params=_cparams("arbitrary"),
        name="dispatch",
    )(pad_start, pad_cnt, n_used, dest_t, xp)


def _experts_body(be_ref, nu_ref, xs_ref, wg_ref, wu_ref, wd_ref, ys_ref, wgu_s, wd_s):
    j = pl.program_id(0)
    de = wg_ref.shape[2]
    changed = jnp.logical_or(j == 0, be_ref[j] != be_ref[jnp.maximum(j - 1, 0)])

    @pl.when(changed)
    def _():
        wgu_s[:, :de] = wg_ref[0].astype(BF16)
        wgu_s[:, de:] = wu_ref[0].astype(BF16)
        wd_s[...] = wd_ref[0].astype(BF16)

    @pl.when(j < nu_ref[0])
    def _():
        lo, hi = _unpack_rows(xs_ref[...])
        xb = jnp.concatenate([lo.astype(BF16), hi.astype(BF16)], axis=1)
        h = jnp.dot(xb, wgu_s[...], preferred_element_type=F32)
        hg = h[:, :de]
        act = (hg * _sigmoid(hg) * h[:, de:]).astype(BF16)
        ys_ref[...] = _pack_rows(jnp.dot(act, wd_s[...], preferred_element_type=F32))

    @pl.when(j >= nu_ref[0])
    def _():
        ys_ref[...] = jnp.zeros_like(ys_ref)


def _experts(xs, blk_e, n_used, wg, wu, wd):
    n_rows, W = xs.shape
    R = EXPERT_ROWS
    D, de = wg.shape[1], wg.shape[2]
    xmap = lambda j, be, nu: (jnp.minimum(j, nu[0] - 1), 0)
    wmap = lambda j, be, nu: (be[j], 0, 0)
    return pl.pallas_call(
        _experts_body,
        grid_spec=pltpu.PrefetchScalarGridSpec(
            num_scalar_prefetch=2, grid=(n_rows // R,),
            in_specs=[pl.BlockSpec((R, W), xmap),
                      pl.BlockSpec((1, D, de), wmap), pl.BlockSpec((1, D, de), wmap),
                      pl.BlockSpec((1, de, D), wmap)],
            out_specs=pl.BlockSpec((R, W), lambda j, be, nu: (j, 0)),
            scratch_shapes=[pltpu.VMEM((D, 2 * de), BF16), pltpu.VMEM((de, D), BF16)]),
        out_shape=jax.ShapeDtypeStruct((n_rows, W), U32),
        compiler_params=_cparams("arbitrary"),
        name="experts",
    )(blk_e, n_used, xs, wg, wu, wd)


def _combine_body(rs_ref, nr_ref, ys_hbm, e_ref, f_ref, gate_ref, x1_ref, wgu_ref, wd_ref, lg_ref, lb_ref, o_ref,
                  buf, acc, sem, *, K, Lb, ce, alpha):
    i = pl.program_id(0)
    tc, D = x1_ref.shape
    E = buf.shape[0] // Lb
    n_rows = ys_hbm.shape[0]

    x1 = x1_ref[...]
    ds_ = wd_ref.shape[0]
    h = jnp.dot(x1.astype(BF16), wgu_ref[...], preferred_element_type=F32)
    hg = h[:, :ds_]
    act = (hg * _sigmoid(hg) * h[:, ds_:]).astype(BF16)
    acc[...] = jnp.dot(act, wd_ref[...], preferred_element_type=F32)

    def one_round(j, c):
        def issue(e, c2):
            start = jnp.minimum(rs_ref[i * E + e] + j * Lb, n_rows - Lb)
            pltpu.make_async_copy(ys_hbm.at[pl.ds(pl.multiple_of(start, SUBLANES), Lb), :],
                                  buf.at[pl.ds(pl.multiple_of(e * Lb, SUBLANES), Lb), :], sem).start()
            return c2

        lax.fori_loop(0, E, issue, 0, unroll=4)
        off = f_ref[...] - j * Lb
        col = jnp.where(jnp.logical_and(off >= 0, off < Lb), e_ref[...] * Lb + off, -1)
        gate = gate_ref[...]
        pltpu.make_async_copy(ys_hbm.at[pl.ds(0, E * Lb), :], buf, sem).wait()

        def chunk(q, c2):
            r0 = pl.multiple_of(q * (ce * Lb), SUBLANES)
            lo, hi = _unpack_rows(buf[pl.ds(r0, ce * Lb), :])
            rows = jnp.concatenate([lo.astype(BF16), hi.astype(BF16)], axis=1)
            lane = lax.broadcasted_iota(I32, (tc, ce * Lb), 1) + q * (ce * Lb)
            sel = jnp.zeros((tc, ce * Lb), F32)
            for k in range(K):
                sel = jnp.where(lane == col[:, k:k + 1], gate[:, k:k + 1], sel)
            acc[...] += jnp.dot(sel.astype(BF16), rows, preferred_element_type=F32)
            return c2

        lax.fori_loop(0, E // ce, chunk, 0)
        return c

    lax.fori_loop(0, nr_ref[i], one_round, 0)

    z = alpha * x1 + acc[...]
    mu = jnp.mean(z, axis=-1, keepdims=True)
    zc = z - mu
    var = jnp.mean(zc * zc, axis=-1, keepdims=True)
    o_ref[...] = zc * lax.rsqrt(var + LN_EPS) * lg_ref[...] + lb_ref[...]


def _combine(ys, win_start, n_rounds, e_tk, f_tk, gate_tk, x1, wgu, wd, ln_g, ln_b, *, alpha, tc, n_experts):
    T, D = x1.shape
    K = gate_tk.shape[1]
    W = ys.shape[1]
    Lb = COMBINE_WINDOW
    any_spec = pl.BlockSpec(memory_space=pl.ANY)
    tk = pl.BlockSpec((tc, K), lambda i, rs, nr: (i, 0))
    full = lambda a: pl.BlockSpec(a.shape, lambda i, rs, nr: (0,) * a.ndim)
    return pl.pallas_call(
        functools.partial(_combine_body, K=K, Lb=Lb, ce=COMBINE_EXPERT_CHUNK, alpha=alpha),
        grid_spec=pltpu.PrefetchScalarGridSpec(
            num_scalar_prefetch=2, grid=(T // tc,),
            in_specs=[any_spec, tk, tk, tk, pl.BlockSpec((tc, D), lambda i, rs, nr: (i, 0)),
                      full(wgu), full(wd), full(ln_g), full(ln_b)],
            out_specs=pl.BlockSpec((tc, D), lambda i, rs, nr: (i, 0)),
            scratch_shapes=[pltpu.VMEM((n_experts * Lb, W), U32), pltpu.VMEM((tc, D), F32),
                            pltpu.SemaphoreType.DMA]),
        out_shape=jax.ShapeDtypeStruct((T, D), F32),
        compiler_params=_cparams("arbitrary"),
        name="combine",
    )(win_start, n_rounds, ys, e_tk, f_tk, gate_tk, x1, wgu, wd, ln_g, ln_b)


def _mixer_ln1(x, w_in, b_in, conv_w, conv_b, w_rg_a, b_rg_a, w_rg_i, b_rg_i, lru_lambda,
               w_proj_rnn, w_proj_att, rel_bias, w_out, ln1_g, ln1_b, w_router, alpha):
    B, S, D = x.shape
    T = B * S
    d_rnn = conv_w.shape[-1]
    gw = HEADS_PER_GROUP * HEAD_DIM
    d_att = gw * len(DILATED_GROUPS)
    a0 = 2 * d_rnn
    a1 = a0 + 3 * d_att
    head = lambda j, g: slice(a0 + j * d_att + g * gw, a0 + j * d_att + (g + 1) * gw)
    plain = [g for g, (_, d) in enumerate(DILATED_GROUPS) if d == 1]
    dilated = [g for g, (_, d) in enumerate(DILATED_GROUPS) if d > 1]
    order = ([slice(0, a0), slice(a1, None)] + [head(j, g) for g in plain for j in range(3)]
             + [head(j, g) for g in dilated for j in (1, 2, 0)])
    perm = lambda w: jnp.concatenate([w[..., s] for s in order], axis=-1)
    w_p = perm(w_in).astype(BF16)
    b_p = perm(b_in).reshape(1, -1)
    qkv0 = a0 + 2 * D
    n_main = qkv0 + 3 * gw * len(plain)
    x2 = x.reshape(T, D)

    proj, *qkv_dil = _in_proj(x2, w_p, b_p, n_main, tuple(DILATED_GROUPS[g][1] for g in dilated))
    proj3 = proj.reshape(B, S, n_main)
    y_rnn = _rnn(proj3, conv_w, conv_b, w_rg_a, b_rg_a, w_rg_i, b_rg_i, lru_lambda)

    os_, lses = [], []
    for g, (window, d) in enumerate(DILATED_GROUPS):
        blk = window // d
        bias = _attn_bias(rel_bias[:, g * HEADS_PER_GROUP:(g + 1) * HEADS_PER_GROUP], window, d)
        if d == 1:
            c0 = (qkv0 + 3 * gw * plain.index(g)) // gw
            o, lse = _attn_group(proj3, bias, blk=blk, d=1, gw=gw, q_blk=c0, k_blk=c0 + 1, v_blk=c0 + 2,
                                 row_blk=n_main // gw)
        else:
            o, lse = _attn_dilated(qkv_dil[dilated.index(g)], bias, B, blk=blk, d=d, gw=gw)
        os_.append(o.reshape(T, gw))
        lses.append(lse.reshape(T, gw))

    return _merge(y_rnn.reshape(T, d_rnn), os_, lses, proj, a0 // (2 * D), x2,
                  w_proj_rnn.astype(BF16), w_proj_att.astype(BF16), w_out.astype(BF16),
                  ln1_g.reshape(1, D), ln1_b.reshape(1, D), w_router.T, alpha=alpha)


def _moe_ln2(x1, x1p, logits_t, router_bias, w_exp_gate, w_exp_up, w_exp_down,
             w_sh_gate, w_sh_up, w_sh_down, ln2_g, ln2_b, alpha):
    T, D = x1.shape
    E = logits_t.shape[0]
    R = EXPERT_ROWS
    n_blk = T * TOP_K // R + E
    n_rows = n_blk * R

    tl = ROW_TILE
    e_idx, gate, rank, off, before, counts = _route(logits_t, router_bias, tl=tl)
    counts = counts.reshape(E).astype(I32)
    padded = (counts + R - 1) // R * R
    pad_end = jnp.cumsum(padded)
    seg_start = pad_end - padded
    n_used = (pad_end[-1:] // R).astype(I32)
    blk_e = jnp.minimum(jnp.sum(jnp.arange(n_blk, dtype=I32)[:, None] >= (pad_end // R)[None, :], axis=1),
                        E - 1).astype(I32)
    before = before.reshape(T // tl, E).astype(I32)
    in_tile = jnp.concatenate([before[1:], counts[None]], axis=0) - before
    win_start = seg_start[None, :] + before // SUBLANES * SUBLANES
    n_rounds = jnp.maximum(jnp.max((before % SUBLANES + in_tile + COMBINE_WINDOW - 1) // COMBINE_WINDOW, axis=1), 1)

    dest = _dest(e_idx, rank, seg_start)
    dest_t = dest.reshape(TOP_K, T // tl, tl).transpose(1, 0, 2).reshape(T // tl, TOP_K * tl)
    xs = _dispatch(x1p, dest_t, seg_start + counts, padded - counts, n_used, n_rows, tl=tl)
    ys = _experts(xs, blk_e, n_used, w_exp_gate, w_exp_up, w_exp_down)
    wgu = jnp.concatenate([w_sh_gate, w_sh_up], axis=-1).astype(BF16)
    return _combine(ys, win_start.reshape(-1), n_rounds.astype(I32), e_idx.T, off.T, gate.T, x1, wgu,
                    w_sh_down.astype(BF16), ln2_g.reshape(1, D), ln2_b.reshape(1, D),
                    alpha=alpha, tc=tl, n_experts=E)


def kernel(x, w_in, b_in, conv_w, conv_b, w_rg_a, b_rg_a, w_rg_i, b_rg_i, lru_lambda,
           w_proj_rnn, w_proj_att, rel_bias, w_out, ln1_g, ln1_b, w_router, router_bias,
           w_exp_gate, w_exp_up, w_exp_down, w_sh_gate, w_sh_up, w_sh_down, ln2_g, ln2_b):
    B, S, D = x.shape
    depth = w_in.shape[0]
    alpha = (2 * depth) ** 0.25
    for i in range(depth):
        x1, x1p, logits_t = _mixer_ln1(
            x, w_in[i], b_in[i], conv_w[i], conv_b[i], w_rg_a[i], b_rg_a[i], w_rg_i[i], b_rg_i[i],
            lru_lambda[i], w_proj_rnn[i], w_proj_att[i], rel_bias, w_out[i], ln1_g[i], ln1_b[i],
            w_router[i], alpha)
        out = _moe_ln2(x1, x1p, logits_t, router_bias[i], w_exp_gate[i], w_exp_up[i], w_exp_down[i],
                       w_sh_gate[i], w_sh_up[i], w_sh_down[i], ln2_g[i], ln2_b[i], alpha)
        x = out.reshape(B, S, D)
    return x
```

```python
import functools

import numpy as np
import jax
import jax.numpy as jnp
from jax import lax
from jax.experimental import pallas as pl
from jax.experimental.pallas import tpu as pltpu

F32 = jnp.float32
BF16 = jnp.bfloat16
I32 = jnp.int32
U32 = jnp.uint32

RNN_HEADS = 16
LRU_C = 8.0
HEAD_DIM = 64
HEADS_PER_GROUP = 8
DILATED_GROUPS = ((128, 1), (512, 4), (2048, 16))
NUM_BUCKETS = 32
MAX_DISTANCE = 2048
TOP_K = 8
N_EXPERT_GROUPS = 8
TOPK_GROUPS = 4
ROUTED_SCALE = 2.5
LN_EPS = 1e-5

LANES = 128
SUBLANES = 8
VMEM_LIMIT_BYTES = 56 * 1024 * 1024

MASK_VALUE = -1e30
EXPERT_ROWS = 512
ROW_TILE = 256


def _cparams(*sem):
    return pltpu.CompilerParams(dimension_semantics=sem, vmem_limit_bytes=VMEM_LIMIT_BYTES)


def _sigmoid(v):
    return 0.5 * (jnp.tanh(0.5 * v) + 1.0)


def _in_proj_body(x_ref, w_ref, b_ref, main_ref, *rest, tn, dils):
    dil_refs, scr = rest[:-1], rest[-1]
    xb = x_ref[...].astype(BF16)
    tm = xb.shape[0]

    def chunk(j):
        sl = slice(j * tn, (j + 1) * tn)
        return jnp.dot(xb, w_ref[:, sl], preferred_element_type=F32) + b_ref[:, sl]

    n_main = main_ref.shape[1] // tn
    for j in range(n_main):
        main_ref[:, j * tn:(j + 1) * tn] = chunk(j).astype(main_ref.dtype)
    j = n_main
    for ref, d in zip(dil_refs, dils):
        for c in range(ref.shape[3] // tn):
            acc = chunk(j)
            j += 1
            for q in range(tn // LANES):
                scr[q] = acc[:, q * LANES:(q + 1) * LANES]
            for r in range(d):
                part = jnp.concatenate([scr[q, pl.ds(r, tm // d, stride=d), :] for q in range(tn // LANES)], axis=1)
                ref[0, r, :, c * tn:(c + 1) * tn] = part.astype(ref.dtype)


def _in_proj(x2, w, b, n_main, dils, *, tm=512, tn=512):
    T, D = x2.shape
    N = w.shape[1]
    wd = (N - n_main) // len(dils)
    out_specs = [pl.BlockSpec((tm, n_main), lambda i: (i, 0))]
    out_shape = [jax.ShapeDtypeStruct((T, n_main), BF16)]
    for d in dils:
        out_specs.append(pl.BlockSpec((1, d, tm // d, wd), lambda i: (i, 0, 0, 0)))
        out_shape.append(jax.ShapeDtypeStruct((T // tm, d, tm // d, wd), BF16))
    return pl.pallas_call(
        functools.partial(_in_proj_body, tn=tn, dils=dils),
        grid=(T // tm,),
        in_specs=[pl.BlockSpec((tm, D), lambda i: (i, 0)),
                  pl.BlockSpec((D, N), lambda i: (0, 0), pipeline_mode=pl.Buffered(1)),
                  pl.BlockSpec((1, N), lambda i: (0, 0))],
        out_specs=out_specs,
        out_shape=out_shape,
        scratch_shapes=[pltpu.VMEM((tn // LANES, tm, LANES), F32)],
        compiler_params=_cparams("parallel"),
        name="in_proj",
    )(x2, w, b)


def _rnn_body(xr_ref, gr_ref, cw_ref, cb_ref, wg_ref, bg_ref, lam_ref, y_ref,
              xext, a_s, b_s, hc, *, ts, cb):
    s = pl.program_id(2)

    @pl.when(s == 0)
    def _():
        xext[0:SUBLANES, :] = jnp.zeros((SUBLANES, cb), F32)
        hc[...] = jnp.zeros_like(hc)

    xr = xr_ref[0].astype(F32)
    xext[SUBLANES:SUBLANES + ts, :] = xr
    nw = cw_ref.shape[0]
    xc = cw_ref[nw - 1:nw, :] * xr + cb_ref[...]
    for j in range(nw - 1):
        off = SUBLANES - (nw - 1 - j)
        xc = xc + cw_ref[j:j + 1, :] * xext[off:off + ts, :]
    xext[0:SUBLANES, :] = xext[ts:ts + SUBLANES, :]

    gates = jnp.dot(xc.astype(BF16), wg_ref[0], preferred_element_type=F32) + bg_ref[...]
    r = _sigmoid(gates[:, :cb])
    ig = _sigmoid(gates[:, cb:])
    nl = -lam_ref[...]
    sp = jnp.maximum(nl, 0.0) + jnp.log1p(jnp.exp(-jnp.abs(nl)))
    log_a = (-LRU_C) * r * sp
    a = jnp.exp(log_a)
    u = jnp.sqrt(1.0 - a * a) * (ig * xc)

    row = lax.broadcasted_iota(I32, (ts, cb), 0) & (SUBLANES - 1)
    av, bv = a, u
    for sft in (1, 2, 4):
        a_sh = pltpu.roll(av, sft, 0)
        b_sh = pltpu.roll(bv, sft, 0)
        m = row >= sft
        bv = jnp.where(m, av * b_sh + bv, bv)
        av = jnp.where(m, av * a_sh, av)
    a_s[...] = av
    b_s[...] = bv

    def carry(g, h):
        i0 = pl.multiple_of(g * SUBLANES, SUBLANES)
        h8 = b_s[pl.ds(i0, SUBLANES), :] + a_s[pl.ds(i0, SUBLANES), :] * h
        b_s[pl.ds(i0, SUBLANES), :] = h8
        return h8[SUBLANES - 1:SUBLANES, :]

    hc[0:1, :] = lax.fori_loop(0, ts // SUBLANES, carry, hc[0:1, :], unroll=8)
    gr = gr_ref[0].astype(F32)
    y_ref[0] = (b_s[...] * jax.nn.gelu(gr)).astype(y_ref.dtype)


def _block_diag(w, per):
    H, d, _ = w.shape
    w4 = w.reshape(H // per, per, d, d)
    out = jnp.einsum('gpij,pq->gpiqj', w4, jnp.eye(per, dtype=w.dtype))
    return out.reshape(H // per, per * d, per * d)


def _rnn(proj3, conv_w, conv_b, w_rg_a, b_rg_a, w_rg_i, b_rg_i, lam, *, ts=512, cb=256):
    B, S, _ = proj3.shape
    d_rnn = conv_w.shape[-1]
    nc = d_rnn // cb
    per = cb // (d_rnn // RNN_HEADS)
    wg = jnp.concatenate([_block_diag(w_rg_a, per), _block_diag(w_rg_i, per)], axis=-1).astype(BF16)
    bg = jnp.concatenate([b_rg_a.reshape(nc, 1, cb), b_rg_i.reshape(nc, 1, cb)], axis=-1)
    return pl.pallas_call(
        functools.partial(_rnn_body, ts=ts, cb=cb),
        grid=(B, nc, S // ts),
        in_specs=[pl.BlockSpec((1, ts, cb), lambda b, c, s: (b, s, c)),
                  pl.BlockSpec((1, ts, cb), lambda b, c, s: (b, s, nc + c)),
                  pl.BlockSpec((conv_w.shape[0], cb), lambda b, c, s: (0, c)),
                  pl.BlockSpec((1, cb), lambda b, c, s: (0, c)),
                  pl.BlockSpec((1, cb, 2 * cb), lambda b, c, s: (c, 0, 0)),
                  pl.BlockSpec((None, 1, 2 * cb), lambda b, c, s: (c, 0, 0)),
                  pl.BlockSpec((1, cb), lambda b, c, s: (0, c))],
        out_specs=pl.BlockSpec((1, ts, cb), lambda b, c, s: (b, s, c)),
        out_shape=jax.ShapeDtypeStruct((B, S, d_rnn), BF16),
        scratch_shapes=[pltpu.VMEM((ts + SUBLANES, cb), F32), pltpu.VMEM((ts, cb), F32),
                        pltpu.VMEM((ts, cb), F32), pltpu.VMEM((SUBLANES, cb), F32)],
        compiler_params=_cparams("parallel", "parallel", "arbitrary"),
        name="rnn",
    )(proj3, proj3, conv_w, conv_b.reshape(1, d_rnn), wg, bg, lam.reshape(1, d_rnn))


def _t5_bucket(dist):
    max_exact = NUM_BUCKETS // 2
    d = np.maximum(dist, 1).astype(np.float64)
    large = max_exact + (np.log(d / max_exact) / np.log(MAX_DISTANCE / max_exact)
                         * (NUM_BUCKETS - max_exact)).astype(np.int64)
    large = np.minimum(large, NUM_BUCKETS - 1)
    return np.where(dist < max_exact, dist, large).astype(np.int32)


def _attn_bias(table, window, dilation):
    blk = window // dilation
    qi = np.arange(blk)[:, None]
    ki = np.arange(2 * blk)[None, :]
    rel = qi + blk - ki
    in_window = (rel >= 0) & (rel <= blk)
    bucket = _t5_bucket(np.clip(rel, 0, None) * dilation)
    onehot = (bucket[..., None] == np.arange(NUM_BUCKETS)).astype(np.float32)
    bias = jnp.einsum('qkn,nh->hqk', onehot, table.astype(F32), precision=lax.Precision.HIGHEST)
    first = in_window & (ki >= blk)
    return jnp.stack([jnp.where(first[None], bias, MASK_VALUE),
                      jnp.where(in_window[None], bias, MASK_VALUE)])


def _attn_heads(q, kp, kc, vp, vc, bias_ref):
    blk, gw = q.shape
    lo = lax.broadcasted_iota(I32, (blk, LANES), 1) < HEAD_DIM
    scale = HEAD_DIM ** -0.5
    o_parts, l_parts = [], []
    for p in range(gw // LANES):
        sl = slice(p * LANES, (p + 1) * LANES)
        q2 = q[:, sl] * scale
        k2 = jnp.concatenate([kp[:, sl], kc[:, sl]], axis=0)
        v2 = jnp.concatenate([vp[:, sl], vc[:, sl]], axis=0)
        outs, lses = [], []
        for hh in range(2):
            qm = jnp.where(lo if hh == 0 else jnp.logical_not(lo), q2, jnp.zeros_like(q2))
            sc = lax.dot_general(qm, k2, (((1,), (1,)), ((), ())), preferred_element_type=F32)
            sc = sc + bias_ref[0, 2 * p + hh]
            m = jnp.max(sc, axis=-1, keepdims=True)
            e = jnp.exp(sc - m)
            l = jnp.sum(e, axis=-1, keepdims=True)
            o = jnp.dot(e.astype(BF16), v2, preferred_element_type=F32)
            outs.append(o / l)
            lses.append(jnp.broadcast_to(m + jnp.log(l), (blk, LANES)))
        o_parts.append(jnp.where(lo, outs[0], outs[1]))
        l_parts.append(jnp.where(lo, lses[0], lses[1]))
    return jnp.concatenate(o_parts, axis=1), jnp.concatenate(l_parts, axis=1)


def _attn_body(q_ref, kp_ref, kc_ref, vp_ref, vc_ref, bias_ref, o_ref, lse_ref):
    o, lse = _attn_heads(q_ref[0], kp_ref[0], kc_ref[0], vp_ref[0], vc_ref[0], bias_ref)
    o_ref[0] = o.astype(o_ref.dtype)
    lse_ref[0] = lse


def _attn_dil_body(cur_ref, prev_ref, bias_ref, o_ref, lse_ref, oscr, lscr, *, d, blk, gw):
    nt = cur_ref.shape[0]
    nq = gw // LANES

    def rows(ref, r, c):
        parts = [ref[j, r, :, c * gw:(c + 1) * gw] for j in range(nt)]
        return parts[0] if nt == 1 else jnp.concatenate(parts, axis=0)

    def residue(r, carry):
        o, lse = _attn_heads(rows(cur_ref, r, 2), rows(prev_ref, r, 0), rows(cur_ref, r, 0),
                             rows(prev_ref, r, 1), rows(cur_ref, r, 1), bias_ref)
        for q in range(nq):
            oscr[q, pl.ds(r, blk, stride=d), :] = o[:, q * LANES:(q + 1) * LANES]
            lscr[q, pl.ds(r, blk, stride=d), :] = lse[:, q * LANES:(q + 1) * LANES]
        return carry

    lax.fori_loop(0, d, residue, 0)
    for q in range(nq):
        o_ref[0, :, q * LANES:(q + 1) * LANES] = oscr[q].astype(o_ref.dtype)
        lse_ref[0, :, q * LANES:(q + 1) * LANES] = lscr[q]


def _attn_dilated(qkv_t, bias, B, *, blk, d, gw):
    n_tiles, _, rows_t, _ = qkv_t.shape
    tm = rows_t * d
    nt = blk * d // tm
    nb = n_tiles // (B * nt)
    S = n_tiles * tm // B
    return pl.pallas_call(
        functools.partial(_attn_dil_body, d=d, blk=blk, gw=gw),
        grid=(B, nb),
        in_specs=[pl.BlockSpec((nt, d, rows_t, 3 * gw), lambda b, n: (b * nb + n, 0, 0, 0)),
                  pl.BlockSpec((nt, d, rows_t, 2 * gw), lambda b, n: (b * nb + jnp.maximum(n - 1, 0), 0, 0, 0)),
                  pl.BlockSpec((1,) + bias.shape[1:], lambda b, n: (jnp.minimum(n, 1), 0, 0, 0))],
        out_specs=[pl.BlockSpec((1, blk * d, gw), lambda b, n: (b, n, 0)),
                   pl.BlockSpec((1, blk * d, gw), lambda b, n: (b, n, 0))],
        out_shape=[jax.ShapeDtypeStruct((B, S, gw), BF16), jax.ShapeDtypeStruct((B, S, gw), F32)],
        scratch_shapes=[pltpu.VMEM((gw // LANES, blk * d, LANES), F32),
                        pltpu.VMEM((gw // LANES, blk * d, LANES), F32)],
        compiler_params=_cparams("parallel", "parallel"),
        name=f"attn_d{d}",
    )(qkv_t, qkv_t, bias)


def _attn_group(qkv, bias, *, blk, d, gw, q_blk, k_blk, v_blk, row_blk):
    B, L, _ = qkv.shape
    nb = L // blk

    def cur(col):
        return pl.BlockSpec((1, blk, gw), lambda b, r, n: (b, n, r * row_blk + col))

    def prev(col):
        return pl.BlockSpec((1, blk, gw), lambda b, r, n: (b, jnp.maximum(n - 1, 0), r * row_blk + col))

    return pl.pallas_call(
        _attn_body,
        grid=(B, d, nb),
        in_specs=[cur(q_blk), prev(k_blk), cur(k_blk), prev(v_blk), cur(v_blk),
                  pl.BlockSpec((1,) + bias.shape[1:], lambda b, r, n: (jnp.minimum(n, 1), 0, 0, 0))],
        out_specs=[pl.BlockSpec((1, blk, gw), lambda b, r, n: (b, n, r)),
                   pl.BlockSpec((1, blk, gw), lambda b, r, n: (b, n, r))],
        out_shape=[jax.ShapeDtypeStruct((B, L, d * gw), BF16),
                   jax.ShapeDtypeStruct((B, L, d * gw), F32)],
        compiler_params=_cparams("parallel", "parallel", "parallel"),
        name=f"attn_d{d}",
    )(qkv, qkv, qkv, qkv, qkv, bias)


def _merge_body(yr_ref, o1_ref, o2_ref, o3_ref, l1_ref, l2_ref, l3_ref, g_ref, x_ref,
                wr_ref, wa_ref, wo_ref, lg_ref, lb_ref, wrt_ref,
                x1_ref, x1p_ref, lt_ref, *, alpha):
    l1, l2, l3 = l1_ref[...], l2_ref[...], l3_ref[...]
    mx = jnp.maximum(jnp.maximum(l1, l2), l3)
    w1, w2, w3 = jnp.exp(l1 - mx), jnp.exp(l2 - mx), jnp.exp(l3 - mx)
    y_att = (w1 * o1_ref[...].astype(F32) + w2 * o2_ref[...].astype(F32)
             + w3 * o3_ref[...].astype(F32)) / (w1 + w2 + w3)
    pr = jnp.dot(yr_ref[...], wr_ref[...], preferred_element_type=F32)
    pa = jnp.dot(y_att.astype(BF16), wa_ref[...], preferred_element_type=F32)
    dm = pr.shape[1]
    g = g_ref[...].astype(F32)
    merged = _sigmoid(g[:, :dm]) * pr + _sigmoid(g[:, dm:]) * pa
    mix = jnp.dot(merged.astype(BF16), wo_ref[...], preferred_element_type=F32)
    z = alpha * x_ref[...] + mix
    mu = jnp.mean(z, axis=-1, keepdims=True)
    zc = z - mu
    var = jnp.mean(zc * zc, axis=-1, keepdims=True)
    x1 = zc * lax.rsqrt(var + LN_EPS) * lg_ref[...] + lb_ref[...]
    x1_ref[...] = x1
    x1p_ref[...] = _pack_rows(x1)
    lt_ref[...] = lax.dot_general(wrt_ref[...], x1, (((1,), (1,)), ((), ())),
                                  precision=lax.Precision.HIGHEST, preferred_element_type=F32)


def _merge(y_rnn, os_, lses, proj, gate_blk, x2, wr, wa, wo, ln_g, ln_b, w_router_t, *, alpha, tm=512):
    T, D = x2.shape
    da = os_[0].shape[1]
    E = w_router_t.shape[0]
    row = lambda w: pl.BlockSpec((tm, w), lambda i: (i, 0))
    full = lambda a: pl.BlockSpec(a.shape, lambda i: (0,) * a.ndim)
    return pl.pallas_call(
        functools.partial(_merge_body, alpha=alpha),
        grid=(T // tm,),
        in_specs=[row(D), row(da), row(da), row(da), row(da), row(da), row(da),
                  pl.BlockSpec((tm, 2 * D), lambda i: (i, gate_blk)), row(D),
                  full(wr), full(wa), full(wo), full(ln_g), full(ln_b), full(w_router_t)],
        out_specs=[row(D), row(D // 2), pl.BlockSpec((E, tm), lambda i: (0, i))],
        out_shape=[jax.ShapeDtypeStruct((T, D), F32), jax.ShapeDtypeStruct((T, D // 2), U32),
                   jax.ShapeDtypeStruct((E, T), F32)],
        compiler_params=_cparams("parallel"),
        name="merge",
    )(y_rnn, *os_, *lses, proj, x2, wr, wa, wo, ln_g, ln_b, w_router_t)


def _first_max(vals, idx, big):
    m = jnp.max(vals, axis=0, keepdims=True)
    i = jnp.min(jnp.where(vals == m, idx, big), axis=0, keepdims=True)
    return m, i


def _route_body(lt_ref, rb_ref, e_ref, g_ref, r_ref, cnt_ref, *, tl):
    E = lt_ref.shape[0]
    per = E // N_EXPERT_GROUPS
    neg = -jnp.inf

    @pl.when(pl.program_id(0) == 0)
    def _():
        cnt_ref[...] = jnp.zeros_like(cnt_ref)

    scores = jax.nn.sigmoid(lt_ref[...])
    sel = scores + rb_ref[...]
    rowi = lax.broadcasted_iota(I32, (E, tl), 0)

    gi = lax.broadcasted_iota(I32, (N_EXPERT_GROUPS, tl), 0)
    gsc = jnp.zeros((N_EXPERT_GROUPS, tl), F32)
    for g in range(N_EXPERT_GROUPS):
        blk = sel[g * per:(g + 1) * per]
        ri = lax.broadcasted_iota(I32, (per, tl), 0) + g * per
        m1, i1 = _first_max(blk, ri, E)
        m2 = jnp.max(jnp.where(ri == i1, neg, blk), axis=0, keepdims=True)
        gsc = jnp.where(gi == g, m1 + m2, gsc)
    keep = jnp.zeros((N_EXPERT_GROUPS, tl), F32)
    for _ in range(TOPK_GROUPS):
        _, ig = _first_max(gsc, gi, N_EXPERT_GROUPS)
        hit = gi == ig
        keep = jnp.where(hit, 1.0, keep)
        gsc = jnp.where(hit, neg, gsc)
    cur = jnp.concatenate(
        [jnp.where(keep[g:g + 1] > 0.5, sel[g * per:(g + 1) * per], neg) for g in range(N_EXPERT_GROUPS)],
        axis=0)

    ti = lax.broadcasted_iota(I32, (tl, tl), 0)
    tj = lax.broadcasted_iota(I32, (tl, tl), 1)
    earlier = jnp.where(ti < tj, 1.0, 0.0).astype(BF16)

    es, gv = [], []
    onehot = jnp.zeros((E, tl), F32)
    for k in range(TOP_K):
        _, ie = _first_max(cur, rowi, E)
        hit = rowi == ie
        es.append(ie)
        gv.append(jnp.sum(jnp.where(hit, scores, 0.0), axis=0, keepdims=True))
        onehot = jnp.where(hit, 1.0, onehot)
        cur = jnp.where(hit, neg, cur)
    gsum = gv[0]
    for k in range(1, TOP_K):
        gsum = gsum + gv[k]
    ranks = jnp.dot(onehot.astype(BF16), earlier, preferred_element_type=F32) + cnt_ref[...]
    for k in range(TOP_K):
        e_ref[k:k + 1, :] = es[k]
        g_ref[k:k + 1, :] = gv[k] / gsum * ROUTED_SCALE
        r_ref[k:k + 1, :] = jnp.sum(jnp.where(rowi == es[k], ranks, 0.0), axis=0, keepdims=True).astype(I32)
    cnt_ref[...] += jnp.sum(onehot, axis=1, keepdims=True)


def _route(logits_t, router_bias, *, tl=256):
    E, T = logits_t.shape
    kt = pl.BlockSpec((TOP_K, tl), lambda i: (0, i))
    return pl.pallas_call(
        functools.partial(_route_body, tl=tl),
        grid=(T // tl,),
        in_specs=[pl.BlockSpec((E, tl), lambda i: (0, i)), pl.BlockSpec((E, 1), lambda i: (0, 0))],
        out_specs=[kt, kt, kt, pl.BlockSpec((E, 1), lambda i: (0, 0))],
        out_shape=[jax.ShapeDtypeStruct((TOP_K, T), I32), jax.ShapeDtypeStruct((TOP_K, T), F32),
                   jax.ShapeDtypeStruct((TOP_K, T), I32), jax.ShapeDtypeStruct((E, 1), F32)],
        compiler_params=_cparams("arbitrary"),
        name="route",
    )(logits_t, router_bias.reshape(E, 1))


def _dest_body(e_ref, r_ref, ss_ref, d_ref):
    E = ss_ref.shape[0]
    tl = e_ref.shape[1]
    rowi = lax.broadcasted_iota(I32, (E, tl), 0)
    e = e_ref[...]
    rows = [jnp.sum(jnp.where(rowi == e[k:k + 1], ss_ref[...], 0), axis=0, keepdims=True)
            for k in range(e.shape[0])]
    d_ref[...] = jnp.concatenate(rows, axis=0) + r_ref[...]


def _dest(e_idx, rank, seg_start, *, tl=512):
    K, T = e_idx.shape
    E = seg_start.shape[0]
    kt = pl.BlockSpec((K, tl), lambda i: (0, i))
    return pl.pallas_call(
        _dest_body, grid=(T // tl,),
        in_specs=[kt, kt, pl.BlockSpec((E, 1), lambda i: (0, 0))],
        out_specs=kt, out_shape=jax.ShapeDtypeStruct((K, T), I32),
        compiler_params=_cparams("parallel"), name="dest",
    )(e_idx, rank, seg_start.reshape(E, 1))


def _pack_rows(v):
    w = v.shape[1] // 2
    lo = pltpu.bitcast(v[:, :w].astype(BF16).astype(F32), U32) >> 16
    hi = pltpu.bitcast(v[:, w:].astype(BF16).astype(F32), U32) & jnp.uint32(0xFFFF0000)
    return lo | hi


def _unpack_rows(p):
    lo = pltpu.bitcast(p << 16, F32)
    hi = pltpu.bitcast(p & jnp.uint32(0xFFFF0000), F32)
    return lo, hi


def _dispatch_body(ps_ref, pc_ref, nu_ref, dest_hbm, x_ref, xs_hbm, dsm, zrow, sem_i, sem_r, sem_z, sem_c, sem_t,
                   *, tl, K):
    i = pl.program_id(0)
    n = pl.num_programs(0)
    R = zrow.shape[0]

    def pad_copy(row):
        return pltpu.make_async_copy(zrow.at[pl.ds(0, 1), :], xs_hbm.at[pl.ds(row, 1), :], sem_z)

    idx_cp = pltpu.make_async_copy(dest_hbm.at[i], dsm, sem_i)
    idx_cp.start()
    zrow[...] = jnp.zeros_like(zrow)
    idx_cp.wait()

    n_blk = xs_hbm.shape[0] // R
    tail_per_step = -(-n_blk // n)

    def tail(j, c, wait):
        blk = i * tail_per_step + j

        @pl.when(jnp.logical_and(blk >= nu_ref[0], blk < n_blk))
        def _():
            cp = pltpu.make_async_copy(zrow, xs_hbm.at[pl.ds(pl.multiple_of(blk * R, R), R), :], sem_t)
            cp.wait() if wait else cp.start()

        return c

    lax.fori_loop(0, tail_per_step, functools.partial(tail, wait=False), 0)

    def issue(t, c):
        src = x_ref.at[pl.ds(t, 1), :]
        for k in range(K):
            pltpu.make_async_copy(src, xs_hbm.at[pl.ds(dsm[k * tl + t], 1), :], sem_r).start(priority=k % 2)
        return c

    lax.fori_loop(0, tl, issue, 0, unroll=2)

    E = ps_ref.shape[0]
    per_step = -(-E // n)

    def pads(j, c, wait):
        e = jnp.minimum(i * per_step + j, E - 1)
        cnt = jnp.where(i * per_step + j < E, pc_ref[e], 0)
        start = ps_ref[e]
        n_single = jnp.minimum(cnt, (-start) & (SUBLANES - 1))
        start8 = start + n_single

        def one(r, c2):
            cp = pad_copy(start + r)
            cp.wait() if wait else cp.start()
            return c2

        def eight(r, c2):
            row = pl.multiple_of(start8 + r * SUBLANES, SUBLANES)
            cp = pltpu.make_async_copy(zrow.at[pl.ds(0, SUBLANES), :], xs_hbm.at[pl.ds(row, SUBLANES), :], sem_c)
            cp.wait() if wait else cp.start()
            return c2

        c = lax.fori_loop(0, n_single, one, c)
        return lax.fori_loop(0, lax.shift_right_logical(cnt - n_single, 3), eight, c)

    lax.fori_loop(0, per_step, functools.partial(pads, wait=False), 0)
    for k in range(K):
        pltpu.make_async_copy(x_ref, xs_hbm.at[pl.ds(0, tl), :], sem_r).wait()
    lax.fori_loop(0, per_step, functools.partial(pads, wait=True), 0)
    lax.fori_loop(0, tail_per_step, functools.partial(tail, wait=True), 0)


def _dispatch(xp, dest_t, pad_start, pad_cnt, n_used, n_rows, *, tl):
    T, W = xp.shape
    K = dest_t.shape[1] // tl
    any_spec = pl.BlockSpec(memory_space=pl.ANY)
    return pl.pallas_call(
        functools.partial(_dispatch_body, tl=tl, K=K),
        grid_spec=pltpu.PrefetchScalarGridSpec(
            num_scalar_prefetch=3, grid=(T // tl,),
            in_specs=[any_spec, pl.BlockSpec((tl, W), lambda i, ps, pc, nu: (i, 0))],
            out_specs=any_spec,
            scratch_shapes=[pltpu.SMEM((K * tl,), I32), pltpu.VMEM((EXPERT_ROWS, W), U32),
                            pltpu.SemaphoreType.DMA, pltpu.SemaphoreType.DMA, pltpu.SemaphoreType.DMA,
                            pltpu.SemaphoreType.DMA, pltpu.SemaphoreType.DMA]),
        out_shape=jax.ShapeDtypeStruct((n_rows, W), U32),
        compiler_params=_cparams("arbitrary"),
        name="dispatch",
    )(pad_start, pad_cnt, n_used, dest_t, xp)


def _experts_body(be_ref, nu_ref, xs_ref, wg_ref, wu_ref, wd_ref, ys_ref, wgu_s, wd_s):
    j = pl.program_id(0)
    de = wg_ref.shape[2]
    changed = jnp.logical_or(j == 0, be_ref[j] != be_ref[jnp.maximum(j - 1, 0)])

    @pl.when(changed)
    def _():
        wgu_s[:, :de] = wg_ref[0].astype(BF16)
        wgu_s[:, de:] = wu_ref[0].astype(BF16)
        wd_s[...] = wd_ref[0].astype(BF16)

    @pl.when(j < nu_ref[0])
    def _():
        lo, hi = _unpack_rows(xs_ref[...])
        xb = jnp.concatenate([lo.astype(BF16), hi.astype(BF16)], axis=1)
        h = jnp.dot(xb, wgu_s[...], preferred_element_type=F32)
        hg = h[:, :de]
        act = (hg * _sigmoid(hg) * h[:, de:]).astype(BF16)
        ys_ref[...] = _pack_rows(jnp.dot(act, wd_s[...], preferred_element_type=F32))

    @pl.when(j >= nu_ref[0])
    def _():
        ys_ref[...] = jnp.zeros_like(ys_ref)


def _experts(xs, blk_e, n_used, wg, wu, wd):
    n_rows, W = xs.shape
    R = EXPERT_ROWS
    D, de = wg.shape[1], wg.shape[2]
    xmap = lambda j, be, nu: (jnp.minimum(j, nu[0] - 1), 0)
    wmap = lambda j, be, nu: (be[j], 0, 0)
    return pl.pallas_call(
        _experts_body,
        grid_spec=pltpu.PrefetchScalarGridSpec(
            num_scalar_prefetch=2, grid=(n_rows // R,),
            in_specs=[pl.BlockSpec((R, W), xmap),
                      pl.BlockSpec((1, D, de), wmap), pl.BlockSpec((1, D, de), wmap),
                      pl.BlockSpec((1, de, D), wmap)],
            out_specs=pl.BlockSpec((R, W), lambda j, be, nu: (j, 0)),
            scratch_shapes=[pltpu.VMEM((D, 2 * de), BF16), pltpu.VMEM((de, D), BF16)]),
        out_shape=jax.ShapeDtypeStruct((n_rows, W), U32),
        compiler_params=_cparams("arbitrary"),
        name="experts",
    )(blk_e, n_used, xs, wg, wu, wd)


def _combine_body(dest_hbm, ys_hbm, gate_ref, x1_ref, wgu_ref, wd_ref, lg_ref, lb_ref, o_ref,
                  dsm, buf, sem_i, sem_r, *, tc, K, alpha):
    i = pl.program_id(0)
    idx_cp = pltpu.make_async_copy(dest_hbm.at[i], dsm, sem_i)
    idx_cp.start()
    idx_cp.wait()

    def issue(t, c):
        for k in range(K):
            r = k * tc + t
            pltpu.make_async_copy(ys_hbm.at[pl.ds(dsm[r], 1), :], buf.at[pl.ds(r, 1), :],
                                  sem_r).start(priority=k % 2)
        return c

    lax.fori_loop(0, tc, issue, 0, unroll=2)

    x1 = x1_ref[...]
    ds_ = wd_ref.shape[0]
    h = jnp.dot(x1.astype(BF16), wgu_ref[...], preferred_element_type=F32)
    hg = h[:, :ds_]
    act = (hg * _sigmoid(hg) * h[:, ds_:]).astype(BF16)
    shared = jnp.dot(act, wd_ref[...], preferred_element_type=F32)

    pltpu.make_async_copy(ys_hbm.at[pl.ds(0, K * tc), :], buf, sem_r).wait()
    g = gate_ref[...]
    lo_acc = hi_acc = None
    for k in range(K):
        lo, hi = _unpack_rows(buf[k * tc:(k + 1) * tc, :])
        gk = g[:, k:k + 1]
        lo_acc = gk * lo if k == 0 else lo_acc + gk * lo
        hi_acc = gk * hi if k == 0 else hi_acc + gk * hi
    routed = jnp.concatenate([lo_acc, hi_acc], axis=1)

    z = alpha * x1 + (routed + shared)
    mu = jnp.mean(z, axis=-1, keepdims=True)
    zc = z - mu
    var = jnp.mean(zc * zc, axis=-1, keepdims=True)
    o_ref[...] = zc * lax.rsqrt(var + LN_EPS) * lg_ref[...] + lb_ref[...]


def _combine(ys, dest_t, gate_tk, x1, wgu, wd, ln_g, ln_b, *, alpha, tc):
    T, D = x1.shape
    K = gate_tk.shape[1]
    W = ys.shape[1]
    any_spec = pl.BlockSpec(memory_space=pl.ANY)
    full = lambda a: pl.BlockSpec(a.shape, lambda i: (0,) * a.ndim)
    return pl.pallas_call(
        functools.partial(_combine_body, tc=tc, K=K, alpha=alpha),
        grid=(T // tc,),
        in_specs=[any_spec, any_spec, pl.BlockSpec((tc, K), lambda i: (i, 0)),
                  pl.BlockSpec((tc, D), lambda i: (i, 0)), full(wgu), full(wd), full(ln_g), full(ln_b)],
        out_specs=pl.BlockSpec((tc, D), lambda i: (i, 0)),
        out_shape=jax.ShapeDtypeStruct((T, D), F32),
        scratch_shapes=[pltpu.SMEM((K * tc,), I32), pltpu.VMEM((K * tc, W), U32),
                        pltpu.SemaphoreType.DMA, pltpu.SemaphoreType.DMA],
        compiler_params=_cparams("arbitrary"),
        name="combine",
    )(dest_t, ys, gate_tk, x1, wgu, wd, ln_g, ln_b)


def _mixer_ln1(x, w_in, b_in, conv_w, conv_b, w_rg_a, b_rg_a, w_rg_i, b_rg_i, lru_lambda,
               w_proj_rnn, w_proj_att, rel_bias, w_out, ln1_g, ln1_b, w_router, alpha):
    B, S, D = x.shape
    T = B * S
    d_rnn = conv_w.shape[-1]
    gw = HEADS_PER_GROUP * HEAD_DIM
    d_att = gw * len(DILATED_GROUPS)
    a0 = 2 * d_rnn
    a1 = a0 + 3 * d_att
    head = lambda j, g: slice(a0 + j * d_att + g * gw, a0 + j * d_att + (g + 1) * gw)
    plain = [g for g, (_, d) in enumerate(DILATED_GROUPS) if d == 1]
    dilated = [g for g, (_, d) in enumerate(DILATED_GROUPS) if d > 1]
    order = ([slice(0, a0), slice(a1, None)] + [head(j, g) for g in plain for j in range(3)]
             + [head(j, g) for g in dilated for j in (1, 2, 0)])
    perm = lambda w: jnp.concatenate([w[..., s] for s in order], axis=-1)
    w_p = perm(w_in).astype(BF16)
    b_p = perm(b_in).reshape(1, -1)
    qkv0 = a0 + 2 * D
    n_main = qkv0 + 3 * gw * len(plain)
    x2 = x.reshape(T, D)

    proj, *qkv_dil = _in_proj(x2, w_p, b_p, n_main, tuple(DILATED_GROUPS[g][1] for g in dilated))
    proj3 = proj.reshape(B, S, n_main)
    y_rnn = _rnn(proj3, conv_w, conv_b, w_rg_a, b_rg_a, w_rg_i, b_rg_i, lru_lambda)

    os_, lses = [], []
    for g, (window, d) in enumerate(DILATED_GROUPS):
        blk = window // d
        bias = _attn_bias(rel_bias[:, g * HEADS_PER_GROUP:(g + 1) * HEADS_PER_GROUP], window, d)
        if d == 1:
            c0 = (qkv0 + 3 * gw * plain.index(g)) // gw
            o, lse = _attn_group(proj3, bias, blk=blk, d=1, gw=gw, q_blk=c0, k_blk=c0 + 1, v_blk=c0 + 2,
                                 row_blk=n_main // gw)
        else:
            o, lse = _attn_dilated(qkv_dil[dilated.index(g)], bias, B, blk=blk, d=d, gw=gw)
        os_.append(o.reshape(T, gw))
        lses.append(lse.reshape(T, gw))

    return _merge(y_rnn.reshape(T, d_rnn), os_, lses, proj, a0 // (2 * D), x2,
                  w_proj_rnn.astype(BF16), w_proj_att.astype(BF16), w_out.astype(BF16),
                  ln1_g.reshape(1, D), ln1_b.reshape(1, D), w_router.T, alpha=alpha)


def _moe_ln2(x1, x1p, logits_t, router_bias, w_exp_gate, w_exp_up, w_exp_down,
             w_sh_gate, w_sh_up, w_sh_down, ln2_g, ln2_b, alpha):
    T, D = x1.shape
    E = logits_t.shape[0]
    R = EXPERT_ROWS
    n_blk = T * TOP_K // R + E
    n_rows = n_blk * R

    e_idx, gate, rank, counts = _route(logits_t, router_bias)
    counts = counts.reshape(E).astype(I32)
    padded = (counts + R - 1) // R * R
    pad_end = jnp.cumsum(padded)
    seg_start = pad_end - padded
    n_used = (pad_end[-1:] // R).astype(I32)
    blk_e = jnp.minimum(jnp.sum(jnp.arange(n_blk, dtype=I32)[:, None] >= (pad_end // R)[None, :], axis=1),
                        E - 1).astype(I32)
    dest = _dest(e_idx, rank, seg_start)
    tl = ROW_TILE
    dest_t = dest.reshape(TOP_K, T // tl, tl).transpose(1, 0, 2).reshape(T // tl, TOP_K * tl)

    xs = _dispatch(x1p, dest_t, seg_start + counts, padded - counts, n_used, n_rows, tl=tl)
    ys = _experts(xs, blk_e, n_used, w_exp_gate, w_exp_up, w_exp_down)
    wgu = jnp.concatenate([w_sh_gate, w_sh_up], axis=-1).astype(BF16)
    return _combine(ys, dest_t, gate.T, x1, wgu, w_sh_down.astype(BF16),
                    ln2_g.reshape(1, D), ln2_b.reshape(1, D), alpha=alpha, tc=tl)


def kernel(x, w_in, b_in, conv_w, conv_b, w_rg_a, b_rg_a, w_rg_i, b_rg_i, lru_lambda,
           w_proj_rnn, w_proj_att, rel_bias, w_out, ln1_g, ln1_b, w_router, router_bias,
           w_exp_gate, w_exp_up, w_exp_down, w_sh_gate, w_sh_up, w_sh_down, ln2_g, ln2_b):
    B, S, D = x.shape
    depth = w_in.shape[0]
    alpha = (2 * depth) ** 0.25
    for i in range(depth):
        x1, x1p, logits_t = _mixer_ln1(
            x, w_in[i], b_in[i], conv_w[i], conv_b[i], w_rg_a[i], b_rg_a[i], w_rg_i[i], b_rg_i[i],
            lru_lambda[i], w_proj_rnn[i], w_proj_att[i], rel_bias, w_out[i], ln1_g[i], ln1_b[i],
            w_router[i], alpha)
        out = _moe_ln2(x1, x1p, logits_t, router_bias[i], w_exp_gate[i], w_exp_up[i], w_exp_down[i],
                       w_sh_gate[i], w_sh_up[i], w_sh_down[i], ln2_g[i], ln2_b[i], alpha)
        x = out.reshape(B, S, D)
    return x
```

```python
import functools

import numpy as np
import jax
import jax.numpy as jnp
from jax import lax
from jax.experimental import pallas as pl
from jax.experimental.pallas import tpu as pltpu

F32 = jnp.float32
BF16 = jnp.bfloat16
I32 = jnp.int32
U32 = jnp.uint32

RNN_HEADS = 16
LRU_C = 8.0
HEAD_DIM = 64
HEADS_PER_GROUP = 8
DILATED_GROUPS = ((128, 1), (512, 4), (2048, 16))
NUM_BUCKETS = 32
MAX_DISTANCE = 2048
TOP_K = 8
N_EXPERT_GROUPS = 8
TOPK_GROUPS = 4
ROUTED_SCALE = 2.5
LN_EPS = 1e-5

LANES = 128
SUBLANES = 8
VMEM_LIMIT_BYTES = 56 * 1024 * 1024

MASK_VALUE = -1e30
EXPERT_ROWS = 512
ROW_TILE = 256
SC_GATHER_CHUNK = 64


def _cparams(*sem):
    return pltpu.CompilerParams(dimension_semantics=sem, vmem_limit_bytes=VMEM_LIMIT_BYTES)


def _sigmoid(v):
    return 0.5 * (jnp.tanh(0.5 * v) + 1.0)


def _in_proj_body(x_ref, w_ref, b_ref, main_ref, *rest, tn, dils):
    dil_refs, scr = rest[:-1], rest[-1]
    xb = x_ref[...].astype(BF16)
    tm = xb.shape[0]

    def chunk(j):
        sl = slice(j * tn, (j + 1) * tn)
        return jnp.dot(xb, w_ref[:, sl], preferred_element_type=F32) + b_ref[:, sl]

    n_main = main_ref.shape[1] // tn
    for j in range(n_main):
        main_ref[:, j * tn:(j + 1) * tn] = chunk(j).astype(main_ref.dtype)
    j = n_main
    for ref, d in zip(dil_refs, dils):
        for c in range(ref.shape[3] // tn):
            acc = chunk(j)
            j += 1
            for q in range(tn // LANES):
                scr[q] = acc[:, q * LANES:(q + 1) * LANES]
            for r in range(d):
                part = jnp.concatenate([scr[q, pl.ds(r, tm // d, stride=d), :] for q in range(tn // LANES)], axis=1)
                ref[0, r, :, c * tn:(c + 1) * tn] = part.astype(ref.dtype)


def _in_proj(x2, w, b, n_main, dils, *, tm=512, tn=512):
    T, D = x2.shape
    N = w.shape[1]
    wd = (N - n_main) // len(dils)
    out_specs = [pl.BlockSpec((tm, n_main), lambda i: (i, 0))]
    out_shape = [jax.ShapeDtypeStruct((T, n_main), BF16)]
    for d in dils:
        out_specs.append(pl.BlockSpec((1, d, tm // d, wd), lambda i: (i, 0, 0, 0)))
        out_shape.append(jax.ShapeDtypeStruct((T // tm, d, tm // d, wd), BF16))
    return pl.pallas_call(
        functools.partial(_in_proj_body, tn=tn, dils=dils),
        grid=(T // tm,),
        in_specs=[pl.BlockSpec((tm, D), lambda i: (i, 0)),
                  pl.BlockSpec((D, N), lambda i: (0, 0), pipeline_mode=pl.Buffered(1)),
                  pl.BlockSpec((1, N), lambda i: (0, 0))],
        out_specs=out_specs,
        out_shape=out_shape,
        scratch_shapes=[pltpu.VMEM((tn // LANES, tm, LANES), F32)],
        compiler_params=_cparams("parallel"),
        name="in_proj",
    )(x2, w, b)


def _rnn_body(xr_ref, gr_ref, cw_ref, cb_ref, wg_ref, bg_ref, lam_ref, y_ref,
              xext, a_s, b_s, hc, *, ts, cb):
    s = pl.program_id(2)

    @pl.when(s == 0)
    def _():
        xext[0:SUBLANES, :] = jnp.zeros((SUBLANES, cb), F32)
        hc[...] = jnp.zeros_like(hc)

    xr = xr_ref[0].astype(F32)
    xext[SUBLANES:SUBLANES + ts, :] = xr
    nw = cw_ref.shape[0]
    xc = cw_ref[nw - 1:nw, :] * xr + cb_ref[...]
    for j in range(nw - 1):
        off = SUBLANES - (nw - 1 - j)
        xc = xc + cw_ref[j:j + 1, :] * xext[off:off + ts, :]
    xext[0:SUBLANES, :] = xext[ts:ts + SUBLANES, :]

    gates = jnp.dot(xc.astype(BF16), wg_ref[0], preferred_element_type=F32) + bg_ref[...]
    r = _sigmoid(gates[:, :cb])
    ig = _sigmoid(gates[:, cb:])
    nl = -lam_ref[...]
    sp = jnp.maximum(nl, 0.0) + jnp.log1p(jnp.exp(-jnp.abs(nl)))
    log_a = (-LRU_C) * r * sp
    a = jnp.exp(log_a)
    u = jnp.sqrt(1.0 - a * a) * (ig * xc)

    row = lax.broadcasted_iota(I32, (ts, cb), 0) & (SUBLANES - 1)
    av, bv = a, u
    for sft in (1, 2, 4):
        a_sh = pltpu.roll(av, sft, 0)
        b_sh = pltpu.roll(bv, sft, 0)
        m = row >= sft
        bv = jnp.where(m, av * b_sh + bv, bv)
        av = jnp.where(m, av * a_sh, av)
    a_s[...] = av
    b_s[...] = bv

    def carry(g, h):
        i0 = pl.multiple_of(g * SUBLANES, SUBLANES)
        h8 = b_s[pl.ds(i0, SUBLANES), :] + a_s[pl.ds(i0, SUBLANES), :] * h
        b_s[pl.ds(i0, SUBLANES), :] = h8
        return h8[SUBLANES - 1:SUBLANES, :]

    hc[0:1, :] = lax.fori_loop(0, ts // SUBLANES, carry, hc[0:1, :], unroll=8)
    gr = gr_ref[0].astype(F32)
    y_ref[0] = (b_s[...] * jax.nn.gelu(gr)).astype(y_ref.dtype)


def _block_diag(w, per):
    H, d, _ = w.shape
    w4 = w.reshape(H // per, per, d, d)
    out = jnp.einsum('gpij,pq->gpiqj', w4, jnp.eye(per, dtype=w.dtype))
    return out.reshape(H // per, per * d, per * d)


def _rnn(proj3, conv_w, conv_b, w_rg_a, b_rg_a, w_rg_i, b_rg_i, lam, *, ts=512, cb=256):
    B, S, _ = proj3.shape
    d_rnn = conv_w.shape[-1]
    nc = d_rnn // cb
    per = cb // (d_rnn // RNN_HEADS)
    wg = jnp.concatenate([_block_diag(w_rg_a, per), _block_diag(w_rg_i, per)], axis=-1).astype(BF16)
    bg = jnp.concatenate([b_rg_a.reshape(nc, 1, cb), b_rg_i.reshape(nc, 1, cb)], axis=-1)
    return pl.pallas_call(
        functools.partial(_rnn_body, ts=ts, cb=cb),
        grid=(B, nc, S // ts),
        in_specs=[pl.BlockSpec((1, ts, cb), lambda b, c, s: (b, s, c)),
                  pl.BlockSpec((1, ts, cb), lambda b, c, s: (b, s, nc + c)),
                  pl.BlockSpec((conv_w.shape[0], cb), lambda b, c, s: (0, c)),
                  pl.BlockSpec((1, cb), lambda b, c, s: (0, c)),
                  pl.BlockSpec((1, cb, 2 * cb), lambda b, c, s: (c, 0, 0)),
                  pl.BlockSpec((None, 1, 2 * cb), lambda b, c, s: (c, 0, 0)),
                  pl.BlockSpec((1, cb), lambda b, c, s: (0, c))],
        out_specs=pl.BlockSpec((1, ts, cb), lambda b, c, s: (b, s, c)),
        out_shape=jax.ShapeDtypeStruct((B, S, d_rnn), BF16),
        scratch_shapes=[pltpu.VMEM((ts + SUBLANES, cb), F32), pltpu.VMEM((ts, cb), F32),
                        pltpu.VMEM((ts, cb), F32), pltpu.VMEM((SUBLANES, cb), F32)],
        compiler_params=_cparams("parallel", "parallel", "arbitrary"),
        name="rnn",
    )(proj3, proj3, conv_w, conv_b.reshape(1, d_rnn), wg, bg, lam.reshape(1, d_rnn))


def _t5_bucket(dist):
    max_exact = NUM_BUCKETS // 2
    d = np.maximum(dist, 1).astype(np.float64)
    large = max_exact + (np.log(d / max_exact) / np.log(MAX_DISTANCE / max_exact)
                         * (NUM_BUCKETS - max_exact)).astype(np.int64)
    large = np.minimum(large, NUM_BUCKETS - 1)
    return np.where(dist < max_exact, dist, large).astype(np.int32)


def _attn_bias(table, window, dilation):
    blk = window // dilation
    qi = np.arange(blk)[:, None]
    ki = np.arange(2 * blk)[None, :]
    rel = qi + blk - ki
    in_window = (rel >= 0) & (rel <= blk)
    bucket = _t5_bucket(np.clip(rel, 0, None) * dilation)
    onehot = (bucket[..., None] == np.arange(NUM_BUCKETS)).astype(np.float32)
    bias = jnp.einsum('qkn,nh->hqk', onehot, table.astype(F32), precision=lax.Precision.HIGHEST)
    first = in_window & (ki >= blk)
    return jnp.stack([jnp.where(first[None], bias, MASK_VALUE),
                      jnp.where(in_window[None], bias, MASK_VALUE)])


def _attn_heads(q, kp, kc, vp, vc, bias_ref):
    blk, gw = q.shape
    lo = lax.broadcasted_iota(I32, (blk, LANES), 1) < HEAD_DIM
    scale = HEAD_DIM ** -0.5
    o_parts, l_parts = [], []
    for p in range(gw // LANES):
        sl = slice(p * LANES, (p + 1) * LANES)
        q2 = q[:, sl] * scale
        k2 = jnp.concatenate([kp[:, sl], kc[:, sl]], axis=0)
        v2 = jnp.concatenate([vp[:, sl], vc[:, sl]], axis=0)
        outs, lses = [], []
        for hh in range(2):
            qm = jnp.where(lo if hh == 0 else jnp.logical_not(lo), q2, jnp.zeros_like(q2))
            sc = lax.dot_general(qm, k2, (((1,), (1,)), ((), ())), preferred_element_type=F32)
            sc = sc + bias_ref[0, 2 * p + hh]
            m = jnp.max(sc, axis=-1, keepdims=True)
            e = jnp.exp(sc - m)
            l = jnp.sum(e, axis=-1, keepdims=True)
            o = jnp.dot(e.astype(BF16), v2, preferred_element_type=F32)
            outs.append(o / l)
            lses.append(jnp.broadcast_to(m + jnp.log(l), (blk, LANES)))
        o_parts.append(jnp.where(lo, outs[0], outs[1]))
        l_parts.append(jnp.where(lo, lses[0], lses[1]))
    return jnp.concatenate(o_parts, axis=1), jnp.concatenate(l_parts, axis=1)


def _attn_body(q_ref, kp_ref, kc_ref, vp_ref, vc_ref, bias_ref, o_ref, lse_ref):
    o, lse = _attn_heads(q_ref[0], kp_ref[0], kc_ref[0], vp_ref[0], vc_ref[0], bias_ref)
    o_ref[0] = o.astype(o_ref.dtype)
    lse_ref[0] = lse


def _attn_dil_body(cur_ref, prev_ref, bias_ref, o_ref, lse_ref, oscr, lscr, *, d, blk, gw):
    nt = cur_ref.shape[0]
    nq = gw // LANES

    def rows(ref, r, c):
        parts = [ref[j, r, :, c * gw:(c + 1) * gw] for j in range(nt)]
        return parts[0] if nt == 1 else jnp.concatenate(parts, axis=0)

    def residue(r, carry):
        o, lse = _attn_heads(rows(cur_ref, r, 2), rows(prev_ref, r, 0), rows(cur_ref, r, 0),
                             rows(prev_ref, r, 1), rows(cur_ref, r, 1), bias_ref)
        for q in range(nq):
            oscr[q, pl.ds(r, blk, stride=d), :] = o[:, q * LANES:(q + 1) * LANES]
            lscr[q, pl.ds(r, blk, stride=d), :] = lse[:, q * LANES:(q + 1) * LANES]
        return carry

    lax.fori_loop(0, d, residue, 0)
    for q in range(nq):
        o_ref[0, :, q * LANES:(q + 1) * LANES] = oscr[q].astype(o_ref.dtype)
        lse_ref[0, :, q * LANES:(q + 1) * LANES] = lscr[q]


def _attn_dilated(qkv_t, bias, B, *, blk, d, gw):
    n_tiles, _, rows_t, _ = qkv_t.shape
    tm = rows_t * d
    nt = blk * d // tm
    nb = n_tiles // (B * nt)
    S = n_tiles * tm // B
    return pl.pallas_call(
        functools.partial(_attn_dil_body, d=d, blk=blk, gw=gw),
        grid=(B, nb),
        in_specs=[pl.BlockSpec((nt, d, rows_t, 3 * gw), lambda b, n: (b * nb + n, 0, 0, 0)),
                  pl.BlockSpec((nt, d, rows_t, 2 * gw), lambda b, n: (b * nb + jnp.maximum(n - 1, 0), 0, 0, 0)),
                  pl.BlockSpec((1,) + bias.shape[1:], lambda b, n: (jnp.minimum(n, 1), 0, 0, 0))],
        out_specs=[pl.BlockSpec((1, blk * d, gw), lambda b, n: (b, n, 0)),
                   pl.BlockSpec((1, blk * d, gw), lambda b, n: (b, n, 0))],
        out_shape=[jax.ShapeDtypeStruct((B, S, gw), BF16), jax.ShapeDtypeStruct((B, S, gw), F32)],
        scratch_shapes=[pltpu.VMEM((gw // LANES, blk * d, LANES), F32),
                        pltpu.VMEM((gw // LANES, blk * d, LANES), F32)],
        compiler_params=_cparams("parallel", "parallel"),
        name=f"attn_d{d}",
    )(qkv_t, qkv_t, bias)


def _attn_group(qkv, bias, *, blk, d, gw, q_blk, k_blk, v_blk, row_blk):
    B, L, _ = qkv.shape
    nb = L // blk

    def cur(col):
        return pl.BlockSpec((1, blk, gw), lambda b, r, n: (b, n, r * row_blk + col))

    def prev(col):
        return pl.BlockSpec((1, blk, gw), lambda b, r, n: (b, jnp.maximum(n - 1, 0), r * row_blk + col))

    return pl.pallas_call(
        _attn_body,
        grid=(B, d, nb),
        in_specs=[cur(q_blk), prev(k_blk), cur(k_blk), prev(v_blk), cur(v_blk),
                  pl.BlockSpec((1,) + bias.shape[1:], lambda b, r, n: (jnp.minimum(n, 1), 0, 0, 0))],
        out_specs=[pl.BlockSpec((1, blk, gw), lambda b, r, n: (b, n, r)),
                   pl.BlockSpec((1, blk, gw), lambda b, r, n: (b, n, r))],
        out_shape=[jax.ShapeDtypeStruct((B, L, d * gw), BF16),
                   jax.ShapeDtypeStruct((B, L, d * gw), F32)],
        compiler_params=_cparams("parallel", "parallel", "parallel"),
        name=f"attn_d{d}",
    )(qkv, qkv, qkv, qkv, qkv, bias)


def _merge_body(yr_ref, o1_ref, o2_ref, o3_ref, l1_ref, l2_ref, l3_ref, g_ref, x_ref,
                wr_ref, wa_ref, wo_ref, lg_ref, lb_ref, wrt_ref,
                x1_ref, x1p_ref, lt_ref, *, alpha):
    l1, l2, l3 = l1_ref[...], l2_ref[...], l3_ref[...]
    mx = jnp.maximum(jnp.maximum(l1, l2), l3)
    w1, w2, w3 = jnp.exp(l1 - mx), jnp.exp(l2 - mx), jnp.exp(l3 - mx)
    y_att = (w1 * o1_ref[...].astype(F32) + w2 * o2_ref[...].astype(F32)
             + w3 * o3_ref[...].astype(F32)) / (w1 + w2 + w3)
    pr = jnp.dot(yr_ref[...], wr_ref[...], preferred_element_type=F32)
    pa = jnp.dot(y_att.astype(BF16), wa_ref[...], preferred_element_type=F32)
    dm = pr.shape[1]
    g = g_ref[...].astype(F32)
    merged = _sigmoid(g[:, :dm]) * pr + _sigmoid(g[:, dm:]) * pa
    mix = jnp.dot(merged.astype(BF16), wo_ref[...], preferred_element_type=F32)
    z = alpha * x_ref[...] + mix
    mu = jnp.mean(z, axis=-1, keepdims=True)
    zc = z - mu
    var = jnp.mean(zc * zc, axis=-1, keepdims=True)
    x1 = zc * lax.rsqrt(var + LN_EPS) * lg_ref[...] + lb_ref[...]
    x1_ref[...] = x1
    x1p_ref[...] = _pack_rows(x1)
    lt_ref[...] = lax.dot_general(wrt_ref[...], x1, (((1,), (1,)), ((), ())),
                                  precision=lax.Precision.HIGHEST, preferred_element_type=F32)


def _merge(y_rnn, os_, lses, proj, gate_blk, x2, wr, wa, wo, ln_g, ln_b, w_router_t, *, alpha, tm=512):
    T, D = x2.shape
    da = os_[0].shape[1]
    E = w_router_t.shape[0]
    row = lambda w: pl.BlockSpec((tm, w), lambda i: (i, 0))
    full = lambda a: pl.BlockSpec(a.shape, lambda i: (0,) * a.ndim)
    return pl.pallas_call(
        functools.partial(_merge_body, alpha=alpha),
        grid=(T // tm,),
        in_specs=[row(D), row(da), row(da), row(da), row(da), row(da), row(da),
                  pl.BlockSpec((tm, 2 * D), lambda i: (i, gate_blk)), row(D),
                  full(wr), full(wa), full(wo), full(ln_g), full(ln_b), full(w_router_t)],
        out_specs=[row(D), row(D // 2), pl.BlockSpec((E, tm), lambda i: (0, i))],
        out_shape=[jax.ShapeDtypeStruct((T, D), F32), jax.ShapeDtypeStruct((T, D // 2), U32),
                   jax.ShapeDtypeStruct((E, T), F32)],
        compiler_params=_cparams("parallel"),
        name="merge",
    )(y_rnn, *os_, *lses, proj, x2, wr, wa, wo, ln_g, ln_b, w_router_t)


def _first_max(vals, idx, big):
    m = jnp.max(vals, axis=0, keepdims=True)
    i = jnp.min(jnp.where(vals == m, idx, big), axis=0, keepdims=True)
    return m, i


def _route_body(lt_ref, rb_ref, e_ref, g_ref, r_ref, cnt_ref, *, tl):
    E = lt_ref.shape[0]
    per = E // N_EXPERT_GROUPS
    neg = -jnp.inf

    @pl.when(pl.program_id(0) == 0)
    def _():
        cnt_ref[...] = jnp.zeros_like(cnt_ref)

    scores = jax.nn.sigmoid(lt_ref[...])
    sel = scores + rb_ref[...]
    rowi = lax.broadcasted_iota(I32, (E, tl), 0)

    gi = lax.broadcasted_iota(I32, (N_EXPERT_GROUPS, tl), 0)
    gsc = jnp.zeros((N_EXPERT_GROUPS, tl), F32)
    for g in range(N_EXPERT_GROUPS):
        blk = sel[g * per:(g + 1) * per]
        ri = lax.broadcasted_iota(I32, (per, tl), 0) + g * per
        m1, i1 = _first_max(blk, ri, E)
        m2 = jnp.max(jnp.where(ri == i1, neg, blk), axis=0, keepdims=True)
        gsc = jnp.where(gi == g, m1 + m2, gsc)
    keep = jnp.zeros((N_EXPERT_GROUPS, tl), F32)
    for _ in range(TOPK_GROUPS):
        _, ig = _first_max(gsc, gi, N_EXPERT_GROUPS)
        hit = gi == ig
        keep = jnp.where(hit, 1.0, keep)
        gsc = jnp.where(hit, neg, gsc)
    cur = jnp.concatenate(
        [jnp.where(keep[g:g + 1] > 0.5, sel[g * per:(g + 1) * per], neg) for g in range(N_EXPERT_GROUPS)],
        axis=0)

    ti = lax.broadcasted_iota(I32, (tl, tl), 0)
    tj = lax.broadcasted_iota(I32, (tl, tl), 1)
    earlier = jnp.where(ti < tj, 1.0, 0.0).astype(BF16)

    es, gv = [], []
    onehot = jnp.zeros((E, tl), F32)
    for k in range(TOP_K):
        _, ie = _first_max(cur, rowi, E)
        hit = rowi == ie
        es.append(ie)
        gv.append(jnp.sum(jnp.where(hit, scores, 0.0), axis=0, keepdims=True))
        onehot = jnp.where(hit, 1.0, onehot)
        cur = jnp.where(hit, neg, cur)
    gsum = gv[0]
    for k in range(1, TOP_K):
        gsum = gsum + gv[k]
    ranks = jnp.dot(onehot.astype(BF16), earlier, preferred_element_type=F32) + cnt_ref[...]
    for k in range(TOP_K):
        e_ref[k:k + 1, :] = es[k]
        g_ref[k:k + 1, :] = gv[k] / gsum * ROUTED_SCALE
        r_ref[k:k + 1, :] = jnp.sum(jnp.where(rowi == es[k], ranks, 0.0), axis=0, keepdims=True).astype(I32)
    cnt_ref[...] += jnp.sum(onehot, axis=1, keepdims=True)


def _route(logits_t, router_bias, *, tl=256):
    E, T = logits_t.shape
    kt = pl.BlockSpec((TOP_K, tl), lambda i: (0, i))
    return pl.pallas_call(
        functools.partial(_route_body, tl=tl),
        grid=(T // tl,),
        in_specs=[pl.BlockSpec((E, tl), lambda i: (0, i)), pl.BlockSpec((E, 1), lambda i: (0, 0))],
        out_specs=[kt, kt, kt, pl.BlockSpec((E, 1), lambda i: (0, 0))],
        out_shape=[jax.ShapeDtypeStruct((TOP_K, T), I32), jax.ShapeDtypeStruct((TOP_K, T), F32),
                   jax.ShapeDtypeStruct((TOP_K, T), I32), jax.ShapeDtypeStruct((E, 1), F32)],
        compiler_params=_cparams("arbitrary"),
        name="route",
    )(logits_t, router_bias.reshape(E, 1))


def _dest_body(e_ref, r_ref, ss_ref, d_ref):
    E = ss_ref.shape[0]
    tl = e_ref.shape[1]
    rowi = lax.broadcasted_iota(I32, (E, tl), 0)
    e = e_ref[...]
    rows = [jnp.sum(jnp.where(rowi == e[k:k + 1], ss_ref[...], 0), axis=0, keepdims=True)
            for k in range(e.shape[0])]
    d_ref[...] = jnp.concatenate(rows, axis=0) + r_ref[...]


def _dest(e_idx, rank, seg_start, *, tl=512):
    K, T = e_idx.shape
    E = seg_start.shape[0]
    kt = pl.BlockSpec((K, tl), lambda i: (0, i))
    return pl.pallas_call(
        _dest_body, grid=(T // tl,),
        in_specs=[kt, kt, pl.BlockSpec((E, 1), lambda i: (0, 0))],
        out_specs=kt, out_shape=jax.ShapeDtypeStruct((K, T), I32),
        compiler_params=_cparams("parallel"), name="dest",
    )(e_idx, rank, seg_start.reshape(E, 1))


def _pack_rows(v):
    w = v.shape[1] // 2
    lo = pltpu.bitcast(v[:, :w].astype(BF16).astype(F32), U32) >> 16
    hi = pltpu.bitcast(v[:, w:].astype(BF16).astype(F32), U32) & jnp.uint32(0xFFFF0000)
    return lo | hi


def _unpack_rows(p):
    lo = pltpu.bitcast(p << 16, F32)
    hi = pltpu.bitcast(p & jnp.uint32(0xFFFF0000), F32)
    return lo, hi


def _dispatch_body(ps_ref, pc_ref, nu_ref, dest_hbm, x_ref, xs_hbm, dsm, zrow, sem_i, sem_r, sem_z, sem_c, sem_t,
                   *, tl, K):
    i = pl.program_id(0)
    n = pl.num_programs(0)
    R = zrow.shape[0]

    def pad_copy(row):
        return pltpu.make_async_copy(zrow.at[pl.ds(0, 1), :], xs_hbm.at[pl.ds(row, 1), :], sem_z)

    idx_cp = pltpu.make_async_copy(dest_hbm.at[i], dsm, sem_i)
    idx_cp.start()
    zrow[...] = jnp.zeros_like(zrow)
    idx_cp.wait()

    n_blk = xs_hbm.shape[0] // R
    tail_per_step = -(-n_blk // n)

    def tail(j, c, wait):
        blk = i * tail_per_step + j

        @pl.when(jnp.logical_and(blk >= nu_ref[0], blk < n_blk))
        def _():
            cp = pltpu.make_async_copy(zrow, xs_hbm.at[pl.ds(pl.multiple_of(blk * R, R), R), :], sem_t)
            cp.wait() if wait else cp.start()

        return c

    lax.fori_loop(0, tail_per_step, functools.partial(tail, wait=False), 0)

    def issue(t, c):
        src = x_ref.at[pl.ds(t, 1), :]
        for k in range(K):
            pltpu.make_async_copy(src, xs_hbm.at[pl.ds(dsm[k * tl + t], 1), :], sem_r).start(priority=k % 2)
        return c

    lax.fori_loop(0, tl, issue, 0, unroll=2)

    E = ps_ref.shape[0]
    per_step = -(-E // n)

    def pads(j, c, wait):
        e = jnp.minimum(i * per_step + j, E - 1)
        cnt = jnp.where(i * per_step + j < E, pc_ref[e], 0)
        start = ps_ref[e]
        n_single = jnp.minimum(cnt, (-start) & (SUBLANES - 1))
        start8 = start + n_single

        def one(r, c2):
            cp = pad_copy(start + r)
            cp.wait() if wait else cp.start()
            return c2

        def eight(r, c2):
            row = pl.multiple_of(start8 + r * SUBLANES, SUBLANES)
            cp = pltpu.make_async_copy(zrow.at[pl.ds(0, SUBLANES), :], xs_hbm.at[pl.ds(row, SUBLANES), :], sem_c)
            cp.wait() if wait else cp.start()
            return c2

        c = lax.fori_loop(0, n_single, one, c)
        return lax.fori_loop(0, lax.shift_right_logical(cnt - n_single, 3), eight, c)

    lax.fori_loop(0, per_step, functools.partial(pads, wait=False), 0)
    for k in range(K):
        pltpu.make_async_copy(x_ref, xs_hbm.at[pl.ds(0, tl), :], sem_r).wait()
    lax.fori_loop(0, per_step, functools.partial(pads, wait=True), 0)
    lax.fori_loop(0, tail_per_step, functools.partial(tail, wait=True), 0)


def _dispatch(xp, dest_t, pad_start, pad_cnt, n_used, n_rows, *, tl):
    T, W = xp.shape
    K = dest_t.shape[1] // tl
    any_spec = pl.BlockSpec(memory_space=pl.ANY)
    return pl.pallas_call(
        functools.partial(_dispatch_body, tl=tl, K=K),
        grid_spec=pltpu.PrefetchScalarGridSpec(
            num_scalar_prefetch=3, grid=(T // tl,),
            in_specs=[any_spec, pl.BlockSpec((tl, W), lambda i, ps, pc, nu: (i, 0))],
            out_specs=any_spec,
            scratch_shapes=[pltpu.SMEM((K * tl,), I32), pltpu.VMEM((EXPERT_ROWS, W), U32),
                            pltpu.SemaphoreType.DMA, pltpu.SemaphoreType.DMA, pltpu.SemaphoreType.DMA,
                            pltpu.SemaphoreType.DMA, pltpu.SemaphoreType.DMA]),
        out_shape=jax.ShapeDtypeStruct((n_rows, W), U32),
        compiler_params=_cparams("arbitrary"),
        name="dispatch",
    )(pad_start, pad_cnt, n_used, dest_t, xp)


def _experts_body(be_ref, nu_ref, xs_ref, wg_ref, wu_ref, wd_ref, ys_ref, wgu_s, wd_s):
    j = pl.program_id(0)
    de = wg_ref.shape[2]
    changed = jnp.logical_or(j == 0, be_ref[j] != be_ref[jnp.maximum(j - 1, 0)])

    @pl.when(changed)
    def _():
        wgu_s[:, :de] = wg_ref[0].astype(BF16)
        wgu_s[:, de:] = wu_ref[0].astype(BF16)
        wd_s[...] = wd_ref[0].astype(BF16)

    @pl.when(j < nu_ref[0])
    def _():
        lo, hi = _unpack_rows(xs_ref[...])
        xb = jnp.concatenate([lo.astype(BF16), hi.astype(BF16)], axis=1)
        h = jnp.dot(xb, wgu_s[...], preferred_element_type=F32)
        hg = h[:, :de]
        act = (hg * _sigmoid(hg) * h[:, de:]).astype(BF16)
        ys_ref[...] = _pack_rows(jnp.dot(act, wd_s[...], preferred_element_type=F32))

    @pl.when(j >= nu_ref[0])
    def _():
        ys_ref[...] = jnp.zeros_like(ys_ref)


def _experts(xs, blk_e, n_used, wg, wu, wd):
    n_rows, W = xs.shape
    R = EXPERT_ROWS
    D, de = wg.shape[1], wg.shape[2]
    xmap = lambda j, be, nu: (jnp.minimum(j, nu[0] - 1), 0)
    wmap = lambda j, be, nu: (be[j], 0, 0)
    return pl.pallas_call(
        _experts_body,
        grid_spec=pltpu.PrefetchScalarGridSpec(
            num_scalar_prefetch=2, grid=(n_rows // R,),
            in_specs=[pl.BlockSpec((R, W), xmap),
                      pl.BlockSpec((1, D, de), wmap), pl.BlockSpec((1, D, de), wmap),
                      pl.BlockSpec((1, de, D), wmap)],
            out_specs=pl.BlockSpec((R, W), lambda j, be, nu: (j, 0)),
            scratch_shapes=[pltpu.VMEM((D, 2 * de), BF16), pltpu.VMEM((de, D), BF16)]),
        out_shape=jax.ShapeDtypeStruct((n_rows, W), U32),
        compiler_params=_cparams("arbitrary"),
        name="experts",
    )(blk_e, n_used, xs, wg, wu, wd)


def _sc_gather_rows(table, idx2):
    from jax.experimental.pallas import tpu_sc as plsc
    n_chunks, CH = idx2.shape
    W = table.shape[1]
    info = plsc.get_sparse_core_info()
    NC, n_workers = info.num_cores, info.num_cores * info.num_subcores
    per_w = n_chunks // n_workers
    assert per_w * n_workers == n_chunks and per_w % 2 == 0

    def body(table_hbm, idx_hbm, out_hbm, idx_v, rows_v, sem):
        c0 = (lax.axis_index("s") * NC + lax.axis_index("c")) * per_w

        def gather(b):
            return pltpu.make_async_copy(table_hbm.at[idx_v.at[b]], rows_v.at[b], sem.at[b])

        pltpu.sync_copy(idx_hbm.at[c0], idx_v.at[0])
        gather(0).start()

        @pl.loop(0, per_w, step=2)
        def _(c):
            for b in range(2):
                cc = c + b

                @pl.when(cc + 1 < per_w)
                def _():
                    pltpu.sync_copy(idx_hbm.at[c0 + cc + 1], idx_v.at[1 - b])
                    gather(1 - b).start()

                gather(b).wait()
                pltpu.sync_copy(rows_v.at[b], out_hbm.at[pl.ds(pl.multiple_of((c0 + cc) * CH, CH), CH)])

    return pl.kernel(
        body, mesh=plsc.VectorSubcoreMesh(core_axis_name="c", subcore_axis_name="s"),
        out_type=jax.ShapeDtypeStruct((n_chunks * CH, W), table.dtype),
        scratch_types=[pltpu.VMEM((2, CH), I32), pltpu.VMEM((2, CH, W), table.dtype),
                       pltpu.SemaphoreType.DMA((2,))],
    )(table, idx2)


def _combine_body(yg_ref, gate_ref, x1_ref, wgu_ref, wd_ref, lg_ref, lb_ref, o_ref, *, alpha):
    x1 = x1_ref[...]
    ds_ = wd_ref.shape[0]
    h = jnp.dot(x1.astype(BF16), wgu_ref[...], preferred_element_type=F32)
    hg = h[:, :ds_]
    act = (hg * _sigmoid(hg) * h[:, ds_:]).astype(BF16)
    shared = jnp.dot(act, wd_ref[...], preferred_element_type=F32)

    g = gate_ref[...]
    lo_acc = hi_acc = None
    for k in range(yg_ref.shape[0]):
        lo, hi = _unpack_rows(yg_ref[k])
        gk = g[:, k:k + 1]
        lo_acc = gk * lo if k == 0 else lo_acc + gk * lo
        hi_acc = gk * hi if k == 0 else hi_acc + gk * hi
    routed = jnp.concatenate([lo_acc, hi_acc], axis=1)

    z = alpha * x1 + (routed + shared)
    mu = jnp.mean(z, axis=-1, keepdims=True)
    zc = z - mu
    var = jnp.mean(zc * zc, axis=-1, keepdims=True)
    o_ref[...] = zc * lax.rsqrt(var + LN_EPS) * lg_ref[...] + lb_ref[...]


def _combine(yg, gate_tk, x1, wgu, wd, ln_g, ln_b, *, alpha, tc):
    T, D = x1.shape
    K, _, W = yg.shape
    full = lambda a: pl.BlockSpec(a.shape, lambda i: (0,) * a.ndim)
    return pl.pallas_call(
        functools.partial(_combine_body, alpha=alpha),
        grid=(T // tc,),
        in_specs=[pl.BlockSpec((K, tc, W), lambda i: (0, i, 0)), pl.BlockSpec((tc, K), lambda i: (i, 0)),
                  pl.BlockSpec((tc, D), lambda i: (i, 0)), full(wgu), full(wd), full(ln_g), full(ln_b)],
        out_specs=pl.BlockSpec((tc, D), lambda i: (i, 0)),
        out_shape=jax.ShapeDtypeStruct((T, D), F32),
        compiler_params=_cparams("parallel"),
        name="combine",
    )(yg, gate_tk, x1, wgu, wd, ln_g, ln_b)


def _mixer_ln1(x, w_in, b_in, conv_w, conv_b, w_rg_a, b_rg_a, w_rg_i, b_rg_i, lru_lambda,
               w_proj_rnn, w_proj_att, rel_bias, w_out, ln1_g, ln1_b, w_router, alpha):
    B, S, D = x.shape
    T = B * S
    d_rnn = conv_w.shape[-1]
    gw = HEADS_PER_GROUP * HEAD_DIM
    d_att = gw * len(DILATED_GROUPS)
    a0 = 2 * d_rnn
    a1 = a0 + 3 * d_att
    head = lambda j, g: slice(a0 + j * d_att + g * gw, a0 + j * d_att + (g + 1) * gw)
    plain = [g for g, (_, d) in enumerate(DILATED_GROUPS) if d == 1]
    dilated = [g for g, (_, d) in enumerate(DILATED_GROUPS) if d > 1]
    order = ([slice(0, a0), slice(a1, None)] + [head(j, g) for g in plain for j in range(3)]
             + [head(j, g) for g in dilated for j in (1, 2, 0)])
    perm = lambda w: jnp.concatenate([w[..., s] for s in order], axis=-1)
    w_p = perm(w_in).astype(BF16)
    b_p = perm(b_in).reshape(1, -1)
    qkv0 = a0 + 2 * D
    n_main = qkv0 + 3 * gw * len(plain)
    x2 = x.reshape(T, D)

    proj, *qkv_dil = _in_proj(x2, w_p, b_p, n_main, tuple(DILATED_GROUPS[g][1] for g in dilated))
    proj3 = proj.reshape(B, S, n_main)
    y_rnn = _rnn(proj3, conv_w, conv_b, w_rg_a, b_rg_a, w_rg_i, b_rg_i, lru_lambda)

    os_, lses = [], []
    for g, (window, d) in enumerate(DILATED_GROUPS):
        blk = window // d
        bias = _attn_bias(rel_bias[:, g * HEADS_PER_GROUP:(g + 1) * HEADS_PER_GROUP], window, d)
        if d == 1:
            c0 = (qkv0 + 3 * gw * plain.index(g)) // gw
            o, lse = _attn_group(proj3, bias, blk=blk, d=1, gw=gw, q_blk=c0, k_blk=c0 + 1, v_blk=c0 + 2,
                                 row_blk=n_main // gw)
        else:
            o, lse = _attn_dilated(qkv_dil[dilated.index(g)], bias, B, blk=blk, d=d, gw=gw)
        os_.append(o.reshape(T, gw))
        lses.append(lse.reshape(T, gw))

    return _merge(y_rnn.reshape(T, d_rnn), os_, lses, proj, a0 // (2 * D), x2,
                  w_proj_rnn.astype(BF16), w_proj_att.astype(BF16), w_out.astype(BF16),
                  ln1_g.reshape(1, D), ln1_b.reshape(1, D), w_router.T, alpha=alpha)


def _moe_ln2(x1, x1p, logits_t, router_bias, w_exp_gate, w_exp_up, w_exp_down,
             w_sh_gate, w_sh_up, w_sh_down, ln2_g, ln2_b, alpha):
    T, D = x1.shape
    E = logits_t.shape[0]
    R = EXPERT_ROWS
    n_blk = T * TOP_K // R + E
    n_rows = n_blk * R

    e_idx, gate, rank, counts = _route(logits_t, router_bias)
    counts = counts.reshape(E).astype(I32)
    padded = (counts + R - 1) // R * R
    pad_end = jnp.cumsum(padded)
    seg_start = pad_end - padded
    n_used = (pad_end[-1:] // R).astype(I32)
    blk_e = jnp.minimum(jnp.sum(jnp.arange(n_blk, dtype=I32)[:, None] >= (pad_end // R)[None, :], axis=1),
                        E - 1).astype(I32)
    dest = _dest(e_idx, rank, seg_start)
    tl = ROW_TILE
    dest_t = dest.reshape(TOP_K, T // tl, tl).transpose(1, 0, 2).reshape(T // tl, TOP_K * tl)

    xs = _dispatch(x1p, dest_t, seg_start + counts, padded - counts, n_used, n_rows, tl=tl)
    ys = _experts(xs, blk_e, n_used, w_exp_gate, w_exp_up, w_exp_down)
    wgu = jnp.concatenate([w_sh_gate, w_sh_up], axis=-1).astype(BF16)
    yg = _sc_gather_rows(ys, dest.reshape(-1, SC_GATHER_CHUNK)).reshape(TOP_K, T, ys.shape[1])
    return _combine(yg, gate.T, x1, wgu, w_sh_down.astype(BF16),
                    ln2_g.reshape(1, D), ln2_b.reshape(1, D), alpha=alpha, tc=tl)


def kernel(x, w_in, b_in, conv_w, conv_b, w_rg_a, b_rg_a, w_rg_i, b_rg_i, lru_lambda,
           w_proj_rnn, w_proj_att, rel_bias, w_out, ln1_g, ln1_b, w_router, router_bias,
           w_exp_gate, w_exp_up, w_exp_down, w_sh_gate, w_sh_up, w_sh_down, ln2_g, ln2_b):
    B, S, D = x.shape
    depth = w_in.shape[0]
    alpha = (2 * depth) ** 0.25
    for i in range(depth):
        x1, x1p, logits_t = _mixer_ln1(
            x, w_in[i], b_in[i], conv_w[i], conv_b[i], w_rg_a[i], b_rg_a[i], w_rg_i[i], b_rg_i[i],
            lru_lambda[i], w_proj_rnn[i], w_proj_att[i], rel_bias, w_out[i], ln1_g[i], ln1_b[i],
            w_router[i], alpha)
        out = _moe_ln2(x1, x1p, logits_t, router_bias[i], w_exp_gate[i], w_exp_up[i], w_exp_down[i],
                       w_sh_gate[i], w_sh_up[i], w_sh_down[i], ln2_g[i], ln2_b[i], alpha)
        x = out.reshape(B, S, D)
    return x
```

```python
import functools

import numpy as np
import jax
import jax.numpy as jnp
from jax import lax
from jax.experimental import pallas as pl
from jax.experimental.pallas import tpu as pltpu

F32 = jnp.float32
BF16 = jnp.bfloat16
I32 = jnp.int32
U32 = jnp.uint32

RNN_HEADS = 16
LRU_C = 8.0
HEAD_DIM = 64
HEADS_PER_GROUP = 8
DILATED_GROUPS = ((128, 1), (512, 4), (2048, 16))
NUM_BUCKETS = 32
MAX_DISTANCE = 2048
TOP_K = 8
N_EXPERT_GROUPS = 8
TOPK_GROUPS = 4
ROUTED_SCALE = 2.5
LN_EPS = 1e-5

LANES = 128
SUBLANES = 8
VMEM_LIMIT_BYTES = 56 * 1024 * 1024

MASK_VALUE = -1e30
EXPERT_ROWS = 512
ROW_TILE = 256
SC_GATHER_CHUNK = 64


def _cparams(*sem):
    return pltpu.CompilerParams(dimension_semantics=sem, vmem_limit_bytes=VMEM_LIMIT_BYTES)


def _sigmoid(v):
    return 0.5 * (jnp.tanh(0.5 * v) + 1.0)


def _in_proj_body(x_ref, w_ref, b_ref, main_ref, *rest, tn, dils):
    dil_refs, scr = rest[:-1], rest[-1]
    xb = x_ref[...].astype(BF16)
    tm = xb.shape[0]

    def chunk(j):
        sl = slice(j * tn, (j + 1) * tn)
        return jnp.dot(xb, w_ref[:, sl], preferred_element_type=F32) + b_ref[:, sl]

    n_main = main_ref.shape[1] // tn
    for j in range(n_main):
        main_ref[:, j * tn:(j + 1) * tn] = chunk(j).astype(main_ref.dtype)
    j = n_main
    for ref, d in zip(dil_refs, dils):
        for c in range(ref.shape[3] // tn):
            acc = chunk(j)
            j += 1
            for q in range(tn // LANES):
                scr[q] = acc[:, q * LANES:(q + 1) * LANES]
            for r in range(d):
                part = jnp.concatenate([scr[q, pl.ds(r, tm // d, stride=d), :] for q in range(tn // LANES)], axis=1)
                ref[0, r, :, c * tn:(c + 1) * tn] = part.astype(ref.dtype)


def _in_proj(x2, w, b, n_main, dils, *, tm=512, tn=512):
    T, D = x2.shape
    N = w.shape[1]
    wd = (N - n_main) // len(dils)
    out_specs = [pl.BlockSpec((tm, n_main), lambda i: (i, 0))]
    out_shape = [jax.ShapeDtypeStruct((T, n_main), BF16)]
    for d in dils:
        out_specs.append(pl.BlockSpec((1, d, tm // d, wd), lambda i: (i, 0, 0, 0)))
        out_shape.append(jax.ShapeDtypeStruct((T // tm, d, tm // d, wd), BF16))
    return pl.pallas_call(
        functools.partial(_in_proj_body, tn=tn, dils=dils),
        grid=(T // tm,),
        in_specs=[pl.BlockSpec((tm, D), lambda i: (i, 0)),
                  pl.BlockSpec((D, N), lambda i: (0, 0), pipeline_mode=pl.Buffered(1)),
                  pl.BlockSpec((1, N), lambda i: (0, 0))],
        out_specs=out_specs,
        out_shape=out_shape,
        scratch_shapes=[pltpu.VMEM((tn // LANES, tm, LANES), F32)],
        compiler_params=_cparams("parallel"),
        name="in_proj",
    )(x2, w, b)


def _rnn_body(xr_ref, gr_ref, cw_ref, cb_ref, wg_ref, bg_ref, lam_ref, y_ref,
              xext, a_s, b_s, hc, *, ts, cb):
    s = pl.program_id(2)

    @pl.when(s == 0)
    def _():
        xext[0:SUBLANES, :] = jnp.zeros((SUBLANES, cb), F32)
        hc[...] = jnp.zeros_like(hc)

    xr = xr_ref[0].astype(F32)
    xext[SUBLANES:SUBLANES + ts, :] = xr
    nw = cw_ref.shape[0]
    xc = cw_ref[nw - 1:nw, :] * xr + cb_ref[...]
    for j in range(nw - 1):
        off = SUBLANES - (nw - 1 - j)
        xc = xc + cw_ref[j:j + 1, :] * xext[off:off + ts, :]
    xext[0:SUBLANES, :] = xext[ts:ts + SUBLANES, :]

    gates = jnp.dot(xc.astype(BF16), wg_ref[0], preferred_element_type=F32) + bg_ref[...]
    r = _sigmoid(gates[:, :cb])
    ig = _sigmoid(gates[:, cb:])
    nl = -lam_ref[...]
    sp = jnp.maximum(nl, 0.0) + jnp.log1p(jnp.exp(-jnp.abs(nl)))
    log_a = (-LRU_C) * r * sp
    a = jnp.exp(log_a)
    u = jnp.sqrt(1.0 - a * a) * (ig * xc)

    row = lax.broadcasted_iota(I32, (ts, cb), 0) & (SUBLANES - 1)
    av, bv = a, u
    for sft in (1, 2, 4):
        a_sh = pltpu.roll(av, sft, 0)
        b_sh = pltpu.roll(bv, sft, 0)
        m = row >= sft
        bv = jnp.where(m, av * b_sh + bv, bv)
        av = jnp.where(m, av * a_sh, av)
    a_s[...] = av
    b_s[...] = bv

    def carry(g, h):
        i0 = pl.multiple_of(g * SUBLANES, SUBLANES)
        h8 = b_s[pl.ds(i0, SUBLANES), :] + a_s[pl.ds(i0, SUBLANES), :] * h
        b_s[pl.ds(i0, SUBLANES), :] = h8
        return h8[SUBLANES - 1:SUBLANES, :]

    hc[0:1, :] = lax.fori_loop(0, ts // SUBLANES, carry, hc[0:1, :], unroll=8)
    gr = gr_ref[0].astype(F32)
    y_ref[0] = (b_s[...] * jax.nn.gelu(gr)).astype(y_ref.dtype)


def _block_diag(w, per):
    H, d, _ = w.shape
    w4 = w.reshape(H // per, per, d, d)
    out = jnp.einsum('gpij,pq->gpiqj', w4, jnp.eye(per, dtype=w.dtype))
    return out.reshape(H // per, per * d, per * d)


def _rnn(proj3, conv_w, conv_b, w_rg_a, b_rg_a, w_rg_i, b_rg_i, lam, *, ts=512, cb=256):
    B, S, _ = proj3.shape
    d_rnn = conv_w.shape[-1]
    nc = d_rnn // cb
    per = cb // (d_rnn // RNN_HEADS)
    wg = jnp.concatenate([_block_diag(w_rg_a, per), _block_diag(w_rg_i, per)], axis=-1).astype(BF16)
    bg = jnp.concatenate([b_rg_a.reshape(nc, 1, cb), b_rg_i.reshape(nc, 1, cb)], axis=-1)
    return pl.pallas_call(
        functools.partial(_rnn_body, ts=ts, cb=cb),
        grid=(B, nc, S // ts),
        in_specs=[pl.BlockSpec((1, ts, cb), lambda b, c, s: (b, s, c)),
                  pl.BlockSpec((1, ts, cb), lambda b, c, s: (b, s, nc + c)),
                  pl.BlockSpec((conv_w.shape[0], cb), lambda b, c, s: (0, c)),
                  pl.BlockSpec((1, cb), lambda b, c, s: (0, c)),
                  pl.BlockSpec((1, cb, 2 * cb), lambda b, c, s: (c, 0, 0)),
                  pl.BlockSpec((None, 1, 2 * cb), lambda b, c, s: (c, 0, 0)),
                  pl.BlockSpec((1, cb), lambda b, c, s: (0, c))],
        out_specs=pl.BlockSpec((1, ts, cb), lambda b, c, s: (b, s, c)),
        out_shape=jax.ShapeDtypeStruct((B, S, d_rnn), BF16),
        scratch_shapes=[pltpu.VMEM((ts + SUBLANES, cb), F32), pltpu.VMEM((ts, cb), F32),
                        pltpu.VMEM((ts, cb), F32), pltpu.VMEM((SUBLANES, cb), F32)],
        compiler_params=_cparams("parallel", "parallel", "arbitrary"),
        name="rnn",
    )(proj3, proj3, conv_w, conv_b.reshape(1, d_rnn), wg, bg, lam.reshape(1, d_rnn))


def _t5_bucket(dist):
    max_exact = NUM_BUCKETS // 2
    d = np.maximum(dist, 1).astype(np.float64)
    large = max_exact + (np.log(d / max_exact) / np.log(MAX_DISTANCE / max_exact)
                         * (NUM_BUCKETS - max_exact)).astype(np.int64)
    large = np.minimum(large, NUM_BUCKETS - 1)
    return np.where(dist < max_exact, dist, large).astype(np.int32)


def _attn_bias(table, window, dilation):
    blk = window // dilation
    qi = np.arange(blk)[:, None]
    ki = np.arange(2 * blk)[None, :]
    rel = qi + blk - ki
    in_window = (rel >= 0) & (rel <= blk)
    bucket = _t5_bucket(np.clip(rel, 0, None) * dilation)
    onehot = (bucket[..., None] == np.arange(NUM_BUCKETS)).astype(np.float32)
    bias = jnp.einsum('qkn,nh->hqk', onehot, table.astype(F32), precision=lax.Precision.HIGHEST)
    first = in_window & (ki >= blk)
    return jnp.stack([jnp.where(first[None], bias, MASK_VALUE),
                      jnp.where(in_window[None], bias, MASK_VALUE)])


def _attn_heads(q, kp, kc, vp, vc, bias_ref):
    blk, gw = q.shape
    lo = lax.broadcasted_iota(I32, (blk, LANES), 1) < HEAD_DIM
    scale = HEAD_DIM ** -0.5
    o_parts, l_parts = [], []
    for p in range(gw // LANES):
        sl = slice(p * LANES, (p + 1) * LANES)
        q2 = q[:, sl] * scale
        k2 = jnp.concatenate([kp[:, sl], kc[:, sl]], axis=0)
        v2 = jnp.concatenate([vp[:, sl], vc[:, sl]], axis=0)
        outs, lses = [], []
        for hh in range(2):
            qm = jnp.where(lo if hh == 0 else jnp.logical_not(lo), q2, jnp.zeros_like(q2))
            sc = lax.dot_general(qm, k2, (((1,), (1,)), ((), ())), preferred_element_type=F32)
            sc = sc + bias_ref[0, 2 * p + hh]
            m = jnp.max(sc, axis=-1, keepdims=True)
            e = jnp.exp(sc - m)
            l = jnp.sum(e, axis=-1, keepdims=True)
            o = jnp.dot(e.astype(BF16), v2, preferred_element_type=F32)
            outs.append(o / l)
            lses.append(jnp.broadcast_to(m + jnp.log(l), (blk, LANES)))
        o_parts.append(jnp.where(lo, outs[0], outs[1]))
        l_parts.append(jnp.where(lo, lses[0], lses[1]))
    return jnp.concatenate(o_parts, axis=1), jnp.concatenate(l_parts, axis=1)


def _attn_body(q_ref, kp_ref, kc_ref, vp_ref, vc_ref, bias_ref, o_ref, lse_ref):
    o, lse = _attn_heads(q_ref[0], kp_ref[0], kc_ref[0], vp_ref[0], vc_ref[0], bias_ref)
    o_ref[0] = o.astype(o_ref.dtype)
    lse_ref[0] = lse


def _attn_dil_body(cur_ref, prev_ref, bias_ref, o_ref, lse_ref, oscr, lscr, *, d, blk, gw):
    nt = cur_ref.shape[0]
    nq = gw // LANES

    def rows(ref, r, c):
        parts = [ref[j, r, :, c * gw:(c + 1) * gw] for j in range(nt)]
        return parts[0] if nt == 1 else jnp.concatenate(parts, axis=0)

    def residue(r, carry):
        o, lse = _attn_heads(rows(cur_ref, r, 2), rows(prev_ref, r, 0), rows(cur_ref, r, 0),
                             rows(prev_ref, r, 1), rows(cur_ref, r, 1), bias_ref)
        for q in range(nq):
            oscr[q, pl.ds(r, blk, stride=d), :] = o[:, q * LANES:(q + 1) * LANES]
            lscr[q, pl.ds(r, blk, stride=d), :] = lse[:, q * LANES:(q + 1) * LANES]
        return carry

    lax.fori_loop(0, d, residue, 0)
    for q in range(nq):
        o_ref[0, :, q * LANES:(q + 1) * LANES] = oscr[q].astype(o_ref.dtype)
        lse_ref[0, :, q * LANES:(q + 1) * LANES] = lscr[q]


def _attn_dilated(qkv_t, bias, B, *, blk, d, gw):
    n_tiles, _, rows_t, _ = qkv_t.shape
    tm = rows_t * d
    nt = blk * d // tm
    nb = n_tiles // (B * nt)
    S = n_tiles * tm // B
    return pl.pallas_call(
        functools.partial(_attn_dil_body, d=d, blk=blk, gw=gw),
        grid=(B, nb),
        in_specs=[pl.BlockSpec((nt, d, rows_t, 3 * gw), lambda b, n: (b * nb + n, 0, 0, 0)),
                  pl.BlockSpec((nt, d, rows_t, 2 * gw), lambda b, n: (b * nb + jnp.maximum(n - 1, 0), 0, 0, 0)),
                  pl.BlockSpec((1,) + bias.shape[1:], lambda b, n: (jnp.minimum(n, 1), 0, 0, 0))],
        out_specs=[pl.BlockSpec((1, blk * d, gw), lambda b, n: (b, n, 0)),
                   pl.BlockSpec((1, blk * d, gw), lambda b, n: (b, n, 0))],
        out_shape=[jax.ShapeDtypeStruct((B, S, gw), BF16), jax.ShapeDtypeStruct((B, S, gw), F32)],
        scratch_shapes=[pltpu.VMEM((gw // LANES, blk * d, LANES), F32),
                        pltpu.VMEM((gw // LANES, blk * d, LANES), F32)],
        compiler_params=_cparams("parallel", "parallel"),
        name=f"attn_d{d}",
    )(qkv_t, qkv_t, bias)


def _attn_group(qkv, bias, *, blk, d, gw, q_blk, k_blk, v_blk, row_blk):
    B, L, _ = qkv.shape
    nb = L // blk

    def cur(col):
        return pl.BlockSpec((1, blk, gw), lambda b, r, n: (b, n, r * row_blk + col))

    def prev(col):
        return pl.BlockSpec((1, blk, gw), lambda b, r, n: (b, jnp.maximum(n - 1, 0), r * row_blk + col))

    return pl.pallas_call(
        _attn_body,
        grid=(B, d, nb),
        in_specs=[cur(q_blk), prev(k_blk), cur(k_blk), prev(v_blk), cur(v_blk),
                  pl.BlockSpec((1,) + bias.shape[1:], lambda b, r, n: (jnp.minimum(n, 1), 0, 0, 0))],
        out_specs=[pl.BlockSpec((1, blk, gw), lambda b, r, n: (b, n, r)),
                   pl.BlockSpec((1, blk, gw), lambda b, r, n: (b, n, r))],
        out_shape=[jax.ShapeDtypeStruct((B, L, d * gw), BF16),
                   jax.ShapeDtypeStruct((B, L, d * gw), F32)],
        compiler_params=_cparams("parallel", "parallel", "parallel"),
        name=f"attn_d{d}",
    )(qkv, qkv, qkv, qkv, qkv, bias)


def _merge_body(yr_ref, o1_ref, o2_ref, o3_ref, l1_ref, l2_ref, l3_ref, g_ref, x_ref,
                wr_ref, wa_ref, wo_ref, lg_ref, lb_ref, wrt_ref,
                x1_ref, x1p_ref, lt_ref, *, alpha):
    l1, l2, l3 = l1_ref[...], l2_ref[...], l3_ref[...]
    mx = jnp.maximum(jnp.maximum(l1, l2), l3)
    w1, w2, w3 = jnp.exp(l1 - mx), jnp.exp(l2 - mx), jnp.exp(l3 - mx)
    y_att = (w1 * o1_ref[...].astype(F32) + w2 * o2_ref[...].astype(F32)
             + w3 * o3_ref[...].astype(F32)) / (w1 + w2 + w3)
    pr = jnp.dot(yr_ref[...], wr_ref[...], preferred_element_type=F32)
    pa = jnp.dot(y_att.astype(BF16), wa_ref[...], preferred_element_type=F32)
    dm = pr.shape[1]
    g = g_ref[...].astype(F32)
    merged = _sigmoid(g[:, :dm]) * pr + _sigmoid(g[:, dm:]) * pa
    mix = jnp.dot(merged.astype(BF16), wo_ref[...], preferred_element_type=F32)
    z = alpha * x_ref[...] + mix
    mu = jnp.mean(z, axis=-1, keepdims=True)
    zc = z - mu
    var = jnp.mean(zc * zc, axis=-1, keepdims=True)
    x1 = zc * lax.rsqrt(var + LN_EPS) * lg_ref[...] + lb_ref[...]
    x1_ref[...] = x1
    x1p_ref[...] = _pack_rows(x1)
    lt_ref[...] = lax.dot_general(wrt_ref[...], x1, (((1,), (1,)), ((), ())),
                                  precision=lax.Precision.HIGHEST, preferred_element_type=F32)


def _merge(y_rnn, os_, lses, proj, gate_blk, x2, wr, wa, wo, ln_g, ln_b, w_router_t, *, alpha, tm=512):
    T, D = x2.shape
    da = os_[0].shape[1]
    E = w_router_t.shape[0]
    row = lambda w: pl.BlockSpec((tm, w), lambda i: (i, 0))
    full = lambda a: pl.BlockSpec(a.shape, lambda i: (0,) * a.ndim)
    return pl.pallas_call(
        functools.partial(_merge_body, alpha=alpha),
        grid=(T // tm,),
        in_specs=[row(D), row(da), row(da), row(da), row(da), row(da), row(da),
                  pl.BlockSpec((tm, 2 * D), lambda i: (i, gate_blk)), row(D),
                  full(wr), full(wa), full(wo), full(ln_g), full(ln_b), full(w_router_t)],
        out_specs=[row(D), row(D // 2), pl.BlockSpec((E, tm), lambda i: (0, i))],
        out_shape=[jax.ShapeDtypeStruct((T, D), F32), jax.ShapeDtypeStruct((T, D // 2), U32),
                   jax.ShapeDtypeStruct((E, T), F32)],
        compiler_params=_cparams("parallel"),
        name="merge",
    )(y_rnn, *os_, *lses, proj, x2, wr, wa, wo, ln_g, ln_b, w_router_t)


def _first_max(vals, idx, big):
    m = jnp.max(vals, axis=0, keepdims=True)
    i = jnp.min(jnp.where(vals == m, idx, big), axis=0, keepdims=True)
    return m, i


def _route_body(lt_ref, rb_ref, e_ref, g_ref, r_ref, cnt_ref, *, tl):
    E = lt_ref.shape[0]
    per = E // N_EXPERT_GROUPS
    neg = -jnp.inf

    @pl.when(pl.program_id(0) == 0)
    def _():
        cnt_ref[...] = jnp.zeros_like(cnt_ref)

    scores = jax.nn.sigmoid(lt_ref[...])
    sel = scores + rb_ref[...]
    rowi = lax.broadcasted_iota(I32, (E, tl), 0)

    gi = lax.broadcasted_iota(I32, (N_EXPERT_GROUPS, tl), 0)
    gsc = jnp.zeros((N_EXPERT_GROUPS, tl), F32)
    for g in range(N_EXPERT_GROUPS):
        blk = sel[g * per:(g + 1) * per]
        ri = lax.broadcasted_iota(I32, (per, tl), 0) + g * per
        m1, i1 = _first_max(blk, ri, E)
        m2 = jnp.max(jnp.where(ri == i1, neg, blk), axis=0, keepdims=True)
        gsc = jnp.where(gi == g, m1 + m2, gsc)
    keep = jnp.zeros((N_EXPERT_GROUPS, tl), F32)
    for _ in range(TOPK_GROUPS):
        _, ig = _first_max(gsc, gi, N_EXPERT_GROUPS)
        hit = gi == ig
        keep = jnp.where(hit, 1.0, keep)
        gsc = jnp.where(hit, neg, gsc)
    cur = jnp.concatenate(
        [jnp.where(keep[g:g + 1] > 0.5, sel[g * per:(g + 1) * per], neg) for g in range(N_EXPERT_GROUPS)],
        axis=0)

    ti = lax.broadcasted_iota(I32, (tl, tl), 0)
    tj = lax.broadcasted_iota(I32, (tl, tl), 1)
    earlier = jnp.where(ti < tj, 1.0, 0.0).astype(BF16)

    es, gv = [], []
    onehot = jnp.zeros((E, tl), F32)
    for k in range(TOP_K):
        _, ie = _first_max(cur, rowi, E)
        hit = rowi == ie
        es.append(ie)
        gv.append(jnp.sum(jnp.where(hit, scores, 0.0), axis=0, keepdims=True))
        onehot = jnp.where(hit, 1.0, onehot)
        cur = jnp.where(hit, neg, cur)
    gsum = gv[0]
    for k in range(1, TOP_K):
        gsum = gsum + gv[k]
    ranks = jnp.dot(onehot.astype(BF16), earlier, preferred_element_type=F32) + cnt_ref[...]
    for k in range(TOP_K):
        e_ref[k:k + 1, :] = es[k]
        g_ref[k:k + 1, :] = gv[k] / gsum * ROUTED_SCALE
        r_ref[k:k + 1, :] = jnp.sum(jnp.where(rowi == es[k], ranks, 0.0), axis=0, keepdims=True).astype(I32)
    cnt_ref[...] += jnp.sum(onehot, axis=1, keepdims=True)


def _route(logits_t, router_bias, *, tl=256):
    E, T = logits_t.shape
    kt = pl.BlockSpec((TOP_K, tl), lambda i: (0, i))
    return pl.pallas_call(
        functools.partial(_route_body, tl=tl),
        grid=(T // tl,),
        in_specs=[pl.BlockSpec((E, tl), lambda i: (0, i)), pl.BlockSpec((E, 1), lambda i: (0, 0))],
        out_specs=[kt, kt, kt, pl.BlockSpec((E, 1), lambda i: (0, 0))],
        out_shape=[jax.ShapeDtypeStruct((TOP_K, T), I32), jax.ShapeDtypeStruct((TOP_K, T), F32),
                   jax.ShapeDtypeStruct((TOP_K, T), I32), jax.ShapeDtypeStruct((E, 1), F32)],
        compiler_params=_cparams("arbitrary"),
        name="route",
    )(logits_t, router_bias.reshape(E, 1))


def _dest_body(e_ref, r_ref, ss_ref, d_ref):
    E = ss_ref.shape[0]
    tl = e_ref.shape[1]
    rowi = lax.broadcasted_iota(I32, (E, tl), 0)
    e = e_ref[...]
    rows = [jnp.sum(jnp.where(rowi == e[k:k + 1], ss_ref[...], 0), axis=0, keepdims=True)
            for k in range(e.shape[0])]
    d_ref[...] = jnp.concatenate(rows, axis=0) + r_ref[...]


def _dest(e_idx, rank, seg_start, *, tl=512):
    K, T = e_idx.shape
    E = seg_start.shape[0]
    kt = pl.BlockSpec((K, tl), lambda i: (0, i))
    return pl.pallas_call(
        _dest_body, grid=(T // tl,),
        in_specs=[kt, kt, pl.BlockSpec((E, 1), lambda i: (0, 0))],
        out_specs=kt, out_shape=jax.ShapeDtypeStruct((K, T), I32),
        compiler_params=_cparams("parallel"), name="dest",
    )(e_idx, rank, seg_start.reshape(E, 1))


def _pack_rows(v):
    w = v.shape[1] // 2
    lo = pltpu.bitcast(v[:, :w].astype(BF16).astype(F32), U32) >> 16
    hi = pltpu.bitcast(v[:, w:].astype(BF16).astype(F32), U32) & jnp.uint32(0xFFFF0000)
    return lo | hi


def _unpack_rows(p):
    lo = pltpu.bitcast(p << 16, F32)
    hi = pltpu.bitcast(p & jnp.uint32(0xFFFF0000), F32)
    return lo, hi


def _sc_scatter_rows(x, idx3, n_rows):
    from jax.experimental.pallas import tpu_sc as plsc
    n_chunks, K, CH = idx3.shape
    W = x.shape[1]
    info = plsc.get_sparse_core_info()
    NC, n_workers = info.num_cores, info.num_cores * info.num_subcores
    per_w = n_chunks // n_workers
    assert per_w * n_workers == n_chunks and per_w % 2 == 0

    def body(x_hbm, idx_hbm, out_hbm, idx_v, rows_v, sem_l, sem_s):
        c0 = (lax.axis_index("s") * NC + lax.axis_index("c")) * per_w

        def load(cc, b):
            return pltpu.make_async_copy(x_hbm.at[pl.ds(pl.multiple_of((c0 + cc) * CH, CH), CH)], rows_v.at[b],
                                         sem_l.at[b])

        def scatter(b, k):
            return pltpu.make_async_copy(rows_v.at[b], out_hbm.at[idx_v.at[b, k]], sem_s.at[b])

        pltpu.sync_copy(idx_hbm.at[c0], idx_v.at[0])
        load(0, 0).start()

        @pl.loop(0, per_w, step=2)
        def _(c):
            for b in range(2):
                cc = c + b
                load(cc, b).wait()
                for k in range(K):
                    scatter(b, k).start()

                @pl.when(cc >= 1)
                def _():
                    for k in range(K):
                        scatter(1 - b, k).wait()

                @pl.when(cc + 1 < per_w)
                def _():
                    pltpu.sync_copy(idx_hbm.at[c0 + cc + 1], idx_v.at[1 - b])
                    load(cc + 1, 1 - b).start()

        for k in range(K):
            scatter(1, k).wait()

    return pl.kernel(
        body, mesh=plsc.VectorSubcoreMesh(core_axis_name="c", subcore_axis_name="s"),
        out_type=jax.ShapeDtypeStruct((n_rows, W), x.dtype),
        scratch_types=[pltpu.VMEM((2, K, CH), I32), pltpu.VMEM((2, CH, W), x.dtype),
                       pltpu.SemaphoreType.DMA((2,)), pltpu.SemaphoreType.DMA((2,))],
    )(x, idx3)


def _pad_fill_body(ps_ref, pc_ref, nu_ref, xs_in, xs_hbm, zrow, sem_z, sem_c, sem_t):
    del xs_in
    i = pl.program_id(0)
    n = pl.num_programs(0)
    R = zrow.shape[0]

    def pad_copy(row):
        return pltpu.make_async_copy(zrow.at[pl.ds(0, 1), :], xs_hbm.at[pl.ds(row, 1), :], sem_z)

    zrow[...] = jnp.zeros_like(zrow)

    n_blk = xs_hbm.shape[0] // R
    tail_per_step = -(-n_blk // n)

    def tail(j, c, wait):
        blk = i * tail_per_step + j

        @pl.when(jnp.logical_and(blk >= nu_ref[0], blk < n_blk))
        def _():
            cp = pltpu.make_async_copy(zrow, xs_hbm.at[pl.ds(pl.multiple_of(blk * R, R), R), :], sem_t)
            cp.wait() if wait else cp.start()

        return c

    lax.fori_loop(0, tail_per_step, functools.partial(tail, wait=False), 0)

    E = ps_ref.shape[0]
    per_step = -(-E // n)

    def pads(j, c, wait):
        e = jnp.minimum(i * per_step + j, E - 1)
        cnt = jnp.where(i * per_step + j < E, pc_ref[e], 0)
        start = ps_ref[e]
        n_single = jnp.minimum(cnt, (-start) & (SUBLANES - 1))
        start8 = start + n_single

        def one(r, c2):
            cp = pad_copy(start + r)
            cp.wait() if wait else cp.start()
            return c2

        def eight(r, c2):
            row = pl.multiple_of(start8 + r * SUBLANES, SUBLANES)
            cp = pltpu.make_async_copy(zrow.at[pl.ds(0, SUBLANES), :], xs_hbm.at[pl.ds(row, SUBLANES), :], sem_c)
            cp.wait() if wait else cp.start()
            return c2

        c = lax.fori_loop(0, n_single, one, c)
        return lax.fori_loop(0, lax.shift_right_logical(cnt - n_single, 3), eight, c)

    lax.fori_loop(0, per_step, functools.partial(pads, wait=False), 0)
    lax.fori_loop(0, per_step, functools.partial(pads, wait=True), 0)
    lax.fori_loop(0, tail_per_step, functools.partial(tail, wait=True), 0)


def _pad_fill(xs, pad_start, pad_cnt, n_used, *, steps=16):
    any_spec = pl.BlockSpec(memory_space=pl.ANY)
    return pl.pallas_call(
        _pad_fill_body,
        grid_spec=pltpu.PrefetchScalarGridSpec(
            num_scalar_prefetch=3, grid=(steps,),
            in_specs=[any_spec], out_specs=any_spec,
            scratch_shapes=[pltpu.VMEM((EXPERT_ROWS, xs.shape[1]), xs.dtype),
                            pltpu.SemaphoreType.DMA, pltpu.SemaphoreType.DMA, pltpu.SemaphoreType.DMA]),
        out_shape=jax.ShapeDtypeStruct(xs.shape, xs.dtype),
        input_output_aliases={3: 0},
        compiler_params=_cparams("arbitrary"),
        name="pad_fill",
    )(pad_start, pad_cnt, n_used, xs)


def _experts_body(be_ref, nu_ref, xs_ref, wg_ref, wu_ref, wd_ref, ys_ref, wgu_s, wd_s):
    j = pl.program_id(0)
    de = wg_ref.shape[2]
    changed = jnp.logical_or(j == 0, be_ref[j] != be_ref[jnp.maximum(j - 1, 0)])

    @pl.when(changed)
    def _():
        wgu_s[:, :de] = wg_ref[0].astype(BF16)
        wgu_s[:, de:] = wu_ref[0].astype(BF16)
        wd_s[...] = wd_ref[0].astype(BF16)

    @pl.when(j < nu_ref[0])
    def _():
        lo, hi = _unpack_rows(xs_ref[...])
        xb = jnp.concatenate([lo.astype(BF16), hi.astype(BF16)], axis=1)
        h = jnp.dot(xb, wgu_s[...], preferred_element_type=F32)
        hg = h[:, :de]
        act = (hg * _sigmoid(hg) * h[:, de:]).astype(BF16)
        ys_ref[...] = _pack_rows(jnp.dot(act, wd_s[...], preferred_element_type=F32))

    @pl.when(j >= nu_ref[0])
    def _():
        ys_ref[...] = jnp.zeros_like(ys_ref)


def _experts(xs, blk_e, n_used, wg, wu, wd):
    n_rows, W = xs.shape
    R = EXPERT_ROWS
    D, de = wg.shape[1], wg.shape[2]
    xmap = lambda j, be, nu: (jnp.minimum(j, nu[0] - 1), 0)
    wmap = lambda j, be, nu: (be[j], 0, 0)
    return pl.pallas_call(
        _experts_body,
        grid_spec=pltpu.PrefetchScalarGridSpec(
            num_scalar_prefetch=2, grid=(n_rows // R,),
            in_specs=[pl.BlockSpec((R, W), xmap),
                      pl.BlockSpec((1, D, de), wmap), pl.BlockSpec((1, D, de), wmap),
                      pl.BlockSpec((1, de, D), wmap)],
            out_specs=pl.BlockSpec((R, W), lambda j, be, nu: (j, 0)),
            scratch_shapes=[pltpu.VMEM((D, 2 * de), BF16), pltpu.VMEM((de, D), BF16)]),
        out_shape=jax.ShapeDtypeStruct((n_rows, W), U32),
        compiler_params=_cparams("arbitrary"),
        name="experts",
    )(blk_e, n_used, xs, wg, wu, wd)


def _sc_gather_rows(table, idx2):
    from jax.experimental.pallas import tpu_sc as plsc
    n_chunks, CH = idx2.shape
    W = table.shape[1]
    info = plsc.get_sparse_core_info()
    NC, n_workers = info.num_cores, info.num_cores * info.num_subcores
    per_w = n_chunks // n_workers
    assert per_w * n_workers == n_chunks and per_w % 2 == 0

    def body(table_hbm, idx_hbm, out_hbm, idx_v, rows_v, sem):
        c0 = (lax.axis_index("s") * NC + lax.axis_index("c")) * per_w

        def gather(b):
            return pltpu.make_async_copy(table_hbm.at[idx_v.at[b]], rows_v.at[b], sem.at[b])

        pltpu.sync_copy(idx_hbm.at[c0], idx_v.at[0])
        gather(0).start()

        @pl.loop(0, per_w, step=2)
        def _(c):
            for b in range(2):
                cc = c + b

                @pl.when(cc + 1 < per_w)
                def _():
                    pltpu.sync_copy(idx_hbm.at[c0 + cc + 1], idx_v.at[1 - b])
                    gather(1 - b).start()

                gather(b).wait()
                pltpu.sync_copy(rows_v.at[b], out_hbm.at[pl.ds(pl.multiple_of((c0 + cc) * CH, CH), CH)])

    return pl.kernel(
        body, mesh=plsc.VectorSubcoreMesh(core_axis_name="c", subcore_axis_name="s"),
        out_type=jax.ShapeDtypeStruct((n_chunks * CH, W), table.dtype),
        scratch_types=[pltpu.VMEM((2, CH), I32), pltpu.VMEM((2, CH, W), table.dtype),
                       pltpu.SemaphoreType.DMA((2,))],
    )(table, idx2)


def _combine_body(yg_ref, gate_ref, x1_ref, wgu_ref, wd_ref, lg_ref, lb_ref, o_ref, *, alpha):
    x1 = x1_ref[...]
    ds_ = wd_ref.shape[0]
    h = jnp.dot(x1.astype(BF16), wgu_ref[...], preferred_element_type=F32)
    hg = h[:, :ds_]
    act = (hg * _sigmoid(hg) * h[:, ds_:]).astype(BF16)
    shared = jnp.dot(act, wd_ref[...], preferred_element_type=F32)

    g = gate_ref[...]
    lo_acc = hi_acc = None
    for k in range(yg_ref.shape[0]):
        lo, hi = _unpack_rows(yg_ref[k])
        gk = g[:, k:k + 1]
        lo_acc = gk * lo if k == 0 else lo_acc + gk * lo
        hi_acc = gk * hi if k == 0 else hi_acc + gk * hi
    routed = jnp.concatenate([lo_acc, hi_acc], axis=1)

    z = alpha * x1 + (routed + shared)
    mu = jnp.mean(z, axis=-1, keepdims=True)
    zc = z - mu
    var = jnp.mean(zc * zc, axis=-1, keepdims=True)
    o_ref[...] = zc * lax.rsqrt(var + LN_EPS) * lg_ref[...] + lb_ref[...]


def _combine(yg, gate_tk, x1, wgu, wd, ln_g, ln_b, *, alpha, tc):
    T, D = x1.shape
    K, _, W = yg.shape
    full = lambda a: pl.BlockSpec(a.shape, lambda i: (0,) * a.ndim)
    return pl.pallas_call(
        functools.partial(_combine_body, alpha=alpha),
        grid=(T // tc,),
        in_specs=[pl.BlockSpec((K, tc, W), lambda i: (0, i, 0)), pl.BlockSpec((tc, K), lambda i: (i, 0)),
                  pl.BlockSpec((tc, D), lambda i: (i, 0)), full(wgu), full(wd), full(ln_g), full(ln_b)],
        out_specs=pl.BlockSpec((tc, D), lambda i: (i, 0)),
        out_shape=jax.ShapeDtypeStruct((T, D), F32),
        compiler_params=_cparams("parallel"),
        name="combine",
    )(yg, gate_tk, x1, wgu, wd, ln_g, ln_b)


def _mixer_ln1(x, w_in, b_in, conv_w, conv_b, w_rg_a, b_rg_a, w_rg_i, b_rg_i, lru_lambda,
               w_proj_rnn, w_proj_att, rel_bias, w_out, ln1_g, ln1_b, w_router, alpha):
    B, S, D = x.shape
    T = B * S
    d_rnn = conv_w.shape[-1]
    gw = HEADS_PER_GROUP * HEAD_DIM
    d_att = gw * len(DILATED_GROUPS)
    a0 = 2 * d_rnn
    a1 = a0 + 3 * d_att
    head = lambda j, g: slice(a0 + j * d_att + g * gw, a0 + j * d_att + (g + 1) * gw)
    plain = [g for g, (_, d) in enumerate(DILATED_GROUPS) if d == 1]
    dilated = [g for g, (_, d) in enumerate(DILATED_GROUPS) if d > 1]
    order = ([slice(0, a0), slice(a1, None)] + [head(j, g) for g in plain for j in range(3)]
             + [head(j, g) for g in dilated for j in (1, 2, 0)])
    perm = lambda w: jnp.concatenate([w[..., s] for s in order], axis=-1)
    w_p = perm(w_in).astype(BF16)
    b_p = perm(b_in).reshape(1, -1)
    qkv0 = a0 + 2 * D
    n_main = qkv0 + 3 * gw * len(plain)
    x2 = x.reshape(T, D)

    proj, *qkv_dil = _in_proj(x2, w_p, b_p, n_main, tuple(DILATED_GROUPS[g][1] for g in dilated))
    proj3 = proj.reshape(B, S, n_main)
    y_rnn = _rnn(proj3, conv_w, conv_b, w_rg_a, b_rg_a, w_rg_i, b_rg_i, lru_lambda)

    os_, lses = [], []
    for g, (window, d) in enumerate(DILATED_GROUPS):
        blk = window // d
        bias = _attn_bias(rel_bias[:, g * HEADS_PER_GROUP:(g + 1) * HEADS_PER_GROUP], window, d)
        if d == 1:
            c0 = (qkv0 + 3 * gw * plain.index(g)) // gw
            o, lse = _attn_group(proj3, bias, blk=blk, d=1, gw=gw, q_blk=c0, k_blk=c0 + 1, v_blk=c0 + 2,
                                 row_blk=n_main // gw)
        else:
            o, lse = _attn_dilated(qkv_dil[dilated.index(g)], bias, B, blk=blk, d=d, gw=gw)
        os_.append(o.reshape(T, gw))
        lses.append(lse.reshape(T, gw))

    return _merge(y_rnn.reshape(T, d_rnn), os_, lses, proj, a0 // (2 * D), x2,
                  w_proj_rnn.astype(BF16), w_proj_att.astype(BF16), w_out.astype(BF16),
                  ln1_g.reshape(1, D), ln1_b.reshape(1, D), w_router.T, alpha=alpha)


def _moe_ln2(x1, x1p, logits_t, router_bias, w_exp_gate, w_exp_up, w_exp_down,
             w_sh_gate, w_sh_up, w_sh_down, ln2_g, ln2_b, alpha):
    T, D = x1.shape
    E = logits_t.shape[0]
    R = EXPERT_ROWS
    n_blk = T * TOP_K // R + E
    n_rows = n_blk * R

    e_idx, gate, rank, counts = _route(logits_t, router_bias)
    counts = counts.reshape(E).astype(I32)
    padded = (counts + R - 1) // R * R
    pad_end = jnp.cumsum(padded)
    seg_start = pad_end - padded
    n_used = (pad_end[-1:] // R).astype(I32)
    blk_e = jnp.minimum(jnp.sum(jnp.arange(n_blk, dtype=I32)[:, None] >= (pad_end // R)[None, :], axis=1),
                        E - 1).astype(I32)
    dest = _dest(e_idx, rank, seg_start)
    tl = ROW_TILE
    ch = SC_GATHER_CHUNK
    dest_c = dest.reshape(TOP_K, T // ch, ch).transpose(1, 0, 2)

    xs = _sc_scatter_rows(x1p, dest_c, n_rows)
    xs = _pad_fill(xs, seg_start + counts, padded - counts, n_used)
    ys = _experts(xs, blk_e, n_used, w_exp_gate, w_exp_up, w_exp_down)
    wgu = jnp.concatenate([w_sh_gate, w_sh_up], axis=-1).astype(BF16)
    yg = _sc_gather_rows(ys, dest.reshape(-1, SC_GATHER_CHUNK)).reshape(TOP_K, T, ys.shape[1])
    return _combine(yg, gate.T, x1, wgu, w_sh_down.astype(BF16),
                    ln2_g.reshape(1, D), ln2_b.reshape(1, D), alpha=alpha, tc=tl)


def kernel(x, w_in, b_in, conv_w, conv_b, w_rg_a, b_rg_a, w_rg_i, b_rg_i, lru_lambda,
           w_proj_rnn, w_proj_att, rel_bias, w_out, ln1_g, ln1_b, w_router, router_bias,
           w_exp_gate, w_exp_up, w_exp_down, w_sh_gate, w_sh_up, w_sh_down, ln2_g, ln2_b):
    B, S, D = x.shape
    depth = w_in.shape[0]
    alpha = (2 * depth) ** 0.25
    for i in range(depth):
        x1, x1p, logits_t = _mixer_ln1(
            x, w_in[i], b_in[i], conv_w[i], conv_b[i], w_rg_a[i], b_rg_a[i], w_rg_i[i], b_rg_i[i],
            lru_lambda[i], w_proj_rnn[i], w_proj_att[i], rel_bias, w_out[i], ln1_g[i], ln1_b[i],
            w_router[i], alpha)
        out = _moe_ln2(x1, x1p, logits_t, router_bias[i], w_exp_gate[i], w_exp_up[i], w_exp_down[i],
                       w_sh_gate[i], w_sh_up[i], w_sh_down[i], ln2_g[i], ln2_b[i], alpha)
        x = out.reshape(B, S, D)
    return x
```

```python
import functools

import numpy as np
import jax
import jax.numpy as jnp
from jax import lax
from jax.experimental import pallas as pl
from jax.experimental.pallas import tpu as pltpu

F32 = jnp.float32
BF16 = jnp.bfloat16
I32 = jnp.int32
U32 = jnp.uint32

RNN_HEADS = 16
LRU_C = 8.0
HEAD_DIM = 64
HEADS_PER_GROUP = 8
DILATED_GROUPS = ((128, 1), (512, 4), (2048, 16))
NUM_BUCKETS = 32
MAX_DISTANCE = 2048
TOP_K = 8
N_EXPERT_GROUPS = 8
TOPK_GROUPS = 4
ROUTED_SCALE = 2.5
LN_EPS = 1e-5

LANES = 128
SUBLANES = 8
VMEM_LIMIT_BYTES = 56 * 1024 * 1024

MASK_VALUE = -1e30
EXPERT_ROWS = 512
ROW_TILE = 256
SC_GATHER_CHUNK = 64


def _cparams(*sem):
    return pltpu.CompilerParams(dimension_semantics=sem, vmem_limit_bytes=VMEM_LIMIT_BYTES)


def _sigmoid(v):
    return 0.5 * (jnp.tanh(0.5 * v) + 1.0)


def _in_proj_body(x_ref, w_ref, b_ref, main_ref, *rest, tn, dils):
    dil_refs, scr = rest[:-1], rest[-1]
    xb = x_ref[...].astype(BF16)
    tm = xb.shape[0]

    def chunk(j):
        sl = slice(j * tn, (j + 1) * tn)
        return jnp.dot(xb, w_ref[:, sl], preferred_element_type=F32) + b_ref[:, sl]

    n_main = main_ref.shape[1] // tn
    for j in range(n_main):
        main_ref[:, j * tn:(j + 1) * tn] = chunk(j).astype(main_ref.dtype)
    j = n_main
    for ref, d in zip(dil_refs, dils):
        for c in range(ref.shape[3] // tn):
            acc = chunk(j)
            j += 1
            for q in range(tn // LANES):
                scr[q] = acc[:, q * LANES:(q + 1) * LANES]
            for r in range(d):
                part = jnp.concatenate([scr[q, pl.ds(r, tm // d, stride=d), :] for q in range(tn // LANES)], axis=1)
                ref[0, r, :, c * tn:(c + 1) * tn] = part.astype(ref.dtype)


def _in_proj(x2, w, b, n_main, dils, *, tm=512, tn=512):
    T, D = x2.shape
    N = w.shape[1]
    wd = (N - n_main) // len(dils)
    out_specs = [pl.BlockSpec((tm, n_main), lambda i: (i, 0))]
    out_shape = [jax.ShapeDtypeStruct((T, n_main), BF16)]
    for d in dils:
        out_specs.append(pl.BlockSpec((1, d, tm // d, wd), lambda i: (i, 0, 0, 0)))
        out_shape.append(jax.ShapeDtypeStruct((T // tm, d, tm // d, wd), BF16))
    return pl.pallas_call(
        functools.partial(_in_proj_body, tn=tn, dils=dils),
        grid=(T // tm,),
        in_specs=[pl.BlockSpec((tm, D), lambda i: (i, 0)),
                  pl.BlockSpec((D, N), lambda i: (0, 0), pipeline_mode=pl.Buffered(1)),
                  pl.BlockSpec((1, N), lambda i: (0, 0))],
        out_specs=out_specs,
        out_shape=out_shape,
        scratch_shapes=[pltpu.VMEM((tn // LANES, tm, LANES), F32)],
        compiler_params=_cparams("parallel"),
        name="in_proj",
    )(x2, w, b)


def _rnn_body(xr_ref, gr_ref, cw_ref, cb_ref, wg_ref, bg_ref, lam_ref, y_ref,
              xext, a_s, b_s, hc, *, ts, cb):
    s = pl.program_id(2)

    @pl.when(s == 0)
    def _():
        xext[0:SUBLANES, :] = jnp.zeros((SUBLANES, cb), F32)
        hc[...] = jnp.zeros_like(hc)

    xr = xr_ref[0].astype(F32)
    xext[SUBLANES:SUBLANES + ts, :] = xr
    nw = cw_ref.shape[0]
    xc = cw_ref[nw - 1:nw, :] * xr + cb_ref[...]
    for j in range(nw - 1):
        off = SUBLANES - (nw - 1 - j)
        xc = xc + cw_ref[j:j + 1, :] * xext[off:off + ts, :]
    xext[0:SUBLANES, :] = xext[ts:ts + SUBLANES, :]

    gates = jnp.dot(xc.astype(BF16), wg_ref[0], preferred_element_type=F32) + bg_ref[...]
    r = _sigmoid(gates[:, :cb])
    ig = _sigmoid(gates[:, cb:])
    nl = -lam_ref[...]
    sp = jnp.maximum(nl, 0.0) + jnp.log1p(jnp.exp(-jnp.abs(nl)))
    log_a = (-LRU_C) * r * sp
    a = jnp.exp(log_a)
    u = jnp.sqrt(1.0 - a * a) * (ig * xc)

    row = lax.broadcasted_iota(I32, (ts, cb), 0) & (SUBLANES - 1)
    av, bv = a, u
    for sft in (1, 2, 4):
        a_sh = pltpu.roll(av, sft, 0)
        b_sh = pltpu.roll(bv, sft, 0)
        m = row >= sft
        bv = jnp.where(m, av * b_sh + bv, bv)
        av = jnp.where(m, av * a_sh, av)
    a_s[...] = av
    b_s[...] = bv

    def carry(g, h):
        i0 = pl.multiple_of(g * SUBLANES, SUBLANES)
        h8 = b_s[pl.ds(i0, SUBLANES), :] + a_s[pl.ds(i0, SUBLANES), :] * h
        b_s[pl.ds(i0, SUBLANES), :] = h8
        return h8[SUBLANES - 1:SUBLANES, :]

    hc[0:1, :] = lax.fori_loop(0, ts // SUBLANES, carry, hc[0:1, :], unroll=8)
    gr = gr_ref[0].astype(F32)
    y_ref[0] = (b_s[...] * jax.nn.gelu(gr)).astype(y_ref.dtype)


def _block_diag(w, per):
    H, d, _ = w.shape
    w4 = w.reshape(H // per, per, d, d)
    out = jnp.einsum('gpij,pq->gpiqj', w4, jnp.eye(per, dtype=w.dtype))
    return out.reshape(H // per, per * d, per * d)


def _rnn(proj3, conv_w, conv_b, w_rg_a, b_rg_a, w_rg_i, b_rg_i, lam, *, ts=512, cb=256):
    B, S, _ = proj3.shape
    d_rnn = conv_w.shape[-1]
    nc = d_rnn // cb
    per = cb // (d_rnn // RNN_HEADS)
    wg = jnp.concatenate([_block_diag(w_rg_a, per), _block_diag(w_rg_i, per)], axis=-1).astype(BF16)
    bg = jnp.concatenate([b_rg_a.reshape(nc, 1, cb), b_rg_i.reshape(nc, 1, cb)], axis=-1)
    return pl.pallas_call(
        functools.partial(_rnn_body, ts=ts, cb=cb),
        grid=(B, nc, S // ts),
        in_specs=[pl.BlockSpec((1, ts, cb), lambda b, c, s: (b, s, c)),
                  pl.BlockSpec((1, ts, cb), lambda b, c, s: (b, s, nc + c)),
                  pl.BlockSpec((conv_w.shape[0], cb), lambda b, c, s: (0, c)),
                  pl.BlockSpec((1, cb), lambda b, c, s: (0, c)),
                  pl.BlockSpec((1, cb, 2 * cb), lambda b, c, s: (c, 0, 0)),
                  pl.BlockSpec((None, 1, 2 * cb), lambda b, c, s: (c, 0, 0)),
                  pl.BlockSpec((1, cb), lambda b, c, s: (0, c))],
        out_specs=pl.BlockSpec((1, ts, cb), lambda b, c, s: (b, s, c)),
        out_shape=jax.ShapeDtypeStruct((B, S, d_rnn), BF16),
        scratch_shapes=[pltpu.VMEM((ts + SUBLANES, cb), F32), pltpu.VMEM((ts, cb), F32),
                        pltpu.VMEM((ts, cb), F32), pltpu.VMEM((SUBLANES, cb), F32)],
        compiler_params=_cparams("parallel", "parallel", "arbitrary"),
        name="rnn",
    )(proj3, proj3, conv_w, conv_b.reshape(1, d_rnn), wg, bg, lam.reshape(1, d_rnn))


def _t5_bucket(dist):
    max_exact = NUM_BUCKETS // 2
    d = np.maximum(dist, 1).astype(np.float64)
    large = max_exact + (np.log(d / max_exact) / np.log(MAX_DISTANCE / max_exact)
                         * (NUM_BUCKETS - max_exact)).astype(np.int64)
    large = np.minimum(large, NUM_BUCKETS - 1)
    return np.where(dist < max_exact, dist, large).astype(np.int32)


def _attn_bias(table, window, dilation):
    blk = window // dilation
    qi = np.arange(blk)[:, None]
    ki = np.arange(2 * blk)[None, :]
    rel = qi + blk - ki
    in_window = (rel >= 0) & (rel <= blk)
    bucket = _t5_bucket(np.clip(rel, 0, None) * dilation)
    onehot = (bucket[..., None] == np.arange(NUM_BUCKETS)).astype(np.float32)
    bias = jnp.einsum('qkn,nh->hqk', onehot, table.astype(F32), precision=lax.Precision.HIGHEST)
    first = in_window & (ki >= blk)
    return jnp.stack([jnp.where(first[None], bias, MASK_VALUE),
                      jnp.where(in_window[None], bias, MASK_VALUE)])


def _attn_heads(q, kp, kc, vp, vc, bias_ref):
    blk, gw = q.shape
    lo = lax.broadcasted_iota(I32, (blk, LANES), 1) < HEAD_DIM
    scale = HEAD_DIM ** -0.5
    o_parts, l_parts = [], []
    for p in range(gw // LANES):
        sl = slice(p * LANES, (p + 1) * LANES)
        q2 = q[:, sl] * scale
        k2 = jnp.concatenate([kp[:, sl], kc[:, sl]], axis=0)
        v2 = jnp.concatenate([vp[:, sl], vc[:, sl]], axis=0)
        outs, lses = [], []
        for hh in range(2):
            qm = jnp.where(lo if hh == 0 else jnp.logical_not(lo), q2, jnp.zeros_like(q2))
            sc = lax.dot_general(qm, k2, (((1,), (1,)), ((), ())), preferred_element_type=F32)
            sc = sc + bias_ref[0, 2 * p + hh]
            m = jnp.max(sc, axis=-1, keepdims=True)
            e = jnp.exp(sc - m)
            l = jnp.sum(e, axis=-1, keepdims=True)
            o = jnp.dot(e.astype(BF16), v2, preferred_element_type=F32)
            outs.append(o / l)
            lses.append(jnp.broadcast_to(m + jnp.log(l), (blk, LANES)))
        o_parts.append(jnp.where(lo, outs[0], outs[1]))
        l_parts.append(jnp.where(lo, lses[0], lses[1]))
    return jnp.concatenate(o_parts, axis=1), jnp.concatenate(l_parts, axis=1)


def _attn_body(q_ref, kp_ref, kc_ref, vp_ref, vc_ref, bias_ref, o_ref, lse_ref):
    o, lse = _attn_heads(q_ref[0], kp_ref[0], kc_ref[0], vp_ref[0], vc_ref[0], bias_ref)
    o_ref[0] = o.astype(o_ref.dtype)
    lse_ref[0] = lse


def _attn_dil_body(cur_ref, prev_ref, bias_ref, o_ref, lse_ref, oscr, lscr, *, d, blk, gw):
    nt = cur_ref.shape[0]
    nq = gw // LANES

    def rows(ref, r, c):
        parts = [ref[j, r, :, c * gw:(c + 1) * gw] for j in range(nt)]
        return parts[0] if nt == 1 else jnp.concatenate(parts, axis=0)

    def residue(r, carry):
        o, lse = _attn_heads(rows(cur_ref, r, 2), rows(prev_ref, r, 0), rows(cur_ref, r, 0),
                             rows(prev_ref, r, 1), rows(cur_ref, r, 1), bias_ref)
        for q in range(nq):
            oscr[q, pl.ds(r, blk, stride=d), :] = o[:, q * LANES:(q + 1) * LANES]
            lscr[q, pl.ds(r, blk, stride=d), :] = lse[:, q * LANES:(q + 1) * LANES]
        return carry

    lax.fori_loop(0, d, residue, 0)
    for q in range(nq):
        o_ref[0, :, q * LANES:(q + 1) * LANES] = oscr[q].astype(o_ref.dtype)
        lse_ref[0, :, q * LANES:(q + 1) * LANES] = lscr[q]


def _attn_dilated(qkv_t, bias, B, *, blk, d, gw):
    n_tiles, _, rows_t, _ = qkv_t.shape
    tm = rows_t * d
    nt = blk * d // tm
    nb = n_tiles // (B * nt)
    S = n_tiles * tm // B
    return pl.pallas_call(
        functools.partial(_attn_dil_body, d=d, blk=blk, gw=gw),
        grid=(B, nb),
        in_specs=[pl.BlockSpec((nt, d, rows_t, 3 * gw), lambda b, n: (b * nb + n, 0, 0, 0)),
                  pl.BlockSpec((nt, d, rows_t, 2 * gw), lambda b, n: (b * nb + jnp.maximum(n - 1, 0), 0, 0, 0)),
                  pl.BlockSpec((1,) + bias.shape[1:], lambda b, n: (jnp.minimum(n, 1), 0, 0, 0))],
        out_specs=[pl.BlockSpec((1, blk * d, gw), lambda b, n: (b, n, 0)),
                   pl.BlockSpec((1, blk * d, gw), lambda b, n: (b, n, 0))],
        out_shape=[jax.ShapeDtypeStruct((B, S, gw), BF16), jax.ShapeDtypeStruct((B, S, gw), F32)],
        scratch_shapes=[pltpu.VMEM((gw // LANES, blk * d, LANES), F32),
                        pltpu.VMEM((gw // LANES, blk * d, LANES), F32)],
        compiler_params=_cparams("parallel", "parallel"),
        name=f"attn_d{d}",
    )(qkv_t, qkv_t, bias)


def _attn_group(qkv, bias, *, blk, d, gw, q_blk, k_blk, v_blk, row_blk):
    B, L, _ = qkv.shape
    nb = L // blk

    def cur(col):
        return pl.BlockSpec((1, blk, gw), lambda b, r, n: (b, n, r * row_blk + col))

    def prev(col):
        return pl.BlockSpec((1, blk, gw), lambda b, r, n: (b, jnp.maximum(n - 1, 0), r * row_blk + col))

    return pl.pallas_call(
        _attn_body,
        grid=(B, d, nb),
        in_specs=[cur(q_blk), prev(k_blk), cur(k_blk), prev(v_blk), cur(v_blk),
                  pl.BlockSpec((1,) + bias.shape[1:], lambda b, r, n: (jnp.minimum(n, 1), 0, 0, 0))],
        out_specs=[pl.BlockSpec((1, blk, gw), lambda b, r, n: (b, n, r)),
                   pl.BlockSpec((1, blk, gw), lambda b, r, n: (b, n, r))],
        out_shape=[jax.ShapeDtypeStruct((B, L, d * gw), BF16),
                   jax.ShapeDtypeStruct((B, L, d * gw), F32)],
        compiler_params=_cparams("parallel", "parallel", "parallel"),
        name=f"attn_d{d}",
    )(qkv, qkv, qkv, qkv, qkv, bias)


def _merge_body(yr_ref, o1_ref, o2_ref, o3_ref, l1_ref, l2_ref, l3_ref, g_ref, x_ref,
                wr_ref, wa_ref, wo_ref, lg_ref, lb_ref, wrt_ref,
                x1_ref, x1p_ref, lt_ref, *, alpha):
    l1, l2, l3 = l1_ref[...], l2_ref[...], l3_ref[...]
    mx = jnp.maximum(jnp.maximum(l1, l2), l3)
    w1, w2, w3 = jnp.exp(l1 - mx), jnp.exp(l2 - mx), jnp.exp(l3 - mx)
    y_att = (w1 * o1_ref[...].astype(F32) + w2 * o2_ref[...].astype(F32)
             + w3 * o3_ref[...].astype(F32)) / (w1 + w2 + w3)
    pr = jnp.dot(yr_ref[...], wr_ref[...], preferred_element_type=F32)
    pa = jnp.dot(y_att.astype(BF16), wa_ref[...], preferred_element_type=F32)
    dm = pr.shape[1]
    g = g_ref[...].astype(F32)
    merged = _sigmoid(g[:, :dm]) * pr + _sigmoid(g[:, dm:]) * pa
    mix = jnp.dot(merged.astype(BF16), wo_ref[...], preferred_element_type=F32)
    z = alpha * x_ref[...] + mix
    mu = jnp.mean(z, axis=-1, keepdims=True)
    zc = z - mu
    var = jnp.mean(zc * zc, axis=-1, keepdims=True)
    x1 = zc * lax.rsqrt(var + LN_EPS) * lg_ref[...] + lb_ref[...]
    x1_ref[...] = x1
    x1p_ref[...] = _pack_rows(x1)
    lt_ref[...] = lax.dot_general(wrt_ref[...], x1, (((1,), (1,)), ((), ())),
                                  precision=lax.Precision.HIGHEST, preferred_element_type=F32)


def _merge(y_rnn, os_, lses, proj, gate_blk, x2, wr, wa, wo, ln_g, ln_b, w_router_t, *, alpha, tm=512):
    T, D = x2.shape
    da = os_[0].shape[1]
    E = w_router_t.shape[0]
    row = lambda w: pl.BlockSpec((tm, w), lambda i: (i, 0))
    full = lambda a: pl.BlockSpec(a.shape, lambda i: (0,) * a.ndim)
    return pl.pallas_call(
        functools.partial(_merge_body, alpha=alpha),
        grid=(T // tm,),
        in_specs=[row(D), row(da), row(da), row(da), row(da), row(da), row(da),
                  pl.BlockSpec((tm, 2 * D), lambda i: (i, gate_blk)), row(D),
                  full(wr), full(wa), full(wo), full(ln_g), full(ln_b), full(w_router_t)],
        out_specs=[row(D), row(D // 2), pl.BlockSpec((E, tm), lambda i: (0, i))],
        out_shape=[jax.ShapeDtypeStruct((T, D), F32), jax.ShapeDtypeStruct((T, D // 2), U32),
                   jax.ShapeDtypeStruct((E, T), F32)],
        compiler_params=_cparams("parallel"),
        name="merge",
    )(y_rnn, *os_, *lses, proj, x2, wr, wa, wo, ln_g, ln_b, w_router_t)


def _first_max(vals, idx, big):
    m = jnp.max(vals, axis=0, keepdims=True)
    i = jnp.min(jnp.where(vals == m, idx, big), axis=0, keepdims=True)
    return m, i


def _route_body(lt_ref, rb_ref, e_ref, g_ref, r_ref, cnt_ref, *, tl):
    E = lt_ref.shape[0]
    per = E // N_EXPERT_GROUPS
    neg = -jnp.inf

    @pl.when(pl.program_id(0) == 0)
    def _():
        cnt_ref[...] = jnp.zeros_like(cnt_ref)

    scores = jax.nn.sigmoid(lt_ref[...])
    sel = scores + rb_ref[...]
    rowi = lax.broadcasted_iota(I32, (E, tl), 0)

    gi = lax.broadcasted_iota(I32, (N_EXPERT_GROUPS, tl), 0)
    gsc = jnp.zeros((N_EXPERT_GROUPS, tl), F32)
    for g in range(N_EXPERT_GROUPS):
        blk = sel[g * per:(g + 1) * per]
        ri = lax.broadcasted_iota(I32, (per, tl), 0) + g * per
        m1, i1 = _first_max(blk, ri, E)
        m2 = jnp.max(jnp.where(ri == i1, neg, blk), axis=0, keepdims=True)
        gsc = jnp.where(gi == g, m1 + m2, gsc)
    keep = jnp.zeros((N_EXPERT_GROUPS, tl), F32)
    for _ in range(TOPK_GROUPS):
        _, ig = _first_max(gsc, gi, N_EXPERT_GROUPS)
        hit = gi == ig
        keep = jnp.where(hit, 1.0, keep)
        gsc = jnp.where(hit, neg, gsc)
    cur = jnp.concatenate(
        [jnp.where(keep[g:g + 1] > 0.5, sel[g * per:(g + 1) * per], neg) for g in range(N_EXPERT_GROUPS)],
        axis=0)

    ti = lax.broadcasted_iota(I32, (tl, tl), 0)
    tj = lax.broadcasted_iota(I32, (tl, tl), 1)
    earlier = jnp.where(ti < tj, 1.0, 0.0).astype(BF16)

    es, gv = [], []
    onehot = jnp.zeros((E, tl), F32)
    for k in range(TOP_K):
        _, ie = _first_max(cur, rowi, E)
        hit = rowi == ie
        es.append(ie)
        gv.append(jnp.sum(jnp.where(hit, scores, 0.0), axis=0, keepdims=True))
        onehot = jnp.where(hit, 1.0, onehot)
        cur = jnp.where(hit, neg, cur)
    gsum = gv[0]
    for k in range(1, TOP_K):
        gsum = gsum + gv[k]
    ranks = jnp.dot(onehot.astype(BF16), earlier, preferred_element_type=F32) + cnt_ref[...]
    for k in range(TOP_K):
        e_ref[k:k + 1, :] = es[k]
        g_ref[k:k + 1, :] = gv[k] / gsum * ROUTED_SCALE
        r_ref[k:k + 1, :] = jnp.sum(jnp.where(rowi == es[k], ranks, 0.0), axis=0, keepdims=True).astype(I32)
    cnt_ref[...] += jnp.sum(onehot, axis=1, keepdims=True)


def _route(logits_t, router_bias, *, tl=256):
    E, T = logits_t.shape
    kt = pl.BlockSpec((TOP_K, tl), lambda i: (0, i))
    return pl.pallas_call(
        functools.partial(_route_body, tl=tl),
        grid=(T // tl,),
        in_specs=[pl.BlockSpec((E, tl), lambda i: (0, i)), pl.BlockSpec((E, 1), lambda i: (0, 0))],
        out_specs=[kt, kt, kt, pl.BlockSpec((E, 1), lambda i: (0, 0))],
        out_shape=[jax.ShapeDtypeStruct((TOP_K, T), I32), jax.ShapeDtypeStruct((TOP_K, T), F32),
                   jax.ShapeDtypeStruct((TOP_K, T), I32), jax.ShapeDtypeStruct((E, 1), F32)],
        compiler_params=_cparams("arbitrary"),
        name="route",
    )(logits_t, router_bias.reshape(E, 1))


def _dest_body(e_ref, r_ref, ss_ref, d_ref):
    E = ss_ref.shape[0]
    tl = e_ref.shape[1]
    rowi = lax.broadcasted_iota(I32, (E, tl), 0)
    e = e_ref[...]
    rows = [jnp.sum(jnp.where(rowi == e[k:k + 1], ss_ref[...], 0), axis=0, keepdims=True)
            for k in range(e.shape[0])]
    d_ref[...] = jnp.concatenate(rows, axis=0) + r_ref[...]


def _dest(e_idx, rank, seg_start, *, tl=512):
    K, T = e_idx.shape
    E = seg_start.shape[0]
    kt = pl.BlockSpec((K, tl), lambda i: (0, i))
    return pl.pallas_call(
        _dest_body, grid=(T // tl,),
        in_specs=[kt, kt, pl.BlockSpec((E, 1), lambda i: (0, 0))],
        out_specs=kt, out_shape=jax.ShapeDtypeStruct((K, T), I32),
        compiler_params=_cparams("parallel"), name="dest",
    )(e_idx, rank, seg_start.reshape(E, 1))


def _pack_rows(v):
    w = v.shape[1] // 2
    lo = pltpu.bitcast(v[:, :w].astype(BF16).astype(F32), U32) >> 16
    hi = pltpu.bitcast(v[:, w:].astype(BF16).astype(F32), U32) & jnp.uint32(0xFFFF0000)
    return lo | hi


def _unpack_rows(p):
    lo = pltpu.bitcast(p << 16, F32)
    hi = pltpu.bitcast(p & jnp.uint32(0xFFFF0000), F32)
    return lo, hi


def _sc_scatter_rows(x, idx3, n_rows):
    from jax.experimental.pallas import tpu_sc as plsc
    n_chunks, K, CH = idx3.shape
    W = x.shape[1]
    info = plsc.get_sparse_core_info()
    NC, n_workers = info.num_cores, info.num_cores * info.num_subcores
    per_w = n_chunks // n_workers
    assert per_w * n_workers == n_chunks and per_w % 2 == 0

    def body(x_hbm, idx_hbm, out_hbm, idx_v, rows_v, sem_l, sem_s):
        c0 = (lax.axis_index("s") * NC + lax.axis_index("c")) * per_w

        def load(cc, b):
            return pltpu.make_async_copy(x_hbm.at[pl.ds(pl.multiple_of((c0 + cc) * CH, CH), CH)], rows_v.at[b],
                                         sem_l.at[b])

        def scatter(b, k):
            return pltpu.make_async_copy(rows_v.at[b], out_hbm.at[idx_v.at[b, k]], sem_s.at[b])

        pltpu.sync_copy(idx_hbm.at[c0], idx_v.at[0])
        load(0, 0).start()

        @pl.loop(0, per_w, step=2)
        def _(c):
            for b in range(2):
                cc = c + b
                load(cc, b).wait()
                for k in range(K):
                    scatter(b, k).start()

                @pl.when(cc >= 1)
                def _():
                    for k in range(K):
                        scatter(1 - b, k).wait()

                @pl.when(cc + 1 < per_w)
                def _():
                    pltpu.sync_copy(idx_hbm.at[c0 + cc + 1], idx_v.at[1 - b])
                    load(cc + 1, 1 - b).start()

        for k in range(K):
            scatter(1, k).wait()

    return pl.kernel(
        body, mesh=plsc.VectorSubcoreMesh(core_axis_name="c", subcore_axis_name="s"),
        out_type=jax.ShapeDtypeStruct((n_rows, W), x.dtype),
        scratch_types=[pltpu.VMEM((2, K, CH), I32), pltpu.VMEM((2, CH, W), x.dtype),
                       pltpu.SemaphoreType.DMA((2,)), pltpu.SemaphoreType.DMA((2,))],
    )(x, idx3)


def _pad_fill_body(ps_ref, pc_ref, nu_ref, xs_in, xs_hbm, zrow, sem_z, sem_c, sem_t):
    del xs_in
    i = pl.program_id(0)
    n = pl.num_programs(0)
    R = zrow.shape[0]

    def pad_copy(row):
        return pltpu.make_async_copy(zrow.at[pl.ds(0, 1), :], xs_hbm.at[pl.ds(row, 1), :], sem_z)

    zrow[...] = jnp.zeros_like(zrow)

    n_blk = xs_hbm.shape[0] // R
    tail_per_step = -(-n_blk // n)

    def tail(j, c, wait):
        blk = i * tail_per_step + j

        @pl.when(jnp.logical_and(blk >= nu_ref[0], blk < n_blk))
        def _():
            cp = pltpu.make_async_copy(zrow, xs_hbm.at[pl.ds(pl.multiple_of(blk * R, R), R), :], sem_t)
            cp.wait() if wait else cp.start()

        return c

    lax.fori_loop(0, tail_per_step, functools.partial(tail, wait=False), 0)

    E = ps_ref.shape[0]
    per_step = -(-E // n)

    def pads(j, c, wait):
        e = jnp.minimum(i * per_step + j, E - 1)
        cnt = jnp.where(i * per_step + j < E, pc_ref[e], 0)
        start = ps_ref[e]
        n_single = jnp.minimum(cnt, (-start) & (SUBLANES - 1))
        start8 = start + n_single

        def one(r, c2):
            cp = pad_copy(start + r)
            cp.wait() if wait else cp.start()
            return c2

        def eight(r, c2):
            row = pl.multiple_of(start8 + r * SUBLANES, SUBLANES)
            cp = pltpu.make_async_copy(zrow.at[pl.ds(0, SUBLANES), :], xs_hbm.at[pl.ds(row, SUBLANES), :], sem_c)
            cp.wait() if wait else cp.start()
            return c2

        c = lax.fori_loop(0, n_single, one, c)
        return lax.fori_loop(0, lax.shift_right_logical(cnt - n_single, 3), eight, c)

    lax.fori_loop(0, per_step, functools.partial(pads, wait=False), 0)
    lax.fori_loop(0, per_step, functools.partial(pads, wait=True), 0)
    lax.fori_loop(0, tail_per_step, functools.partial(tail, wait=True), 0)


def _pad_fill(xs, pad_start, pad_cnt, n_used, *, steps=16):
    any_spec = pl.BlockSpec(memory_space=pl.ANY)
    return pl.pallas_call(
        _pad_fill_body,
        grid_spec=pltpu.PrefetchScalarGridSpec(
            num_scalar_prefetch=3, grid=(steps,),
            in_specs=[any_spec], out_specs=any_spec,
            scratch_shapes=[pltpu.VMEM((EXPERT_ROWS, xs.shape[1]), xs.dtype),
                            pltpu.SemaphoreType.DMA, pltpu.SemaphoreType.DMA, pltpu.SemaphoreType.DMA]),
        out_shape=jax.ShapeDtypeStruct(xs.shape, xs.dtype),
        input_output_aliases={3: 0},
        compiler_params=_cparams("arbitrary"),
        name="pad_fill",
    )(pad_start, pad_cnt, n_used, xs)


def _experts_body(fb_ref, xs_hbm, wg_ref, wu_ref, wd_ref, ys_hbm, xbuf, ybuf, wgu_s, wd_s, sem_x, sem_y):
    e = pl.program_id(0)
    n_e = pl.num_programs(0)
    de = wg_ref.shape[2]
    R = xbuf.shape[1]
    n_blk = xs_hbm.shape[0] // R
    n_used = fb_ref[n_e]

    def rows(ref, g):
        return ref.at[pl.ds(pl.multiple_of(g * R, R), R), :]

    def x_copy(g, slot):
        return pltpu.make_async_copy(rows(xs_hbm, g), xbuf.at[slot], sem_x.at[slot])

    def y_copy(g, slot):
        return pltpu.make_async_copy(ybuf.at[slot], rows(ys_hbm, g), sem_y.at[slot])

    @pl.when(jnp.logical_and(e == 0, n_used > 0))
    def _():
        x_copy(0, 0).start()

    wgu_s[:, :de] = wg_ref[0].astype(BF16)
    wgu_s[:, de:] = wu_ref[0].astype(BF16)
    wd_s[...] = wd_ref[0].astype(BF16)

    def block(g, c):
        slot = g & 1
        x_copy(g, slot).wait()

        @pl.when(g + 1 < n_used)
        def _():
            x_copy(g + 1, 1 - slot).start()

        @pl.when(g >= 2)
        def _():
            y_copy(g - 2, slot).wait()

        lo, hi = _unpack_rows(xbuf[slot])
        xb = jnp.concatenate([lo.astype(BF16), hi.astype(BF16)], axis=1)
        h = jnp.dot(xb, wgu_s[...], preferred_element_type=F32)
        hg = h[:, :de]
        act = (hg * _sigmoid(hg) * h[:, de:]).astype(BF16)
        ybuf[slot] = _pack_rows(jnp.dot(act, wd_s[...], preferred_element_type=F32))
        y_copy(g, slot).start()
        return c

    lax.fori_loop(fb_ref[e], fb_ref[e + 1], block, 0)

    @pl.when(e == n_e - 1)
    def _():
        for back in (2, 1):
            g = n_used - back

            @pl.when(g >= 0)
            def _():
                y_copy(g, g & 1).wait()

        ybuf[0] = jnp.zeros(ybuf.shape[1:], ybuf.dtype)

        def tail(g, c, wait):
            cp = y_copy(g, 0)
            cp.wait() if wait else cp.start()
            return c

        lax.fori_loop(n_used, n_blk, functools.partial(tail, wait=False), 0)
        lax.fori_loop(n_used, n_blk, functools.partial(tail, wait=True), 0)


def _experts(xs, first_blk, wg, wu, wd):
    n_rows, W = xs.shape
    R = EXPERT_ROWS
    E, D, de = wg.shape
    any_spec = pl.BlockSpec(memory_space=pl.ANY)
    wmap = lambda e, fb: (e, 0, 0)
    return pl.pallas_call(
        _experts_body,
        grid_spec=pltpu.PrefetchScalarGridSpec(
            num_scalar_prefetch=1, grid=(E,),
            in_specs=[any_spec, pl.BlockSpec((1, D, de), wmap), pl.BlockSpec((1, D, de), wmap),
                      pl.BlockSpec((1, de, D), wmap)],
            out_specs=any_spec,
            scratch_shapes=[pltpu.VMEM((2, R, W), U32), pltpu.VMEM((2, R, W), U32),
                            pltpu.VMEM((D, 2 * de), BF16), pltpu.VMEM((de, D), BF16),
                            pltpu.SemaphoreType.DMA((2,)), pltpu.SemaphoreType.DMA((2,))]),
        out_shape=jax.ShapeDtypeStruct((n_rows, W), U32),
        compiler_params=_cparams("arbitrary"),
        name="experts",
    )(first_blk, xs, wg, wu, wd)


def _sc_gather_rows(table, idx2):
    from jax.experimental.pallas import tpu_sc as plsc
    n_chunks, CH = idx2.shape
    W = table.shape[1]
    info = plsc.get_sparse_core_info()
    NC, n_workers = info.num_cores, info.num_cores * info.num_subcores
    per_w = n_chunks // n_workers
    assert per_w * n_workers == n_chunks and per_w % 2 == 0

    def body(table_hbm, idx_hbm, out_hbm, idx_v, rows_v, sem):
        c0 = (lax.axis_index("s") * NC + lax.axis_index("c")) * per_w

        def gather(b):
            return pltpu.make_async_copy(table_hbm.at[idx_v.at[b]], rows_v.at[b], sem.at[b])

        pltpu.sync_copy(idx_hbm.at[c0], idx_v.at[0])
        gather(0).start()

        @pl.loop(0, per_w, step=2)
        def _(c):
            for b in range(2):
                cc = c + b

                @pl.when(cc + 1 < per_w)
                def _():
                    pltpu.sync_copy(idx_hbm.at[c0 + cc + 1], idx_v.at[1 - b])
                    gather(1 - b).start()

                gather(b).wait()
                pltpu.sync_copy(rows_v.at[b], out_hbm.at[pl.ds(pl.multiple_of((c0 + cc) * CH, CH), CH)])

    return pl.kernel(
        body, mesh=plsc.VectorSubcoreMesh(core_axis_name="c", subcore_axis_name="s"),
        out_type=jax.ShapeDtypeStruct((n_chunks * CH, W), table.dtype),
        scratch_types=[pltpu.VMEM((2, CH), I32), pltpu.VMEM((2, CH, W), table.dtype),
                       pltpu.SemaphoreType.DMA((2,))],
    )(table, idx2)


def _combine_body(yg_ref, gate_ref, x1_ref, wgu_ref, wd_ref, lg_ref, lb_ref, o_ref, *, alpha):
    x1 = x1_ref[...]
    ds_ = wd_ref.shape[0]
    h = jnp.dot(x1.astype(BF16), wgu_ref[...], preferred_element_type=F32)
    hg = h[:, :ds_]
    act = (hg * _sigmoid(hg) * h[:, ds_:]).astype(BF16)
    shared = jnp.dot(act, wd_ref[...], preferred_element_type=F32)

    g = gate_ref[...]
    lo_acc = hi_acc = None
    for k in range(yg_ref.shape[0]):
        lo, hi = _unpack_rows(yg_ref[k])
        gk = g[:, k:k + 1]
        lo_acc = gk * lo if k == 0 else lo_acc + gk * lo
        hi_acc = gk * hi if k == 0 else hi_acc + gk * hi
    routed = jnp.concatenate([lo_acc, hi_acc], axis=1)

    z = alpha * x1 + (routed + shared)
    mu = jnp.mean(z, axis=-1, keepdims=True)
    zc = z - mu
    var = jnp.mean(zc * zc, axis=-1, keepdims=True)
    o_ref[...] = zc * lax.rsqrt(var + LN_EPS) * lg_ref[...] + lb_ref[...]


def _combine(yg, gate_tk, x1, wgu, wd, ln_g, ln_b, *, alpha, tc):
    T, D = x1.shape
    K, _, W = yg.shape
    full = lambda a: pl.BlockSpec(a.shape, lambda i: (0,) * a.ndim)
    return pl.pallas_call(
        functools.partial(_combine_body, alpha=alpha),
        grid=(T // tc,),
        in_specs=[pl.BlockSpec((K, tc, W), lambda i: (0, i, 0)), pl.BlockSpec((tc, K), lambda i: (i, 0)),
                  pl.BlockSpec((tc, D), lambda i: (i, 0)), full(wgu), full(wd), full(ln_g), full(ln_b)],
        out_specs=pl.BlockSpec((tc, D), lambda i: (i, 0)),
        out_shape=jax.ShapeDtypeStruct((T, D), F32),
        compiler_params=_cparams("parallel"),
        name="combine",
    )(yg, gate_tk, x1, wgu, wd, ln_g, ln_b)


def _mixer_ln1(x, w_in, b_in, conv_w, conv_b, w_rg_a, b_rg_a, w_rg_i, b_rg_i, lru_lambda,
               w_proj_rnn, w_proj_att, rel_bias, w_out, ln1_g, ln1_b, w_router, alpha):
    B, S, D = x.shape
    T = B * S
    d_rnn = conv_w.shape[-1]
    gw = HEADS_PER_GROUP * HEAD_DIM
    d_att = gw * len(DILATED_GROUPS)
    a0 = 2 * d_rnn
    a1 = a0 + 3 * d_att
    head = lambda j, g: slice(a0 + j * d_att + g * gw, a0 + j * d_att + (g + 1) * gw)
    plain = [g for g, (_, d) in enumerate(DILATED_GROUPS) if d == 1]
    dilated = [g for g, (_, d) in enumerate(DILATED_GROUPS) if d > 1]
    order = ([slice(0, a0), slice(a1, None)] + [head(j, g) for g in plain for j in range(3)]
             + [head(j, g) for g in dilated for j in (1, 2, 0)])
    perm = lambda w: jnp.concatenate([w[..., s] for s in order], axis=-1)
    w_p = perm(w_in).astype(BF16)
    b_p = perm(b_in).reshape(1, -1)
    qkv0 = a0 + 2 * D
    n_main = qkv0 + 3 * gw * len(plain)
    x2 = x.reshape(T, D)

    proj, *qkv_dil = _in_proj(x2, w_p, b_p, n_main, tuple(DILATED_GROUPS[g][1] for g in dilated))
    proj3 = proj.reshape(B, S, n_main)
    y_rnn = _rnn(proj3, conv_w, conv_b, w_rg_a, b_rg_a, w_rg_i, b_rg_i, lru_lambda)

    os_, lses = [], []
    for g, (window, d) in enumerate(DILATED_GROUPS):
        blk = window // d
        bias = _attn_bias(rel_bias[:, g * HEADS_PER_GROUP:(g + 1) * HEADS_PER_GROUP], window, d)
        if d == 1:
            c0 = (qkv0 + 3 * gw * plain.index(g)) // gw
            o, lse = _attn_group(proj3, bias, blk=blk, d=1, gw=gw, q_blk=c0, k_blk=c0 + 1, v_blk=c0 + 2,
                                 row_blk=n_main // gw)
        else:
            o, lse = _attn_dilated(qkv_dil[dilated.index(g)], bias, B, blk=blk, d=d, gw=gw)
        os_.append(o.reshape(T, gw))
        lses.append(lse.reshape(T, gw))

    return _merge(y_rnn.reshape(T, d_rnn), os_, lses, proj, a0 // (2 * D), x2,
                  w_proj_rnn.astype(BF16), w_proj_att.astype(BF16), w_out.astype(BF16),
                  ln1_g.reshape(1, D), ln1_b.reshape(1, D), w_router.T, alpha=alpha)


def _moe_ln2(x1, x1p, logits_t, router_bias, w_exp_gate, w_exp_up, w_exp_down,
             w_sh_gate, w_sh_up, w_sh_down, ln2_g, ln2_b, alpha):
    T, D = x1.shape
    E = logits_t.shape[0]
    R = EXPERT_ROWS
    n_blk = T * TOP_K // R + E
    n_rows = n_blk * R

    e_idx, gate, rank, counts = _route(logits_t, router_bias)
    counts = counts.reshape(E).astype(I32)
    padded = (counts + R - 1) // R * R
    pad_end = jnp.cumsum(padded)
    seg_start = pad_end - padded
    n_used = (pad_end[-1:] // R).astype(I32)
    first_blk = jnp.concatenate([jnp.zeros((1,), I32), (pad_end // R).astype(I32)])
    dest = _dest(e_idx, rank, seg_start)
    tl = ROW_TILE
    ch = SC_GATHER_CHUNK
    dest_c = dest.reshape(TOP_K, T // ch, ch).transpose(1, 0, 2)

    xs = _sc_scatter_rows(x1p, dest_c, n_rows)
    xs = _pad_fill(xs, seg_start + counts, padded - counts, n_used)
    ys = _experts(xs, first_blk, w_exp_gate, w_exp_up, w_exp_down)
    wgu = jnp.concatenate([w_sh_gate, w_sh_up], axis=-1).astype(BF16)
    yg = _sc_gather_rows(ys, dest.reshape(-1, SC_GATHER_CHUNK)).reshape(TOP_K, T, ys.shape[1])
    return _combine(yg, gate.T, x1, wgu, w_sh_down.astype(BF16),
                    ln2_g.reshape(1, D), ln2_b.reshape(1, D), alpha=alpha, tc=tl)


def kernel(x, w_in, b_in, conv_w, conv_b, w_rg_a, b_rg_a, w_rg_i, b_rg_i, lru_lambda,
           w_proj_rnn, w_proj_att, rel_bias, w_out, ln1_g, ln1_b, w_router, router_bias,
           w_exp_gate, w_exp_up, w_exp_down, w_sh_gate, w_sh_up, w_sh_down, ln2_g, ln2_b):
    B, S, D = x.shape
    depth = w_in.shape[0]
    alpha = (2 * depth) ** 0.25
    for i in range(depth):
        x1, x1p, logits_t = _mixer_ln1(
            x, w_in[i], b_in[i], conv_w[i], conv_b[i], w_rg_a[i], b_rg_a[i], w_rg_i[i], b_rg_i[i],
            lru_lambda[i], w_proj_rnn[i], w_proj_att[i], rel_bias, w_out[i], ln1_g[i], ln1_b[i],
            w_router[i], alpha)
        out = _moe_ln2(x1, x1p, logits_t, router_bias[i], w_exp_gate[i], w_exp_up[i], w_exp_down[i],
                       w_sh_gate[i], w_sh_up[i], w_sh_down[i], ln2_g[i], ln2_b[i], alpha)
        x = out.reshape(B, S, D)
    return x
```

```python
import functools

import numpy as np
import jax
import jax.numpy as jnp
from jax import lax
from jax.experimental import pallas as pl
from jax.experimental.pallas import tpu as pltpu

F32 = jnp.float32
BF16 = jnp.bfloat16
I32 = jnp.int32
U32 = jnp.uint32

RNN_HEADS = 16
LRU_C = 8.0
HEAD_DIM = 64
HEADS_PER_GROUP = 8
DILATED_GROUPS = ((128, 1), (512, 4), (2048, 16))
NUM_BUCKETS = 32
MAX_DISTANCE = 2048
TOP_K = 8
N_EXPERT_GROUPS = 8
TOPK_GROUPS = 4
ROUTED_SCALE = 2.5
LN_EPS = 1e-5

LANES = 128
SUBLANES = 8
VMEM_LIMIT_BYTES = 56 * 1024 * 1024

MASK_VALUE = -1e30
EXPERT_ROWS = 512
ROW_TILE = 256
SC_GATHER_CHUNK = 64


def _cparams(*sem):
    return pltpu.CompilerParams(dimension_semantics=sem, vmem_limit_bytes=VMEM_LIMIT_BYTES)


def _sigmoid(v):
    return 0.5 * (jnp.tanh(0.5 * v) + 1.0)


def _in_proj_body(x_ref, w_ref, b_ref, main_ref, *rest, tn, dils):
    dil_refs, scr = rest[:-1], rest[-1]
    xb = x_ref[...].astype(BF16)
    tm = xb.shape[0]

    def chunk(j):
        sl = slice(j * tn, (j + 1) * tn)
        return jnp.dot(xb, w_ref[:, sl], preferred_element_type=F32) + b_ref[:, sl]

    n_main = main_ref.shape[1] // tn
    for j in range(n_main):
        main_ref[:, j * tn:(j + 1) * tn] = chunk(j).astype(main_ref.dtype)
    j = n_main
    for ref, d in zip(dil_refs, dils):
        for c in range(ref.shape[3] // tn):
            acc = chunk(j)
            j += 1
            for q in range(tn // LANES):
                scr[q] = acc[:, q * LANES:(q + 1) * LANES]
            for r in range(d):
                part = jnp.concatenate([scr[q, pl.ds(r, tm // d, stride=d), :] for q in range(tn // LANES)], axis=1)
                ref[0, r, :, c * tn:(c + 1) * tn] = part.astype(ref.dtype)


def _in_proj(x2, w, b, n_main, dils, *, tm=512, tn=512):
    T, D = x2.shape
    N = w.shape[1]
    wd = (N - n_main) // len(dils)
    out_specs = [pl.BlockSpec((tm, n_main), lambda i: (i, 0))]
    out_shape = [jax.ShapeDtypeStruct((T, n_main), BF16)]
    for d in dils:
        out_specs.append(pl.BlockSpec((1, d, tm // d, wd), lambda i: (i, 0, 0, 0)))
        out_shape.append(jax.ShapeDtypeStruct((T // tm, d, tm // d, wd), BF16))
    return pl.pallas_call(
        functools.partial(_in_proj_body, tn=tn, dils=dils),
        grid=(T // tm,),
        in_specs=[pl.BlockSpec((tm, D), lambda i: (i, 0)),
                  pl.BlockSpec((D, N), lambda i: (0, 0), pipeline_mode=pl.Buffered(1)),
                  pl.BlockSpec((1, N), lambda i: (0, 0))],
        out_specs=out_specs,
        out_shape=out_shape,
        scratch_shapes=[pltpu.VMEM((tn // LANES, tm, LANES), F32)],
        compiler_params=_cparams("parallel"),
        name="in_proj",
    )(x2, w, b)


def _rnn_body(xr_ref, gr_ref, cw_ref, cb_ref, wg_ref, bg_ref, lam_ref, y_ref,
              xext, a_s, b_s, hc, *, ts, cb):
    s = pl.program_id(2)

    @pl.when(s == 0)
    def _():
        xext[0:SUBLANES, :] = jnp.zeros((SUBLANES, cb), F32)
        hc[...] = jnp.zeros_like(hc)

    xr = xr_ref[0].astype(F32)
    xext[SUBLANES:SUBLANES + ts, :] = xr
    nw = cw_ref.shape[0]
    xc = cw_ref[nw - 1:nw, :] * xr + cb_ref[...]
    for j in range(nw - 1):
        off = SUBLANES - (nw - 1 - j)
        xc = xc + cw_ref[j:j + 1, :] * xext[off:off + ts, :]
    xext[0:SUBLANES, :] = xext[ts:ts + SUBLANES, :]

    gates = jnp.dot(xc.astype(BF16), wg_ref[0], preferred_element_type=F32) + bg_ref[...]
    r = _sigmoid(gates[:, :cb])
    ig = _sigmoid(gates[:, cb:])
    nl = -lam_ref[...]
    sp = jnp.maximum(nl, 0.0) + jnp.log1p(jnp.exp(-jnp.abs(nl)))
    log_a = (-LRU_C) * r * sp
    a = jnp.exp(log_a)
    u = jnp.sqrt(1.0 - a * a) * (ig * xc)

    row = lax.broadcasted_iota(I32, (ts, cb), 0) & (SUBLANES - 1)
    av, bv = a, u
    for sft in (1, 2, 4):
        a_sh = pltpu.roll(av, sft, 0)
        b_sh = pltpu.roll(bv, sft, 0)
        m = row >= sft
        bv = jnp.where(m, av * b_sh + bv, bv)
        av = jnp.where(m, av * a_sh, av)
    a_s[...] = av
    b_s[...] = bv

    def carry(g, h):
        i0 = pl.multiple_of(g * SUBLANES, SUBLANES)
        h8 = b_s[pl.ds(i0, SUBLANES), :] + a_s[pl.ds(i0, SUBLANES), :] * h
        b_s[pl.ds(i0, SUBLANES), :] = h8
        return h8[SUBLANES - 1:SUBLANES, :]

    hc[0:1, :] = lax.fori_loop(0, ts // SUBLANES, carry, hc[0:1, :], unroll=8)
    gr = gr_ref[0].astype(F32)
    y_ref[0] = (b_s[...] * jax.nn.gelu(gr)).astype(y_ref.dtype)


def _block_diag(w, per):
    H, d, _ = w.shape
    w4 = w.reshape(H // per, per, d, d)
    out = jnp.einsum('gpij,pq->gpiqj', w4, jnp.eye(per, dtype=w.dtype))
    return out.reshape(H // per, per * d, per * d)


def _rnn(proj3, conv_w, conv_b, w_rg_a, b_rg_a, w_rg_i, b_rg_i, lam, *, ts=512, cb=256):
    B, S, _ = proj3.shape
    d_rnn = conv_w.shape[-1]
    nc = d_rnn // cb
    per = cb // (d_rnn // RNN_HEADS)
    wg = jnp.concatenate([_block_diag(w_rg_a, per), _block_diag(w_rg_i, per)], axis=-1).astype(BF16)
    bg = jnp.concatenate([b_rg_a.reshape(nc, 1, cb), b_rg_i.reshape(nc, 1, cb)], axis=-1)
    return pl.pallas_call(
        functools.partial(_rnn_body, ts=ts, cb=cb),
        grid=(B, nc, S // ts),
        in_specs=[pl.BlockSpec((1, ts, cb), lambda b, c, s: (b, s, c)),
                  pl.BlockSpec((1, ts, cb), lambda b, c, s: (b, s, nc + c)),
                  pl.BlockSpec((conv_w.shape[0], cb), lambda b, c, s: (0, c)),
                  pl.BlockSpec((1, cb), lambda b, c, s: (0, c)),
                  pl.BlockSpec((1, cb, 2 * cb), lambda b, c, s: (c, 0, 0)),
                  pl.BlockSpec((None, 1, 2 * cb), lambda b, c, s: (c, 0, 0)),
                  pl.BlockSpec((1, cb), lambda b, c, s: (0, c))],
        out_specs=pl.BlockSpec((1, ts, cb), lambda b, c, s: (b, s, c)),
        out_shape=jax.ShapeDtypeStruct((B, S, d_rnn), BF16),
        scratch_shapes=[pltpu.VMEM((ts + SUBLANES, cb), F32), pltpu.VMEM((ts, cb), F32),
                        pltpu.VMEM((ts, cb), F32), pltpu.VMEM((SUBLANES, cb), F32)],
        compiler_params=_cparams("parallel", "parallel", "arbitrary"),
        name="rnn",
    )(proj3, proj3, conv_w, conv_b.reshape(1, d_rnn), wg, bg, lam.reshape(1, d_rnn))


def _t5_bucket(dist):
    max_exact = NUM_BUCKETS // 2
    d = np.maximum(dist, 1).astype(np.float64)
    large = max_exact + (np.log(d / max_exact) / np.log(MAX_DISTANCE / max_exact)
                         * (NUM_BUCKETS - max_exact)).astype(np.int64)
    large = np.minimum(large, NUM_BUCKETS - 1)
    return np.where(dist < max_exact, dist, large).astype(np.int32)


def _attn_bias(table, window, dilation):
    blk = window // dilation
    qi = np.arange(blk)[:, None]
    ki = np.arange(2 * blk)[None, :]
    rel = qi + blk - ki
    in_window = (rel >= 0) & (rel <= blk)
    bucket = _t5_bucket(np.clip(rel, 0, None) * dilation)
    onehot = (bucket[..., None] == np.arange(NUM_BUCKETS)).astype(np.float32)
    bias = jnp.einsum('qkn,nh->hqk', onehot, table.astype(F32), precision=lax.Precision.HIGHEST)
    first = in_window & (ki >= blk)
    return jnp.stack([jnp.where(first[None], bias, MASK_VALUE),
                      jnp.where(in_window[None], bias, MASK_VALUE)])


def _attn_heads(q, kp, kc, vp, vc, bias_ref):
    blk, gw = q.shape
    lo = lax.broadcasted_iota(I32, (blk, LANES), 1) < HEAD_DIM
    scale = HEAD_DIM ** -0.5
    o_parts, l_parts = [], []
    for p in range(gw // LANES):
        sl = slice(p * LANES, (p + 1) * LANES)
        q2 = q[:, sl] * scale
        k2 = jnp.concatenate([kp[:, sl], kc[:, sl]], axis=0)
        v2 = jnp.concatenate([vp[:, sl], vc[:, sl]], axis=0)
        outs, lses = [], []
        for hh in range(2):
            qm = jnp.where(lo if hh == 0 else jnp.logical_not(lo), q2, jnp.zeros_like(q2))
            sc = lax.dot_general(qm, k2, (((1,), (1,)), ((), ())), preferred_element_type=F32)
            sc = sc + bias_ref[0, 2 * p + hh]
            m = jnp.max(sc, axis=-1, keepdims=True)
            e = jnp.exp(sc - m)
            l = jnp.sum(e, axis=-1, keepdims=True)
            o = jnp.dot(e.astype(BF16), v2, preferred_element_type=F32)
            outs.append(o / l)
            lses.append(jnp.broadcast_to(m + jnp.log(l), (blk, LANES)))
        o_parts.append(jnp.where(lo, outs[0], outs[1]))
        l_parts.append(jnp.where(lo, lses[0], lses[1]))
    return jnp.concatenate(o_parts, axis=1), jnp.concatenate(l_parts, axis=1)


def _attn_body(q_ref, kp_ref, kc_ref, vp_ref, vc_ref, bias_ref, o_ref, lse_ref):
    o, lse = _attn_heads(q_ref[0], kp_ref[0], kc_ref[0], vp_ref[0], vc_ref[0], bias_ref)
    o_ref[0] = o.astype(o_ref.dtype)
    lse_ref[0] = lse


def _attn_dil_body(cur_ref, prev_ref, bias_ref, o_ref, lse_ref, oscr, lscr, *, d, blk, gw):
    nt = cur_ref.shape[0]
    nq = gw // LANES

    def rows(ref, r, c):
        parts = [ref[j, r, :, c * gw:(c + 1) * gw] for j in range(nt)]
        return parts[0] if nt == 1 else jnp.concatenate(parts, axis=0)

    def residue(r, carry):
        o, lse = _attn_heads(rows(cur_ref, r, 2), rows(prev_ref, r, 0), rows(cur_ref, r, 0),
                             rows(prev_ref, r, 1), rows(cur_ref, r, 1), bias_ref)
        for q in range(nq):
            oscr[q, pl.ds(r, blk, stride=d), :] = o[:, q * LANES:(q + 1) * LANES]
            lscr[q, pl.ds(r, blk, stride=d), :] = lse[:, q * LANES:(q + 1) * LANES]
        return carry

    lax.fori_loop(0, d, residue, 0)
    for q in range(nq):
        o_ref[0, :, q * LANES:(q + 1) * LANES] = oscr[q].astype(o_ref.dtype)
        lse_ref[0, :, q * LANES:(q + 1) * LANES] = lscr[q]


def _attn_dilated(qkv_t, bias, B, *, blk, d, gw):
    n_tiles, _, rows_t, _ = qkv_t.shape
    tm = rows_t * d
    nt = blk * d // tm
    nb = n_tiles // (B * nt)
    S = n_tiles * tm // B
    return pl.pallas_call(
        functools.partial(_attn_dil_body, d=d, blk=blk, gw=gw),
        grid=(B, nb),
        in_specs=[pl.BlockSpec((nt, d, rows_t, 3 * gw), lambda b, n: (b * nb + n, 0, 0, 0)),
                  pl.BlockSpec((nt, d, rows_t, 2 * gw), lambda b, n: (b * nb + jnp.maximum(n - 1, 0), 0, 0, 0)),
                  pl.BlockSpec((1,) + bias.shape[1:], lambda b, n: (jnp.minimum(n, 1), 0, 0, 0))],
        out_specs=[pl.BlockSpec((1, blk * d, gw), lambda b, n: (b, n, 0)),
                   pl.BlockSpec((1, blk * d, gw), lambda b, n: (b, n, 0))],
        out_shape=[jax.ShapeDtypeStruct((B, S, gw), BF16), jax.ShapeDtypeStruct((B, S, gw), F32)],
        scratch_shapes=[pltpu.VMEM((gw // LANES, blk * d, LANES), F32),
                        pltpu.VMEM((gw // LANES, blk * d, LANES), F32)],
        compiler_params=_cparams("parallel", "parallel"),
        name=f"attn_d{d}",
    )(qkv_t, qkv_t, bias)


def _attn_group(qkv, bias, *, blk, d, gw, q_blk, k_blk, v_blk, row_blk):
    B, L, _ = qkv.shape
    nb = L // blk

    def cur(col):
        return pl.BlockSpec((1, blk, gw), lambda b, r, n: (b, n, r * row_blk + col))

    def prev(col):
        return pl.BlockSpec((1, blk, gw), lambda b, r, n: (b, jnp.maximum(n - 1, 0), r * row_blk + col))

    return pl.pallas_call(
        _attn_body,
        grid=(B, d, nb),
        in_specs=[cur(q_blk), prev(k_blk), cur(k_blk), prev(v_blk), cur(v_blk),
                  pl.BlockSpec((1,) + bias.shape[1:], lambda b, r, n: (jnp.minimum(n, 1), 0, 0, 0))],
        out_specs=[pl.BlockSpec((1, blk, gw), lambda b, r, n: (b, n, r)),
                   pl.BlockSpec((1, blk, gw), lambda b, r, n: (b, n, r))],
        out_shape=[jax.ShapeDtypeStruct((B, L, d * gw), BF16),
                   jax.ShapeDtypeStruct((B, L, d * gw), F32)],
        compiler_params=_cparams("parallel", "parallel", "parallel"),
        name=f"attn_d{d}",
    )(qkv, qkv, qkv, qkv, qkv, bias)


def _merge_body(yr_ref, o1_ref, o2_ref, o3_ref, l1_ref, l2_ref, l3_ref, g_ref, x_ref,
                wr_ref, wa_ref, wo_ref, lg_ref, lb_ref, wrt_ref,
                x1_ref, x1p_ref, lt_ref, *, alpha):
    l1, l2, l3 = l1_ref[...], l2_ref[...], l3_ref[...]
    mx = jnp.maximum(jnp.maximum(l1, l2), l3)
    w1, w2, w3 = jnp.exp(l1 - mx), jnp.exp(l2 - mx), jnp.exp(l3 - mx)
    y_att = (w1 * o1_ref[...].astype(F32) + w2 * o2_ref[...].astype(F32)
             + w3 * o3_ref[...].astype(F32)) / (w1 + w2 + w3)
    pr = jnp.dot(yr_ref[...], wr_ref[...], preferred_element_type=F32)
    pa = jnp.dot(y_att.astype(BF16), wa_ref[...], preferred_element_type=F32)
    dm = pr.shape[1]
    g = g_ref[...].astype(F32)
    merged = _sigmoid(g[:, :dm]) * pr + _sigmoid(g[:, dm:]) * pa
    mix = jnp.dot(merged.astype(BF16), wo_ref[...], preferred_element_type=F32)
    z = alpha * x_ref[...] + mix
    mu = jnp.mean(z, axis=-1, keepdims=True)
    zc = z - mu
    var = jnp.mean(zc * zc, axis=-1, keepdims=True)
    x1 = zc * lax.rsqrt(var + LN_EPS) * lg_ref[...] + lb_ref[...]
    x1_ref[...] = x1
    x1p_ref[...] = _pack_rows(x1)
    lt_ref[...] = lax.dot_general(wrt_ref[...], x1, (((1,), (1,)), ((), ())),
                                  precision=lax.Precision.HIGHEST, preferred_element_type=F32)


def _merge(y_rnn, os_, lses, proj, gate_blk, x2, wr, wa, wo, ln_g, ln_b, w_router_t, *, alpha, tm=512):
    T, D = x2.shape
    da = os_[0].shape[1]
    E = w_router_t.shape[0]
    row = lambda w: pl.BlockSpec((tm, w), lambda i: (i, 0))
    full = lambda a: pl.BlockSpec(a.shape, lambda i: (0,) * a.ndim)
    return pl.pallas_call(
        functools.partial(_merge_body, alpha=alpha),
        grid=(T // tm,),
        in_specs=[row(D), row(da), row(da), row(da), row(da), row(da), row(da),
                  pl.BlockSpec((tm, 2 * D), lambda i: (i, gate_blk)), row(D),
                  full(wr), full(wa), full(wo), full(ln_g), full(ln_b), full(w_router_t)],
        out_specs=[row(D), row(D // 2), pl.BlockSpec((E, tm), lambda i: (0, i))],
        out_shape=[jax.ShapeDtypeStruct((T, D), F32), jax.ShapeDtypeStruct((T, D // 2), U32),
                   jax.ShapeDtypeStruct((E, T), F32)],
        compiler_params=_cparams("parallel"),
        name="merge",
    )(y_rnn, *os_, *lses, proj, x2, wr, wa, wo, ln_g, ln_b, w_router_t)


def _first_max(vals, idx, big):
    m = jnp.max(vals, axis=0, keepdims=True)
    i = jnp.min(jnp.where(vals == m, idx, big), axis=0, keepdims=True)
    return m, i


def _route_body(lt_ref, rb_ref, e_ref, g_ref, r_ref, cnt_ref, *, tl):
    E = lt_ref.shape[0]
    per = E // N_EXPERT_GROUPS
    neg = -jnp.inf

    @pl.when(pl.program_id(0) == 0)
    def _():
        cnt_ref[...] = jnp.zeros_like(cnt_ref)

    scores = jax.nn.sigmoid(lt_ref[...])
    sel = scores + rb_ref[...]
    rowi = lax.broadcasted_iota(I32, (E, tl), 0)

    gi = lax.broadcasted_iota(I32, (N_EXPERT_GROUPS, tl), 0)
    gsc = jnp.zeros((N_EXPERT_GROUPS, tl), F32)
    for g in range(N_EXPERT_GROUPS):
        blk = sel[g * per:(g + 1) * per]
        ri = lax.broadcasted_iota(I32, (per, tl), 0) + g * per
        m1, i1 = _first_max(blk, ri, E)
        m2 = jnp.max(jnp.where(ri == i1, neg, blk), axis=0, keepdims=True)
        gsc = jnp.where(gi == g, m1 + m2, gsc)
    keep = jnp.zeros((N_EXPERT_GROUPS, tl), F32)
    for _ in range(TOPK_GROUPS):
        _, ig = _first_max(gsc, gi, N_EXPERT_GROUPS)
        hit = gi == ig
        keep = jnp.where(hit, 1.0, keep)
        gsc = jnp.where(hit, neg, gsc)
    cur = jnp.concatenate(
        [jnp.where(keep[g:g + 1] > 0.5, sel[g * per:(g + 1) * per], neg) for g in range(N_EXPERT_GROUPS)],
        axis=0)

    ti = lax.broadcasted_iota(I32, (tl, tl), 0)
    tj = lax.broadcasted_iota(I32, (tl, tl), 1)
    earlier = jnp.where(ti < tj, 1.0, 0.0).astype(BF16)

    es, gv = [], []
    onehot = jnp.zeros((E, tl), F32)
    for k in range(TOP_K):
        _, ie = _first_max(cur, rowi, E)
        hit = rowi == ie
        es.append(ie)
        gv.append(jnp.sum(jnp.where(hit, scores, 0.0), axis=0, keepdims=True))
        onehot = jnp.where(hit, 1.0, onehot)
        cur = jnp.where(hit, neg, cur)
    gsum = gv[0]
    for k in range(1, TOP_K):
        gsum = gsum + gv[k]
    ranks = jnp.dot(onehot.astype(BF16), earlier, preferred_element_type=F32) + cnt_ref[...]
    for k in range(TOP_K):
        e_ref[k:k + 1, :] = es[k]
        g_ref[k:k + 1, :] = gv[k] / gsum * ROUTED_SCALE
        r_ref[k:k + 1, :] = jnp.sum(jnp.where(rowi == es[k], ranks, 0.0), axis=0, keepdims=True).astype(I32)
    cnt_ref[...] += jnp.sum(onehot, axis=1, keepdims=True)


def _route(logits_t, router_bias, *, tl=256):
    E, T = logits_t.shape
    kt = pl.BlockSpec((TOP_K, tl), lambda i: (0, i))
    return pl.pallas_call(
        functools.partial(_route_body, tl=tl),
        grid=(T // tl,),
        in_specs=[pl.BlockSpec((E, tl), lambda i: (0, i)), pl.BlockSpec((E, 1), lambda i: (0, 0))],
        out_specs=[kt, kt, kt, pl.BlockSpec((E, 1), lambda i: (0, 0))],
        out_shape=[jax.ShapeDtypeStruct((TOP_K, T), I32), jax.ShapeDtypeStruct((TOP_K, T), F32),
                   jax.ShapeDtypeStruct((TOP_K, T), I32), jax.ShapeDtypeStruct((E, 1), F32)],
        compiler_params=_cparams("arbitrary"),
        name="route",
    )(logits_t, router_bias.reshape(E, 1))


def _dest_body(e_ref, r_ref, ss_ref, d_ref):
    E = ss_ref.shape[0]
    tl = e_ref.shape[1]
    rowi = lax.broadcasted_iota(I32, (E, tl), 0)
    e = e_ref[...]
    rows = [jnp.sum(jnp.where(rowi == e[k:k + 1], ss_ref[...], 0), axis=0, keepdims=True)
            for k in range(e.shape[0])]
    d_ref[...] = jnp.concatenate(rows, axis=0) + r_ref[...]


def _dest(e_idx, rank, seg_start, *, tl=512):
    K, T = e_idx.shape
    E = seg_start.shape[0]
    kt = pl.BlockSpec((K, tl), lambda i: (0, i))
    return pl.pallas_call(
        _dest_body, grid=(T // tl,),
        in_specs=[kt, kt, pl.BlockSpec((E, 1), lambda i: (0, 0))],
        out_specs=kt, out_shape=jax.ShapeDtypeStruct((K, T), I32),
        compiler_params=_cparams("parallel"), name="dest",
    )(e_idx, rank, seg_start.reshape(E, 1))


def _pack_rows(v):
    w = v.shape[1] // 2
    lo = pltpu.bitcast(v[:, :w].astype(BF16).astype(F32), U32) >> 16
    hi = pltpu.bitcast(v[:, w:].astype(BF16).astype(F32), U32) & jnp.uint32(0xFFFF0000)
    return lo | hi


def _unpack_rows(p):
    lo = pltpu.bitcast(p << 16, F32)
    hi = pltpu.bitcast(p & jnp.uint32(0xFFFF0000), F32)
    return lo, hi


def _sc_scatter_rows(x, idx3, n_rows):
    from jax.experimental.pallas import tpu_sc as plsc
    n_chunks, K, CH = idx3.shape
    W = x.shape[1]
    info = plsc.get_sparse_core_info()
    NC, n_workers = info.num_cores, info.num_cores * info.num_subcores
    per_w = n_chunks // n_workers
    assert per_w * n_workers == n_chunks and per_w % 2 == 0

    def body(x_hbm, idx_hbm, out_hbm, idx_v, rows_v, sem_l, sem_s):
        c0 = (lax.axis_index("s") * NC + lax.axis_index("c")) * per_w

        def load(cc, b):
            return pltpu.make_async_copy(x_hbm.at[pl.ds(pl.multiple_of((c0 + cc) * CH, CH), CH)], rows_v.at[b],
                                         sem_l.at[b])

        def scatter(b, k):
            return pltpu.make_async_copy(rows_v.at[b], out_hbm.at[idx_v.at[b, k]], sem_s.at[b])

        pltpu.sync_copy(idx_hbm.at[c0], idx_v.at[0])
        load(0, 0).start()

        @pl.loop(0, per_w, step=2)
        def _(c):
            for b in range(2):
                cc = c + b
                load(cc, b).wait()
                for k in range(K):
                    scatter(b, k).start()

                @pl.when(cc >= 1)
                def _():
                    for k in range(K):
                        scatter(1 - b, k).wait()

                @pl.when(cc + 1 < per_w)
                def _():
                    pltpu.sync_copy(idx_hbm.at[c0 + cc + 1], idx_v.at[1 - b])
                    load(cc + 1, 1 - b).start()

        for k in range(K):
            scatter(1, k).wait()

    return pl.kernel(
        body, mesh=plsc.VectorSubcoreMesh(core_axis_name="c", subcore_axis_name="s"),
        out_type=jax.ShapeDtypeStruct((n_rows, W), x.dtype),
        scratch_types=[pltpu.VMEM((2, K, CH), I32), pltpu.VMEM((2, CH, W), x.dtype),
                       pltpu.SemaphoreType.DMA((2,)), pltpu.SemaphoreType.DMA((2,))],
    )(x, idx3)


def _pad_fill_body(ps_ref, pc_ref, nu_ref, xs_in, xs_hbm, zrow, sem_z, sem_c, sem_b, sem_t):
    del xs_in
    i = pl.program_id(0)
    n = pl.num_programs(0)
    R = zrow.shape[0]

    def pad_copy(row):
        return pltpu.make_async_copy(zrow.at[pl.ds(0, 1), :], xs_hbm.at[pl.ds(row, 1), :], sem_z)

    zrow[...] = jnp.zeros_like(zrow)

    n_blk = xs_hbm.shape[0] // R
    tail_per_step = -(-n_blk // n)

    def tail(j, c, wait):
        blk = i * tail_per_step + j

        @pl.when(jnp.logical_and(blk >= nu_ref[0], blk < n_blk))
        def _():
            cp = pltpu.make_async_copy(zrow, xs_hbm.at[pl.ds(pl.multiple_of(blk * R, R), R), :], sem_t)
            cp.wait() if wait else cp.start()

        return c

    lax.fori_loop(0, tail_per_step, functools.partial(tail, wait=False), 0)

    E = ps_ref.shape[0]
    per_step = -(-E // n)

    def pads(j, c, wait):
        e = jnp.minimum(i * per_step + j, E - 1)
        cnt = jnp.where(i * per_step + j < E, pc_ref[e], 0)
        start = ps_ref[e]
        n_single = jnp.minimum(cnt, (-start) & (SUBLANES - 1))
        start8 = start + n_single

        def one(r, c2):
            cp = pad_copy(start + r)
            cp.wait() if wait else cp.start()
            return c2

        rem8 = lax.shift_right_logical(cnt - n_single, 3)
        n_eight = jnp.minimum(rem8, lax.shift_right_logical(-start8, 3) & (SUBLANES - 1))
        start64 = start8 + n_eight * SUBLANES
        big = SUBLANES * SUBLANES

        def eight(r, c2):
            row = pl.multiple_of(start8 + r * SUBLANES, SUBLANES)
            cp = pltpu.make_async_copy(zrow.at[pl.ds(0, SUBLANES), :], xs_hbm.at[pl.ds(row, SUBLANES), :], sem_c)
            cp.wait() if wait else cp.start()
            return c2

        def sixty_four(r, c2):
            row = pl.multiple_of(start64 + r * big, big)
            cp = pltpu.make_async_copy(zrow.at[pl.ds(0, big), :], xs_hbm.at[pl.ds(row, big), :], sem_b)
            cp.wait() if wait else cp.start()
            return c2

        c = lax.fori_loop(0, n_single, one, c)
        c = lax.fori_loop(0, n_eight, eight, c)
        return lax.fori_loop(0, lax.shift_right_logical(rem8 - n_eight, 3), sixty_four, c)

    lax.fori_loop(0, per_step, functools.partial(pads, wait=False), 0)
    lax.fori_loop(0, per_step, functools.partial(pads, wait=True), 0)
    lax.fori_loop(0, tail_per_step, functools.partial(tail, wait=True), 0)


def _pad_fill(xs, pad_start, pad_cnt, n_used, *, steps=16):
    any_spec = pl.BlockSpec(memory_space=pl.ANY)
    return pl.pallas_call(
        _pad_fill_body,
        grid_spec=pltpu.PrefetchScalarGridSpec(
            num_scalar_prefetch=3, grid=(steps,),
            in_specs=[any_spec], out_specs=any_spec,
            scratch_shapes=[pltpu.VMEM((EXPERT_ROWS, xs.shape[1]), xs.dtype),
                            pltpu.SemaphoreType.DMA, pltpu.SemaphoreType.DMA, pltpu.SemaphoreType.DMA,
                            pltpu.SemaphoreType.DMA]),
        out_shape=jax.ShapeDtypeStruct(xs.shape, xs.dtype),
        input_output_aliases={3: 0},
        compiler_params=_cparams("arbitrary"),
        name="pad_fill",
    )(pad_start, pad_cnt, n_used, xs)


def _experts_body(fb_ref, xs_hbm, wg_ref, wu_ref, wd_ref, ys_hbm, xbuf, ybuf, wgu_s, wd_s, sem_x, sem_y):
    e = pl.program_id(0)
    n_e = pl.num_programs(0)
    de = wg_ref.shape[2]
    R = xbuf.shape[1]
    n_blk = xs_hbm.shape[0] // R
    n_used = fb_ref[n_e]

    def rows(ref, g):
        return ref.at[pl.ds(pl.multiple_of(g * R, R), R), :]

    def x_copy(g, slot):
        return pltpu.make_async_copy(rows(xs_hbm, g), xbuf.at[slot], sem_x.at[slot])

    def y_copy(g, slot):
        return pltpu.make_async_copy(ybuf.at[slot], rows(ys_hbm, g), sem_y.at[slot])

    @pl.when(jnp.logical_and(e == 0, n_used > 0))
    def _():
        x_copy(0, 0).start(priority=1)

    wgu_s[:, :de] = wg_ref[0].astype(BF16)
    wgu_s[:, de:] = wu_ref[0].astype(BF16)
    wd_s[...] = wd_ref[0].astype(BF16)

    def block(g, c):
        slot = g & 1
        x_copy(g, slot).wait()

        @pl.when(g + 1 < n_used)
        def _():
            x_copy(g + 1, 1 - slot).start(priority=1)

        @pl.when(g >= 2)
        def _():
            y_copy(g - 2, slot).wait()

        lo, hi = _unpack_rows(xbuf[slot])
        xb = jnp.concatenate([lo.astype(BF16), hi.astype(BF16)], axis=1)
        h = jnp.dot(xb, wgu_s[...], preferred_element_type=F32)
        hg = h[:, :de]
        act = (hg * _sigmoid(hg) * h[:, de:]).astype(BF16)
        ybuf[slot] = _pack_rows(jnp.dot(act, wd_s[...], preferred_element_type=F32))
        y_copy(g, slot).start(priority=1)
        return c

    lax.fori_loop(fb_ref[e], fb_ref[e + 1], block, 0)

    @pl.when(e == n_e - 1)
    def _():
        for back in (2, 1):
            g = n_used - back

            @pl.when(g >= 0)
            def _():
                y_copy(g, g & 1).wait()

        ybuf[0] = jnp.zeros(ybuf.shape[1:], ybuf.dtype)

        def tail(g, c, wait):
            cp = y_copy(g, 0)
            cp.wait() if wait else cp.start()
            return c

        lax.fori_loop(n_used, n_blk, functools.partial(tail, wait=False), 0)
        lax.fori_loop(n_used, n_blk, functools.partial(tail, wait=True), 0)


def _experts(xs, first_blk, wg, wu, wd):
    n_rows, W = xs.shape
    R = EXPERT_ROWS
    E, D, de = wg.shape
    any_spec = pl.BlockSpec(memory_space=pl.ANY)
    wmap = lambda e, fb: (e, 0, 0)
    return pl.pallas_call(
        _experts_body,
        grid_spec=pltpu.PrefetchScalarGridSpec(
            num_scalar_prefetch=1, grid=(E,),
            in_specs=[any_spec, pl.BlockSpec((1, D, de), wmap), pl.BlockSpec((1, D, de), wmap),
                      pl.BlockSpec((1, de, D), wmap)],
            out_specs=any_spec,
            scratch_shapes=[pltpu.VMEM((2, R, W), U32), pltpu.VMEM((2, R, W), U32),
                            pltpu.VMEM((D, 2 * de), BF16), pltpu.VMEM((de, D), BF16),
                            pltpu.SemaphoreType.DMA((2,)), pltpu.SemaphoreType.DMA((2,))]),
        out_shape=jax.ShapeDtypeStruct((n_rows, W), U32),
        compiler_params=_cparams("arbitrary"),
        name="experts",
    )(first_blk, xs, wg, wu, wd)


def _sc_gather_rows(table, idx2):
    from jax.experimental.pallas import tpu_sc as plsc
    n_chunks, CH = idx2.shape
    W = table.shape[1]
    info = plsc.get_sparse_core_info()
    NC, n_workers = info.num_cores, info.num_cores * info.num_subcores
    per_w = n_chunks // n_workers
    assert per_w * n_workers == n_chunks and per_w % 2 == 0

    def body(table_hbm, idx_hbm, out_hbm, idx_v, rows_v, sem):
        c0 = (lax.axis_index("s") * NC + lax.axis_index("c")) * per_w

        def gather(b):
            return pltpu.make_async_copy(table_hbm.at[idx_v.at[b]], rows_v.at[b], sem.at[b])

        pltpu.sync_copy(idx_hbm.at[c0], idx_v.at[0])
        gather(0).start()

        @pl.loop(0, per_w, step=2)
        def _(c):
            for b in range(2):
                cc = c + b

                @pl.when(cc + 1 < per_w)
                def _():
                    pltpu.sync_copy(idx_hbm.at[c0 + cc + 1], idx_v.at[1 - b])
                    gather(1 - b).start()

                gather(b).wait()
                pltpu.sync_copy(rows_v.at[b], out_hbm.at[pl.ds(pl.multiple_of((c0 + cc) * CH, CH), CH)])

    return pl.kernel(
        body, mesh=plsc.VectorSubcoreMesh(core_axis_name="c", subcore_axis_name="s"),
        out_type=jax.ShapeDtypeStruct((n_chunks * CH, W), table.dtype),
        scratch_types=[pltpu.VMEM((2, CH), I32), pltpu.VMEM((2, CH, W), table.dtype),
                       pltpu.SemaphoreType.DMA((2,))],
    )(table, idx2)


def _combine_body(yg_ref, gate_ref, x1_ref, wgu_ref, wd_ref, lg_ref, lb_ref, o_ref, *, alpha):
    x1 = x1_ref[...]
    ds_ = wd_ref.shape[0]
    h = jnp.dot(x1.astype(BF16), wgu_ref[...], preferred_element_type=F32)
    hg = h[:, :ds_]
    act = (hg * _sigmoid(hg) * h[:, ds_:]).astype(BF16)
    shared = jnp.dot(act, wd_ref[...], preferred_element_type=F32)

    g = gate_ref[...]
    lo_acc = hi_acc = None
    for k in range(yg_ref.shape[0]):
        lo, hi = _unpack_rows(yg_ref[k])
        gk = g[:, k:k + 1]
        lo_acc = gk * lo if k == 0 else lo_acc + gk * lo
        hi_acc = gk * hi if k == 0 else hi_acc + gk * hi
    routed = jnp.concatenate([lo_acc, hi_acc], axis=1)

    z = alpha * x1 + (routed + shared)
    mu = jnp.mean(z, axis=-1, keepdims=True)
    zc = z - mu
    var = jnp.mean(zc * zc, axis=-1, keepdims=True)
    o_ref[...] = zc * lax.rsqrt(var + LN_EPS) * lg_ref[...] + lb_ref[...]


def _combine(yg, gate_tk, x1, wgu, wd, ln_g, ln_b, *, alpha, tc):
    T, D = x1.shape
    K, _, W = yg.shape
    full = lambda a: pl.BlockSpec(a.shape, lambda i: (0,) * a.ndim)
    return pl.pallas_call(
        functools.partial(_combine_body, alpha=alpha),
        grid=(T // tc,),
        in_specs=[pl.BlockSpec((K, tc, W), lambda i: (0, i, 0)), pl.BlockSpec((tc, K), lambda i: (i, 0)),
                  pl.BlockSpec((tc, D), lambda i: (i, 0)), full(wgu), full(wd), full(ln_g), full(ln_b)],
        out_specs=pl.BlockSpec((tc, D), lambda i: (i, 0)),
        out_shape=jax.ShapeDtypeStruct((T, D), F32),
        compiler_params=_cparams("parallel"),
        name="combine",
    )(yg, gate_tk, x1, wgu, wd, ln_g, ln_b)


def _mixer_ln1(x, w_in, b_in, conv_w, conv_b, w_rg_a, b_rg_a, w_rg_i, b_rg_i, lru_lambda,
               w_proj_rnn, w_proj_att, rel_bias, w_out, ln1_g, ln1_b, w_router, alpha):
    B, S, D = x.shape
    T = B * S
    d_rnn = conv_w.shape[-1]
    gw = HEADS_PER_GROUP * HEAD_DIM
    d_att = gw * len(DILATED_GROUPS)
    a0 = 2 * d_rnn
    a1 = a0 + 3 * d_att
    head = lambda j, g: slice(a0 + j * d_att + g * gw, a0 + j * d_att + (g + 1) * gw)
    plain = [g for g, (_, d) in enumerate(DILATED_GROUPS) if d == 1]
    dilated = [g for g, (_, d) in enumerate(DILATED_GROUPS) if d > 1]
    order = ([slice(0, a0), slice(a1, None)] + [head(j, g) for g in plain for j in range(3)]
             + [head(j, g) for g in dilated for j in (1, 2, 0)])
    perm = lambda w: jnp.concatenate([w[..., s] for s in order], axis=-1)
    w_p = perm(w_in).astype(BF16)
    b_p = perm(b_in).reshape(1, -1)
    qkv0 = a0 + 2 * D
    n_main = qkv0 + 3 * gw * len(plain)
    x2 = x.reshape(T, D)

    proj, *qkv_dil = _in_proj(x2, w_p, b_p, n_main, tuple(DILATED_GROUPS[g][1] for g in dilated))
    proj3 = proj.reshape(B, S, n_main)
    y_rnn = _rnn(proj3, conv_w, conv_b, w_rg_a, b_rg_a, w_rg_i, b_rg_i, lru_lambda)

    os_, lses = [], []
    for g, (window, d) in enumerate(DILATED_GROUPS):
        blk = window // d
        bias = _attn_bias(rel_bias[:, g * HEADS_PER_GROUP:(g + 1) * HEADS_PER_GROUP], window, d)
        if d == 1:
            c0 = (qkv0 + 3 * gw * plain.index(g)) // gw
            o, lse = _attn_group(proj3, bias, blk=blk, d=1, gw=gw, q_blk=c0, k_blk=c0 + 1, v_blk=c0 + 2,
                                 row_blk=n_main // gw)
        else:
            o, lse = _attn_dilated(qkv_dil[dilated.index(g)], bias, B, blk=blk, d=d, gw=gw)
        os_.append(o.reshape(T, gw))
        lses.append(lse.reshape(T, gw))

    return _merge(y_rnn.reshape(T, d_rnn), os_, lses, proj, a0 // (2 * D), x2,
                  w_proj_rnn.astype(BF16), w_proj_att.astype(BF16), w_out.astype(BF16),
                  ln1_g.reshape(1, D), ln1_b.reshape(1, D), w_router.T, alpha=alpha)


def _moe_ln2(x1, x1p, logits_t, router_bias, w_exp_gate, w_exp_up, w_exp_down,
             w_sh_gate, w_sh_up, w_sh_down, ln2_g, ln2_b, alpha):
    T, D = x1.shape
    E = logits_t.shape[0]
    R = EXPERT_ROWS
    n_blk = T * TOP_K // R + E
    n_rows = n_blk * R

    e_idx, gate, rank, counts = _route(logits_t, router_bias)
    counts = counts.reshape(E).astype(I32)
    padded = (counts + R - 1) // R * R
    pad_end = jnp.cumsum(padded)
    seg_start = pad_end - padded
    n_used = (pad_end[-1:] // R).astype(I32)
    first_blk = jnp.concatenate([jnp.zeros((1,), I32), (pad_end // R).astype(I32)])
    dest = _dest(e_idx, rank, seg_start)
    tl = ROW_TILE
    ch = SC_GATHER_CHUNK
    dest_c = dest.reshape(TOP_K, T // ch, ch).transpose(1, 0, 2)

    xs = _sc_scatter_rows(x1p, dest_c, n_rows)
    xs = _pad_fill(xs, seg_start + counts, padded - counts, n_used)
    ys = _experts(xs, first_blk, w_exp_gate, w_exp_up, w_exp_down)
    wgu = jnp.concatenate([w_sh_gate, w_sh_up], axis=-1).astype(BF16)
    yg = _sc_gather_rows(ys, dest.reshape(-1, SC_GATHER_CHUNK)).reshape(TOP_K, T, ys.shape[1])
    return _combine(yg, gate.T, x1, wgu, w_sh_down.astype(BF16),
                    ln2_g.reshape(1, D), ln2_b.reshape(1, D), alpha=alpha, tc=tl)


def kernel(x, w_in, b_in, conv_w, conv_b, w_rg_a, b_rg_a, w_rg_i, b_rg_i, lru_lambda,
           w_proj_rnn, w_proj_att, rel_bias, w_out, ln1_g, ln1_b, w_router, router_bias,
           w_exp_gate, w_exp_up, w_exp_down, w_sh_gate, w_sh_up, w_sh_down, ln2_g, ln2_b):
    B, S, D = x.shape
    depth = w_in.shape[0]
    alpha = (2 * depth) ** 0.25
    for i in range(depth):
        x1, x1p, logits_t = _mixer_ln1(
            x, w_in[i], b_in[i], conv_w[i], conv_b[i], w_rg_a[i], b_rg_a[i], w_rg_i[i], b_rg_i[i],
            lru_lambda[i], w_proj_rnn[i], w_proj_att[i], rel_bias, w_out[i], ln1_g[i], ln1_b[i],
            w_router[i], alpha)
        out = _moe_ln2(x1, x1p, logits_t, router_bias[i], w_exp_gate[i], w_exp_up[i], w_exp_down[i],
                       w_sh_gate[i], w_sh_up[i], w_sh_down[i], ln2_g[i], ln2_b[i], alpha)
        x = out.reshape(B, S, D)
    return x
```

```python
import functools

import numpy as np
import jax
import jax.numpy as jnp
from jax import lax
from jax.experimental import pallas as pl
from jax.experimental.pallas import tpu as pltpu

F32 = jnp.float32
BF16 = jnp.bfloat16
I32 = jnp.int32
U32 = jnp.uint32

RNN_HEADS = 16
LRU_C = 8.0
HEAD_DIM = 64
HEADS_PER_GROUP = 8
DILATED_GROUPS = ((128, 1), (512, 4), (2048, 16))
NUM_BUCKETS = 32
MAX_DISTANCE = 2048
TOP_K = 8
N_EXPERT_GROUPS = 8
TOPK_GROUPS = 4
ROUTED_SCALE = 2.5
LN_EPS = 1e-5

LANES = 128
SUBLANES = 8
VMEM_LIMIT_BYTES = 56 * 1024 * 1024

MASK_VALUE = -1e30
EXPERT_ROWS = 512
ROW_TILE = 256
SC_GATHER_CHUNK = 64
COMBINE_PARTS = 4


def _cparams(*sem):
    return pltpu.CompilerParams(dimension_semantics=sem, vmem_limit_bytes=VMEM_LIMIT_BYTES)


def _sigmoid(v):
    return 0.5 * (jnp.tanh(0.5 * v) + 1.0)


def _in_proj_body(x_ref, w_ref, b_ref, main_ref, *rest, tn, dils):
    dil_refs, scr = rest[:-1], rest[-1]
    xb = x_ref[...].astype(BF16)
    tm = xb.shape[0]

    def chunk(j):
        sl = slice(j * tn, (j + 1) * tn)
        return jnp.dot(xb, w_ref[:, sl], preferred_element_type=F32) + b_ref[:, sl]

    n_main = main_ref.shape[1] // tn
    for j in range(n_main):
        main_ref[:, j * tn:(j + 1) * tn] = chunk(j).astype(main_ref.dtype)
    j = n_main
    for ref, d in zip(dil_refs, dils):
        for c in range(ref.shape[3] // tn):
            acc = chunk(j)
            j += 1
            for q in range(tn // LANES):
                scr[q] = acc[:, q * LANES:(q + 1) * LANES]
            for r in range(d):
                part = jnp.concatenate([scr[q, pl.ds(r, tm // d, stride=d), :] for q in range(tn // LANES)], axis=1)
                ref[0, r, :, c * tn:(c + 1) * tn] = part.astype(ref.dtype)


def _in_proj(x2, w, b, n_main, dils, *, tm=512, tn=512):
    T, D = x2.shape
    N = w.shape[1]
    wd = (N - n_main) // len(dils)
    out_specs = [pl.BlockSpec((tm, n_main), lambda i: (i, 0))]
    out_shape = [jax.ShapeDtypeStruct((T, n_main), BF16)]
    for d in dils:
        out_specs.append(pl.BlockSpec((1, d, tm // d, wd), lambda i: (i, 0, 0, 0)))
        out_shape.append(jax.ShapeDtypeStruct((T // tm, d, tm // d, wd), BF16))
    return pl.pallas_call(
        functools.partial(_in_proj_body, tn=tn, dils=dils),
        grid=(T // tm,),
        in_specs=[pl.BlockSpec((tm, D), lambda i: (i, 0)),
                  pl.BlockSpec((D, N), lambda i: (0, 0), pipeline_mode=pl.Buffered(1)),
                  pl.BlockSpec((1, N), lambda i: (0, 0))],
        out_specs=out_specs,
        out_shape=out_shape,
        scratch_shapes=[pltpu.VMEM((tn // LANES, tm, LANES), F32)],
        compiler_params=_cparams("parallel"),
        name="in_proj",
    )(x2, w, b)


def _rnn_body(xr_ref, gr_ref, cw_ref, cb_ref, wg_ref, bg_ref, lam_ref, y_ref,
              xext, a_s, b_s, hc, *, ts, cb):
    s = pl.program_id(2)

    @pl.when(s == 0)
    def _():
        xext[0:SUBLANES, :] = jnp.zeros((SUBLANES, cb), F32)
        hc[...] = jnp.zeros_like(hc)

    xr = xr_ref[0].astype(F32)
    xext[SUBLANES:SUBLANES + ts, :] = xr
    nw = cw_ref.shape[0]
    xc = cw_ref[nw - 1:nw, :] * xr + cb_ref[...]
    for j in range(nw - 1):
        off = SUBLANES - (nw - 1 - j)
        xc = xc + cw_ref[j:j + 1, :] * xext[off:off + ts, :]
    xext[0:SUBLANES, :] = xext[ts:ts + SUBLANES, :]

    gates = jnp.dot(xc.astype(BF16), wg_ref[0], preferred_element_type=F32) + bg_ref[...]
    r = _sigmoid(gates[:, :cb])
    ig = _sigmoid(gates[:, cb:])
    nl = -lam_ref[...]
    sp = jnp.maximum(nl, 0.0) + jnp.log1p(jnp.exp(-jnp.abs(nl)))
    log_a = (-LRU_C) * r * sp
    a = jnp.exp(log_a)
    u = jnp.sqrt(1.0 - a * a) * (ig * xc)

    row = lax.broadcasted_iota(I32, (ts, cb), 0) & (SUBLANES - 1)
    av, bv = a, u
    for sft in (1, 2, 4):
        a_sh = pltpu.roll(av, sft, 0)
        b_sh = pltpu.roll(bv, sft, 0)
        m = row >= sft
        bv = jnp.where(m, av * b_sh + bv, bv)
        av = jnp.where(m, av * a_sh, av)
    a_s[...] = av
    b_s[...] = bv

    def carry(g, h):
        i0 = pl.multiple_of(g * SUBLANES, SUBLANES)
        h8 = b_s[pl.ds(i0, SUBLANES), :] + a_s[pl.ds(i0, SUBLANES), :] * h
        b_s[pl.ds(i0, SUBLANES), :] = h8
        return h8[SUBLANES - 1:SUBLANES, :]

    hc[0:1, :] = lax.fori_loop(0, ts // SUBLANES, carry, hc[0:1, :], unroll=8)
    gr = gr_ref[0].astype(F32)
    y_ref[0] = (b_s[...] * jax.nn.gelu(gr)).astype(y_ref.dtype)


def _block_diag(w, per):
    H, d, _ = w.shape
    w4 = w.reshape(H // per, per, d, d)
    out = jnp.einsum('gpij,pq->gpiqj', w4, jnp.eye(per, dtype=w.dtype))
    return out.reshape(H // per, per * d, per * d)


def _rnn(proj3, conv_w, conv_b, w_rg_a, b_rg_a, w_rg_i, b_rg_i, lam, *, ts=512, cb=256):
    B, S, _ = proj3.shape
    d_rnn = conv_w.shape[-1]
    nc = d_rnn // cb
    per = cb // (d_rnn // RNN_HEADS)
    wg = jnp.concatenate([_block_diag(w_rg_a, per), _block_diag(w_rg_i, per)], axis=-1).astype(BF16)
    bg = jnp.concatenate([b_rg_a.reshape(nc, 1, cb), b_rg_i.reshape(nc, 1, cb)], axis=-1)
    return pl.pallas_call(
        functools.partial(_rnn_body, ts=ts, cb=cb),
        grid=(B, nc, S // ts),
        in_specs=[pl.BlockSpec((1, ts, cb), lambda b, c, s: (b, s, c)),
                  pl.BlockSpec((1, ts, cb), lambda b, c, s: (b, s, nc + c)),
                  pl.BlockSpec((conv_w.shape[0], cb), lambda b, c, s: (0, c)),
                  pl.BlockSpec((1, cb), lambda b, c, s: (0, c)),
                  pl.BlockSpec((1, cb, 2 * cb), lambda b, c, s: (c, 0, 0)),
                  pl.BlockSpec((None, 1, 2 * cb), lambda b, c, s: (c, 0, 0)),
                  pl.BlockSpec((1, cb), lambda b, c, s: (0, c))],
        out_specs=pl.BlockSpec((1, ts, cb), lambda b, c, s: (b, s, c)),
        out_shape=jax.ShapeDtypeStruct((B, S, d_rnn), BF16),
        scratch_shapes=[pltpu.VMEM((ts + SUBLANES, cb), F32), pltpu.VMEM((ts, cb), F32),
                        pltpu.VMEM((ts, cb), F32), pltpu.VMEM((SUBLANES, cb), F32)],
        compiler_params=_cparams("parallel", "parallel", "arbitrary"),
        name="rnn",
    )(proj3, proj3, conv_w, conv_b.reshape(1, d_rnn), wg, bg, lam.reshape(1, d_rnn))


def _t5_bucket(dist):
    max_exact = NUM_BUCKETS // 2
    d = np.maximum(dist, 1).astype(np.float64)
    large = max_exact + (np.log(d / max_exact) / np.log(MAX_DISTANCE / max_exact)
                         * (NUM_BUCKETS - max_exact)).astype(np.int64)
    large = np.minimum(large, NUM_BUCKETS - 1)
    return np.where(dist < max_exact, dist, large).astype(np.int32)


def _attn_bias(table, window, dilation):
    blk = window // dilation
    qi = np.arange(blk)[:, None]
    ki = np.arange(2 * blk)[None, :]
    rel = qi + blk - ki
    in_window = (rel >= 0) & (rel <= blk)
    bucket = _t5_bucket(np.clip(rel, 0, None) * dilation)
    onehot = (bucket[..., None] == np.arange(NUM_BUCKETS)).astype(np.float32)
    bias = jnp.einsum('qkn,nh->hqk', onehot, table.astype(F32), precision=lax.Precision.HIGHEST)
    first = in_window & (ki >= blk)
    return jnp.stack([jnp.where(first[None], bias, MASK_VALUE),
                      jnp.where(in_window[None], bias, MASK_VALUE)])


def _attn_heads(q, kp, kc, vp, vc, bias_ref):
    blk, gw = q.shape
    lo = lax.broadcasted_iota(I32, (blk, LANES), 1) < HEAD_DIM
    scale = HEAD_DIM ** -0.5
    o_parts, l_parts = [], []
    for p in range(gw // LANES):
        sl = slice(p * LANES, (p + 1) * LANES)
        q2 = q[:, sl] * scale
        k2 = jnp.concatenate([kp[:, sl], kc[:, sl]], axis=0)
        v2 = jnp.concatenate([vp[:, sl], vc[:, sl]], axis=0)
        outs, lses = [], []
        for hh in range(2):
            qm = jnp.where(lo if hh == 0 else jnp.logical_not(lo), q2, jnp.zeros_like(q2))
            sc = lax.dot_general(qm, k2, (((1,), (1,)), ((), ())), preferred_element_type=F32)
            sc = sc + bias_ref[0, 2 * p + hh]
            m = jnp.max(sc, axis=-1, keepdims=True)
            e = jnp.exp(sc - m)
            l = jnp.sum(e, axis=-1, keepdims=True)
            o = jnp.dot(e.astype(BF16), v2, preferred_element_type=F32)
            outs.append(o / l)
            lses.append(jnp.broadcast_to(m + jnp.log(l), (blk, LANES)))
        o_parts.append(jnp.where(lo, outs[0], outs[1]))
        l_parts.append(jnp.where(lo, lses[0], lses[1]))
    return jnp.concatenate(o_parts, axis=1), jnp.concatenate(l_parts, axis=1)


def _attn_body(q_ref, kp_ref, kc_ref, vp_ref, vc_ref, bias_ref, o_ref, lse_ref):
    o, lse = _attn_heads(q_ref[0], kp_ref[0], kc_ref[0], vp_ref[0], vc_ref[0], bias_ref)
    o_ref[0] = o.astype(o_ref.dtype)
    lse_ref[0] = lse


def _attn_dil_body(cur_ref, prev_ref, bias_ref, o_ref, lse_ref, oscr, lscr, *, d, blk, gw):
    nt = cur_ref.shape[0]
    nq = gw // LANES

    def rows(ref, r, c):
        parts = [ref[j, r, :, c * gw:(c + 1) * gw] for j in range(nt)]
        return parts[0] if nt == 1 else jnp.concatenate(parts, axis=0)

    def residue(r, carry):
        o, lse = _attn_heads(rows(cur_ref, r, 2), rows(prev_ref, r, 0), rows(cur_ref, r, 0),
                             rows(prev_ref, r, 1), rows(cur_ref, r, 1), bias_ref)
        for q in range(nq):
            oscr[q, pl.ds(r, blk, stride=d), :] = o[:, q * LANES:(q + 1) * LANES]
            lscr[q, pl.ds(r, blk, stride=d), :] = lse[:, q * LANES:(q + 1) * LANES]
        return carry

    lax.fori_loop(0, d, residue, 0)
    for q in range(nq):
        o_ref[0, :, q * LANES:(q + 1) * LANES] = oscr[q].astype(o_ref.dtype)
        lse_ref[0, :, q * LANES:(q + 1) * LANES] = lscr[q]


def _attn_dilated(qkv_t, bias, B, *, blk, d, gw):
    n_tiles, _, rows_t, _ = qkv_t.shape
    tm = rows_t * d
    nt = blk * d // tm
    nb = n_tiles // (B * nt)
    S = n_tiles * tm // B
    return pl.pallas_call(
        functools.partial(_attn_dil_body, d=d, blk=blk, gw=gw),
        grid=(B, nb),
        in_specs=[pl.BlockSpec((nt, d, rows_t, 3 * gw), lambda b, n: (b * nb + n, 0, 0, 0)),
                  pl.BlockSpec((nt, d, rows_t, 2 * gw), lambda b, n: (b * nb + jnp.maximum(n - 1, 0), 0, 0, 0)),
                  pl.BlockSpec((1,) + bias.shape[1:], lambda b, n: (jnp.minimum(n, 1), 0, 0, 0))],
        out_specs=[pl.BlockSpec((1, blk * d, gw), lambda b, n: (b, n, 0)),
                   pl.BlockSpec((1, blk * d, gw), lambda b, n: (b, n, 0))],
        out_shape=[jax.ShapeDtypeStruct((B, S, gw), BF16), jax.ShapeDtypeStruct((B, S, gw), F32)],
        scratch_shapes=[pltpu.VMEM((gw // LANES, blk * d, LANES), F32),
                        pltpu.VMEM((gw // LANES, blk * d, LANES), F32)],
        compiler_params=_cparams("parallel", "parallel"),
        name=f"attn_d{d}",
    )(qkv_t, qkv_t, bias)


def _attn_group(qkv, bias, *, blk, d, gw, q_blk, k_blk, v_blk, row_blk):
    B, L, _ = qkv.shape
    nb = L // blk

    def cur(col):
        return pl.BlockSpec((1, blk, gw), lambda b, r, n: (b, n, r * row_blk + col))

    def prev(col):
        return pl.BlockSpec((1, blk, gw), lambda b, r, n: (b, jnp.maximum(n - 1, 0), r * row_blk + col))

    return pl.pallas_call(
        _attn_body,
        grid=(B, d, nb),
        in_specs=[cur(q_blk), prev(k_blk), cur(k_blk), prev(v_blk), cur(v_blk),
                  pl.BlockSpec((1,) + bias.shape[1:], lambda b, r, n: (jnp.minimum(n, 1), 0, 0, 0))],
        out_specs=[pl.BlockSpec((1, blk, gw), lambda b, r, n: (b, n, r)),
                   pl.BlockSpec((1, blk, gw), lambda b, r, n: (b, n, r))],
        out_shape=[jax.ShapeDtypeStruct((B, L, d * gw), BF16),
                   jax.ShapeDtypeStruct((B, L, d * gw), F32)],
        compiler_params=_cparams("parallel", "parallel", "parallel"),
        name=f"attn_d{d}",
    )(qkv, qkv, qkv, qkv, qkv, bias)


def _merge_body(yr_ref, o1_ref, o2_ref, o3_ref, l1_ref, l2_ref, l3_ref, g_ref, x_ref,
                wr_ref, wa_ref, wo_ref, lg_ref, lb_ref, wrh_ref, wrl_ref,
                x1_ref, x1p_ref, lt_ref, *, alpha):
    l1, l2, l3 = l1_ref[...], l2_ref[...], l3_ref[...]
    mx = jnp.maximum(jnp.maximum(l1, l2), l3)
    w1, w2, w3 = jnp.exp(l1 - mx), jnp.exp(l2 - mx), jnp.exp(l3 - mx)
    y_att = (w1 * o1_ref[...].astype(F32) + w2 * o2_ref[...].astype(F32)
             + w3 * o3_ref[...].astype(F32)) / (w1 + w2 + w3)
    pr = jnp.dot(yr_ref[...], wr_ref[...], preferred_element_type=F32)
    pa = jnp.dot(y_att.astype(BF16), wa_ref[...], preferred_element_type=F32)
    dm = pr.shape[1]
    g = g_ref[...].astype(F32)
    merged = _sigmoid(g[:, :dm]) * pr + _sigmoid(g[:, dm:]) * pa
    mix = jnp.dot(merged.astype(BF16), wo_ref[...], preferred_element_type=F32)
    z = alpha * x_ref[...] + mix
    mu = jnp.mean(z, axis=-1, keepdims=True)
    zc = z - mu
    var = jnp.mean(zc * zc, axis=-1, keepdims=True)
    x1 = zc * lax.rsqrt(var + LN_EPS) * lg_ref[...] + lb_ref[...]
    x1_ref[...] = x1
    x1p_ref[...] = _pack_rows(x1)
    nt = (((1,), (1,)), ((), ()))
    x1h = x1.astype(BF16)
    x1l = (x1 - x1h.astype(F32)).astype(BF16)
    lt_ref[...] = (lax.dot_general(wrh_ref[...], x1h, nt, preferred_element_type=F32)
                   + (lax.dot_general(wrh_ref[...], x1l, nt, preferred_element_type=F32)
                      + lax.dot_general(wrl_ref[...], x1h, nt, preferred_element_type=F32)))


def _merge(y_rnn, os_, lses, proj, gate_blk, x2, wr, wa, wo, ln_g, ln_b, w_router_t, *, alpha, tm=512):
    T, D = x2.shape
    da = os_[0].shape[1]
    E = w_router_t.shape[0]
    wrh = w_router_t.astype(BF16)
    wrl = (w_router_t - wrh.astype(F32)).astype(BF16)
    row = lambda w: pl.BlockSpec((tm, w), lambda i: (i, 0))
    full = lambda a: pl.BlockSpec(a.shape, lambda i: (0,) * a.ndim)
    return pl.pallas_call(
        functools.partial(_merge_body, alpha=alpha),
        grid=(T // tm,),
        in_specs=[row(D), row(da), row(da), row(da), row(da), row(da), row(da),
                  pl.BlockSpec((tm, 2 * D), lambda i: (i, gate_blk)), row(D),
                  full(wr), full(wa), full(wo), full(ln_g), full(ln_b), full(wrh), full(wrl)],
        out_specs=[row(D), row(D // 2), pl.BlockSpec((E, tm), lambda i: (0, i))],
        out_shape=[jax.ShapeDtypeStruct((T, D), F32), jax.ShapeDtypeStruct((T, D // 2), U32),
                   jax.ShapeDtypeStruct((E, T), F32)],
        compiler_params=_cparams("parallel"),
        name="merge",
    )(y_rnn, *os_, *lses, proj, x2, wr, wa, wo, ln_g, ln_b, wrh, wrl)


def _first_max(vals, idx, big):
    m = jnp.max(vals, axis=0, keepdims=True)
    i = jnp.min(jnp.where(vals == m, idx, big), axis=0, keepdims=True)
    return m, i


def _route_body(lt_ref, rb_ref, e_ref, g_ref, r_ref, cnt_ref, *, tl):
    E = lt_ref.shape[0]
    per = E // N_EXPERT_GROUPS
    neg = -jnp.inf

    @pl.when(pl.program_id(0) == 0)
    def _():
        cnt_ref[...] = jnp.zeros_like(cnt_ref)

    scores = jax.nn.sigmoid(lt_ref[...])
    sel = scores + rb_ref[...]
    rowi = lax.broadcasted_iota(I32, (E, tl), 0)

    gi = lax.broadcasted_iota(I32, (N_EXPERT_GROUPS, tl), 0)
    gsc = jnp.zeros((N_EXPERT_GROUPS, tl), F32)
    for g in range(N_EXPERT_GROUPS):
        blk = sel[g * per:(g + 1) * per]
        ri = lax.broadcasted_iota(I32, (per, tl), 0) + g * per
        m1, i1 = _first_max(blk, ri, E)
        m2 = jnp.max(jnp.where(ri == i1, neg, blk), axis=0, keepdims=True)
        gsc = jnp.where(gi == g, m1 + m2, gsc)
    keep = jnp.zeros((N_EXPERT_GROUPS, tl), F32)
    for _ in range(TOPK_GROUPS):
        _, ig = _first_max(gsc, gi, N_EXPERT_GROUPS)
        hit = gi == ig
        keep = jnp.where(hit, 1.0, keep)
        gsc = jnp.where(hit, neg, gsc)
    cur = jnp.concatenate(
        [jnp.where(keep[g:g + 1] > 0.5, sel[g * per:(g + 1) * per], neg) for g in range(N_EXPERT_GROUPS)],
        axis=0)

    ti = lax.broadcasted_iota(I32, (tl, tl), 0)
    tj = lax.broadcasted_iota(I32, (tl, tl), 1)
    earlier = jnp.where(ti < tj, 1.0, 0.0).astype(BF16)

    es, gv = [], []
    onehot = jnp.zeros((E, tl), F32)
    for k in range(TOP_K):
        _, ie = _first_max(cur, rowi, E)
        hit = rowi == ie
        es.append(ie)
        gv.append(jnp.sum(jnp.where(hit, scores, 0.0), axis=0, keepdims=True))
        onehot = jnp.where(hit, 1.0, onehot)
        cur = jnp.where(hit, neg, cur)
    gsum = gv[0]
    for k in range(1, TOP_K):
        gsum = gsum + gv[k]
    ranks = jnp.dot(onehot.astype(BF16), earlier, preferred_element_type=F32) + cnt_ref[...]
    for k in range(TOP_K):
        e_ref[k:k + 1, :] = es[k]
        g_ref[k:k + 1, :] = gv[k] / gsum * ROUTED_SCALE
        r_ref[k:k + 1, :] = jnp.sum(jnp.where(rowi == es[k], ranks, 0.0), axis=0, keepdims=True).astype(I32)
    cnt_ref[...] += jnp.sum(onehot, axis=1, keepdims=True)


def _route(logits_t, router_bias, *, tl=256):
    E, T = logits_t.shape
    kt = pl.BlockSpec((TOP_K, tl), lambda i: (0, i))
    return pl.pallas_call(
        functools.partial(_route_body, tl=tl),
        grid=(T // tl,),
        in_specs=[pl.BlockSpec((E, tl), lambda i: (0, i)), pl.BlockSpec((E, 1), lambda i: (0, 0))],
        out_specs=[kt, kt, kt, pl.BlockSpec((E, 1), lambda i: (0, 0))],
        out_shape=[jax.ShapeDtypeStruct((TOP_K, T), I32), jax.ShapeDtypeStruct((TOP_K, T), F32),
                   jax.ShapeDtypeStruct((TOP_K, T), I32), jax.ShapeDtypeStruct((E, 1), F32)],
        compiler_params=_cparams("arbitrary"),
        name="route",
    )(logits_t, router_bias.reshape(E, 1))


def _dest_body(e_ref, r_ref, ss_ref, d_ref):
    E = ss_ref.shape[0]
    tl = e_ref.shape[1]
    rowi = lax.broadcasted_iota(I32, (E, tl), 0)
    e = e_ref[...]
    rows = [jnp.sum(jnp.where(rowi == e[k:k + 1], ss_ref[...], 0), axis=0, keepdims=True)
            for k in range(e.shape[0])]
    d_ref[...] = jnp.concatenate(rows, axis=0) + r_ref[...]


def _dest(e_idx, rank, seg_start, *, tl=512):
    K, T = e_idx.shape
    E = seg_start.shape[0]
    kt = pl.BlockSpec((K, tl), lambda i: (0, i))
    return pl.pallas_call(
        _dest_body, grid=(T // tl,),
        in_specs=[kt, kt, pl.BlockSpec((E, 1), lambda i: (0, 0))],
        out_specs=kt, out_shape=jax.ShapeDtypeStruct((K, T), I32),
        compiler_params=_cparams("parallel"), name="dest",
    )(e_idx, rank, seg_start.reshape(E, 1))


def _pack_rows(v):
    w = v.shape[1] // 2
    lo = pltpu.bitcast(v[:, :w].astype(BF16).astype(F32), U32) >> 16
    hi = pltpu.bitcast(v[:, w:].astype(BF16).astype(F32), U32) & jnp.uint32(0xFFFF0000)
    return lo | hi


def _unpack_rows(p):
    lo = pltpu.bitcast(p << 16, F32)
    hi = pltpu.bitcast(p & jnp.uint32(0xFFFF0000), F32)
    return lo, hi


def _sc_scatter_rows(x, idx3, n_rows):
    from jax.experimental.pallas import tpu_sc as plsc
    n_chunks, K, CH = idx3.shape
    W = x.shape[1]
    info = plsc.get_sparse_core_info()
    NC, n_workers = info.num_cores, info.num_cores * info.num_subcores
    per_w = n_chunks // n_workers
    assert per_w * n_workers == n_chunks and per_w % 2 == 0

    def body(x_hbm, idx_hbm, out_hbm, idx_v, rows_v, sem_l, sem_s):
        c0 = (lax.axis_index("s") * NC + lax.axis_index("c")) * per_w

        def load(cc, b):
            return pltpu.make_async_copy(x_hbm.at[pl.ds(pl.multiple_of((c0 + cc) * CH, CH), CH)], rows_v.at[b],
                                         sem_l.at[b])

        def scatter(b, k):
            return pltpu.make_async_copy(rows_v.at[b], out_hbm.at[idx_v.at[b, k]], sem_s.at[b])

        pltpu.sync_copy(idx_hbm.at[c0], idx_v.at[0])
        load(0, 0).start()

        @pl.loop(0, per_w, step=2)
        def _(c):
            for b in range(2):
                cc = c + b
                load(cc, b).wait()
                for k in range(K):
                    scatter(b, k).start()

                @pl.when(cc >= 1)
                def _():
                    for k in range(K):
                        scatter(1 - b, k).wait()

                @pl.when(cc + 1 < per_w)
                def _():
                    pltpu.sync_copy(idx_hbm.at[c0 + cc + 1], idx_v.at[1 - b])
                    load(cc + 1, 1 - b).start()

        for k in range(K):
            scatter(1, k).wait()

    return pl.kernel(
        body, mesh=plsc.VectorSubcoreMesh(core_axis_name="c", subcore_axis_name="s"),
        out_type=jax.ShapeDtypeStruct((n_rows, W), x.dtype),
        scratch_types=[pltpu.VMEM((2, K, CH), I32), pltpu.VMEM((2, CH, W), x.dtype),
                       pltpu.SemaphoreType.DMA((2,)), pltpu.SemaphoreType.DMA((2,))],
    )(x, idx3)


def _pad_fill_body(ps_ref, pc_ref, nu_ref, xs_in, xs_hbm, zrow, sem_z, sem_c, sem_b, sem_t):
    del xs_in
    i = pl.program_id(0)
    n = pl.num_programs(0)
    R = zrow.shape[0]

    def pad_copy(row):
        return pltpu.make_async_copy(zrow.at[pl.ds(0, 1), :], xs_hbm.at[pl.ds(row, 1), :], sem_z)

    zrow[...] = jnp.zeros_like(zrow)

    n_blk = xs_hbm.shape[0] // R
    tail_per_step = -(-n_blk // n)

    def tail(j, c, wait):
        blk = i * tail_per_step + j

        @pl.when(jnp.logical_and(blk >= nu_ref[0], blk < n_blk))
        def _():
            cp = pltpu.make_async_copy(zrow, xs_hbm.at[pl.ds(pl.multiple_of(blk * R, R), R), :], sem_t)
            cp.wait() if wait else cp.start()

        return c

    lax.fori_loop(0, tail_per_step, functools.partial(tail, wait=False), 0)

    E = ps_ref.shape[0]
    per_step = -(-E // n)

    def pads(j, c, wait):
        e = jnp.minimum(i * per_step + j, E - 1)
        cnt = jnp.where(i * per_step + j < E, pc_ref[e], 0)
        start = ps_ref[e]
        n_single = jnp.minimum(cnt, (-start) & (SUBLANES - 1))
        start8 = start + n_single

        def one(r, c2):
            cp = pad_copy(start + r)
            cp.wait() if wait else cp.start()
            return c2

        rem8 = lax.shift_right_logical(cnt - n_single, 3)
        n_eight = jnp.minimum(rem8, lax.shift_right_logical(-start8, 3) & (SUBLANES - 1))
        start64 = start8 + n_eight * SUBLANES
        big = SUBLANES * SUBLANES

        def eight(r, c2):
            row = pl.multiple_of(start8 + r * SUBLANES, SUBLANES)
            cp = pltpu.make_async_copy(zrow.at[pl.ds(0, SUBLANES), :], xs_hbm.at[pl.ds(row, SUBLANES), :], sem_c)
            cp.wait() if wait else cp.start()
            return c2

        def sixty_four(r, c2):
            row = pl.multiple_of(start64 + r * big, big)
            cp = pltpu.make_async_copy(zrow.at[pl.ds(0, big), :], xs_hbm.at[pl.ds(row, big), :], sem_b)
            cp.wait() if wait else cp.start()
            return c2

        c = lax.fori_loop(0, n_single, one, c)
        c = lax.fori_loop(0, n_eight, eight, c)
        return lax.fori_loop(0, lax.shift_right_logical(rem8 - n_eight, 3), sixty_four, c)

    lax.fori_loop(0, per_step, functools.partial(pads, wait=False), 0)
    lax.fori_loop(0, per_step, functools.partial(pads, wait=True), 0)
    lax.fori_loop(0, tail_per_step, functools.partial(tail, wait=True), 0)


def _pad_fill(xs, pad_start, pad_cnt, n_used, *, steps=16):
    any_spec = pl.BlockSpec(memory_space=pl.ANY)
    return pl.pallas_call(
        _pad_fill_body,
        grid_spec=pltpu.PrefetchScalarGridSpec(
            num_scalar_prefetch=3, grid=(steps,),
            in_specs=[any_spec], out_specs=any_spec,
            scratch_shapes=[pltpu.VMEM((EXPERT_ROWS, xs.shape[1]), xs.dtype),
                            pltpu.SemaphoreType.DMA, pltpu.SemaphoreType.DMA, pltpu.SemaphoreType.DMA,
                            pltpu.SemaphoreType.DMA]),
        out_shape=jax.ShapeDtypeStruct(xs.shape, xs.dtype),
        input_output_aliases={3: 0},
        compiler_params=_cparams("arbitrary"),
        name="pad_fill",
    )(pad_start, pad_cnt, n_used, xs)


def _experts_body(fb_ref, xs_hbm, wg_ref, wu_ref, wd_ref, ys_hbm, xbuf, ybuf, wgu_s, wd_s, sem_x, sem_y):
    e = pl.program_id(0)
    n_e = pl.num_programs(0)
    de = wg_ref.shape[2]
    R = xbuf.shape[1]
    n_blk = xs_hbm.shape[0] // R
    n_used = fb_ref[n_e]

    def rows(ref, g):
        return ref.at[pl.ds(pl.multiple_of(g * R, R), R), :]

    def x_copy(g, slot):
        return pltpu.make_async_copy(rows(xs_hbm, g), xbuf.at[slot], sem_x.at[slot])

    def y_copy(g, slot):
        return pltpu.make_async_copy(ybuf.at[slot], rows(ys_hbm, g), sem_y.at[slot])

    @pl.when(jnp.logical_and(e == 0, n_used > 0))
    def _():
        x_copy(0, 0).start(priority=1)

    wgu_s[:, :de] = wg_ref[0].astype(BF16)
    wgu_s[:, de:] = wu_ref[0].astype(BF16)
    wd_s[...] = wd_ref[0].astype(BF16)

    def block(g, c):
        slot = g & 1
        x_copy(g, slot).wait()

        @pl.when(g + 1 < n_used)
        def _():
            x_copy(g + 1, 1 - slot).start(priority=1)

        @pl.when(g >= 2)
        def _():
            y_copy(g - 2, slot).wait()

        lo, hi = _unpack_rows(xbuf[slot])
        xb = jnp.concatenate([lo.astype(BF16), hi.astype(BF16)], axis=1)
        h = jnp.dot(xb, wgu_s[...], preferred_element_type=F32)
        hg = h[:, :de]
        act = (hg * _sigmoid(hg) * h[:, de:]).astype(BF16)
        ybuf[slot] = _pack_rows(jnp.dot(act, wd_s[...], preferred_element_type=F32))
        y_copy(g, slot).start(priority=1)
        return c

    lax.fori_loop(fb_ref[e], fb_ref[e + 1], block, 0)

    @pl.when(e == n_e - 1)
    def _():
        for back in (2, 1):
            g = n_used - back

            @pl.when(g >= 0)
            def _():
                y_copy(g, g & 1).wait()

        ybuf[0] = jnp.zeros(ybuf.shape[1:], ybuf.dtype)

        def tail(g, c, wait):
            cp = y_copy(g, 0)
            cp.wait() if wait else cp.start()
            return c

        lax.fori_loop(n_used, n_blk, functools.partial(tail, wait=False), 0)
        lax.fori_loop(n_used, n_blk, functools.partial(tail, wait=True), 0)


def _experts(xs, first_blk, wg, wu, wd):
    n_rows, W = xs.shape
    R = EXPERT_ROWS
    E, D, de = wg.shape
    any_spec = pl.BlockSpec(memory_space=pl.ANY)
    wmap = lambda e, fb: (e, 0, 0)
    return pl.pallas_call(
        _experts_body,
        grid_spec=pltpu.PrefetchScalarGridSpec(
            num_scalar_prefetch=1, grid=(E,),
            in_specs=[any_spec, pl.BlockSpec((1, D, de), wmap), pl.BlockSpec((1, D, de), wmap),
                      pl.BlockSpec((1, de, D), wmap)],
            out_specs=any_spec,
            scratch_shapes=[pltpu.VMEM((2, R, W), U32), pltpu.VMEM((2, R, W), U32),
                            pltpu.VMEM((D, 2 * de), BF16), pltpu.VMEM((de, D), BF16),
                            pltpu.SemaphoreType.DMA((2,)), pltpu.SemaphoreType.DMA((2,))]),
        out_shape=jax.ShapeDtypeStruct((n_rows, W), U32),
        compiler_params=_cparams("arbitrary"),
        name="experts",
    )(first_blk, xs, wg, wu, wd)


def _sc_gather_rows(table, idx2):
    from jax.experimental.pallas import tpu_sc as plsc
    n_chunks, CH = idx2.shape
    W = table.shape[1]
    info = plsc.get_sparse_core_info()
    NC, n_workers = info.num_cores, info.num_cores * info.num_subcores
    per_w = n_chunks // n_workers
    assert per_w * n_workers == n_chunks and per_w % 2 == 0

    def body(table_hbm, idx_hbm, out_hbm, idx_v, rows_v, sem):
        c0 = (lax.axis_index("s") * NC + lax.axis_index("c")) * per_w

        def gather(b):
            return pltpu.make_async_copy(table_hbm.at[idx_v.at[b]], rows_v.at[b], sem.at[b])

        pltpu.sync_copy(idx_hbm.at[c0], idx_v.at[0])
        gather(0).start()

        @pl.loop(0, per_w, step=2)
        def _(c):
            for b in range(2):
                cc = c + b

                @pl.when(cc + 1 < per_w)
                def _():
                    pltpu.sync_copy(idx_hbm.at[c0 + cc + 1], idx_v.at[1 - b])
                    gather(1 - b).start()

                gather(b).wait()
                pltpu.sync_copy(rows_v.at[b], out_hbm.at[pl.ds(pl.multiple_of((c0 + cc) * CH, CH), CH)])

    return pl.kernel(
        body, mesh=plsc.VectorSubcoreMesh(core_axis_name="c", subcore_axis_name="s"),
        out_type=jax.ShapeDtypeStruct((n_chunks * CH, W), table.dtype),
        scratch_types=[pltpu.VMEM((2, CH), I32), pltpu.VMEM((2, CH, W), table.dtype),
                       pltpu.SemaphoreType.DMA((2,))],
    )(table, idx2)


def _combine_body(yg_ref, gate_ref, x1_ref, wgu_ref, wd_ref, lg_ref, lb_ref, *rest, alpha):
    o_ref = rest[-1]
    x1 = x1_ref[...]
    ds_ = wd_ref.shape[0]
    h = jnp.dot(x1.astype(BF16), wgu_ref[...], preferred_element_type=F32)
    hg = h[:, :ds_]
    act = (hg * _sigmoid(hg) * h[:, ds_:]).astype(BF16)
    shared = jnp.dot(act, wd_ref[...], preferred_element_type=F32)

    g = gate_ref[...]
    lo_acc = hi_acc = None
    for k in range(yg_ref.shape[0]):
        lo, hi = _unpack_rows(yg_ref[k])
        gk = g[:, k:k + 1]
        lo_acc = gk * lo if k == 0 else lo_acc + gk * lo
        hi_acc = gk * hi if k == 0 else hi_acc + gk * hi
    routed = jnp.concatenate([lo_acc, hi_acc], axis=1)

    z = alpha * x1 + (routed + shared)
    mu = jnp.mean(z, axis=-1, keepdims=True)
    zc = z - mu
    var = jnp.mean(zc * zc, axis=-1, keepdims=True)
    o_ref[...] = zc * lax.rsqrt(var + LN_EPS) * lg_ref[...] + lb_ref[...]


def _combine(yg, gate_tk, x1, wgu, wd, ln_g, ln_b, out_prev, part, *, alpha, tc):
    T, D = x1.shape
    K, Tp, W = yg.shape
    off = part * (Tp // tc)
    full = lambda a: pl.BlockSpec(a.shape, lambda i: (0,) * a.ndim)
    args = [yg, gate_tk, x1, wgu, wd, ln_g, ln_b]
    in_specs = [pl.BlockSpec((K, tc, W), lambda i: (0, i, 0)), pl.BlockSpec((tc, K), lambda i: (off + i, 0)),
                pl.BlockSpec((tc, D), lambda i: (off + i, 0)), full(wgu), full(wd), full(ln_g), full(ln_b)]
    aliases = {}
    if out_prev is not None:
        args.append(out_prev)
        in_specs.append(pl.BlockSpec(memory_space=pl.ANY))
        aliases = {len(args) - 1: 0}
    return pl.pallas_call(
        functools.partial(_combine_body, alpha=alpha),
        grid=(Tp // tc,),
        in_specs=in_specs,
        out_specs=pl.BlockSpec((tc, D), lambda i: (off + i, 0)),
        out_shape=jax.ShapeDtypeStruct((T, D), F32),
        input_output_aliases=aliases,
        compiler_params=_cparams("parallel"),
        name="combine",
    )(*args)


def _mixer_ln1(x, w_in, b_in, conv_w, conv_b, w_rg_a, b_rg_a, w_rg_i, b_rg_i, lru_lambda,
               w_proj_rnn, w_proj_att, rel_bias, w_out, ln1_g, ln1_b, w_router, alpha):
    B, S, D = x.shape
    T = B * S
    d_rnn = conv_w.shape[-1]
    gw = HEADS_PER_GROUP * HEAD_DIM
    d_att = gw * len(DILATED_GROUPS)
    a0 = 2 * d_rnn
    a1 = a0 + 3 * d_att
    head = lambda j, g: slice(a0 + j * d_att + g * gw, a0 + j * d_att + (g + 1) * gw)
    plain = [g for g, (_, d) in enumerate(DILATED_GROUPS) if d == 1]
    dilated = [g for g, (_, d) in enumerate(DILATED_GROUPS) if d > 1]
    order = ([slice(0, a0), slice(a1, None)] + [head(j, g) for g in plain for j in range(3)]
             + [head(j, g) for g in dilated for j in (1, 2, 0)])
    perm = lambda w: jnp.concatenate([w[..., s] for s in order], axis=-1)
    w_p = perm(w_in).astype(BF16)
    b_p = perm(b_in).reshape(1, -1)
    qkv0 = a0 + 2 * D
    n_main = qkv0 + 3 * gw * len(plain)
    x2 = x.reshape(T, D)

    proj, *qkv_dil = _in_proj(x2, w_p, b_p, n_main, tuple(DILATED_GROUPS[g][1] for g in dilated))
    proj3 = proj.reshape(B, S, n_main)
    y_rnn = _rnn(proj3, conv_w, conv_b, w_rg_a, b_rg_a, w_rg_i, b_rg_i, lru_lambda)

    os_, lses = [], []
    for g, (window, d) in enumerate(DILATED_GROUPS):
        blk = window // d
        bias = _attn_bias(rel_bias[:, g * HEADS_PER_GROUP:(g + 1) * HEADS_PER_GROUP], window, d)
        if d == 1:
            c0 = (qkv0 + 3 * gw * plain.index(g)) // gw
            o, lse = _attn_group(proj3, bias, blk=blk, d=1, gw=gw, q_blk=c0, k_blk=c0 + 1, v_blk=c0 + 2,
                                 row_blk=n_main // gw)
        else:
            o, lse = _attn_dilated(qkv_dil[dilated.index(g)], bias, B, blk=blk, d=d, gw=gw)
        os_.append(o.reshape(T, gw))
        lses.append(lse.reshape(T, gw))

    return _merge(y_rnn.reshape(T, d_rnn), os_, lses, proj, a0 // (2 * D), x2,
                  w_proj_rnn.astype(BF16), w_proj_att.astype(BF16), w_out.astype(BF16),
                  ln1_g.reshape(1, D), ln1_b.reshape(1, D), w_router.T, alpha=alpha)


def _moe_ln2(x1, x1p, logits_t, router_bias, w_exp_gate, w_exp_up, w_exp_down,
             w_sh_gate, w_sh_up, w_sh_down, ln2_g, ln2_b, alpha):
    T, D = x1.shape
    E = logits_t.shape[0]
    R = EXPERT_ROWS
    n_blk = T * TOP_K // R + E
    n_rows = n_blk * R

    e_idx, gate, rank, counts = _route(logits_t, router_bias)
    counts = counts.reshape(E).astype(I32)
    padded = (counts + R - 1) // R * R
    pad_end = jnp.cumsum(padded)
    seg_start = pad_end - padded
    n_used = (pad_end[-1:] // R).astype(I32)
    first_blk = jnp.concatenate([jnp.zeros((1,), I32), (pad_end // R).astype(I32)])
    dest = _dest(e_idx, rank, seg_start)
    tl = ROW_TILE
    ch = SC_GATHER_CHUNK
    dest_c = dest.reshape(TOP_K, T // ch, ch).transpose(1, 0, 2)

    xs = _sc_scatter_rows(x1p, dest_c, n_rows)
    xs = _pad_fill(xs, seg_start + counts, padded - counts, n_used)
    ys = _experts(xs, first_blk, w_exp_gate, w_exp_up, w_exp_down)
    wgu = jnp.concatenate([w_sh_gate, w_sh_up], axis=-1).astype(BF16)
    gate_tk = gate.T
    wd_sh = w_sh_down.astype(BF16)
    tp = T // COMBINE_PARTS
    out = None
    for p in range(COMBINE_PARTS):
        idx = dest[:, p * tp:(p + 1) * tp].reshape(-1, ch)
        yg = _sc_gather_rows(ys, idx).reshape(TOP_K, tp, ys.shape[1])
        out = _combine(yg, gate_tk, x1, wgu, wd_sh, ln2_g.reshape(1, D), ln2_b.reshape(1, D), out, p,
                       alpha=alpha, tc=tl)
    return out


def kernel(x, w_in, b_in, conv_w, conv_b, w_rg_a, b_rg_a, w_rg_i, b_rg_i, lru_lambda,
           w_proj_rnn, w_proj_att, rel_bias, w_out, ln1_g, ln1_b, w_router, router_bias,
           w_exp_gate, w_exp_up, w_exp_down, w_sh_gate, w_sh_up, w_sh_down, ln2_g, ln2_b):
    B, S, D = x.shape
    depth = w_in.shape[0]
    alpha = (2 * depth) ** 0.25
    for i in range(depth):
        x1, x1p, logits_t = _mixer_ln1(
            x, w_in[i], b_in[i], conv_w[i], conv_b[i], w_rg_a[i], b_rg_a[i], w_rg_i[i], b_rg_i[i],
            lru_lambda[i], w_proj_rnn[i], w_proj_att[i], rel_bias, w_out[i], ln1_g[i], ln1_b[i],
            w_router[i], alpha)
        out = _moe_ln2(x1, x1p, logits_t, router_bias[i], w_exp_gate[i], w_exp_up[i], w_exp_down[i],
                       w_sh_gate[i], w_sh_up[i], w_sh_down[i], ln2_g[i], ln2_b[i], alpha)
        x = out.reshape(B, S, D)
    return x
```

```python
import functools

import numpy as np
import jax
import jax.numpy as jnp
from jax import lax
from jax.experimental import pallas as pl
from jax.experimental.pallas import tpu as pltpu

F32 = jnp.float32
BF16 = jnp.bfloat16
I32 = jnp.int32
U32 = jnp.uint32

RNN_HEADS = 16
LRU_C = 8.0
HEAD_DIM = 64
HEADS_PER_GROUP = 8
DILATED_GROUPS = ((128, 1), (512, 4), (2048, 16))
NUM_BUCKETS = 32
MAX_DISTANCE = 2048
TOP_K = 8
N_EXPERT_GROUPS = 8
TOPK_GROUPS = 4
ROUTED_SCALE = 2.5
LN_EPS = 1e-5

LANES = 128
SUBLANES = 8
VMEM_LIMIT_BYTES = 56 * 1024 * 1024

MASK_VALUE = -1e30
EXPERT_ROWS = 512
ROW_TILE = 256
SC_GATHER_CHUNK = 64
COMBINE_PARTS = 4


def _cparams(*sem):
    return pltpu.CompilerParams(dimension_semantics=sem, vmem_limit_bytes=VMEM_LIMIT_BYTES)


def _sigmoid(v):
    return 0.5 * (jnp.tanh(0.5 * v) + 1.0)


def _in_proj_body(x_ref, w_ref, b_ref, main_ref, *rest, tn, dils):
    dil_refs, scr = rest[:-1], rest[-1]
    xb = x_ref[...].astype(BF16)
    tm = xb.shape[0]

    def chunk(j):
        sl = slice(j * tn, (j + 1) * tn)
        return jnp.dot(xb, w_ref[:, sl], preferred_element_type=F32) + b_ref[:, sl]

    n_main = main_ref.shape[1] // tn
    for j in range(n_main):
        main_ref[:, j * tn:(j + 1) * tn] = chunk(j).astype(main_ref.dtype)
    j = n_main
    for ref, d in zip(dil_refs, dils):
        for c in range(ref.shape[3] // tn):
            acc = chunk(j)
            j += 1
            for q in range(tn // LANES):
                scr[q] = acc[:, q * LANES:(q + 1) * LANES]
            for r in range(d):
                part = jnp.concatenate([scr[q, pl.ds(r, tm // d, stride=d), :] for q in range(tn // LANES)], axis=1)
                ref[0, r, :, c * tn:(c + 1) * tn] = part.astype(ref.dtype)


def _in_proj(x2, w, b, n_main, dils, *, tm=512, tn=512):
    T, D = x2.shape
    N = w.shape[1]
    wd = (N - n_main) // len(dils)
    out_specs = [pl.BlockSpec((tm, n_main), lambda i: (i, 0))]
    out_shape = [jax.ShapeDtypeStruct((T, n_main), BF16)]
    for d in dils:
        out_specs.append(pl.BlockSpec((1, d, tm // d, wd), lambda i: (i, 0, 0, 0)))
        out_shape.append(jax.ShapeDtypeStruct((T // tm, d, tm // d, wd), BF16))
    return pl.pallas_call(
        functools.partial(_in_proj_body, tn=tn, dils=dils),
        grid=(T // tm,),
        in_specs=[pl.BlockSpec((tm, D), lambda i: (i, 0)),
                  pl.BlockSpec((D, N), lambda i: (0, 0), pipeline_mode=pl.Buffered(1)),
                  pl.BlockSpec((1, N), lambda i: (0, 0))],
        out_specs=out_specs,
        out_shape=out_shape,
        scratch_shapes=[pltpu.VMEM((tn // LANES, tm, LANES), F32)],
        compiler_params=_cparams("parallel"),
        name="in_proj",
    )(x2, w, b)


def _rnn_body(xr_ref, gr_ref, cw_ref, cb_ref, wg_ref, bg_ref, lam_ref, y_ref,
              xext, a_s, b_s, hc, *, ts, cb):
    s = pl.program_id(2)

    @pl.when(s == 0)
    def _():
        xext[0:SUBLANES, :] = jnp.zeros((SUBLANES, cb), F32)
        hc[...] = jnp.zeros_like(hc)

    xr = xr_ref[0].astype(F32)
    xext[SUBLANES:SUBLANES + ts, :] = xr
    nw = cw_ref.shape[0]
    xc = cw_ref[nw - 1:nw, :] * xr + cb_ref[...]
    for j in range(nw - 1):
        off = SUBLANES - (nw - 1 - j)
        xc = xc + cw_ref[j:j + 1, :] * xext[off:off + ts, :]
    xext[0:SUBLANES, :] = xext[ts:ts + SUBLANES, :]

    gates = jnp.dot(xc.astype(BF16), wg_ref[0], preferred_element_type=F32) + bg_ref[...]
    r = _sigmoid(gates[:, :cb])
    ig = _sigmoid(gates[:, cb:])
    nl = -lam_ref[...]
    sp = jnp.maximum(nl, 0.0) + jnp.log1p(jnp.exp(-jnp.abs(nl)))
    log_a = (-LRU_C) * r * sp
    a = jnp.exp(log_a)
    u = jnp.sqrt(1.0 - a * a) * (ig * xc)

    row = lax.broadcasted_iota(I32, (ts, cb), 0) & (SUBLANES - 1)
    av, bv = a, u
    for sft in (1, 2, 4):
        a_sh = pltpu.roll(av, sft, 0)
        b_sh = pltpu.roll(bv, sft, 0)
        m = row >= sft
        bv = jnp.where(m, av * b_sh + bv, bv)
        av = jnp.where(m, av * a_sh, av)
    a_s[...] = av
    b_s[...] = bv

    def carry(g, h):
        i0 = pl.multiple_of(g * SUBLANES, SUBLANES)
        h8 = b_s[pl.ds(i0, SUBLANES), :] + a_s[pl.ds(i0, SUBLANES), :] * h
        b_s[pl.ds(i0, SUBLANES), :] = h8
        return h8[SUBLANES - 1:SUBLANES, :]

    hc[0:1, :] = lax.fori_loop(0, ts // SUBLANES, carry, hc[0:1, :], unroll=8)
    gr = gr_ref[0].astype(F32)
    y_ref[0] = (b_s[...] * jax.nn.gelu(gr)).astype(y_ref.dtype)


def _block_diag(w, per):
    H, d, _ = w.shape
    w4 = w.reshape(H // per, per, d, d)
    out = jnp.einsum('gpij,pq->gpiqj', w4, jnp.eye(per, dtype=w.dtype))
    return out.reshape(H // per, per * d, per * d)


def _rnn(proj3, conv_w, conv_b, w_rg_a, b_rg_a, w_rg_i, b_rg_i, lam, *, ts=512, cb=256):
    B, S, _ = proj3.shape
    d_rnn = conv_w.shape[-1]
    nc = d_rnn // cb
    per = cb // (d_rnn // RNN_HEADS)
    wg = jnp.concatenate([_block_diag(w_rg_a, per), _block_diag(w_rg_i, per)], axis=-1).astype(BF16)
    bg = jnp.concatenate([b_rg_a.reshape(nc, 1, cb), b_rg_i.reshape(nc, 1, cb)], axis=-1)
    return pl.pallas_call(
        functools.partial(_rnn_body, ts=ts, cb=cb),
        grid=(B, nc, S // ts),
        in_specs=[pl.BlockSpec((1, ts, cb), lambda b, c, s: (b, s, c)),
                  pl.BlockSpec((1, ts, cb), lambda b, c, s: (b, s, nc + c)),
                  pl.BlockSpec((conv_w.shape[0], cb), lambda b, c, s: (0, c)),
                  pl.BlockSpec((1, cb), lambda b, c, s: (0, c)),
                  pl.BlockSpec((1, cb, 2 * cb), lambda b, c, s: (c, 0, 0)),
                  pl.BlockSpec((None, 1, 2 * cb), lambda b, c, s: (c, 0, 0)),
                  pl.BlockSpec((1, cb), lambda b, c, s: (0, c))],
        out_specs=pl.BlockSpec((1, ts, cb), lambda b, c, s: (b, s, c)),
        out_shape=jax.ShapeDtypeStruct((B, S, d_rnn), BF16),
        scratch_shapes=[pltpu.VMEM((ts + SUBLANES, cb), F32), pltpu.VMEM((ts, cb), F32),
                        pltpu.VMEM((ts, cb), F32), pltpu.VMEM((SUBLANES, cb), F32)],
        compiler_params=_cparams("parallel", "parallel", "arbitrary"),
        name="rnn",
    )(proj3, proj3, conv_w, conv_b.reshape(1, d_rnn), wg, bg, lam.reshape(1, d_rnn))


def _t5_bucket(dist):
    max_exact = NUM_BUCKETS // 2
    d = np.maximum(dist, 1).astype(np.float64)
    large = max_exact + (np.log(d / max_exact) / np.log(MAX_DISTANCE / max_exact)
                         * (NUM_BUCKETS - max_exact)).astype(np.int64)
    large = np.minimum(large, NUM_BUCKETS - 1)
    return np.where(dist < max_exact, dist, large).astype(np.int32)


def _attn_bias(table, window, dilation):
    blk = window // dilation
    qi = np.arange(blk)[:, None]
    ki = np.arange(2 * blk)[None, :]
    rel = qi + blk - ki
    in_window = (rel >= 0) & (rel <= blk)
    bucket = _t5_bucket(np.clip(rel, 0, None) * dilation)
    onehot = (bucket[..., None] == np.arange(NUM_BUCKETS)).astype(np.float32)
    bias = jnp.einsum('qkn,nh->hqk', onehot, table.astype(F32), precision=lax.Precision.HIGHEST)
    first = in_window & (ki >= blk)
    return jnp.stack([jnp.where(first[None], bias, MASK_VALUE),
                      jnp.where(in_window[None], bias, MASK_VALUE)])


def _attn_heads(q, kp, kc, vp, vc, bias_ref):
    blk, gw = q.shape
    lo = lax.broadcasted_iota(I32, (blk, LANES), 1) < HEAD_DIM
    scale = HEAD_DIM ** -0.5
    o_parts, l_parts = [], []
    for p in range(gw // LANES):
        sl = slice(p * LANES, (p + 1) * LANES)
        q2 = q[:, sl] * scale
        k2 = jnp.concatenate([kp[:, sl], kc[:, sl]], axis=0)
        v2 = jnp.concatenate([vp[:, sl], vc[:, sl]], axis=0)
        outs, lses = [], []
        for hh in range(2):
            qm = jnp.where(lo if hh == 0 else jnp.logical_not(lo), q2, jnp.zeros_like(q2))
            sc = lax.dot_general(qm, k2, (((1,), (1,)), ((), ())), preferred_element_type=F32)
            sc = sc + bias_ref[0, 2 * p + hh]
            m = jnp.max(sc, axis=-1, keepdims=True)
            e = jnp.exp(sc - m)
            l = jnp.sum(e, axis=-1, keepdims=True)
            o = jnp.dot(e.astype(BF16), v2, preferred_element_type=F32)
            outs.append(o / l)
            lses.append(jnp.broadcast_to(m + jnp.log(l), (blk, LANES)))
        o_parts.append(jnp.where(lo, outs[0], outs[1]))
        l_parts.append(jnp.where(lo, lses[0], lses[1]))
    return jnp.concatenate(o_parts, axis=1), jnp.concatenate(l_parts, axis=1)


def _attn_body(q_ref, kp_ref, kc_ref, vp_ref, vc_ref, bias_ref, o_ref, lse_ref):
    o, lse = _attn_heads(q_ref[0], kp_ref[0], kc_ref[0], vp_ref[0], vc_ref[0], bias_ref)
    o_ref[0] = o.astype(o_ref.dtype)
    lse_ref[0] = lse


def _attn_dil_body(cur_ref, prev_ref, bias_ref, o_ref, lse_ref, oscr, lscr, *, d, blk, gw):
    nt = cur_ref.shape[0]
    nq = gw // LANES

    def rows(ref, r, c):
        parts = [ref[j, r, :, c * gw:(c + 1) * gw] for j in range(nt)]
        return parts[0] if nt == 1 else jnp.concatenate(parts, axis=0)

    def residue(r, carry):
        o, lse = _attn_heads(rows(cur_ref, r, 2), rows(prev_ref, r, 0), rows(cur_ref, r, 0),
                             rows(prev_ref, r, 1), rows(cur_ref, r, 1), bias_ref)
        for q in range(nq):
            oscr[q, pl.ds(r, blk, stride=d), :] = o[:, q * LANES:(q + 1) * LANES]
            lscr[q, pl.ds(r, blk, stride=d), :] = lse[:, q * LANES:(q + 1) * LANES]
        return carry

    lax.fori_loop(0, d, residue, 0)
    for q in range(nq):
        o_ref[0, :, q * LANES:(q + 1) * LANES] = oscr[q].astype(o_ref.dtype)
        lse_ref[0, :, q * LANES:(q + 1) * LANES] = lscr[q]


def _attn_dilated(qkv_t, bias, B, *, blk, d, gw):
    n_tiles, _, rows_t, _ = qkv_t.shape
    tm = rows_t * d
    nt = blk * d // tm
    nb = n_tiles // (B * nt)
    S = n_tiles * tm // B
    return pl.pallas_call(
        functools.partial(_attn_dil_body, d=d, blk=blk, gw=gw),
        grid=(B, nb),
        in_specs=[pl.BlockSpec((nt, d, rows_t, 3 * gw), lambda b, n: (b * nb + n, 0, 0, 0)),
                  pl.BlockSpec((nt, d, rows_t, 2 * gw), lambda b, n: (b * nb + jnp.maximum(n - 1, 0), 0, 0, 0)),
                  pl.BlockSpec((1,) + bias.shape[1:], lambda b, n: (jnp.minimum(n, 1), 0, 0, 0))],
        out_specs=[pl.BlockSpec((1, blk * d, gw), lambda b, n: (b, n, 0)),
                   pl.BlockSpec((1, blk * d, gw), lambda b, n: (b, n, 0))],
        out_shape=[jax.ShapeDtypeStruct((B, S, gw), BF16), jax.ShapeDtypeStruct((B, S, gw), F32)],
        scratch_shapes=[pltpu.VMEM((gw // LANES, blk * d, LANES), F32),
                        pltpu.VMEM((gw // LANES, blk * d, LANES), F32)],
        compiler_params=_cparams("parallel", "parallel"),
        name=f"attn_d{d}",
    )(qkv_t, qkv_t, bias)


def _attn_group(qkv, bias, *, blk, d, gw, q_blk, k_blk, v_blk, row_blk):
    B, L, _ = qkv.shape
    nb = L // blk

    def cur(col):
        return pl.BlockSpec((1, blk, gw), lambda b, r, n: (b, n, r * row_blk + col))

    def prev(col):
        return pl.BlockSpec((1, blk, gw), lambda b, r, n: (b, jnp.maximum(n - 1, 0), r * row_blk + col))

    return pl.pallas_call(
        _attn_body,
        grid=(B, d, nb),
        in_specs=[cur(q_blk), prev(k_blk), cur(k_blk), prev(v_blk), cur(v_blk),
                  pl.BlockSpec((1,) + bias.shape[1:], lambda b, r, n: (jnp.minimum(n, 1), 0, 0, 0))],
        out_specs=[pl.BlockSpec((1, blk, gw), lambda b, r, n: (b, n, r)),
                   pl.BlockSpec((1, blk, gw), lambda b, r, n: (b, n, r))],
        out_shape=[jax.ShapeDtypeStruct((B, L, d * gw), BF16),
                   jax.ShapeDtypeStruct((B, L, d * gw), F32)],
        compiler_params=_cparams("parallel", "parallel", "parallel"),
        name=f"attn_d{d}",
    )(qkv, qkv, qkv, qkv, qkv, bias)


def _merge_body(yr_ref, o1_ref, o2_ref, o3_ref, l1_ref, l2_ref, l3_ref, g_ref, x_ref,
                wr_ref, wa_ref, wo_ref, lg_ref, lb_ref, wrh_ref, wrl_ref,
                x1_ref, x1p_ref, lt_ref, *, alpha):
    l1, l2, l3 = l1_ref[...], l2_ref[...], l3_ref[...]
    mx = jnp.maximum(jnp.maximum(l1, l2), l3)
    w1, w2, w3 = jnp.exp(l1 - mx), jnp.exp(l2 - mx), jnp.exp(l3 - mx)
    y_att = (w1 * o1_ref[...].astype(F32) + w2 * o2_ref[...].astype(F32)
             + w3 * o3_ref[...].astype(F32)) / (w1 + w2 + w3)
    pr = jnp.dot(yr_ref[...], wr_ref[...], preferred_element_type=F32)
    pa = jnp.dot(y_att.astype(BF16), wa_ref[...], preferred_element_type=F32)
    dm = pr.shape[1]
    g = g_ref[...].astype(F32)
    merged = _sigmoid(g[:, :dm]) * pr + _sigmoid(g[:, dm:]) * pa
    mix = jnp.dot(merged.astype(BF16), wo_ref[...], preferred_element_type=F32)
    z = alpha * x_ref[...] + mix
    mu = jnp.mean(z, axis=-1, keepdims=True)
    zc = z - mu
    var = jnp.mean(zc * zc, axis=-1, keepdims=True)
    x1 = zc * lax.rsqrt(var + LN_EPS) * lg_ref[...] + lb_ref[...]
    x1_ref[...] = x1
    x1p_ref[...] = _pack_rows(x1)
    nt = (((1,), (1,)), ((), ()))
    x1h = x1.astype(BF16)
    x1l = (x1 - x1h.astype(F32)).astype(BF16)
    lt_ref[...] = (lax.dot_general(wrh_ref[...], x1h, nt, preferred_element_type=F32)
                   + (lax.dot_general(wrh_ref[...], x1l, nt, preferred_element_type=F32)
                      + lax.dot_general(wrl_ref[...], x1h, nt, preferred_element_type=F32)))


def _merge(y_rnn, os_, lses, proj, gate_blk, x2, wr, wa, wo, ln_g, ln_b, w_router_t, *, alpha, tm=512):
    T, D = x2.shape
    da = os_[0].shape[1]
    E = w_router_t.shape[0]
    wrh = w_router_t.astype(BF16)
    wrl = (w_router_t - wrh.astype(F32)).astype(BF16)
    row = lambda w: pl.BlockSpec((tm, w), lambda i: (i, 0))
    full = lambda a: pl.BlockSpec(a.shape, lambda i: (0,) * a.ndim)
    return pl.pallas_call(
        functools.partial(_merge_body, alpha=alpha),
        grid=(T // tm,),
        in_specs=[row(D), row(da), row(da), row(da), row(da), row(da), row(da),
                  pl.BlockSpec((tm, 2 * D), lambda i: (i, gate_blk)), row(D),
                  full(wr), full(wa), full(wo), full(ln_g), full(ln_b), full(wrh), full(wrl)],
        out_specs=[row(D), row(D // 2), pl.BlockSpec((E, tm), lambda i: (0, i))],
        out_shape=[jax.ShapeDtypeStruct((T, D), F32), jax.ShapeDtypeStruct((T, D // 2), U32),
                   jax.ShapeDtypeStruct((E, T), F32)],
        compiler_params=_cparams("parallel"),
        name="merge",
    )(y_rnn, *os_, *lses, proj, x2, wr, wa, wo, ln_g, ln_b, wrh, wrl)


def _first_max(vals, idx, big):
    m = jnp.max(vals, axis=0, keepdims=True)
    i = jnp.min(jnp.where(vals == m, idx, big), axis=0, keepdims=True)
    return m, i


def _route_body(lt_ref, rb_ref, e_ref, g_ref, r_ref, cnt_ref, *, tl):
    E = lt_ref.shape[0]
    per = E // N_EXPERT_GROUPS
    neg = -jnp.inf

    @pl.when(pl.program_id(0) == 0)
    def _():
        cnt_ref[...] = jnp.zeros_like(cnt_ref)

    scores = jax.nn.sigmoid(lt_ref[...])
    sel = scores + rb_ref[...]
    rowi = lax.broadcasted_iota(I32, (E, tl), 0)

    gi = lax.broadcasted_iota(I32, (N_EXPERT_GROUPS, tl), 0)
    gsc = jnp.zeros((N_EXPERT_GROUPS, tl), F32)
    for g in range(N_EXPERT_GROUPS):
        blk = sel[g * per:(g + 1) * per]
        ri = lax.broadcasted_iota(I32, (per, tl), 0) + g * per
        m1, i1 = _first_max(blk, ri, E)
        m2 = jnp.max(jnp.where(ri == i1, neg, blk), axis=0, keepdims=True)
        gsc = jnp.where(gi == g, m1 + m2, gsc)
    keep = jnp.zeros((N_EXPERT_GROUPS, tl), F32)
    for _ in range(TOPK_GROUPS):
        _, ig = _first_max(gsc, gi, N_EXPERT_GROUPS)
        hit = gi == ig
        keep = jnp.where(hit, 1.0, keep)
        gsc = jnp.where(hit, neg, gsc)
    cur = jnp.concatenate(
        [jnp.where(keep[g:g + 1] > 0.5, sel[g * per:(g + 1) * per], neg) for g in range(N_EXPERT_GROUPS)],
        axis=0)

    ti = lax.broadcasted_iota(I32, (tl, tl), 0)
    tj = lax.broadcasted_iota(I32, (tl, tl), 1)
    earlier = jnp.where(ti < tj, 1.0, 0.0).astype(BF16)

    es, gv = [], []
    onehot = jnp.zeros((E, tl), F32)
    for k in range(TOP_K):
        _, ie = _first_max(cur, rowi, E)
        hit = rowi == ie
        es.append(ie)
        gv.append(jnp.sum(jnp.where(hit, scores, 0.0), axis=0, keepdims=True))
        onehot = jnp.where(hit, 1.0, onehot)
        cur = jnp.where(hit, neg, cur)
    gsum = gv[0]
    for k in range(1, TOP_K):
        gsum = gsum + gv[k]
    ranks = jnp.dot(onehot.astype(BF16), earlier, preferred_element_type=F32) + cnt_ref[...]
    for k in range(TOP_K):
        e_ref[k:k + 1, :] = es[k]
        g_ref[k:k + 1, :] = gv[k] / gsum * ROUTED_SCALE
        r_ref[k:k + 1, :] = jnp.sum(jnp.where(rowi == es[k], ranks, 0.0), axis=0, keepdims=True).astype(I32)
    cnt_ref[...] += jnp.sum(onehot, axis=1, keepdims=True)


def _route(logits_t, router_bias, *, tl=256):
    E, T = logits_t.shape
    kt = pl.BlockSpec((TOP_K, tl), lambda i: (0, i))
    return pl.pallas_call(
        functools.partial(_route_body, tl=tl),
        grid=(T // tl,),
        in_specs=[pl.BlockSpec((E, tl), lambda i: (0, i)), pl.BlockSpec((E, 1), lambda i: (0, 0))],
        out_specs=[kt, kt, kt, pl.BlockSpec((E, 1), lambda i: (0, 0))],
        out_shape=[jax.ShapeDtypeStruct((TOP_K, T), I32), jax.ShapeDtypeStruct((TOP_K, T), F32),
                   jax.ShapeDtypeStruct((TOP_K, T), I32), jax.ShapeDtypeStruct((E, 1), F32)],
        compiler_params=_cparams("arbitrary"),
        name="route",
    )(logits_t, router_bias.reshape(E, 1))


def _dest_body(e_ref, r_ref, ss_ref, d_ref):
    E = ss_ref.shape[0]
    tl = e_ref.shape[1]
    rowi = lax.broadcasted_iota(I32, (E, tl), 0)
    e = e_ref[...]
    rows = [jnp.sum(jnp.where(rowi == e[k:k + 1], ss_ref[...], 0), axis=0, keepdims=True)
            for k in range(e.shape[0])]
    d_ref[...] = jnp.concatenate(rows, axis=0) + r_ref[...]


def _dest(e_idx, rank, seg_start, *, tl=512):
    K, T = e_idx.shape
    E = seg_start.shape[0]
    kt = pl.BlockSpec((K, tl), lambda i: (0, i))
    return pl.pallas_call(
        _dest_body, grid=(T // tl,),
        in_specs=[kt, kt, pl.BlockSpec((E, 1), lambda i: (0, 0))],
        out_specs=kt, out_shape=jax.ShapeDtypeStruct((K, T), I32),
        compiler_params=_cparams("parallel"), name="dest",
    )(e_idx, rank, seg_start.reshape(E, 1))


def _pack_rows(v):
    w = v.shape[1] // 2
    lo = pltpu.bitcast(v[:, :w].astype(BF16).astype(F32), U32) >> 16
    hi = pltpu.bitcast(v[:, w:].astype(BF16).astype(F32), U32) & jnp.uint32(0xFFFF0000)
    return lo | hi


def _unpack_rows(p):
    lo = pltpu.bitcast(p << 16, F32)
    hi = pltpu.bitcast(p & jnp.uint32(0xFFFF0000), F32)
    return lo, hi


def _sc_scatter_rows(x, idx3, n_rows):
    from jax.experimental.pallas import tpu_sc as plsc
    n_chunks, K, CH = idx3.shape
    W = x.shape[1]
    info = plsc.get_sparse_core_info()
    NC, n_workers = info.num_cores, info.num_cores * info.num_subcores
    per_w = n_chunks // n_workers
    assert per_w * n_workers == n_chunks and per_w % 2 == 0

    def body(x_hbm, idx_hbm, out_hbm, idx_v, rows_v, sem_l, sem_s):
        c0 = (lax.axis_index("s") * NC + lax.axis_index("c")) * per_w

        def load(cc, b):
            return pltpu.make_async_copy(x_hbm.at[pl.ds(pl.multiple_of((c0 + cc) * CH, CH), CH)], rows_v.at[b],
                                         sem_l.at[b])

        def scatter(b, k):
            return pltpu.make_async_copy(rows_v.at[b], out_hbm.at[idx_v.at[b, k]], sem_s.at[b])

        pltpu.sync_copy(idx_hbm.at[c0], idx_v.at[0])
        load(0, 0).start()

        @pl.loop(0, per_w, step=2)
        def _(c):
            for b in range(2):
                cc = c + b
                load(cc, b).wait()
                for k in range(K):
                    scatter(b, k).start()

                @pl.when(cc >= 1)
                def _():
                    for k in range(K):
                        scatter(1 - b, k).wait()

                @pl.when(cc + 1 < per_w)
                def _():
                    pltpu.sync_copy(idx_hbm.at[c0 + cc + 1], idx_v.at[1 - b])
                    load(cc + 1, 1 - b).start()

        for k in range(K):
            scatter(1, k).wait()

    return pl.kernel(
        body, mesh=plsc.VectorSubcoreMesh(core_axis_name="c", subcore_axis_name="s"),
        out_type=jax.ShapeDtypeStruct((n_rows, W), x.dtype),
        scratch_types=[pltpu.VMEM((2, K, CH), I32), pltpu.VMEM((2, CH, W), x.dtype),
                       pltpu.SemaphoreType.DMA((2,)), pltpu.SemaphoreType.DMA((2,))],
    )(x, idx3)


def _pad_fill_body(ps_ref, pc_ref, nu_ref, xs_in, xs_hbm, zrow, sem_z, sem_c, sem_b, sem_t):
    del xs_in
    i = pl.program_id(0)
    n = pl.num_programs(0)
    R = zrow.shape[0]

    def pad_copy(row):
        return pltpu.make_async_copy(zrow.at[pl.ds(0, 1), :], xs_hbm.at[pl.ds(row, 1), :], sem_z)

    zrow[...] = jnp.zeros_like(zrow)

    n_blk = xs_hbm.shape[0] // R
    tail_per_step = -(-n_blk // n)

    def tail(j, c, wait):
        blk = i * tail_per_step + j

        @pl.when(jnp.logical_and(blk >= nu_ref[0], blk < n_blk))
        def _():
            cp = pltpu.make_async_copy(zrow, xs_hbm.at[pl.ds(pl.multiple_of(blk * R, R), R), :], sem_t)
            cp.wait() if wait else cp.start()

        return c

    lax.fori_loop(0, tail_per_step, functools.partial(tail, wait=False), 0)

    E = ps_ref.shape[0]
    per_step = -(-E // n)

    def pads(j, c, wait):
        e = jnp.minimum(i * per_step + j, E - 1)
        cnt = jnp.where(i * per_step + j < E, pc_ref[e], 0)
        start = ps_ref[e]
        n_single = jnp.minimum(cnt, (-start) & (SUBLANES - 1))
        start8 = start + n_single

        def one(r, c2):
            cp = pad_copy(start + r)
            cp.wait() if wait else cp.start()
            return c2

        rem8 = lax.shift_right_logical(cnt - n_single, 3)
        n_eight = jnp.minimum(rem8, lax.shift_right_logical(-start8, 3) & (SUBLANES - 1))
        start64 = start8 + n_eight * SUBLANES
        big = SUBLANES * SUBLANES

        def eight(r, c2):
            row = pl.multiple_of(start8 + r * SUBLANES, SUBLANES)
            cp = pltpu.make_async_copy(zrow.at[pl.ds(0, SUBLANES), :], xs_hbm.at[pl.ds(row, SUBLANES), :], sem_c)
            cp.wait() if wait else cp.start()
            return c2

        def sixty_four(r, c2):
            row = pl.multiple_of(start64 + r * big, big)
            cp = pltpu.make_async_copy(zrow.at[pl.ds(0, big), :], xs_hbm.at[pl.ds(row, big), :], sem_b)
            cp.wait() if wait else cp.start()
            return c2

        c = lax.fori_loop(0, n_single, one, c)
        c = lax.fori_loop(0, n_eight, eight, c)
        return lax.fori_loop(0, lax.shift_right_logical(rem8 - n_eight, 3), sixty_four, c)

    lax.fori_loop(0, per_step, functools.partial(pads, wait=False), 0)
    lax.fori_loop(0, per_step, functools.partial(pads, wait=True), 0)
    lax.fori_loop(0, tail_per_step, functools.partial(tail, wait=True), 0)


def _pad_fill(xs, pad_start, pad_cnt, n_used, *, steps=16):
    any_spec = pl.BlockSpec(memory_space=pl.ANY)
    return pl.pallas_call(
        _pad_fill_body,
        grid_spec=pltpu.PrefetchScalarGridSpec(
            num_scalar_prefetch=3, grid=(steps,),
            in_specs=[any_spec], out_specs=any_spec,
            scratch_shapes=[pltpu.VMEM((EXPERT_ROWS, xs.shape[1]), xs.dtype),
                            pltpu.SemaphoreType.DMA, pltpu.SemaphoreType.DMA, pltpu.SemaphoreType.DMA,
                            pltpu.SemaphoreType.DMA]),
        out_shape=jax.ShapeDtypeStruct(xs.shape, xs.dtype),
        input_output_aliases={3: 0},
        compiler_params=_cparams("arbitrary"),
        name="pad_fill",
    )(pad_start, pad_cnt, n_used, xs)


def _experts_body(fb_ref, xs_hbm, wg_ref, wu_ref, wd_ref, ys_hbm, xbuf, ybuf, wgu_s, wd_s, sem_x, sem_y):
    e = pl.program_id(0)
    n_e = pl.num_programs(0)
    de = wg_ref.shape[2]
    R = xbuf.shape[1] // 2
    n_blk = xs_hbm.shape[0] // R
    n_used = fb_ref[n_e]
    n_pairs = lax.shift_right_logical(n_used + 1, 1)

    def pair_rows(ref, p):
        return ref.at[pl.ds(pl.multiple_of(p * (2 * R), 2 * R), 2 * R), :]

    def x_copy(p, slot):
        return pltpu.make_async_copy(pair_rows(xs_hbm, p), xbuf.at[slot], sem_x.at[slot])

    def y_copy(p, slot):
        return pltpu.make_async_copy(ybuf.at[slot], pair_rows(ys_hbm, p), sem_y.at[slot])

    @pl.when(jnp.logical_and(e == 0, n_used > 0))
    def _():
        x_copy(0, 0).start(priority=1)

    wgu_s[:, :de] = wg_ref[0].astype(BF16)
    wgu_s[:, de:] = wu_ref[0].astype(BF16)
    wd_s[...] = wd_ref[0].astype(BF16)

    def block(g, c):
        p = lax.shift_right_logical(g, 1)
        half = g & 1
        slot = p & 1

        @pl.when(half == 0)
        def _():
            x_copy(p, slot).wait()

            @pl.when(p + 1 < n_pairs)
            def _():
                x_copy(p + 1, 1 - slot).start(priority=1)

            @pl.when(p >= 2)
            def _():
                y_copy(p - 2, slot).wait()

        r0 = pl.multiple_of(half * R, R)
        lo, hi = _unpack_rows(xbuf[slot, pl.ds(r0, R), :])
        xb = jnp.concatenate([lo.astype(BF16), hi.astype(BF16)], axis=1)
        h = jnp.dot(xb, wgu_s[...], preferred_element_type=F32)
        hg = h[:, :de]
        act = (hg * _sigmoid(hg) * h[:, de:]).astype(BF16)
        ybuf[slot, pl.ds(r0, R), :] = _pack_rows(jnp.dot(act, wd_s[...], preferred_element_type=F32))

        @pl.when(half == 1)
        def _():
            y_copy(p, slot).start(priority=1)

        return c

    lax.fori_loop(fb_ref[e], fb_ref[e + 1], block, 0)

    @pl.when(e == n_e - 1)
    def _():
        @pl.when((n_used & 1) == 1)
        def _():
            p = lax.shift_right_logical(n_used, 1)
            y_copy(p, p & 1).start(priority=1)

        for back in (2, 1):
            p = n_pairs - back

            @pl.when(p >= 0)
            def _():
                y_copy(p, p & 1).wait()

        ybuf[0] = jnp.zeros(ybuf.shape[1:], ybuf.dtype)

        def tail(g, c, wait):
            cp = pltpu.make_async_copy(ybuf.at[0, pl.ds(0, R), :],
                                       ys_hbm.at[pl.ds(pl.multiple_of(g * R, R), R), :], sem_y.at[0])
            cp.wait() if wait else cp.start()
            return c

        lax.fori_loop(n_used, n_blk, functools.partial(tail, wait=False), 0)
        lax.fori_loop(n_used, n_blk, functools.partial(tail, wait=True), 0)


def _experts(xs, first_blk, wg, wu, wd):
    n_rows, W = xs.shape
    R = EXPERT_ROWS
    E, D, de = wg.shape
    assert (n_rows // R) % 2 == 0
    any_spec = pl.BlockSpec(memory_space=pl.ANY)
    wmap = lambda e, fb: (e, 0, 0)
    return pl.pallas_call(
        _experts_body,
        grid_spec=pltpu.PrefetchScalarGridSpec(
            num_scalar_prefetch=1, grid=(E,),
            in_specs=[any_spec, pl.BlockSpec((1, D, de), wmap), pl.BlockSpec((1, D, de), wmap),
                      pl.BlockSpec((1, de, D), wmap)],
            out_specs=any_spec,
            scratch_shapes=[pltpu.VMEM((2, 2 * R, W), U32), pltpu.VMEM((2, 2 * R, W), U32),
                            pltpu.VMEM((D, 2 * de), BF16), pltpu.VMEM((de, D), BF16),
                            pltpu.SemaphoreType.DMA((2,)), pltpu.SemaphoreType.DMA((2,))]),
        out_shape=jax.ShapeDtypeStruct((n_rows, W), U32),
        compiler_params=_cparams("arbitrary"),
        name="experts",
    )(first_blk, xs, wg, wu, wd)


def _sc_gather_rows(table, idx2):
    from jax.experimental.pallas import tpu_sc as plsc
    n_chunks, CH = idx2.shape
    W = table.shape[1]
    info = plsc.get_sparse_core_info()
    NC, n_workers = info.num_cores, info.num_cores * info.num_subcores
    per_w = n_chunks // n_workers
    assert per_w * n_workers == n_chunks and per_w % 2 == 0

    def body(table_hbm, idx_hbm, out_hbm, idx_v, rows_v, sem):
        c0 = (lax.axis_index("s") * NC + lax.axis_index("c")) * per_w

        def gather(b):
            return pltpu.make_async_copy(table_hbm.at[idx_v.at[b]], rows_v.at[b], sem.at[b])

        pltpu.sync_copy(idx_hbm.at[c0], idx_v.at[0])
        gather(0).start()

        @pl.loop(0, per_w, step=2)
        def _(c):
            for b in range(2):
                cc = c + b

                @pl.when(cc + 1 < per_w)
                def _():
                    pltpu.sync_copy(idx_hbm.at[c0 + cc + 1], idx_v.at[1 - b])
                    gather(1 - b).start()

                gather(b).wait()
                pltpu.sync_copy(rows_v.at[b], out_hbm.at[pl.ds(pl.multiple_of((c0 + cc) * CH, CH), CH)])

    return pl.kernel(
        body, mesh=plsc.VectorSubcoreMesh(core_axis_name="c", subcore_axis_name="s"),
        out_type=jax.ShapeDtypeStruct((n_chunks * CH, W), table.dtype),
        scratch_types=[pltpu.VMEM((2, CH), I32), pltpu.VMEM((2, CH, W), table.dtype),
                       pltpu.SemaphoreType.DMA((2,))],
    )(table, idx2)


def _combine_body(yg_ref, gate_ref, x1_ref, wgu_ref, wd_ref, lg_ref, lb_ref, *rest, alpha):
    o_ref = rest[-1]
    x1 = x1_ref[...]
    ds_ = wd_ref.shape[0]
    h = jnp.dot(x1.astype(BF16), wgu_ref[...], preferred_element_type=F32)
    hg = h[:, :ds_]
    act = (hg * _sigmoid(hg) * h[:, ds_:]).astype(BF16)
    shared = jnp.dot(act, wd_ref[...], preferred_element_type=F32)

    g = gate_ref[...]
    lo_acc = hi_acc = None
    for k in range(yg_ref.shape[0]):
        lo, hi = _unpack_rows(yg_ref[k])
        gk = g[:, k:k + 1]
        lo_acc = gk * lo if k == 0 else lo_acc + gk * lo
        hi_acc = gk * hi if k == 0 else hi_acc + gk * hi
    routed = jnp.concatenate([lo_acc, hi_acc], axis=1)

    z = alpha * x1 + (routed + shared)
    mu = jnp.mean(z, axis=-1, keepdims=True)
    zc = z - mu
    var = jnp.mean(zc * zc, axis=-1, keepdims=True)
    o_ref[...] = zc * lax.rsqrt(var + LN_EPS) * lg_ref[...] + lb_ref[...]


def _combine(yg, gate_tk, x1, wgu, wd, ln_g, ln_b, out_prev, part, *, alpha, tc):
    T, D = x1.shape
    K, Tp, W = yg.shape
    off = part * (Tp // tc)
    full = lambda a: pl.BlockSpec(a.shape, lambda i: (0,) * a.ndim)
    args = [yg, gate_tk, x1, wgu, wd, ln_g, ln_b]
    in_specs = [pl.BlockSpec((K, tc, W), lambda i: (0, i, 0)), pl.BlockSpec((tc, K), lambda i: (off + i, 0)),
                pl.BlockSpec((tc, D), lambda i: (off + i, 0)), full(wgu), full(wd), full(ln_g), full(ln_b)]
    aliases = {}
    if out_prev is not None:
        args.append(out_prev)
        in_specs.append(pl.BlockSpec(memory_space=pl.ANY))
        aliases = {len(args) - 1: 0}
    return pl.pallas_call(
        functools.partial(_combine_body, alpha=alpha),
        grid=(Tp // tc,),
        in_specs=in_specs,
        out_specs=pl.BlockSpec((tc, D), lambda i: (off + i, 0)),
        out_shape=jax.ShapeDtypeStruct((T, D), F32),
        input_output_aliases=aliases,
        compiler_params=_cparams("parallel"),
        name="combine",
    )(*args)


def _mixer_ln1(x, w_in, b_in, conv_w, conv_b, w_rg_a, b_rg_a, w_rg_i, b_rg_i, lru_lambda,
               w_proj_rnn, w_proj_att, rel_bias, w_out, ln1_g, ln1_b, w_router, alpha):
    B, S, D = x.shape
    T = B * S
    d_rnn = conv_w.shape[-1]
    gw = HEADS_PER_GROUP * HEAD_DIM
    d_att = gw * len(DILATED_GROUPS)
    a0 = 2 * d_rnn
    a1 = a0 + 3 * d_att
    head = lambda j, g: slice(a0 + j * d_att + g * gw, a0 + j * d_att + (g + 1) * gw)
    plain = [g for g, (_, d) in enumerate(DILATED_GROUPS) if d == 1]
    dilated = [g for g, (_, d) in enumerate(DILATED_GROUPS) if d > 1]
    order = ([slice(0, a0), slice(a1, None)] + [head(j, g) for g in plain for j in range(3)]
             + [head(j, g) for g in dilated for j in (1, 2, 0)])
    perm = lambda w: jnp.concatenate([w[..., s] for s in order], axis=-1)
    w_p = perm(w_in).astype(BF16)
    b_p = perm(b_in).reshape(1, -1)
    qkv0 = a0 + 2 * D
    n_main = qkv0 + 3 * gw * len(plain)
    x2 = x.reshape(T, D)

    proj, *qkv_dil = _in_proj(x2, w_p, b_p, n_main, tuple(DILATED_GROUPS[g][1] for g in dilated))
    proj3 = proj.reshape(B, S, n_main)
    y_rnn = _rnn(proj3, conv_w, conv_b, w_rg_a, b_rg_a, w_rg_i, b_rg_i, lru_lambda)

    os_, lses = [], []
    for g, (window, d) in enumerate(DILATED_GROUPS):
        blk = window // d
        bias = _attn_bias(rel_bias[:, g * HEADS_PER_GROUP:(g + 1) * HEADS_PER_GROUP], window, d)
        if d == 1:
            c0 = (qkv0 + 3 * gw * plain.index(g)) // gw
            o, lse = _attn_group(proj3, bias, blk=blk, d=1, gw=gw, q_blk=c0, k_blk=c0 + 1, v_blk=c0 + 2,
                                 row_blk=n_main // gw)
        else:
            o, lse = _attn_dilated(qkv_dil[dilated.index(g)], bias, B, blk=blk, d=d, gw=gw)
        os_.append(o.reshape(T, gw))
        lses.append(lse.reshape(T, gw))

    return _merge(y_rnn.reshape(T, d_rnn), os_, lses, proj, a0 // (2 * D), x2,
                  w_proj_rnn.astype(BF16), w_proj_att.astype(BF16), w_out.astype(BF16),
                  ln1_g.reshape(1, D), ln1_b.reshape(1, D), w_router.T, alpha=alpha)


def _moe_ln2(x1, x1p, logits_t, router_bias, w_exp_gate, w_exp_up, w_exp_down,
             w_sh_gate, w_sh_up, w_sh_down, ln2_g, ln2_b, alpha):
    T, D = x1.shape
    E = logits_t.shape[0]
    R = EXPERT_ROWS
    n_blk = T * TOP_K // R + E
    n_rows = n_blk * R

    e_idx, gate, rank, counts = _route(logits_t, router_bias)
    counts = counts.reshape(E).astype(I32)
    padded = (counts + R - 1) // R * R
    pad_end = jnp.cumsum(padded)
    seg_start = pad_end - padded
    n_used = (pad_end[-1:] // R).astype(I32)
    first_blk = jnp.concatenate([jnp.zeros((1,), I32), (pad_end // R).astype(I32)])
    dest = _dest(e_idx, rank, seg_start)
    tl = ROW_TILE
    ch = SC_GATHER_CHUNK
    dest_c = dest.reshape(TOP_K, T // ch, ch).transpose(1, 0, 2)

    xs = _sc_scatter_rows(x1p, dest_c, n_rows)
    xs = _pad_fill(xs, seg_start + counts, padded - counts, n_used)
    ys = _experts(xs, first_blk, w_exp_gate, w_exp_up, w_exp_down)
    wgu = jnp.concatenate([w_sh_gate, w_sh_up], axis=-1).astype(BF16)
    gate_tk = gate.T
    wd_sh = w_sh_down.astype(BF16)
    tp = T // COMBINE_PARTS
    out = None
    for p in range(COMBINE_PARTS):
        idx = dest[:, p * tp:(p + 1) * tp].reshape(-1, ch)
        yg = _sc_gather_rows(ys, idx).reshape(TOP_K, tp, ys.shape[1])
        out = _combine(yg, gate_tk, x1, wgu, wd_sh, ln2_g.reshape(1, D), ln2_b.reshape(1, D), out, p,
                       alpha=alpha, tc=tl)
    return out


def kernel(x, w_in, b_in, conv_w, conv_b, w_rg_a, b_rg_a, w_rg_i, b_rg_i, lru_lambda,
           w_proj_rnn, w_proj_att, rel_bias, w_out, ln1_g, ln1_b, w_router, router_bias,
           w_exp_gate, w_exp_up, w_exp_down, w_sh_gate, w_sh_up, w_sh_down, ln2_g, ln2_b):
    B, S, D = x.shape
    depth = w_in.shape[0]
    alpha = (2 * depth) ** 0.25
    for i in range(depth):
        x1, x1p, logits_t = _mixer_ln1(
            x, w_in[i], b_in[i], conv_w[i], conv_b[i], w_rg_a[i], b_rg_a[i], w_rg_i[i], b_rg_i[i],
            lru_lambda[i], w_proj_rnn[i], w_proj_att[i], rel_bias, w_out[i], ln1_g[i], ln1_b[i],
            w_router[i], alpha)
        out = _moe_ln2(x1, x1p, logits_t, router_bias[i], w_exp_gate[i], w_exp_up[i], w_exp_down[i],
                       w_sh_gate[i], w_sh_up[i], w_sh_down[i], ln2_g[i], ln2_b[i], alpha)
        x = out.reshape(B, S, D)
    return x
```

```python
import functools

import numpy as np
import jax
import jax.numpy as jnp
from jax import lax
from jax.experimental import pallas as pl
from jax.experimental.pallas import tpu as pltpu

F32 = jnp.float32
BF16 = jnp.bfloat16
I32 = jnp.int32
U32 = jnp.uint32

RNN_HEADS = 16
LRU_C = 8.0
HEAD_DIM = 64
HEADS_PER_GROUP = 8
DILATED_GROUPS = ((128, 1), (512, 4), (2048, 16))
NUM_BUCKETS = 32
MAX_DISTANCE = 2048
TOP_K = 8
N_EXPERT_GROUPS = 8
TOPK_GROUPS = 4
ROUTED_SCALE = 2.5
LN_EPS = 1e-5

LANES = 128
SUBLANES = 8
VMEM_LIMIT_BYTES = 56 * 1024 * 1024

MASK_VALUE = -1e30
EXPERT_ROWS = 512
EXPERT_GROUP = 4
ROW_TILE = 256
SC_GATHER_CHUNK = 64
COMBINE_PARTS = 4


def _cparams(*sem):
    return pltpu.CompilerParams(dimension_semantics=sem, vmem_limit_bytes=VMEM_LIMIT_BYTES)


def _sigmoid(v):
    return 0.5 * (jnp.tanh(0.5 * v) + 1.0)


def _in_proj_body(x_ref, w_ref, b_ref, main_ref, *rest, tn, dils):
    dil_refs, scr = rest[:-1], rest[-1]
    xb = x_ref[...].astype(BF16)
    tm = xb.shape[0]

    def chunk(j):
        sl = slice(j * tn, (j + 1) * tn)
        return jnp.dot(xb, w_ref[:, sl], preferred_element_type=F32) + b_ref[:, sl]

    n_main = main_ref.shape[1] // tn
    for j in range(n_main):
        main_ref[:, j * tn:(j + 1) * tn] = chunk(j).astype(main_ref.dtype)
    j = n_main
    for ref, d in zip(dil_refs, dils):
        for c in range(ref.shape[3] // tn):
            acc = chunk(j)
            j += 1
            for q in range(tn // LANES):
                scr[q] = acc[:, q * LANES:(q + 1) * LANES]
            for r in range(d):
                part = jnp.concatenate([scr[q, pl.ds(r, tm // d, stride=d), :] for q in range(tn // LANES)], axis=1)
                ref[0, r, :, c * tn:(c + 1) * tn] = part.astype(ref.dtype)


def _in_proj(x2, w, b, n_main, dils, *, tm=512, tn=512):
    T, D = x2.shape
    N = w.shape[1]
    wd = (N - n_main) // len(dils)
    out_specs = [pl.BlockSpec((tm, n_main), lambda i: (i, 0))]
    out_shape = [jax.ShapeDtypeStruct((T, n_main), BF16)]
    for d in dils:
        out_specs.append(pl.BlockSpec((1, d, tm // d, wd), lambda i: (i, 0, 0, 0)))
        out_shape.append(jax.ShapeDtypeStruct((T // tm, d, tm // d, wd), BF16))
    return pl.pallas_call(
        functools.partial(_in_proj_body, tn=tn, dils=dils),
        grid=(T // tm,),
        in_specs=[pl.BlockSpec((tm, D), lambda i: (i, 0)),
                  pl.BlockSpec((D, N), lambda i: (0, 0), pipeline_mode=pl.Buffered(1)),
                  pl.BlockSpec((1, N), lambda i: (0, 0))],
        out_specs=out_specs,
        out_shape=out_shape,
        scratch_shapes=[pltpu.VMEM((tn // LANES, tm, LANES), F32)],
        compiler_params=_cparams("parallel"),
        name="in_proj",
    )(x2, w, b)


def _rnn_body(xr_ref, gr_ref, cw_ref, cb_ref, wg_ref, bg_ref, lam_ref, y_ref,
              xext, a_s, b_s, hc, *, ts, cb):
    s = pl.program_id(2)

    @pl.when(s == 0)
    def _():
        xext[0:SUBLANES, :] = jnp.zeros((SUBLANES, cb), F32)
        hc[...] = jnp.zeros_like(hc)

    xr = xr_ref[0].astype(F32)
    xext[SUBLANES:SUBLANES + ts, :] = xr
    nw = cw_ref.shape[0]
    xc = cw_ref[nw - 1:nw, :] * xr + cb_ref[...]
    for j in range(nw - 1):
        off = SUBLANES - (nw - 1 - j)
        xc = xc + cw_ref[j:j + 1, :] * xext[off:off + ts, :]
    xext[0:SUBLANES, :] = xext[ts:ts + SUBLANES, :]

    gates = jnp.dot(xc.astype(BF16), wg_ref[0], preferred_element_type=F32) + bg_ref[...]
    r = _sigmoid(gates[:, :cb])
    ig = _sigmoid(gates[:, cb:])
    nl = -lam_ref[...]
    sp = jnp.maximum(nl, 0.0) + jnp.log1p(jnp.exp(-jnp.abs(nl)))
    log_a = (-LRU_C) * r * sp
    a = jnp.exp(log_a)
    u = jnp.sqrt(1.0 - a * a) * (ig * xc)

    row = lax.broadcasted_iota(I32, (ts, cb), 0) & (SUBLANES - 1)
    av, bv = a, u
    for sft in (1, 2, 4):
        a_sh = pltpu.roll(av, sft, 0)
        b_sh = pltpu.roll(bv, sft, 0)
        m = row >= sft
        bv = jnp.where(m, av * b_sh + bv, bv)
        av = jnp.where(m, av * a_sh, av)
    a_s[...] = av
    b_s[...] = bv

    def carry(g, h):
        i0 = pl.multiple_of(g * SUBLANES, SUBLANES)
        h8 = b_s[pl.ds(i0, SUBLANES), :] + a_s[pl.ds(i0, SUBLANES), :] * h
        b_s[pl.ds(i0, SUBLANES), :] = h8
        return h8[SUBLANES - 1:SUBLANES, :]

    hc[0:1, :] = lax.fori_loop(0, ts // SUBLANES, carry, hc[0:1, :], unroll=8)
    gr = gr_ref[0].astype(F32)
    y_ref[0] = (b_s[...] * jax.nn.gelu(gr)).astype(y_ref.dtype)


def _block_diag(w, per):
    H, d, _ = w.shape
    w4 = w.reshape(H // per, per, d, d)
    out = jnp.einsum('gpij,pq->gpiqj', w4, jnp.eye(per, dtype=w.dtype))
    return out.reshape(H // per, per * d, per * d)


def _rnn(proj3, conv_w, conv_b, w_rg_a, b_rg_a, w_rg_i, b_rg_i, lam, *, ts=512, cb=256):
    B, S, _ = proj3.shape
    d_rnn = conv_w.shape[-1]
    nc = d_rnn // cb
    per = cb // (d_rnn // RNN_HEADS)
    wg = jnp.concatenate([_block_diag(w_rg_a, per), _block_diag(w_rg_i, per)], axis=-1).astype(BF16)
    bg = jnp.concatenate([b_rg_a.reshape(nc, 1, cb), b_rg_i.reshape(nc, 1, cb)], axis=-1)
    return pl.pallas_call(
        functools.partial(_rnn_body, ts=ts, cb=cb),
        grid=(B, nc, S // ts),
        in_specs=[pl.BlockSpec((1, ts, cb), lambda b, c, s: (b, s, c)),
                  pl.BlockSpec((1, ts, cb), lambda b, c, s: (b, s, nc + c)),
                  pl.BlockSpec((conv_w.shape[0], cb), lambda b, c, s: (0, c)),
                  pl.BlockSpec((1, cb), lambda b, c, s: (0, c)),
                  pl.BlockSpec((1, cb, 2 * cb), lambda b, c, s: (c, 0, 0)),
                  pl.BlockSpec((None, 1, 2 * cb), lambda b, c, s: (c, 0, 0)),
                  pl.BlockSpec((1, cb), lambda b, c, s: (0, c))],
        out_specs=pl.BlockSpec((1, ts, cb), lambda b, c, s: (b, s, c)),
        out_shape=jax.ShapeDtypeStruct((B, S, d_rnn), BF16),
        scratch_shapes=[pltpu.VMEM((ts + SUBLANES, cb), F32), pltpu.VMEM((ts, cb), F32),
                        pltpu.VMEM((ts, cb), F32), pltpu.VMEM((SUBLANES, cb), F32)],
        compiler_params=_cparams("parallel", "parallel", "arbitrary"),
        name="rnn",
    )(proj3, proj3, conv_w, conv_b.reshape(1, d_rnn), wg, bg, lam.reshape(1, d_rnn))


def _t5_bucket(dist):
    max_exact = NUM_BUCKETS // 2
    d = np.maximum(dist, 1).astype(np.float64)
    large = max_exact + (np.log(d / max_exact) / np.log(MAX_DISTANCE / max_exact)
                         * (NUM_BUCKETS - max_exact)).astype(np.int64)
    large = np.minimum(large, NUM_BUCKETS - 1)
    return np.where(dist < max_exact, dist, large).astype(np.int32)


def _attn_bias(table, window, dilation):
    blk = window // dilation
    qi = np.arange(blk)[:, None]
    ki = np.arange(2 * blk)[None, :]
    rel = qi + blk - ki
    in_window = (rel >= 0) & (rel <= blk)
    bucket = _t5_bucket(np.clip(rel, 0, None) * dilation)
    onehot = (bucket[..., None] == np.arange(NUM_BUCKETS)).astype(np.float32)
    bias = jnp.einsum('qkn,nh->hqk', onehot, table.astype(F32), precision=lax.Precision.HIGHEST)
    first = in_window & (ki >= blk)
    return jnp.stack([jnp.where(first[None], bias, MASK_VALUE),
                      jnp.where(in_window[None], bias, MASK_VALUE)])


def _attn_heads(q, kp, kc, vp, vc, bias_ref):
    blk, gw = q.shape
    lo = lax.broadcasted_iota(I32, (blk, LANES), 1) < HEAD_DIM
    scale = HEAD_DIM ** -0.5
    o_parts, l_parts = [], []
    for p in range(gw // LANES):
        sl = slice(p * LANES, (p + 1) * LANES)
        q2 = q[:, sl] * scale
        k2 = jnp.concatenate([kp[:, sl], kc[:, sl]], axis=0)
        v2 = jnp.concatenate([vp[:, sl], vc[:, sl]], axis=0)
        outs, lses = [], []
        for hh in range(2):
            qm = jnp.where(lo if hh == 0 else jnp.logical_not(lo), q2, jnp.zeros_like(q2))
            sc = lax.dot_general(qm, k2, (((1,), (1,)), ((), ())), preferred_element_type=F32)
            sc = sc + bias_ref[0, 2 * p + hh]
            m = jnp.max(sc, axis=-1, keepdims=True)
            e = jnp.exp(sc - m)
            l = jnp.sum(e, axis=-1, keepdims=True)
            o = jnp.dot(e.astype(BF16), v2, preferred_element_type=F32)
            outs.append(o / l)
            lses.append(jnp.broadcast_to(m + jnp.log(l), (blk, LANES)))
        o_parts.append(jnp.where(lo, outs[0], outs[1]))
        l_parts.append(jnp.where(lo, lses[0], lses[1]))
    return jnp.concatenate(o_parts, axis=1), jnp.concatenate(l_parts, axis=1)


def _attn_body(q_ref, kp_ref, kc_ref, vp_ref, vc_ref, bias_ref, o_ref, lse_ref):
    o, lse = _attn_heads(q_ref[0], kp_ref[0], kc_ref[0], vp_ref[0], vc_ref[0], bias_ref)
    o_ref[0] = o.astype(o_ref.dtype)
    lse_ref[0] = lse


def _attn_dil_body(cur_ref, prev_ref, bias_ref, o_ref, lse_ref, oscr, lscr, *, d, blk, gw):
    nt = cur_ref.shape[0]
    nq = gw // LANES

    def rows(ref, r, c):
        parts = [ref[j, r, :, c * gw:(c + 1) * gw] for j in range(nt)]
        return parts[0] if nt == 1 else jnp.concatenate(parts, axis=0)

    def residue(r, carry):
        o, lse = _attn_heads(rows(cur_ref, r, 2), rows(prev_ref, r, 0), rows(cur_ref, r, 0),
                             rows(prev_ref, r, 1), rows(cur_ref, r, 1), bias_ref)
        for q in range(nq):
            oscr[q, pl.ds(r, blk, stride=d), :] = o[:, q * LANES:(q + 1) * LANES]
            lscr[q, pl.ds(r, blk, stride=d), :] = lse[:, q * LANES:(q + 1) * LANES]
        return carry

    lax.fori_loop(0, d, residue, 0)
    for q in range(nq):
        o_ref[0, :, q * LANES:(q + 1) * LANES] = oscr[q].astype(o_ref.dtype)
        lse_ref[0, :, q * LANES:(q + 1) * LANES] = lscr[q]


def _attn_dilated(qkv_t, bias, B, *, blk, d, gw):
    n_tiles, _, rows_t, _ = qkv_t.shape
    tm = rows_t * d
    nt = blk * d // tm
    nb = n_tiles // (B * nt)
    S = n_tiles * tm // B
    return pl.pallas_call(
        functools.partial(_attn_dil_body, d=d, blk=blk, gw=gw),
        grid=(B, nb),
        in_specs=[pl.BlockSpec((nt, d, rows_t, 3 * gw), lambda b, n: (b * nb + n, 0, 0, 0)),
                  pl.BlockSpec((nt, d, rows_t, 2 * gw), lambda b, n: (b * nb + jnp.maximum(n - 1, 0), 0, 0, 0)),
                  pl.BlockSpec((1,) + bias.shape[1:], lambda b, n: (jnp.minimum(n, 1), 0, 0, 0))],
        out_specs=[pl.BlockSpec((1, blk * d, gw), lambda b, n: (b, n, 0)),
                   pl.BlockSpec((1, blk * d, gw), lambda b, n: (b, n, 0))],
        out_shape=[jax.ShapeDtypeStruct((B, S, gw), BF16), jax.ShapeDtypeStruct((B, S, gw), F32)],
        scratch_shapes=[pltpu.VMEM((gw // LANES, blk * d, LANES), F32),
                        pltpu.VMEM((gw // LANES, blk * d, LANES), F32)],
        compiler_params=_cparams("parallel", "parallel"),
        name=f"attn_d{d}",
    )(qkv_t, qkv_t, bias)


def _attn_group(qkv, bias, *, blk, d, gw, q_blk, k_blk, v_blk, row_blk):
    B, L, _ = qkv.shape
    nb = L // blk

    def cur(col):
        return pl.BlockSpec((1, blk, gw), lambda b, r, n: (b, n, r * row_blk + col))

    def prev(col):
        return pl.BlockSpec((1, blk, gw), lambda b, r, n: (b, jnp.maximum(n - 1, 0), r * row_blk + col))

    return pl.pallas_call(
        _attn_body,
        grid=(B, d, nb),
        in_specs=[cur(q_blk), prev(k_blk), cur(k_blk), prev(v_blk), cur(v_blk),
                  pl.BlockSpec((1,) + bias.shape[1:], lambda b, r, n: (jnp.minimum(n, 1), 0, 0, 0))],
        out_specs=[pl.BlockSpec((1, blk, gw), lambda b, r, n: (b, n, r)),
                   pl.BlockSpec((1, blk, gw), lambda b, r, n: (b, n, r))],
        out_shape=[jax.ShapeDtypeStruct((B, L, d * gw), BF16),
                   jax.ShapeDtypeStruct((B, L, d * gw), F32)],
        compiler_params=_cparams("parallel", "parallel", "parallel"),
        name=f"attn_d{d}",
    )(qkv, qkv, qkv, qkv, qkv, bias)


def _merge_body(yr_ref, o1_ref, o2_ref, o3_ref, l1_ref, l2_ref, l3_ref, g_ref, x_ref,
                wr_ref, wa_ref, wo_ref, lg_ref, lb_ref, wrh_ref, wrl_ref,
                x1_ref, x1p_ref, lt_ref, *, alpha):
    l1, l2, l3 = l1_ref[...], l2_ref[...], l3_ref[...]
    mx = jnp.maximum(jnp.maximum(l1, l2), l3)
    w1, w2, w3 = jnp.exp(l1 - mx), jnp.exp(l2 - mx), jnp.exp(l3 - mx)
    y_att = (w1 * o1_ref[...].astype(F32) + w2 * o2_ref[...].astype(F32)
             + w3 * o3_ref[...].astype(F32)) / (w1 + w2 + w3)
    pr = jnp.dot(yr_ref[...], wr_ref[...], preferred_element_type=F32)
    pa = jnp.dot(y_att.astype(BF16), wa_ref[...], preferred_element_type=F32)
    dm = pr.shape[1]
    g = g_ref[...].astype(F32)
    merged = _sigmoid(g[:, :dm]) * pr + _sigmoid(g[:, dm:]) * pa
    mix = jnp.dot(merged.astype(BF16), wo_ref[...], preferred_element_type=F32)
    z = alpha * x_ref[...] + mix
    mu = jnp.mean(z, axis=-1, keepdims=True)
    zc = z - mu
    var = jnp.mean(zc * zc, axis=-1, keepdims=True)
    x1 = zc * lax.rsqrt(var + LN_EPS) * lg_ref[...] + lb_ref[...]
    x1_ref[...] = x1
    x1p_ref[...] = _pack_rows(x1)
    nt = (((1,), (1,)), ((), ()))
    x1h = x1.astype(BF16)
    x1l = (x1 - x1h.astype(F32)).astype(BF16)
    lt_ref[...] = (lax.dot_general(wrh_ref[...], x1h, nt, preferred_element_type=F32)
                   + (lax.dot_general(wrh_ref[...], x1l, nt, preferred_element_type=F32)
                      + lax.dot_general(wrl_ref[...], x1h, nt, preferred_element_type=F32)))


def _merge(y_rnn, os_, lses, proj, gate_blk, x2, wr, wa, wo, ln_g, ln_b, w_router_t, *, alpha, tm=512):
    T, D = x2.shape
    da = os_[0].shape[1]
    E = w_router_t.shape[0]
    wrh = w_router_t.astype(BF16)
    wrl = (w_router_t - wrh.astype(F32)).astype(BF16)
    row = lambda w: pl.BlockSpec((tm, w), lambda i: (i, 0))
    full = lambda a: pl.BlockSpec(a.shape, lambda i: (0,) * a.ndim)
    return pl.pallas_call(
        functools.partial(_merge_body, alpha=alpha),
        grid=(T // tm,),
        in_specs=[row(D), row(da), row(da), row(da), row(da), row(da), row(da),
                  pl.BlockSpec((tm, 2 * D), lambda i: (i, gate_blk)), row(D),
                  full(wr), full(wa), full(wo), full(ln_g), full(ln_b), full(wrh), full(wrl)],
        out_specs=[row(D), row(D // 2), pl.BlockSpec((E, tm), lambda i: (0, i))],
        out_shape=[jax.ShapeDtypeStruct((T, D), F32), jax.ShapeDtypeStruct((T, D // 2), U32),
                   jax.ShapeDtypeStruct((E, T), F32)],
        compiler_params=_cparams("parallel"),
        name="merge",
    )(y_rnn, *os_, *lses, proj, x2, wr, wa, wo, ln_g, ln_b, wrh, wrl)


def _first_max(vals, idx, big):
    m = jnp.max(vals, axis=0, keepdims=True)
    i = jnp.min(jnp.where(vals == m, idx, big), axis=0, keepdims=True)
    return m, i


def _route_body(lt_ref, rb_ref, e_ref, g_ref, r_ref, cnt_ref, *, tl):
    E = lt_ref.shape[0]
    per = E // N_EXPERT_GROUPS
    neg = -jnp.inf

    @pl.when(pl.program_id(0) == 0)
    def _():
        cnt_ref[...] = jnp.zeros_like(cnt_ref)

    scores = jax.nn.sigmoid(lt_ref[...])
    sel = scores + rb_ref[...]
    rowi = lax.broadcasted_iota(I32, (E, tl), 0)

    gi = lax.broadcasted_iota(I32, (N_EXPERT_GROUPS, tl), 0)
    gsc = jnp.zeros((N_EXPERT_GROUPS, tl), F32)
    for g in range(N_EXPERT_GROUPS):
        blk = sel[g * per:(g + 1) * per]
        ri = lax.broadcasted_iota(I32, (per, tl), 0) + g * per
        m1, i1 = _first_max(blk, ri, E)
        m2 = jnp.max(jnp.where(ri == i1, neg, blk), axis=0, keepdims=True)
        gsc = jnp.where(gi == g, m1 + m2, gsc)
    keep = jnp.zeros((N_EXPERT_GROUPS, tl), F32)
    for _ in range(TOPK_GROUPS):
        _, ig = _first_max(gsc, gi, N_EXPERT_GROUPS)
        hit = gi == ig
        keep = jnp.where(hit, 1.0, keep)
        gsc = jnp.where(hit, neg, gsc)
    cur = jnp.concatenate(
        [jnp.where(keep[g:g + 1] > 0.5, sel[g * per:(g + 1) * per], neg) for g in range(N_EXPERT_GROUPS)],
        axis=0)

    ti = lax.broadcasted_iota(I32, (tl, tl), 0)
    tj = lax.broadcasted_iota(I32, (tl, tl), 1)
    earlier = jnp.where(ti < tj, 1.0, 0.0).astype(BF16)

    es, gv = [], []
    onehot = jnp.zeros((E, tl), F32)
    for k in range(TOP_K):
        _, ie = _first_max(cur, rowi, E)
        hit = rowi == ie
        es.append(ie)
        gv.append(jnp.sum(jnp.where(hit, scores, 0.0), axis=0, keepdims=True))
        onehot = jnp.where(hit, 1.0, onehot)
        cur = jnp.where(hit, neg, cur)
    gsum = gv[0]
    for k in range(1, TOP_K):
        gsum = gsum + gv[k]
    ranks = jnp.dot(onehot.astype(BF16), earlier, preferred_element_type=F32) + cnt_ref[...]
    for k in range(TOP_K):
        e_ref[k:k + 1, :] = es[k]
        g_ref[k:k + 1, :] = gv[k] / gsum * ROUTED_SCALE
        r_ref[k:k + 1, :] = jnp.sum(jnp.where(rowi == es[k], ranks, 0.0), axis=0, keepdims=True).astype(I32)
    cnt_ref[...] += jnp.sum(onehot, axis=1, keepdims=True)


def _route(logits_t, router_bias, *, tl=256):
    E, T = logits_t.shape
    kt = pl.BlockSpec((TOP_K, tl), lambda i: (0, i))
    return pl.pallas_call(
        functools.partial(_route_body, tl=tl),
        grid=(T // tl,),
        in_specs=[pl.BlockSpec((E, tl), lambda i: (0, i)), pl.BlockSpec((E, 1), lambda i: (0, 0))],
        out_specs=[kt, kt, kt, pl.BlockSpec((E, 1), lambda i: (0, 0))],
        out_shape=[jax.ShapeDtypeStruct((TOP_K, T), I32), jax.ShapeDtypeStruct((TOP_K, T), F32),
                   jax.ShapeDtypeStruct((TOP_K, T), I32), jax.ShapeDtypeStruct((E, 1), F32)],
        compiler_params=_cparams("arbitrary"),
        name="route",
    )(logits_t, router_bias.reshape(E, 1))


def _dest_body(e_ref, r_ref, ss_ref, d_ref):
    E = ss_ref.shape[0]
    tl = e_ref.shape[1]
    rowi = lax.broadcasted_iota(I32, (E, tl), 0)
    e = e_ref[...]
    rows = [jnp.sum(jnp.where(rowi == e[k:k + 1], ss_ref[...], 0), axis=0, keepdims=True)
            for k in range(e.shape[0])]
    d_ref[...] = jnp.concatenate(rows, axis=0) + r_ref[...]


def _dest(e_idx, rank, seg_start, *, tl=512):
    K, T = e_idx.shape
    E = seg_start.shape[0]
    kt = pl.BlockSpec((K, tl), lambda i: (0, i))
    return pl.pallas_call(
        _dest_body, grid=(T // tl,),
        in_specs=[kt, kt, pl.BlockSpec((E, 1), lambda i: (0, 0))],
        out_specs=kt, out_shape=jax.ShapeDtypeStruct((K, T), I32),
        compiler_params=_cparams("parallel"), name="dest",
    )(e_idx, rank, seg_start.reshape(E, 1))


def _pack_rows(v):
    w = v.shape[1] // 2
    lo = pltpu.bitcast(v[:, :w].astype(BF16).astype(F32), U32) >> 16
    hi = pltpu.bitcast(v[:, w:].astype(BF16).astype(F32), U32) & jnp.uint32(0xFFFF0000)
    return lo | hi


def _unpack_rows(p):
    lo = pltpu.bitcast(p << 16, F32)
    hi = pltpu.bitcast(p & jnp.uint32(0xFFFF0000), F32)
    return lo, hi


def _sc_scatter_rows(x, idx3, n_rows):
    from jax.experimental.pallas import tpu_sc as plsc
    n_chunks, K, CH = idx3.shape
    W = x.shape[1]
    info = plsc.get_sparse_core_info()
    NC, n_workers = info.num_cores, info.num_cores * info.num_subcores
    per_w = n_chunks // n_workers
    assert per_w * n_workers == n_chunks and per_w % 2 == 0

    def body(x_hbm, idx_hbm, out_hbm, idx_v, rows_v, sem_l, sem_s):
        c0 = (lax.axis_index("s") * NC + lax.axis_index("c")) * per_w

        def load(cc, b):
            return pltpu.make_async_copy(x_hbm.at[pl.ds(pl.multiple_of((c0 + cc) * CH, CH), CH)], rows_v.at[b],
                                         sem_l.at[b])

        def scatter(b, k):
            return pltpu.make_async_copy(rows_v.at[b], out_hbm.at[idx_v.at[b, k]], sem_s.at[b])

        pltpu.sync_copy(idx_hbm.at[c0], idx_v.at[0])
        load(0, 0).start()

        @pl.loop(0, per_w, step=2)
        def _(c):
            for b in range(2):
                cc = c + b
                load(cc, b).wait()
                for k in range(K):
                    scatter(b, k).start()

                @pl.when(cc >= 1)
                def _():
                    for k in range(K):
                        scatter(1 - b, k).wait()

                @pl.when(cc + 1 < per_w)
                def _():
                    pltpu.sync_copy(idx_hbm.at[c0 + cc + 1], idx_v.at[1 - b])
                    load(cc + 1, 1 - b).start()

        for k in range(K):
            scatter(1, k).wait()

    return pl.kernel(
        body, mesh=plsc.VectorSubcoreMesh(core_axis_name="c", subcore_axis_name="s"),
        out_type=jax.ShapeDtypeStruct((n_rows, W), x.dtype),
        scratch_types=[pltpu.VMEM((2, K, CH), I32), pltpu.VMEM((2, CH, W), x.dtype),
                       pltpu.SemaphoreType.DMA((2,)), pltpu.SemaphoreType.DMA((2,))],
    )(x, idx3)


def _pad_fill_body(ps_ref, pc_ref, nu_ref, xs_in, xs_hbm, zrow, sem_z, sem_c, sem_b, sem_t):
    del xs_in
    i = pl.program_id(0)
    n = pl.num_programs(0)
    R = zrow.shape[0]

    def pad_copy(row):
        return pltpu.make_async_copy(zrow.at[pl.ds(0, 1), :], xs_hbm.at[pl.ds(row, 1), :], sem_z)

    zrow[...] = jnp.zeros_like(zrow)

    n_blk = xs_hbm.shape[0] // R
    tail_per_step = -(-n_blk // n)

    def tail(j, c, wait):
        blk = i * tail_per_step + j

        @pl.when(jnp.logical_and(blk >= nu_ref[0], blk < n_blk))
        def _():
            cp = pltpu.make_async_copy(zrow, xs_hbm.at[pl.ds(pl.multiple_of(blk * R, R), R), :], sem_t)
            cp.wait() if wait else cp.start()

        return c

    lax.fori_loop(0, tail_per_step, functools.partial(tail, wait=False), 0)

    E = ps_ref.shape[0]
    per_step = -(-E // n)

    def pads(j, c, wait):
        e = jnp.minimum(i * per_step + j, E - 1)
        cnt = jnp.where(i * per_step + j < E, pc_ref[e], 0)
        start = ps_ref[e]
        n_single = jnp.minimum(cnt, (-start) & (SUBLANES - 1))
        start8 = start + n_single

        def one(r, c2):
            cp = pad_copy(start + r)
            cp.wait() if wait else cp.start()
            return c2

        rem8 = lax.shift_right_logical(cnt - n_single, 3)
        n_eight = jnp.minimum(rem8, lax.shift_right_logical(-start8, 3) & (SUBLANES - 1))
        start64 = start8 + n_eight * SUBLANES
        big = SUBLANES * SUBLANES

        def eight(r, c2):
            row = pl.multiple_of(start8 + r * SUBLANES, SUBLANES)
            cp = pltpu.make_async_copy(zrow.at[pl.ds(0, SUBLANES), :], xs_hbm.at[pl.ds(row, SUBLANES), :], sem_c)
            cp.wait() if wait else cp.start()
            return c2

        def sixty_four(r, c2):
            row = pl.multiple_of(start64 + r * big, big)
            cp = pltpu.make_async_copy(zrow.at[pl.ds(0, big), :], xs_hbm.at[pl.ds(row, big), :], sem_b)
            cp.wait() if wait else cp.start()
            return c2

        c = lax.fori_loop(0, n_single, one, c)
        c = lax.fori_loop(0, n_eight, eight, c)
        return lax.fori_loop(0, lax.shift_right_logical(rem8 - n_eight, 3), sixty_four, c)

    lax.fori_loop(0, per_step, functools.partial(pads, wait=False), 0)
    lax.fori_loop(0, per_step, functools.partial(pads, wait=True), 0)
    lax.fori_loop(0, tail_per_step, functools.partial(tail, wait=True), 0)


def _pad_fill(xs, pad_start, pad_cnt, n_used, *, steps=16):
    any_spec = pl.BlockSpec(memory_space=pl.ANY)
    return pl.pallas_call(
        _pad_fill_body,
        grid_spec=pltpu.PrefetchScalarGridSpec(
            num_scalar_prefetch=3, grid=(steps,),
            in_specs=[any_spec], out_specs=any_spec,
            scratch_shapes=[pltpu.VMEM((EXPERT_ROWS, xs.shape[1]), xs.dtype),
                            pltpu.SemaphoreType.DMA, pltpu.SemaphoreType.DMA, pltpu.SemaphoreType.DMA,
                            pltpu.SemaphoreType.DMA]),
        out_shape=jax.ShapeDtypeStruct(xs.shape, xs.dtype),
        input_output_aliases={3: 0},
        compiler_params=_cparams("arbitrary"),
        name="pad_fill",
    )(pad_start, pad_cnt, n_used, xs)


def _experts_body(fb_ref, xs_hbm, wg_ref, wu_ref, wd_ref, ys_hbm, xbuf, ybuf, wgu_s, wd_s, sem_x, sem_y):
    e = pl.program_id(0)
    n_e = pl.num_programs(0)
    de = wg_ref.shape[2]
    G = EXPERT_GROUP
    sh = G.bit_length() - 1
    R = xbuf.shape[1] // G
    n_blk = xs_hbm.shape[0] // R
    n_used = fb_ref[n_e]
    n_pairs = lax.shift_right_logical(n_used + (G - 1), sh)

    def pair_rows(ref, p):
        return ref.at[pl.ds(pl.multiple_of(p * (G * R), G * R), G * R), :]

    def x_copy(p, slot):
        return pltpu.make_async_copy(pair_rows(xs_hbm, p), xbuf.at[slot], sem_x.at[slot])

    def y_copy(p, slot):
        return pltpu.make_async_copy(ybuf.at[slot], pair_rows(ys_hbm, p), sem_y.at[slot])

    @pl.when(jnp.logical_and(e == 0, n_used > 0))
    def _():
        x_copy(0, 0).start(priority=1)

    wgu_s[:, :de] = wg_ref[0].astype(BF16)
    wgu_s[:, de:] = wu_ref[0].astype(BF16)
    wd_s[...] = wd_ref[0].astype(BF16)

    def block(g, c):
        p = lax.shift_right_logical(g, sh)
        half = g & (G - 1)
        slot = p & 1

        @pl.when(half == 0)
        def _():
            x_copy(p, slot).wait()

            @pl.when(p + 1 < n_pairs)
            def _():
                x_copy(p + 1, 1 - slot).start(priority=1)

            @pl.when(p >= 2)
            def _():
                y_copy(p - 2, slot).wait()

        r0 = pl.multiple_of(half * R, R)
        lo, hi = _unpack_rows(xbuf[slot, pl.ds(r0, R), :])
        xb = jnp.concatenate([lo.astype(BF16), hi.astype(BF16)], axis=1)
        h = jnp.dot(xb, wgu_s[...], preferred_element_type=F32)
        hg = h[:, :de]
        act = (hg * _sigmoid(hg) * h[:, de:]).astype(BF16)
        ybuf[slot, pl.ds(r0, R), :] = _pack_rows(jnp.dot(act, wd_s[...], preferred_element_type=F32))

        @pl.when(half == G - 1)
        def _():
            y_copy(p, slot).start(priority=1)

        return c

    lax.fori_loop(fb_ref[e], fb_ref[e + 1], block, 0)

    @pl.when(e == n_e - 1)
    def _():
        @pl.when((n_used & (G - 1)) != 0)
        def _():
            p = lax.shift_right_logical(n_used, sh)
            y_copy(p, p & 1).start(priority=1)

        for back in (2, 1):
            p = n_pairs - back

            @pl.when(p >= 0)
            def _():
                y_copy(p, p & 1).wait()

        ybuf[0] = jnp.zeros(ybuf.shape[1:], ybuf.dtype)

        def tail(g, c, wait):
            cp = pltpu.make_async_copy(ybuf.at[0, pl.ds(0, R), :],
                                       ys_hbm.at[pl.ds(pl.multiple_of(g * R, R), R), :], sem_y.at[0])
            cp.wait() if wait else cp.start()
            return c

        lax.fori_loop(n_used, n_blk, functools.partial(tail, wait=False), 0)
        lax.fori_loop(n_used, n_blk, functools.partial(tail, wait=True), 0)


def _experts(xs, first_blk, wg, wu, wd):
    n_rows, W = xs.shape
    R = EXPERT_ROWS
    E, D, de = wg.shape
    G = EXPERT_GROUP
    assert (n_rows // R) % G == 0 and G & (G - 1) == 0
    any_spec = pl.BlockSpec(memory_space=pl.ANY)
    wmap = lambda e, fb: (e, 0, 0)
    return pl.pallas_call(
        _experts_body,
        grid_spec=pltpu.PrefetchScalarGridSpec(
            num_scalar_prefetch=1, grid=(E,),
            in_specs=[any_spec, pl.BlockSpec((1, D, de), wmap), pl.BlockSpec((1, D, de), wmap),
                      pl.BlockSpec((1, de, D), wmap)],
            out_specs=any_spec,
            scratch_shapes=[pltpu.VMEM((2, G * R, W), U32), pltpu.VMEM((2, G * R, W), U32),
                            pltpu.VMEM((D, 2 * de), BF16), pltpu.VMEM((de, D), BF16),
                            pltpu.SemaphoreType.DMA((2,)), pltpu.SemaphoreType.DMA((2,))]),
        out_shape=jax.ShapeDtypeStruct((n_rows, W), U32),
        compiler_params=_cparams("arbitrary"),
        name="experts",
    )(first_blk, xs, wg, wu, wd)


def _sc_gather_rows(table, idx2):
    from jax.experimental.pallas import tpu_sc as plsc
    n_chunks, CH = idx2.shape
    W = table.shape[1]
    info = plsc.get_sparse_core_info()
    NC, n_workers = info.num_cores, info.num_cores * info.num_subcores
    per_w = n_chunks // n_workers
    assert per_w * n_workers == n_chunks and per_w % 2 == 0

    def body(table_hbm, idx_hbm, out_hbm, idx_v, rows_v, sem):
        c0 = (lax.axis_index("s") * NC + lax.axis_index("c")) * per_w

        def gather(b):
            return pltpu.make_async_copy(table_hbm.at[idx_v.at[b]], rows_v.at[b], sem.at[b])

        pltpu.sync_copy(idx_hbm.at[c0], idx_v.at[0])
        gather(0).start()

        @pl.loop(0, per_w, step=2)
        def _(c):
            for b in range(2):
                cc = c + b

                @pl.when(cc + 1 < per_w)
                def _():
                    pltpu.sync_copy(idx_hbm.at[c0 + cc + 1], idx_v.at[1 - b])
                    gather(1 - b).start()

                gather(b).wait()
                pltpu.sync_copy(rows_v.at[b], out_hbm.at[pl.ds(pl.multiple_of((c0 + cc) * CH, CH), CH)])

    return pl.kernel(
        body, mesh=plsc.VectorSubcoreMesh(core_axis_name="c", subcore_axis_name="s"),
        out_type=jax.ShapeDtypeStruct((n_chunks * CH, W), table.dtype),
        scratch_types=[pltpu.VMEM((2, CH), I32), pltpu.VMEM((2, CH, W), table.dtype),
                       pltpu.SemaphoreType.DMA((2,))],
    )(table, idx2)


def _combine_body(yg_ref, gate_ref, x1_ref, wgu_ref, wd_ref, lg_ref, lb_ref, *rest, alpha):
    o_ref = rest[-1]
    x1 = x1_ref[...]
    ds_ = wd_ref.shape[0]
    h = jnp.dot(x1.astype(BF16), wgu_ref[...], preferred_element_type=F32)
    hg = h[:, :ds_]
    act = (hg * _sigmoid(hg) * h[:, ds_:]).astype(BF16)
    shared = jnp.dot(act, wd_ref[...], preferred_element_type=F32)

    g = gate_ref[...]
    lo_acc = hi_acc = None
    for k in range(yg_ref.shape[0]):
        lo, hi = _unpack_rows(yg_ref[k])
        gk = g[:, k:k + 1]
        lo_acc = gk * lo if k == 0 else lo_acc + gk * lo
        hi_acc = gk * hi if k == 0 else hi_acc + gk * hi
    routed = jnp.concatenate([lo_acc, hi_acc], axis=1)

    z = alpha * x1 + (routed + shared)
    mu = jnp.mean(z, axis=-1, keepdims=True)
    zc = z - mu
    var = jnp.mean(zc * zc, axis=-1, keepdims=True)
    o_ref[...] = zc * lax.rsqrt(var + LN_EPS) * lg_ref[...] + lb_ref[...]


def _combine(yg, gate_tk, x1, wgu, wd, ln_g, ln_b, out_prev, part, *, alpha, tc):
    T, D = x1.shape
    K, Tp, W = yg.shape
    off = part * (Tp // tc)
    full = lambda a: pl.BlockSpec(a.shape, lambda i: (0,) * a.ndim)
    args = [yg, gate_tk, x1, wgu, wd, ln_g, ln_b]
    in_specs = [pl.BlockSpec((K, tc, W), lambda i: (0, i, 0)), pl.BlockSpec((tc, K), lambda i: (off + i, 0)),
                pl.BlockSpec((tc, D), lambda i: (off + i, 0)), full(wgu), full(wd), full(ln_g), full(ln_b)]
    aliases = {}
    if out_prev is not None:
        args.append(out_prev)
        in_specs.append(pl.BlockSpec(memory_space=pl.ANY))
        aliases = {len(args) - 1: 0}
    return pl.pallas_call(
        functools.partial(_combine_body, alpha=alpha),
        grid=(Tp // tc,),
        in_specs=in_specs,
        out_specs=pl.BlockSpec((tc, D), lambda i: (off + i, 0)),
        out_shape=jax.ShapeDtypeStruct((T, D), F32),
        input_output_aliases=aliases,
        compiler_params=_cparams("parallel"),
        name="combine",
    )(*args)


def _mixer_ln1(x, w_in, b_in, conv_w, conv_b, w_rg_a, b_rg_a, w_rg_i, b_rg_i, lru_lambda,
               w_proj_rnn, w_proj_att, rel_bias, w_out, ln1_g, ln1_b, w_router, alpha):
    B, S, D = x.shape
    T = B * S
    d_rnn = conv_w.shape[-1]
    gw = HEADS_PER_GROUP * HEAD_DIM
    d_att = gw * len(DILATED_GROUPS)
    a0 = 2 * d_rnn
    a1 = a0 + 3 * d_att
    head = lambda j, g: slice(a0 + j * d_att + g * gw, a0 + j * d_att + (g + 1) * gw)
    plain = [g for g, (_, d) in enumerate(DILATED_GROUPS) if d == 1]
    dilated = [g for g, (_, d) in enumerate(DILATED_GROUPS) if d > 1]
    order = ([slice(0, a0), slice(a1, None)] + [head(j, g) for g in plain for j in range(3)]
             + [head(j, g) for g in dilated for j in (1, 2, 0)])
    perm = lambda w: jnp.concatenate([w[..., s] for s in order], axis=-1)
    w_p = perm(w_in).astype(BF16)
    b_p = perm(b_in).reshape(1, -1)
    qkv0 = a0 + 2 * D
    n_main = qkv0 + 3 * gw * len(plain)
    x2 = x.reshape(T, D)

    proj, *qkv_dil = _in_proj(x2, w_p, b_p, n_main, tuple(DILATED_GROUPS[g][1] for g in dilated))
    proj3 = proj.reshape(B, S, n_main)
    y_rnn = _rnn(proj3, conv_w, conv_b, w_rg_a, b_rg_a, w_rg_i, b_rg_i, lru_lambda)

    os_, lses = [], []
    for g, (window, d) in enumerate(DILATED_GROUPS):
        blk = window // d
        bias = _attn_bias(rel_bias[:, g * HEADS_PER_GROUP:(g + 1) * HEADS_PER_GROUP], window, d)
        if d == 1:
            c0 = (qkv0 + 3 * gw * plain.index(g)) // gw
            o, lse = _attn_group(proj3, bias, blk=blk, d=1, gw=gw, q_blk=c0, k_blk=c0 + 1, v_blk=c0 + 2,
                                 row_blk=n_main // gw)
        else:
            o, lse = _attn_dilated(qkv_dil[dilated.index(g)], bias, B, blk=blk, d=d, gw=gw)
        os_.append(o.reshape(T, gw))
        lses.append(lse.reshape(T, gw))

    return _merge(y_rnn.reshape(T, d_rnn), os_, lses, proj, a0 // (2 * D), x2,
                  w_proj_rnn.astype(BF16), w_proj_att.astype(BF16), w_out.astype(BF16),
                  ln1_g.reshape(1, D), ln1_b.reshape(1, D), w_router.T, alpha=alpha)


def _moe_ln2(x1, x1p, logits_t, router_bias, w_exp_gate, w_exp_up, w_exp_down,
             w_sh_gate, w_sh_up, w_sh_down, ln2_g, ln2_b, alpha):
    T, D = x1.shape
    E = logits_t.shape[0]
    R = EXPERT_ROWS
    n_blk = T * TOP_K // R + E
    n_rows = n_blk * R

    e_idx, gate, rank, counts = _route(logits_t, router_bias)
    counts = counts.reshape(E).astype(I32)
    padded = (counts + R - 1) // R * R
    pad_end = jnp.cumsum(padded)
    seg_start = pad_end - padded
    n_used = (pad_end[-1:] // R).astype(I32)
    first_blk = jnp.concatenate([jnp.zeros((1,), I32), (pad_end // R).astype(I32)])
    dest = _dest(e_idx, rank, seg_start)
    tl = ROW_TILE
    ch = SC_GATHER_CHUNK
    dest_c = dest.reshape(TOP_K, T // ch, ch).transpose(1, 0, 2)

    xs = _sc_scatter_rows(x1p, dest_c, n_rows)
    xs = _pad_fill(xs, seg_start + counts, padded - counts, n_used)
    ys = _experts(xs, first_blk, w_exp_gate, w_exp_up, w_exp_down)
    wgu = jnp.concatenate([w_sh_gate, w_sh_up], axis=-1).astype(BF16)
    gate_tk = gate.T
    wd_sh = w_sh_down.astype(BF16)
    tp = T // COMBINE_PARTS
    out = None
    for p in range(COMBINE_PARTS):
        idx = dest[:, p * tp:(p + 1) * tp].reshape(-1, ch)
        yg = _sc_gather_rows(ys, idx).reshape(TOP_K, tp, ys.shape[1])
        out = _combine(yg, gate_tk, x1, wgu, wd_sh, ln2_g.reshape(1, D), ln2_b.reshape(1, D), out, p,
                       alpha=alpha, tc=tl)
    return out


def kernel(x, w_in, b_in, conv_w, conv_b, w_rg_a, b_rg_a, w_rg_i, b_rg_i, lru_lambda,
           w_proj_rnn, w_proj_att, rel_bias, w_out, ln1_g, ln1_b, w_router, router_bias,
           w_exp_gate, w_exp_up, w_exp_down, w_sh_gate, w_sh_up, w_sh_down, ln2_g, ln2_b):
    B, S, D = x.shape
    depth = w_in.shape[0]
    alpha = (2 * depth) ** 0.25
    for i in range(depth):
        x1, x1p, logits_t = _mixer_ln1(
            x, w_in[i], b_in[i], conv_w[i], conv_b[i], w_rg_a[i], b_rg_a[i], w_rg_i[i], b_rg_i[i],
            lru_lambda[i], w_proj_rnn[i], w_proj_att[i], rel_bias, w_out[i], ln1_g[i], ln1_b[i],
            w_router[i], alpha)
        out = _moe_ln2(x1, x1p, logits_t, router_bias[i], w_exp_gate[i], w_exp_up[i], w_exp_down[i],
                       w_sh_gate[i], w_sh_up[i], w_sh_down[i], ln2_g[i], ln2_b[i], alpha)
        x = out.reshape(B, S, D)
    return x
```

```python
import functools

import numpy as np
import jax
import jax.numpy as jnp
from jax import lax
from jax.experimental import pallas as pl
from jax.experimental.pallas import tpu as pltpu

F32 = jnp.float32
BF16 = jnp.bfloat16
I32 = jnp.int32
U32 = jnp.uint32

RNN_HEADS = 16
LRU_C = 8.0
HEAD_DIM = 64
HEADS_PER_GROUP = 8
DILATED_GROUPS = ((128, 1), (512, 4), (2048, 16))
NUM_BUCKETS = 32
MAX_DISTANCE = 2048
TOP_K = 8
N_EXPERT_GROUPS = 8
TOPK_GROUPS = 4
ROUTED_SCALE = 2.5
LN_EPS = 1e-5

LANES = 128
SUBLANES = 8
VMEM_LIMIT_BYTES = 56 * 1024 * 1024

MASK_VALUE = -1e30
EXPERT_ROWS = 512
EXPERT_GROUP = 4
ROW_TILE = 256
SC_GATHER_CHUNK = 64
COMBINE_PARTS = 4


def _cparams(*sem):
    return pltpu.CompilerParams(dimension_semantics=sem, vmem_limit_bytes=VMEM_LIMIT_BYTES)


def _sigmoid(v):
    return 0.5 * (jnp.tanh(0.5 * v) + 1.0)


def _in_proj_body(x_ref, w_ref, b_ref, main_ref, *rest, tn, dils):
    dil_refs, scr = rest[:-1], rest[-1]
    xb = x_ref[...].astype(BF16)
    tm = xb.shape[0]

    def chunk(j):
        sl = slice(j * tn, (j + 1) * tn)
        return jnp.dot(xb, w_ref[:, sl], preferred_element_type=F32) + b_ref[:, sl]

    n_main = main_ref.shape[1] // tn
    for j in range(n_main):
        main_ref[:, j * tn:(j + 1) * tn] = chunk(j).astype(main_ref.dtype)
    j = n_main
    for ref, d in zip(dil_refs, dils):
        for c in range(ref.shape[3] // tn):
            acc = chunk(j)
            j += 1
            for q in range(tn // LANES):
                scr[q] = acc[:, q * LANES:(q + 1) * LANES]
            for r in range(d):
                part = jnp.concatenate([scr[q, pl.ds(r, tm // d, stride=d), :] for q in range(tn // LANES)], axis=1)
                ref[0, r, :, c * tn:(c + 1) * tn] = part.astype(ref.dtype)


def _in_proj(x2, w, b, n_main, dils, *, tm=512, tn=512):
    T, D = x2.shape
    N = w.shape[1]
    wd = (N - n_main) // len(dils)
    out_specs = [pl.BlockSpec((tm, n_main), lambda i: (i, 0))]
    out_shape = [jax.ShapeDtypeStruct((T, n_main), BF16)]
    for d in dils:
        out_specs.append(pl.BlockSpec((1, d, tm // d, wd), lambda i: (i, 0, 0, 0)))
        out_shape.append(jax.ShapeDtypeStruct((T // tm, d, tm // d, wd), BF16))
    return pl.pallas_call(
        functools.partial(_in_proj_body, tn=tn, dils=dils),
        grid=(T // tm,),
        in_specs=[pl.BlockSpec((tm, D), lambda i: (i, 0)),
                  pl.BlockSpec((D, N), lambda i: (0, 0), pipeline_mode=pl.Buffered(1)),
                  pl.BlockSpec((1, N), lambda i: (0, 0))],
        out_specs=out_specs,
        out_shape=out_shape,
        scratch_shapes=[pltpu.VMEM((tn // LANES, tm, LANES), F32)],
        compiler_params=_cparams("parallel"),
        name="in_proj",
    )(x2, w, b)


def _rnn_body(xr_ref, gr_ref, cw_ref, cb_ref, wg_ref, bg_ref, lam_ref, y_ref,
              xext, a_s, b_s, hc, *, ts, cb):
    s = pl.program_id(2)

    @pl.when(s == 0)
    def _():
        xext[0:SUBLANES, :] = jnp.zeros((SUBLANES, cb), F32)
        hc[...] = jnp.zeros_like(hc)

    xr = xr_ref[0].astype(F32)
    xext[SUBLANES:SUBLANES + ts, :] = xr
    nw = cw_ref.shape[0]
    xc = cw_ref[nw - 1:nw, :] * xr + cb_ref[...]
    for j in range(nw - 1):
        off = SUBLANES - (nw - 1 - j)
        xc = xc + cw_ref[j:j + 1, :] * xext[off:off + ts, :]
    xext[0:SUBLANES, :] = xext[ts:ts + SUBLANES, :]

    gates = jnp.dot(xc.astype(BF16), wg_ref[0], preferred_element_type=F32) + bg_ref[...]
    r = _sigmoid(gates[:, :cb])
    ig = _sigmoid(gates[:, cb:])
    nl = -lam_ref[...]
    sp = jnp.maximum(nl, 0.0) + jnp.log1p(jnp.exp(-jnp.abs(nl)))
    log_a = (-LRU_C) * r * sp
    a = jnp.exp(log_a)
    u = jnp.sqrt(1.0 - a * a) * (ig * xc)

    row = lax.broadcasted_iota(I32, (ts, cb), 0) & (SUBLANES - 1)
    av, bv = a, u
    for sft in (1, 2, 4):
        a_sh = pltpu.roll(av, sft, 0)
        b_sh = pltpu.roll(bv, sft, 0)
        m = row >= sft
        bv = jnp.where(m, av * b_sh + bv, bv)
        av = jnp.where(m, av * a_sh, av)
    a_s[...] = av
    b_s[...] = bv

    def carry(g, h):
        i0 = pl.multiple_of(g * SUBLANES, SUBLANES)
        h8 = b_s[pl.ds(i0, SUBLANES), :] + a_s[pl.ds(i0, SUBLANES), :] * h
        b_s[pl.ds(i0, SUBLANES), :] = h8
        return h8[SUBLANES - 1:SUBLANES, :]

    hc[0:1, :] = lax.fori_loop(0, ts // SUBLANES, carry, hc[0:1, :], unroll=8)
    gr = gr_ref[0].astype(F32)
    y_ref[0] = (b_s[...] * jax.nn.gelu(gr)).astype(y_ref.dtype)


def _block_diag(w, per):
    H, d, _ = w.shape
    w4 = w.reshape(H // per, per, d, d)
    out = jnp.einsum('gpij,pq->gpiqj', w4, jnp.eye(per, dtype=w.dtype))
    return out.reshape(H // per, per * d, per * d)


def _rnn(proj3, conv_w, conv_b, w_rg_a, b_rg_a, w_rg_i, b_rg_i, lam, *, ts=512, cb=512):
    B, S, _ = proj3.shape
    d_rnn = conv_w.shape[-1]
    nc = d_rnn // cb
    per = cb // (d_rnn // RNN_HEADS)
    wg = jnp.concatenate([_block_diag(w_rg_a, per), _block_diag(w_rg_i, per)], axis=-1).astype(BF16)
    bg = jnp.concatenate([b_rg_a.reshape(nc, 1, cb), b_rg_i.reshape(nc, 1, cb)], axis=-1)
    return pl.pallas_call(
        functools.partial(_rnn_body, ts=ts, cb=cb),
        grid=(B, nc, S // ts),
        in_specs=[pl.BlockSpec((1, ts, cb), lambda b, c, s: (b, s, c)),
                  pl.BlockSpec((1, ts, cb), lambda b, c, s: (b, s, nc + c)),
                  pl.BlockSpec((conv_w.shape[0], cb), lambda b, c, s: (0, c)),
                  pl.BlockSpec((1, cb), lambda b, c, s: (0, c)),
                  pl.BlockSpec((1, cb, 2 * cb), lambda b, c, s: (c, 0, 0)),
                  pl.BlockSpec((None, 1, 2 * cb), lambda b, c, s: (c, 0, 0)),
                  pl.BlockSpec((1, cb), lambda b, c, s: (0, c))],
        out_specs=pl.BlockSpec((1, ts, cb), lambda b, c, s: (b, s, c)),
        out_shape=jax.ShapeDtypeStruct((B, S, d_rnn), BF16),
        scratch_shapes=[pltpu.VMEM((ts + SUBLANES, cb), F32), pltpu.VMEM((ts, cb), F32),
                        pltpu.VMEM((ts, cb), F32), pltpu.VMEM((SUBLANES, cb), F32)],
        compiler_params=_cparams("parallel", "parallel", "arbitrary"),
        name="rnn",
    )(proj3, proj3, conv_w, conv_b.reshape(1, d_rnn), wg, bg, lam.reshape(1, d_rnn))


def _t5_bucket(dist):
    max_exact = NUM_BUCKETS // 2
    d = np.maximum(dist, 1).astype(np.float64)
    large = max_exact + (np.log(d / max_exact) / np.log(MAX_DISTANCE / max_exact)
                         * (NUM_BUCKETS - max_exact)).astype(np.int64)
    large = np.minimum(large, NUM_BUCKETS - 1)
    return np.where(dist < max_exact, dist, large).astype(np.int32)


def _attn_bias(table, window, dilation):
    blk = window // dilation
    qi = np.arange(blk)[:, None]
    ki = np.arange(2 * blk)[None, :]
    rel = qi + blk - ki
    in_window = (rel >= 0) & (rel <= blk)
    bucket = _t5_bucket(np.clip(rel, 0, None) * dilation)
    onehot = (bucket[..., None] == np.arange(NUM_BUCKETS)).astype(np.float32)
    bias = jnp.einsum('qkn,nh->hqk', onehot, table.astype(F32), precision=lax.Precision.HIGHEST)
    first = in_window & (ki >= blk)
    return jnp.stack([jnp.where(first[None], bias, MASK_VALUE),
                      jnp.where(in_window[None], bias, MASK_VALUE)])


def _attn_heads(q, kp, kc, vp, vc, bias_ref):
    blk, gw = q.shape
    lo = lax.broadcasted_iota(I32, (blk, LANES), 1) < HEAD_DIM
    scale = HEAD_DIM ** -0.5
    o_parts, l_parts = [], []
    for p in range(gw // LANES):
        sl = slice(p * LANES, (p + 1) * LANES)
        q2 = q[:, sl] * scale
        k2 = jnp.concatenate([kp[:, sl], kc[:, sl]], axis=0)
        v2 = jnp.concatenate([vp[:, sl], vc[:, sl]], axis=0)
        outs, lses = [], []
        for hh in range(2):
            qm = jnp.where(lo if hh == 0 else jnp.logical_not(lo), q2, jnp.zeros_like(q2))
            sc = lax.dot_general(qm, k2, (((1,), (1,)), ((), ())), preferred_element_type=F32)
            sc = sc + bias_ref[0, 2 * p + hh]
            m = jnp.max(sc, axis=-1, keepdims=True)
            e = jnp.exp(sc - m)
            l = jnp.sum(e, axis=-1, keepdims=True)
            o = jnp.dot(e.astype(BF16), v2, preferred_element_type=F32)
            outs.append(o / l)
            lses.append(jnp.broadcast_to(m + jnp.log(l), (blk, LANES)))
        o_parts.append(jnp.where(lo, outs[0], outs[1]))
        l_parts.append(jnp.where(lo, lses[0], lses[1]))
    return jnp.concatenate(o_parts, axis=1), jnp.concatenate(l_parts, axis=1)


def _attn_body(q_ref, kp_ref, kc_ref, vp_ref, vc_ref, bias_ref, o_ref, lse_ref):
    o, lse = _attn_heads(q_ref[0], kp_ref[0], kc_ref[0], vp_ref[0], vc_ref[0], bias_ref)
    o_ref[0] = o.astype(o_ref.dtype)
    lse_ref[0] = lse


def _attn_dil_body(cur_ref, prev_ref, bias_ref, o_ref, lse_ref, oscr, lscr, *, d, blk, gw):
    nt = cur_ref.shape[0]
    nq = gw // LANES

    def rows(ref, r, c):
        parts = [ref[j, r, :, c * gw:(c + 1) * gw] for j in range(nt)]
        return parts[0] if nt == 1 else jnp.concatenate(parts, axis=0)

    def residue(r, carry):
        o, lse = _attn_heads(rows(cur_ref, r, 2), rows(prev_ref, r, 0), rows(cur_ref, r, 0),
                             rows(prev_ref, r, 1), rows(cur_ref, r, 1), bias_ref)
        for q in range(nq):
            oscr[q, pl.ds(r, blk, stride=d), :] = o[:, q * LANES:(q + 1) * LANES]
            lscr[q, pl.ds(r, blk, stride=d), :] = lse[:, q * LANES:(q + 1) * LANES]
        return carry

    lax.fori_loop(0, d, residue, 0)
    for q in range(nq):
        o_ref[0, :, q * LANES:(q + 1) * LANES] = oscr[q].astype(o_ref.dtype)
        lse_ref[0, :, q * LANES:(q + 1) * LANES] = lscr[q]


def _attn_dilated(qkv_t, bias, B, *, blk, d, gw):
    n_tiles, _, rows_t, _ = qkv_t.shape
    tm = rows_t * d
    nt = blk * d // tm
    nb = n_tiles // (B * nt)
    S = n_tiles * tm // B
    return pl.pallas_call(
        functools.partial(_attn_dil_body, d=d, blk=blk, gw=gw),
        grid=(B, nb),
        in_specs=[pl.BlockSpec((nt, d, rows_t, 3 * gw), lambda b, n: (b * nb + n, 0, 0, 0)),
                  pl.BlockSpec((nt, d, rows_t, 2 * gw), lambda b, n: (b * nb + jnp.maximum(n - 1, 0), 0, 0, 0)),
                  pl.BlockSpec((1,) + bias.shape[1:], lambda b, n: (jnp.minimum(n, 1), 0, 0, 0))],
        out_specs=[pl.BlockSpec((1, blk * d, gw), lambda b, n: (b, n, 0)),
                   pl.BlockSpec((1, blk * d, gw), lambda b, n: (b, n, 0))],
        out_shape=[jax.ShapeDtypeStruct((B, S, gw), BF16), jax.ShapeDtypeStruct((B, S, gw), F32)],
        scratch_shapes=[pltpu.VMEM((gw // LANES, blk * d, LANES), F32),
                        pltpu.VMEM((gw // LANES, blk * d, LANES), F32)],
        compiler_params=_cparams("parallel", "parallel"),
        name=f"attn_d{d}",
    )(qkv_t, qkv_t, bias)


def _attn_group(qkv, bias, *, blk, d, gw, q_blk, k_blk, v_blk, row_blk):
    B, L, _ = qkv.shape
    nb = L // blk

    def cur(col):
        return pl.BlockSpec((1, blk, gw), lambda b, r, n: (b, n, r * row_blk + col))

    def prev(col):
        return pl.BlockSpec((1, blk, gw), lambda b, r, n: (b, jnp.maximum(n - 1, 0), r * row_blk + col))

    return pl.pallas_call(
        _attn_body,
        grid=(B, d, nb),
        in_specs=[cur(q_blk), prev(k_blk), cur(k_blk), prev(v_blk), cur(v_blk),
                  pl.BlockSpec((1,) + bias.shape[1:], lambda b, r, n: (jnp.minimum(n, 1), 0, 0, 0))],
        out_specs=[pl.BlockSpec((1, blk, gw), lambda b, r, n: (b, n, r)),
                   pl.BlockSpec((1, blk, gw), lambda b, r, n: (b, n, r))],
        out_shape=[jax.ShapeDtypeStruct((B, L, d * gw), BF16),
                   jax.ShapeDtypeStruct((B, L, d * gw), F32)],
        compiler_params=_cparams("parallel", "parallel", "parallel"),
        name=f"attn_d{d}",
    )(qkv, qkv, qkv, qkv, qkv, bias)


def _merge_body(yr_ref, o1_ref, o2_ref, o3_ref, l1_ref, l2_ref, l3_ref, g_ref, x_ref,
                wr_ref, wa_ref, wo_ref, lg_ref, lb_ref, wrh_ref, wrl_ref,
                x1_ref, x1p_ref, lt_ref, *, alpha):
    l1, l2, l3 = l1_ref[...], l2_ref[...], l3_ref[...]
    mx = jnp.maximum(jnp.maximum(l1, l2), l3)
    w1, w2, w3 = jnp.exp(l1 - mx), jnp.exp(l2 - mx), jnp.exp(l3 - mx)
    y_att = (w1 * o1_ref[...].astype(F32) + w2 * o2_ref[...].astype(F32)
             + w3 * o3_ref[...].astype(F32)) / (w1 + w2 + w3)
    pr = jnp.dot(yr_ref[...], wr_ref[...], preferred_element_type=F32)
    pa = jnp.dot(y_att.astype(BF16), wa_ref[...], preferred_element_type=F32)
    dm = pr.shape[1]
    g = g_ref[...].astype(F32)
    merged = _sigmoid(g[:, :dm]) * pr + _sigmoid(g[:, dm:]) * pa
    mix = jnp.dot(merged.astype(BF16), wo_ref[...], preferred_element_type=F32)
    z = alpha * x_ref[...] + mix
    mu = jnp.mean(z, axis=-1, keepdims=True)
    zc = z - mu
    var = jnp.mean(zc * zc, axis=-1, keepdims=True)
    x1 = zc * lax.rsqrt(var + LN_EPS) * lg_ref[...] + lb_ref[...]
    x1_ref[...] = x1
    x1p_ref[...] = _pack_rows(x1)
    nt = (((1,), (1,)), ((), ()))
    x1h = x1.astype(BF16)
    x1l = (x1 - x1h.astype(F32)).astype(BF16)
    lt_ref[...] = (lax.dot_general(wrh_ref[...], x1h, nt, preferred_element_type=F32)
                   + (lax.dot_general(wrh_ref[...], x1l, nt, preferred_element_type=F32)
                      + lax.dot_general(wrl_ref[...], x1h, nt, preferred_element_type=F32)))


def _merge(y_rnn, os_, lses, proj, gate_blk, x2, wr, wa, wo, ln_g, ln_b, w_router_t, *, alpha, tm=512):
    T, D = x2.shape
    da = os_[0].shape[1]
    E = w_router_t.shape[0]
    wrh = w_router_t.astype(BF16)
    wrl = (w_router_t - wrh.astype(F32)).astype(BF16)
    row = lambda w: pl.BlockSpec((tm, w), lambda i: (i, 0))
    full = lambda a: pl.BlockSpec(a.shape, lambda i: (0,) * a.ndim)
    return pl.pallas_call(
        functools.partial(_merge_body, alpha=alpha),
        grid=(T // tm,),
        in_specs=[row(D), row(da), row(da), row(da), row(da), row(da), row(da),
                  pl.BlockSpec((tm, 2 * D), lambda i: (i, gate_blk)), row(D),
                  full(wr), full(wa), full(wo), full(ln_g), full(ln_b), full(wrh), full(wrl)],
        out_specs=[row(D), row(D // 2), pl.BlockSpec((E, tm), lambda i: (0, i))],
        out_shape=[jax.ShapeDtypeStruct((T, D), F32), jax.ShapeDtypeStruct((T, D // 2), U32),
                   jax.ShapeDtypeStruct((E, T), F32)],
        compiler_params=_cparams("parallel"),
        name="merge",
    )(y_rnn, *os_, *lses, proj, x2, wr, wa, wo, ln_g, ln_b, wrh, wrl)


def _first_max(vals, idx, big):
    m = jnp.max(vals, axis=0, keepdims=True)
    i = jnp.min(jnp.where(vals == m, idx, big), axis=0, keepdims=True)
    return m, i


def _route_body(lt_ref, rb_ref, e_ref, g_ref, r_ref, cnt_ref, *, tl):
    E = lt_ref.shape[0]
    per = E // N_EXPERT_GROUPS
    neg = -jnp.inf

    @pl.when(pl.program_id(0) == 0)
    def _():
        cnt_ref[...] = jnp.zeros_like(cnt_ref)

    scores = jax.nn.sigmoid(lt_ref[...])
    sel = scores + rb_ref[...]
    rowi = lax.broadcasted_iota(I32, (E, tl), 0)

    gi = lax.broadcasted_iota(I32, (N_EXPERT_GROUPS, tl), 0)
    gsc = jnp.zeros((N_EXPERT_GROUPS, tl), F32)
    for g in range(N_EXPERT_GROUPS):
        blk = sel[g * per:(g + 1) * per]
        ri = lax.broadcasted_iota(I32, (per, tl), 0) + g * per
        m1, i1 = _first_max(blk, ri, E)
        m2 = jnp.max(jnp.where(ri == i1, neg, blk), axis=0, keepdims=True)
        gsc = jnp.where(gi == g, m1 + m2, gsc)
    keep = jnp.zeros((N_EXPERT_GROUPS, tl), F32)
    for _ in range(TOPK_GROUPS):
        _, ig = _first_max(gsc, gi, N_EXPERT_GROUPS)
        hit = gi == ig
        keep = jnp.where(hit, 1.0, keep)
        gsc = jnp.where(hit, neg, gsc)
    cur = jnp.concatenate(
        [jnp.where(keep[g:g + 1] > 0.5, sel[g * per:(g + 1) * per], neg) for g in range(N_EXPERT_GROUPS)],
        axis=0)

    ti = lax.broadcasted_iota(I32, (tl, tl), 0)
    tj = lax.broadcasted_iota(I32, (tl, tl), 1)
    earlier = jnp.where(ti < tj, 1.0, 0.0).astype(BF16)

    es, gv = [], []
    onehot = jnp.zeros((E, tl), F32)
    for k in range(TOP_K):
        _, ie = _first_max(cur, rowi, E)
        hit = rowi == ie
        es.append(ie)
        gv.append(jnp.sum(jnp.where(hit, scores, 0.0), axis=0, keepdims=True))
        onehot = jnp.where(hit, 1.0, onehot)
        cur = jnp.where(hit, neg, cur)
    gsum = gv[0]
    for k in range(1, TOP_K):
        gsum = gsum + gv[k]
    ranks = jnp.dot(onehot.astype(BF16), earlier, preferred_element_type=F32) + cnt_ref[...]
    for k in range(TOP_K):
        e_ref[k:k + 1, :] = es[k]
        g_ref[k:k + 1, :] = gv[k] / gsum * ROUTED_SCALE
        r_ref[k:k + 1, :] = jnp.sum(jnp.where(rowi == es[k], ranks, 0.0), axis=0, keepdims=True).astype(I32)
    cnt_ref[...] += jnp.sum(onehot, axis=1, keepdims=True)


def _route(logits_t, router_bias, *, tl=256):
    E, T = logits_t.shape
    kt = pl.BlockSpec((TOP_K, tl), lambda i: (0, i))
    return pl.pallas_call(
        functools.partial(_route_body, tl=tl),
        grid=(T // tl,),
        in_specs=[pl.BlockSpec((E, tl), lambda i: (0, i)), pl.BlockSpec((E, 1), lambda i: (0, 0))],
        out_specs=[kt, kt, kt, pl.BlockSpec((E, 1), lambda i: (0, 0))],
        out_shape=[jax.ShapeDtypeStruct((TOP_K, T), I32), jax.ShapeDtypeStruct((TOP_K, T), F32),
                   jax.ShapeDtypeStruct((TOP_K, T), I32), jax.ShapeDtypeStruct((E, 1), F32)],
        compiler_params=_cparams("arbitrary"),
        name="route",
    )(logits_t, router_bias.reshape(E, 1))


def _dest_body(e_ref, r_ref, ss_ref, d_ref):
    E = ss_ref.shape[0]
    tl = e_ref.shape[1]
    rowi = lax.broadcasted_iota(I32, (E, tl), 0)
    e = e_ref[...]
    rows = [jnp.sum(jnp.where(rowi == e[k:k + 1], ss_ref[...], 0), axis=0, keepdims=True)
            for k in range(e.shape[0])]
    d_ref[...] = jnp.concatenate(rows, axis=0) + r_ref[...]


def _dest(e_idx, rank, seg_start, *, tl=512):
    K, T = e_idx.shape
    E = seg_start.shape[0]
    kt = pl.BlockSpec((K, tl), lambda i: (0, i))
    return pl.pallas_call(
        _dest_body, grid=(T // tl,),
        in_specs=[kt, kt, pl.BlockSpec((E, 1), lambda i: (0, 0))],
        out_specs=kt, out_shape=jax.ShapeDtypeStruct((K, T), I32),
        compiler_params=_cparams("parallel"), name="dest",
    )(e_idx, rank, seg_start.reshape(E, 1))


def _pack_rows(v):
    w = v.shape[1] // 2
    lo = pltpu.bitcast(v[:, :w].astype(BF16).astype(F32), U32) >> 16
    hi = pltpu.bitcast(v[:, w:].astype(BF16).astype(F32), U32) & jnp.uint32(0xFFFF0000)
    return lo | hi


def _unpack_rows(p):
    lo = pltpu.bitcast(p << 16, F32)
    hi = pltpu.bitcast(p & jnp.uint32(0xFFFF0000), F32)
    return lo, hi


def _sc_scatter_rows(x, idx3, n_rows):
    from jax.experimental.pallas import tpu_sc as plsc
    n_chunks, K, CH = idx3.shape
    W = x.shape[1]
    info = plsc.get_sparse_core_info()
    NC, n_workers = info.num_cores, info.num_cores * info.num_subcores
    per_w = n_chunks // n_workers
    assert per_w * n_workers == n_chunks and per_w % 2 == 0

    def body(x_hbm, idx_hbm, out_hbm, idx_v, rows_v, sem_l, sem_s):
        c0 = (lax.axis_index("s") * NC + lax.axis_index("c")) * per_w

        def load(cc, b):
            return pltpu.make_async_copy(x_hbm.at[pl.ds(pl.multiple_of((c0 + cc) * CH, CH), CH)], rows_v.at[b],
                                         sem_l.at[b])

        def scatter(b, k):
            return pltpu.make_async_copy(rows_v.at[b], out_hbm.at[idx_v.at[b, k]], sem_s.at[b])

        pltpu.sync_copy(idx_hbm.at[c0], idx_v.at[0])
        load(0, 0).start()

        @pl.loop(0, per_w, step=2)
        def _(c):
            for b in range(2):
                cc = c + b
                load(cc, b).wait()
                for k in range(K):
                    scatter(b, k).start()

                @pl.when(cc >= 1)
                def _():
                    for k in range(K):
                        scatter(1 - b, k).wait()

                @pl.when(cc + 1 < per_w)
                def _():
                    pltpu.sync_copy(idx_hbm.at[c0 + cc + 1], idx_v.at[1 - b])
                    load(cc + 1, 1 - b).start()

        for k in range(K):
            scatter(1, k).wait()

    return pl.kernel(
        body, mesh=plsc.VectorSubcoreMesh(core_axis_name="c", subcore_axis_name="s"),
        out_type=jax.ShapeDtypeStruct((n_rows, W), x.dtype),
        scratch_types=[pltpu.VMEM((2, K, CH), I32), pltpu.VMEM((2, CH, W), x.dtype),
                       pltpu.SemaphoreType.DMA((2,)), pltpu.SemaphoreType.DMA((2,))],
    )(x, idx3)


def _pad_fill_body(ps_ref, pc_ref, nu_ref, xs_in, xs_hbm, zrow, sem_z, sem_c, sem_b, sem_t):
    del xs_in
    i = pl.program_id(0)
    n = pl.num_programs(0)
    R = zrow.shape[0]

    def pad_copy(row):
        return pltpu.make_async_copy(zrow.at[pl.ds(0, 1), :], xs_hbm.at[pl.ds(row, 1), :], sem_z)

    zrow[...] = jnp.zeros_like(zrow)

    n_blk = xs_hbm.shape[0] // R
    tail_per_step = -(-n_blk // n)

    def tail(j, c, wait):
        blk = i * tail_per_step + j

        @pl.when(jnp.logical_and(blk >= nu_ref[0], blk < n_blk))
        def _():
            cp = pltpu.make_async_copy(zrow, xs_hbm.at[pl.ds(pl.multiple_of(blk * R, R), R), :], sem_t)
            cp.wait() if wait else cp.start()

        return c

    lax.fori_loop(0, tail_per_step, functools.partial(tail, wait=False), 0)

    E = ps_ref.shape[0]
    per_step = -(-E // n)

    def pads(j, c, wait):
        e = jnp.minimum(i * per_step + j, E - 1)
        cnt = jnp.where(i * per_step + j < E, pc_ref[e], 0)
        start = ps_ref[e]
        n_single = jnp.minimum(cnt, (-start) & (SUBLANES - 1))
        start8 = start + n_single

        def one(r, c2):
            cp = pad_copy(start + r)
            cp.wait() if wait else cp.start()
            return c2

        rem8 = lax.shift_right_logical(cnt - n_single, 3)
        n_eight = jnp.minimum(rem8, lax.shift_right_logical(-start8, 3) & (SUBLANES - 1))
        start64 = start8 + n_eight * SUBLANES
        big = SUBLANES * SUBLANES

        def eight(r, c2):
            row = pl.multiple_of(start8 + r * SUBLANES, SUBLANES)
            cp = pltpu.make_async_copy(zrow.at[pl.ds(0, SUBLANES), :], xs_hbm.at[pl.ds(row, SUBLANES), :], sem_c)
            cp.wait() if wait else cp.start()
            return c2

        def sixty_four(r, c2):
            row = pl.multiple_of(start64 + r * big, big)
            cp = pltpu.make_async_copy(zrow.at[pl.ds(0, big), :], xs_hbm.at[pl.ds(row, big), :], sem_b)
            cp.wait() if wait else cp.start()
            return c2

        c = lax.fori_loop(0, n_single, one, c)
        c = lax.fori_loop(0, n_eight, eight, c)
        return lax.fori_loop(0, lax.shift_right_logical(rem8 - n_eight, 3), sixty_four, c)

    lax.fori_loop(0, per_step, functools.partial(pads, wait=False), 0)
    lax.fori_loop(0, per_step, functools.partial(pads, wait=True), 0)
    lax.fori_loop(0, tail_per_step, functools.partial(tail, wait=True), 0)


def _pad_fill(xs, pad_start, pad_cnt, n_used, *, steps=16):
    any_spec = pl.BlockSpec(memory_space=pl.ANY)
    return pl.pallas_call(
        _pad_fill_body,
        grid_spec=pltpu.PrefetchScalarGridSpec(
            num_scalar_prefetch=3, grid=(steps,),
            in_specs=[any_spec], out_specs=any_spec,
            scratch_shapes=[pltpu.VMEM((EXPERT_ROWS, xs.shape[1]), xs.dtype),
                            pltpu.SemaphoreType.DMA, pltpu.SemaphoreType.DMA, pltpu.SemaphoreType.DMA,
                            pltpu.SemaphoreType.DMA]),
        out_shape=jax.ShapeDtypeStruct(xs.shape, xs.dtype),
        input_output_aliases={3: 0},
        compiler_params=_cparams("arbitrary"),
        name="pad_fill",
    )(pad_start, pad_cnt, n_used, xs)


def _experts_body(fb_ref, xs_hbm, wg_ref, wu_ref, wd_ref, ys_hbm, xbuf, ybuf, wgu_s, wd_s, sem_x, sem_y):
    e = pl.program_id(0)
    n_e = pl.num_programs(0)
    de = wg_ref.shape[2]
    G = EXPERT_GROUP
    sh = G.bit_length() - 1
    R = xbuf.shape[1] // G
    n_blk = xs_hbm.shape[0] // R
    n_used = fb_ref[n_e]
    n_pairs = lax.shift_right_logical(n_used + (G - 1), sh)

    def pair_rows(ref, p):
        return ref.at[pl.ds(pl.multiple_of(p * (G * R), G * R), G * R), :]

    def x_copy(p, slot):
        return pltpu.make_async_copy(pair_rows(xs_hbm, p), xbuf.at[slot], sem_x.at[slot])

    def y_copy(p, slot):
        return pltpu.make_async_copy(ybuf.at[slot], pair_rows(ys_hbm, p), sem_y.at[slot])

    @pl.when(jnp.logical_and(e == 0, n_used > 0))
    def _():
        x_copy(0, 0).start(priority=1)

    wgu_s[:, :de] = wg_ref[0].astype(BF16)
    wgu_s[:, de:] = wu_ref[0].astype(BF16)
    wd_s[...] = wd_ref[0].astype(BF16)

    def block(g, c):
        p = lax.shift_right_logical(g, sh)
        half = g & (G - 1)
        slot = p & 1

        @pl.when(half == 0)
        def _():
            x_copy(p, slot).wait()

            @pl.when(p + 1 < n_pairs)
            def _():
                x_copy(p + 1, 1 - slot).start(priority=1)

            @pl.when(p >= 2)
            def _():
                y_copy(p - 2, slot).wait()

        r0 = pl.multiple_of(half * R, R)
        lo, hi = _unpack_rows(xbuf[slot, pl.ds(r0, R), :])
        xb = jnp.concatenate([lo.astype(BF16), hi.astype(BF16)], axis=1)
        h = jnp.dot(xb, wgu_s[...], preferred_element_type=F32)
        hg = h[:, :de]
        act = (hg * _sigmoid(hg) * h[:, de:]).astype(BF16)
        ybuf[slot, pl.ds(r0, R), :] = _pack_rows(jnp.dot(act, wd_s[...], preferred_element_type=F32))

        @pl.when(half == G - 1)
        def _():
            y_copy(p, slot).start(priority=1)

        return c

    lax.fori_loop(fb_ref[e], fb_ref[e + 1], block, 0)

    @pl.when(e == n_e - 1)
    def _():
        @pl.when((n_used & (G - 1)) != 0)
        def _():
            p = lax.shift_right_logical(n_used, sh)
            y_copy(p, p & 1).start(priority=1)

        for back in (2, 1):
            p = n_pairs - back

            @pl.when(p >= 0)
            def _():
                y_copy(p, p & 1).wait()

        ybuf[0] = jnp.zeros(ybuf.shape[1:], ybuf.dtype)

        def tail(g, c, wait):
            cp = pltpu.make_async_copy(ybuf.at[0, pl.ds(0, R), :],
                                       ys_hbm.at[pl.ds(pl.multiple_of(g * R, R), R), :], sem_y.at[0])
            cp.wait() if wait else cp.start()
            return c

        lax.fori_loop(n_used, n_blk, functools.partial(tail, wait=False), 0)
        lax.fori_loop(n_used, n_blk, functools.partial(tail, wait=True), 0)


def _experts(xs, first_blk, wg, wu, wd):
    n_rows, W = xs.shape
    R = EXPERT_ROWS
    E, D, de = wg.shape
    G = EXPERT_GROUP
    assert (n_rows // R) % G == 0 and G & (G - 1) == 0
    any_spec = pl.BlockSpec(memory_space=pl.ANY)
    wmap = lambda e, fb: (e, 0, 0)
    return pl.pallas_call(
        _experts_body,
        grid_spec=pltpu.PrefetchScalarGridSpec(
            num_scalar_prefetch=1, grid=(E,),
            in_specs=[any_spec, pl.BlockSpec((1, D, de), wmap), pl.BlockSpec((1, D, de), wmap),
                      pl.BlockSpec((1, de, D), wmap)],
            out_specs=any_spec,
            scratch_shapes=[pltpu.VMEM((2, G * R, W), U32), pltpu.VMEM((2, G * R, W), U32),
                            pltpu.VMEM((D, 2 * de), BF16), pltpu.VMEM((de, D), BF16),
                            pltpu.SemaphoreType.DMA((2,)), pltpu.SemaphoreType.DMA((2,))]),
        out_shape=jax.ShapeDtypeStruct((n_rows, W), U32),
        compiler_params=_cparams("arbitrary"),
        name="experts",
    )(first_blk, xs, wg, wu, wd)


def _sc_gather_rows(table, idx2):
    from jax.experimental.pallas import tpu_sc as plsc
    n_chunks, CH = idx2.shape
    W = table.shape[1]
    info = plsc.get_sparse_core_info()
    NC, n_workers = info.num_cores, info.num_cores * info.num_subcores
    per_w = n_chunks // n_workers
    assert per_w * n_workers == n_chunks and per_w % 2 == 0

    def body(table_hbm, idx_hbm, out_hbm, idx_v, rows_v, sem):
        c0 = (lax.axis_index("s") * NC + lax.axis_index("c")) * per_w

        def gather(b):
            return pltpu.make_async_copy(table_hbm.at[idx_v.at[b]], rows_v.at[b], sem.at[b])

        pltpu.sync_copy(idx_hbm.at[c0], idx_v.at[0])
        gather(0).start()

        @pl.loop(0, per_w, step=2)
        def _(c):
            for b in range(2):
                cc = c + b

                @pl.when(cc + 1 < per_w)
                def _():
                    pltpu.sync_copy(idx_hbm.at[c0 + cc + 1], idx_v.at[1 - b])
                    gather(1 - b).start()

                gather(b).wait()
                pltpu.sync_copy(rows_v.at[b], out_hbm.at[pl.ds(pl.multiple_of((c0 + cc) * CH, CH), CH)])

    return pl.kernel(
        body, mesh=plsc.VectorSubcoreMesh(core_axis_name="c", subcore_axis_name="s"),
        out_type=jax.ShapeDtypeStruct((n_chunks * CH, W), table.dtype),
        scratch_types=[pltpu.VMEM((2, CH), I32), pltpu.VMEM((2, CH, W), table.dtype),
                       pltpu.SemaphoreType.DMA((2,))],
    )(table, idx2)


def _combine_body(yg_ref, gate_ref, x1_ref, wgu_ref, wd_ref, lg_ref, lb_ref, *rest, alpha):
    o_ref = rest[-1]
    x1 = x1_ref[...]
    ds_ = wd_ref.shape[0]
    h = jnp.dot(x1.astype(BF16), wgu_ref[...], preferred_element_type=F32)
    hg = h[:, :ds_]
    act = (hg * _sigmoid(hg) * h[:, ds_:]).astype(BF16)
    shared = jnp.dot(act, wd_ref[...], preferred_element_type=F32)

    g = gate_ref[...]
    lo_acc = hi_acc = None
    for k in range(yg_ref.shape[0]):
        lo, hi = _unpack_rows(yg_ref[k])
        gk = g[:, k:k + 1]
        lo_acc = gk * lo if k == 0 else lo_acc + gk * lo
        hi_acc = gk * hi if k == 0 else hi_acc + gk * hi
    routed = jnp.concatenate([lo_acc, hi_acc], axis=1)

    z = alpha * x1 + (routed + shared)
    mu = jnp.mean(z, axis=-1, keepdims=True)
    zc = z - mu
    var = jnp.mean(zc * zc, axis=-1, keepdims=True)
    o_ref[...] = zc * lax.rsqrt(var + LN_EPS) * lg_ref[...] + lb_ref[...]


def _combine(yg, gate_tk, x1, wgu, wd, ln_g, ln_b, out_prev, part, *, alpha, tc):
    T, D = x1.shape
    K, Tp, W = yg.shape
    off = part * (Tp // tc)
    full = lambda a: pl.BlockSpec(a.shape, lambda i: (0,) * a.ndim)
    args = [yg, gate_tk, x1, wgu, wd, ln_g, ln_b]
    in_specs = [pl.BlockSpec((K, tc, W), lambda i: (0, i, 0)), pl.BlockSpec((tc, K), lambda i: (off + i, 0)),
                pl.BlockSpec((tc, D), lambda i: (off + i, 0)), full(wgu), full(wd), full(ln_g), full(ln_b)]
    aliases = {}
    if out_prev is not None:
        args.append(out_prev)
        in_specs.append(pl.BlockSpec(memory_space=pl.ANY))
        aliases = {len(args) - 1: 0}
    return pl.pallas_call(
        functools.partial(_combine_body, alpha=alpha),
        grid=(Tp // tc,),
        in_specs=in_specs,
        out_specs=pl.BlockSpec((tc, D), lambda i: (off + i, 0)),
        out_shape=jax.ShapeDtypeStruct((T, D), F32),
        input_output_aliases=aliases,
        compiler_params=_cparams("parallel"),
        name="combine",
    )(*args)


def _mixer_ln1(x, w_in, b_in, conv_w, conv_b, w_rg_a, b_rg_a, w_rg_i, b_rg_i, lru_lambda,
               w_proj_rnn, w_proj_att, rel_bias, w_out, ln1_g, ln1_b, w_router, alpha):
    B, S, D = x.shape
    T = B * S
    d_rnn = conv_w.shape[-1]
    gw = HEADS_PER_GROUP * HEAD_DIM
    d_att = gw * len(DILATED_GROUPS)
    a0 = 2 * d_rnn
    a1 = a0 + 3 * d_att
    head = lambda j, g: slice(a0 + j * d_att + g * gw, a0 + j * d_att + (g + 1) * gw)
    plain = [g for g, (_, d) in enumerate(DILATED_GROUPS) if d == 1]
    dilated = [g for g, (_, d) in enumerate(DILATED_GROUPS) if d > 1]
    order = ([slice(0, a0), slice(a1, None)] + [head(j, g) for g in plain for j in range(3)]
             + [head(j, g) for g in dilated for j in (1, 2, 0)])
    perm = lambda w: jnp.concatenate([w[..., s] for s in order], axis=-1)
    w_p = perm(w_in).astype(BF16)
    b_p = perm(b_in).reshape(1, -1)
    qkv0 = a0 + 2 * D
    n_main = qkv0 + 3 * gw * len(plain)
    x2 = x.reshape(T, D)

    proj, *qkv_dil = _in_proj(x2, w_p, b_p, n_main, tuple(DILATED_GROUPS[g][1] for g in dilated))
    proj3 = proj.reshape(B, S, n_main)
    y_rnn = _rnn(proj3, conv_w, conv_b, w_rg_a, b_rg_a, w_rg_i, b_rg_i, lru_lambda)

    os_, lses = [], []
    for g, (window, d) in enumerate(DILATED_GROUPS):
        blk = window // d
        bias = _attn_bias(rel_bias[:, g * HEADS_PER_GROUP:(g + 1) * HEADS_PER_GROUP], window, d)
        if d == 1:
            c0 = (qkv0 + 3 * gw * plain.index(g)) // gw
            o, lse = _attn_group(proj3, bias, blk=blk, d=1, gw=gw, q_blk=c0, k_blk=c0 + 1, v_blk=c0 + 2,
                                 row_blk=n_main // gw)
        else:
            o, lse = _attn_dilated(qkv_dil[dilated.index(g)], bias, B, blk=blk, d=d, gw=gw)
        os_.append(o.reshape(T, gw))
        lses.append(lse.reshape(T, gw))

    return _merge(y_rnn.reshape(T, d_rnn), os_, lses, proj, a0 // (2 * D), x2,
                  w_proj_rnn.astype(BF16), w_proj_att.astype(BF16), w_out.astype(BF16),
                  ln1_g.reshape(1, D), ln1_b.reshape(1, D), w_router.T, alpha=alpha)


def _moe_ln2(x1, x1p, logits_t, router_bias, w_exp_gate, w_exp_up, w_exp_down,
             w_sh_gate, w_sh_up, w_sh_down, ln2_g, ln2_b, alpha):
    T, D = x1.shape
    E = logits_t.shape[0]
    R = EXPERT_ROWS
    n_blk = T * TOP_K // R + E
    n_rows = n_blk * R

    e_idx, gate, rank, counts = _route(logits_t, router_bias)
    counts = counts.reshape(E).astype(I32)
    padded = (counts + R - 1) // R * R
    pad_end = jnp.cumsum(padded)
    seg_start = pad_end - padded
    n_used = (pad_end[-1:] // R).astype(I32)
    first_blk = jnp.concatenate([jnp.zeros((1,), I32), (pad_end // R).astype(I32)])
    dest = _dest(e_idx, rank, seg_start)
    tl = ROW_TILE
    ch = SC_GATHER_CHUNK
    dest_c = dest.reshape(TOP_K, T // ch, ch).transpose(1, 0, 2)

    xs = _sc_scatter_rows(x1p, dest_c, n_rows)
    xs = _pad_fill(xs, seg_start + counts, padded - counts, n_used)
    ys = _experts(xs, first_blk, w_exp_gate, w_exp_up, w_exp_down)
    wgu = jnp.concatenate([w_sh_gate, w_sh_up], axis=-1).astype(BF16)
    gate_tk = gate.T
    wd_sh = w_sh_down.astype(BF16)
    tp = T // COMBINE_PARTS
    out = None
    for p in range(COMBINE_PARTS):
        idx = dest[:, p * tp:(p + 1) * tp].reshape(-1, ch)
        yg = _sc_gather_rows(ys, idx).reshape(TOP_K, tp, ys.shape[1])
        out = _combine(yg, gate_tk, x1, wgu, wd_sh, ln2_g.reshape(1, D), ln2_b.reshape(1, D), out, p,
                       alpha=alpha, tc=tl)
    return out


def kernel(x, w_in, b_in, conv_w, conv_b, w_rg_a, b_rg_a, w_rg_i, b_rg_i, lru_lambda,
           w_proj_rnn, w_proj_att, rel_bias, w_out, ln1_g, ln1_b, w_router, router_bias,
           w_exp_gate, w_exp_up, w_exp_down, w_sh_gate, w_sh_up, w_sh_down, ln2_g, ln2_b):
    B, S, D = x.shape
    depth = w_in.shape[0]
    alpha = (2 * depth) ** 0.25
    for i in range(depth):
        x1, x1p, logits_t = _mixer_ln1(
            x, w_in[i], b_in[i], conv_w[i], conv_b[i], w_rg_a[i], b_rg_a[i], w_rg_i[i], b_rg_i[i],
            lru_lambda[i], w_proj_rnn[i], w_proj_att[i], rel_bias, w_out[i], ln1_g[i], ln1_b[i],
            w_router[i], alpha)
        out = _moe_ln2(x1, x1p, logits_t, router_bias[i], w_exp_gate[i], w_exp_up[i], w_exp_down[i],
                       w_sh_gate[i], w_sh_up[i], w_sh_down[i], ln2_g[i], ln2_b[i], alpha)
        x = out.reshape(B, S, D)
    return x
```

```python
import functools

import numpy as np
import jax
import jax.numpy as jnp
from jax import lax
from jax.experimental import pallas as pl
from jax.experimental.pallas import tpu as pltpu

F32 = jnp.float32
BF16 = jnp.bfloat16
I32 = jnp.int32
U32 = jnp.uint32

RNN_HEADS = 16
LRU_C = 8.0
HEAD_DIM = 64
HEADS_PER_GROUP = 8
DILATED_GROUPS = ((128, 1), (512, 4), (2048, 16))
NUM_BUCKETS = 32
MAX_DISTANCE = 2048
TOP_K = 8
N_EXPERT_GROUPS = 8
TOPK_GROUPS = 4
ROUTED_SCALE = 2.5
LN_EPS = 1e-5

LANES = 128
SUBLANES = 8
VMEM_LIMIT_BYTES = 56 * 1024 * 1024

MASK_VALUE = -1e30
EXPERT_ROWS = 512
EXPERT_GROUP = 8
ROW_TILE = 256
SC_GATHER_CHUNK = 64
COMBINE_PARTS = 4


def _cparams(*sem):
    return pltpu.CompilerParams(dimension_semantics=sem, vmem_limit_bytes=VMEM_LIMIT_BYTES)


def _sigmoid(v):
    return 0.5 * (jnp.tanh(0.5 * v) + 1.0)


def _in_proj_body(x_ref, w_ref, b_ref, main_ref, *rest, tn, dils):
    dil_refs, scr = rest[:-1], rest[-1]
    xb = x_ref[...].astype(BF16)
    tm = xb.shape[0]

    def chunk(j):
        sl = slice(j * tn, (j + 1) * tn)
        return jnp.dot(xb, w_ref[:, sl], preferred_element_type=F32) + b_ref[:, sl]

    n_main = main_ref.shape[1] // tn
    for j in range(n_main):
        main_ref[:, j * tn:(j + 1) * tn] = chunk(j).astype(main_ref.dtype)
    j = n_main
    for ref, d in zip(dil_refs, dils):
        for c in range(ref.shape[3] // tn):
            acc = chunk(j)
            j += 1
            for q in range(tn // LANES):
                scr[q] = acc[:, q * LANES:(q + 1) * LANES]
            for r in range(d):
                part = jnp.concatenate([scr[q, pl.ds(r, tm // d, stride=d), :] for q in range(tn // LANES)], axis=1)
                ref[0, r, :, c * tn:(c + 1) * tn] = part.astype(ref.dtype)


def _in_proj(x2, w, b, n_main, dils, *, tm=512, tn=512):
    T, D = x2.shape
    N = w.shape[1]
    wd = (N - n_main) // len(dils)
    out_specs = [pl.BlockSpec((tm, n_main), lambda i: (i, 0))]
    out_shape = [jax.ShapeDtypeStruct((T, n_main), BF16)]
    for d in dils:
        out_specs.append(pl.BlockSpec((1, d, tm // d, wd), lambda i: (i, 0, 0, 0)))
        out_shape.append(jax.ShapeDtypeStruct((T // tm, d, tm // d, wd), BF16))
    return pl.pallas_call(
        functools.partial(_in_proj_body, tn=tn, dils=dils),
        grid=(T // tm,),
        in_specs=[pl.BlockSpec((tm, D), lambda i: (i, 0)),
                  pl.BlockSpec((D, N), lambda i: (0, 0), pipeline_mode=pl.Buffered(1)),
                  pl.BlockSpec((1, N), lambda i: (0, 0))],
        out_specs=out_specs,
        out_shape=out_shape,
        scratch_shapes=[pltpu.VMEM((tn // LANES, tm, LANES), F32)],
        compiler_params=_cparams("parallel"),
        name="in_proj",
    )(x2, w, b)


def _rnn_body(xr_ref, gr_ref, cw_ref, cb_ref, wg_ref, bg_ref, lam_ref, y_ref,
              xext, a_s, b_s, hc, *, ts, cb):
    s = pl.program_id(2)

    @pl.when(s == 0)
    def _():
        xext[0:SUBLANES, :] = jnp.zeros((SUBLANES, cb), F32)
        hc[...] = jnp.zeros_like(hc)

    xr = xr_ref[0].astype(F32)
    xext[SUBLANES:SUBLANES + ts, :] = xr
    nw = cw_ref.shape[0]
    xc = cw_ref[nw - 1:nw, :] * xr + cb_ref[...]
    for j in range(nw - 1):
        off = SUBLANES - (nw - 1 - j)
        xc = xc + cw_ref[j:j + 1, :] * xext[off:off + ts, :]
    xext[0:SUBLANES, :] = xext[ts:ts + SUBLANES, :]

    gates = jnp.dot(xc.astype(BF16), wg_ref[0], preferred_element_type=F32) + bg_ref[...]
    r = _sigmoid(gates[:, :cb])
    ig = _sigmoid(gates[:, cb:])
    nl = -lam_ref[...]
    sp = jnp.maximum(nl, 0.0) + jnp.log1p(jnp.exp(-jnp.abs(nl)))
    log_a = (-LRU_C) * r * sp
    a = jnp.exp(log_a)
    u = jnp.sqrt(1.0 - a * a) * (ig * xc)

    row = lax.broadcasted_iota(I32, (ts, cb), 0) & (SUBLANES - 1)
    av, bv = a, u
    for sft in (1, 2, 4):
        a_sh = pltpu.roll(av, sft, 0)
        b_sh = pltpu.roll(bv, sft, 0)
        m = row >= sft
        bv = jnp.where(m, av * b_sh + bv, bv)
        av = jnp.where(m, av * a_sh, av)
    a_s[...] = av
    b_s[...] = bv

    def carry(g, h):
        i0 = pl.multiple_of(g * SUBLANES, SUBLANES)
        h8 = b_s[pl.ds(i0, SUBLANES), :] + a_s[pl.ds(i0, SUBLANES), :] * h
        b_s[pl.ds(i0, SUBLANES), :] = h8
        return h8[SUBLANES - 1:SUBLANES, :]

    hc[0:1, :] = lax.fori_loop(0, ts // SUBLANES, carry, hc[0:1, :], unroll=8)
    gr = gr_ref[0].astype(F32)
    y_ref[0] = (b_s[...] * jax.nn.gelu(gr)).astype(y_ref.dtype)


def _block_diag(w, per):
    H, d, _ = w.shape
    w4 = w.reshape(H // per, per, d, d)
    out = jnp.einsum('gpij,pq->gpiqj', w4, jnp.eye(per, dtype=w.dtype))
    return out.reshape(H // per, per * d, per * d)


def _rnn(proj3, conv_w, conv_b, w_rg_a, b_rg_a, w_rg_i, b_rg_i, lam, *, ts=512, cb=512):
    B, S, _ = proj3.shape
    d_rnn = conv_w.shape[-1]
    nc = d_rnn // cb
    per = cb // (d_rnn // RNN_HEADS)
    wg = jnp.concatenate([_block_diag(w_rg_a, per), _block_diag(w_rg_i, per)], axis=-1).astype(BF16)
    bg = jnp.concatenate([b_rg_a.reshape(nc, 1, cb), b_rg_i.reshape(nc, 1, cb)], axis=-1)
    return pl.pallas_call(
        functools.partial(_rnn_body, ts=ts, cb=cb),
        grid=(B, nc, S // ts),
        in_specs=[pl.BlockSpec((1, ts, cb), lambda b, c, s: (b, s, c)),
                  pl.BlockSpec((1, ts, cb), lambda b, c, s: (b, s, nc + c)),
                  pl.BlockSpec((conv_w.shape[0], cb), lambda b, c, s: (0, c)),
                  pl.BlockSpec((1, cb), lambda b, c, s: (0, c)),
                  pl.BlockSpec((1, cb, 2 * cb), lambda b, c, s: (c, 0, 0)),
                  pl.BlockSpec((None, 1, 2 * cb), lambda b, c, s: (c, 0, 0)),
                  pl.BlockSpec((1, cb), lambda b, c, s: (0, c))],
        out_specs=pl.BlockSpec((1, ts, cb), lambda b, c, s: (b, s, c)),
        out_shape=jax.ShapeDtypeStruct((B, S, d_rnn), BF16),
        scratch_shapes=[pltpu.VMEM((ts + SUBLANES, cb), F32), pltpu.VMEM((ts, cb), F32),
                        pltpu.VMEM((ts, cb), F32), pltpu.VMEM((SUBLANES, cb), F32)],
        compiler_params=_cparams("parallel", "parallel", "arbitrary"),
        name="rnn",
    )(proj3, proj3, conv_w, conv_b.reshape(1, d_rnn), wg, bg, lam.reshape(1, d_rnn))


def _t5_bucket(dist):
    max_exact = NUM_BUCKETS // 2
    d = np.maximum(dist, 1).astype(np.float64)
    large = max_exact + (np.log(d / max_exact) / np.log(MAX_DISTANCE / max_exact)
                         * (NUM_BUCKETS - max_exact)).astype(np.int64)
    large = np.minimum(large, NUM_BUCKETS - 1)
    return np.where(dist < max_exact, dist, large).astype(np.int32)


def _attn_bias(table, window, dilation):
    blk = window // dilation
    qi = np.arange(blk)[:, None]
    ki = np.arange(2 * blk)[None, :]
    rel = qi + blk - ki
    in_window = (rel >= 0) & (rel <= blk)
    bucket = _t5_bucket(np.clip(rel, 0, None) * dilation)
    onehot = (bucket[..., None] == np.arange(NUM_BUCKETS)).astype(np.float32)
    bias = jnp.einsum('qkn,nh->hqk', onehot, table.astype(F32), precision=lax.Precision.HIGHEST)
    first = in_window & (ki >= blk)
    return jnp.stack([jnp.where(first[None], bias, MASK_VALUE),
                      jnp.where(in_window[None], bias, MASK_VALUE)])


def _attn_heads(q, kp, kc, vp, vc, bias_ref):
    blk, gw = q.shape
    lo = lax.broadcasted_iota(I32, (blk, LANES), 1) < HEAD_DIM
    scale = HEAD_DIM ** -0.5
    o_parts, l_parts = [], []
    for p in range(gw // LANES):
        sl = slice(p * LANES, (p + 1) * LANES)
        q2 = q[:, sl] * scale
        k2 = jnp.concatenate([kp[:, sl], kc[:, sl]], axis=0)
        v2 = jnp.concatenate([vp[:, sl], vc[:, sl]], axis=0)
        outs, lses = [], []
        for hh in range(2):
            qm = jnp.where(lo if hh == 0 else jnp.logical_not(lo), q2, jnp.zeros_like(q2))
            sc = lax.dot_general(qm, k2, (((1,), (1,)), ((), ())), preferred_element_type=F32)
            sc = sc + bias_ref[0, 2 * p + hh]
            m = jnp.max(sc, axis=-1, keepdims=True)
            e = jnp.exp(sc - m)
            l = jnp.sum(e, axis=-1, keepdims=True)
            o = jnp.dot(e.astype(BF16), v2, preferred_element_type=F32)
            outs.append(o / l)
            lses.append(jnp.broadcast_to(m + jnp.log(l), (blk, LANES)))
        o_parts.append(jnp.where(lo, outs[0], outs[1]))
        l_parts.append(jnp.where(lo, lses[0], lses[1]))
    return jnp.concatenate(o_parts, axis=1), jnp.concatenate(l_parts, axis=1)


def _attn_body(q_ref, kp_ref, kc_ref, vp_ref, vc_ref, bias_ref, o_ref, lse_ref):
    o, lse = _attn_heads(q_ref[0], kp_ref[0], kc_ref[0], vp_ref[0], vc_ref[0], bias_ref)
    o_ref[0] = o.astype(o_ref.dtype)
    lse_ref[0] = lse


def _attn_dil_body(cur_ref, prev_ref, bias_ref, o_ref, lse_ref, oscr, lscr, *, d, blk, gw):
    nt = cur_ref.shape[0]
    nq = gw // LANES

    def rows(ref, r, c):
        parts = [ref[j, r, :, c * gw:(c + 1) * gw] for j in range(nt)]
        return parts[0] if nt == 1 else jnp.concatenate(parts, axis=0)

    def residue(r, carry):
        o, lse = _attn_heads(rows(cur_ref, r, 2), rows(prev_ref, r, 0), rows(cur_ref, r, 0),
                             rows(prev_ref, r, 1), rows(cur_ref, r, 1), bias_ref)
        for q in range(nq):
            oscr[q, pl.ds(r, blk, stride=d), :] = o[:, q * LANES:(q + 1) * LANES]
            lscr[q, pl.ds(r, blk, stride=d), :] = lse[:, q * LANES:(q + 1) * LANES]
        return carry

    lax.fori_loop(0, d, residue, 0)
    for q in range(nq):
        o_ref[0, :, q * LANES:(q + 1) * LANES] = oscr[q].astype(o_ref.dtype)
        lse_ref[0, :, q * LANES:(q + 1) * LANES] = lscr[q]


def _attn_dilated(qkv_t, bias, B, *, blk, d, gw):
    n_tiles, _, rows_t, _ = qkv_t.shape
    tm = rows_t * d
    nt = blk * d // tm
    nb = n_tiles // (B * nt)
    S = n_tiles * tm // B
    return pl.pallas_call(
        functools.partial(_attn_dil_body, d=d, blk=blk, gw=gw),
        grid=(B, nb),
        in_specs=[pl.BlockSpec((nt, d, rows_t, 3 * gw), lambda b, n: (b * nb + n, 0, 0, 0)),
                  pl.BlockSpec((nt, d, rows_t, 2 * gw), lambda b, n: (b * nb + jnp.maximum(n - 1, 0), 0, 0, 0)),
                  pl.BlockSpec((1,) + bias.shape[1:], lambda b, n: (jnp.minimum(n, 1), 0, 0, 0))],
        out_specs=[pl.BlockSpec((1, blk * d, gw), lambda b, n: (b, n, 0)),
                   pl.BlockSpec((1, blk * d, gw), lambda b, n: (b, n, 0))],
        out_shape=[jax.ShapeDtypeStruct((B, S, gw), BF16), jax.ShapeDtypeStruct((B, S, gw), F32)],
        scratch_shapes=[pltpu.VMEM((gw // LANES, blk * d, LANES), F32),
                        pltpu.VMEM((gw // LANES, blk * d, LANES), F32)],
        compiler_params=_cparams("parallel", "parallel"),
        name=f"attn_d{d}",
    )(qkv_t, qkv_t, bias)


def _attn_group(qkv, bias, *, blk, d, gw, q_blk, k_blk, v_blk, row_blk):
    B, L, _ = qkv.shape
    nb = L // blk

    def cur(col):
        return pl.BlockSpec((1, blk, gw), lambda b, r, n: (b, n, r * row_blk + col))

    def prev(col):
        return pl.BlockSpec((1, blk, gw), lambda b, r, n: (b, jnp.maximum(n - 1, 0), r * row_blk + col))

    return pl.pallas_call(
        _attn_body,
        grid=(B, d, nb),
        in_specs=[cur(q_blk), prev(k_blk), cur(k_blk), prev(v_blk), cur(v_blk),
                  pl.BlockSpec((1,) + bias.shape[1:], lambda b, r, n: (jnp.minimum(n, 1), 0, 0, 0))],
        out_specs=[pl.BlockSpec((1, blk, gw), lambda b, r, n: (b, n, r)),
                   pl.BlockSpec((1, blk, gw), lambda b, r, n: (b, n, r))],
        out_shape=[jax.ShapeDtypeStruct((B, L, d * gw), BF16),
                   jax.ShapeDtypeStruct((B, L, d * gw), F32)],
        compiler_params=_cparams("parallel", "parallel", "parallel"),
        name=f"attn_d{d}",
    )(qkv, qkv, qkv, qkv, qkv, bias)


def _merge_body(yr_ref, o1_ref, o2_ref, o3_ref, l1_ref, l2_ref, l3_ref, g_ref, x_ref,
                wr_ref, wa_ref, wo_ref, lg_ref, lb_ref, wrh_ref, wrl_ref,
                x1_ref, x1p_ref, lt_ref, *, alpha):
    l1, l2, l3 = l1_ref[...], l2_ref[...], l3_ref[...]
    mx = jnp.maximum(jnp.maximum(l1, l2), l3)
    w1, w2, w3 = jnp.exp(l1 - mx), jnp.exp(l2 - mx), jnp.exp(l3 - mx)
    y_att = (w1 * o1_ref[...].astype(F32) + w2 * o2_ref[...].astype(F32)
             + w3 * o3_ref[...].astype(F32)) / (w1 + w2 + w3)
    pr = jnp.dot(yr_ref[...], wr_ref[...], preferred_element_type=F32)
    pa = jnp.dot(y_att.astype(BF16), wa_ref[...], preferred_element_type=F32)
    dm = pr.shape[1]
    g = g_ref[...].astype(F32)
    merged = _sigmoid(g[:, :dm]) * pr + _sigmoid(g[:, dm:]) * pa
    mix = jnp.dot(merged.astype(BF16), wo_ref[...], preferred_element_type=F32)
    z = alpha * x_ref[...] + mix
    mu = jnp.mean(z, axis=-1, keepdims=True)
    zc = z - mu
    var = jnp.mean(zc * zc, axis=-1, keepdims=True)
    x1 = zc * lax.rsqrt(var + LN_EPS) * lg_ref[...] + lb_ref[...]
    x1_ref[...] = x1
    x1p_ref[...] = _pack_rows(x1)
    nt = (((1,), (1,)), ((), ()))
    x1h = x1.astype(BF16)
    x1l = (x1 - x1h.astype(F32)).astype(BF16)
    lt_ref[...] = (lax.dot_general(wrh_ref[...], x1h, nt, preferred_element_type=F32)
                   + (lax.dot_general(wrh_ref[...], x1l, nt, preferred_element_type=F32)
                      + lax.dot_general(wrl_ref[...], x1h, nt, preferred_element_type=F32)))


def _merge(y_rnn, os_, lses, proj, gate_blk, x2, wr, wa, wo, ln_g, ln_b, w_router_t, *, alpha, tm=512):
    T, D = x2.shape
    da = os_[0].shape[1]
    E = w_router_t.shape[0]
    wrh = w_router_t.astype(BF16)
    wrl = (w_router_t - wrh.astype(F32)).astype(BF16)
    row = lambda w: pl.BlockSpec((tm, w), lambda i: (i, 0))
    full = lambda a: pl.BlockSpec(a.shape, lambda i: (0,) * a.ndim)
    return pl.pallas_call(
        functools.partial(_merge_body, alpha=alpha),
        grid=(T // tm,),
        in_specs=[row(D), row(da), row(da), row(da), row(da), row(da), row(da),
                  pl.BlockSpec((tm, 2 * D), lambda i: (i, gate_blk)), row(D),
                  full(wr), full(wa), full(wo), full(ln_g), full(ln_b), full(wrh), full(wrl)],
        out_specs=[row(D), row(D // 2), pl.BlockSpec((E, tm), lambda i: (0, i))],
        out_shape=[jax.ShapeDtypeStruct((T, D), F32), jax.ShapeDtypeStruct((T, D // 2), U32),
                   jax.ShapeDtypeStruct((E, T), F32)],
        compiler_params=_cparams("parallel"),
        name="merge",
    )(y_rnn, *os_, *lses, proj, x2, wr, wa, wo, ln_g, ln_b, wrh, wrl)


def _first_max(vals, idx, big):
    m = jnp.max(vals, axis=0, keepdims=True)
    i = jnp.min(jnp.where(vals == m, idx, big), axis=0, keepdims=True)
    return m, i


def _route_body(lt_ref, rb_ref, e_ref, g_ref, r_ref, cnt_ref, *, tl):
    E = lt_ref.shape[0]
    per = E // N_EXPERT_GROUPS
    neg = -jnp.inf

    @pl.when(pl.program_id(0) == 0)
    def _():
        cnt_ref[...] = jnp.zeros_like(cnt_ref)

    scores = jax.nn.sigmoid(lt_ref[...])
    sel = scores + rb_ref[...]
    rowi = lax.broadcasted_iota(I32, (E, tl), 0)

    gi = lax.broadcasted_iota(I32, (N_EXPERT_GROUPS, tl), 0)
    gsc = jnp.zeros((N_EXPERT_GROUPS, tl), F32)
    for g in range(N_EXPERT_GROUPS):
        blk = sel[g * per:(g + 1) * per]
        ri = lax.broadcasted_iota(I32, (per, tl), 0) + g * per
        m1, i1 = _first_max(blk, ri, E)
        m2 = jnp.max(jnp.where(ri == i1, neg, blk), axis=0, keepdims=True)
        gsc = jnp.where(gi == g, m1 + m2, gsc)
    keep = jnp.zeros((N_EXPERT_GROUPS, tl), F32)
    for _ in range(TOPK_GROUPS):
        _, ig = _first_max(gsc, gi, N_EXPERT_GROUPS)
        hit = gi == ig
        keep = jnp.where(hit, 1.0, keep)
        gsc = jnp.where(hit, neg, gsc)
    cur = jnp.concatenate(
        [jnp.where(keep[g:g + 1] > 0.5, sel[g * per:(g + 1) * per], neg) for g in range(N_EXPERT_GROUPS)],
        axis=0)

    ti = lax.broadcasted_iota(I32, (tl, tl), 0)
    tj = lax.broadcasted_iota(I32, (tl, tl), 1)
    earlier = jnp.where(ti < tj, 1.0, 0.0).astype(BF16)

    es, gv = [], []
    onehot = jnp.zeros((E, tl), F32)
    for k in range(TOP_K):
        _, ie = _first_max(cur, rowi, E)
        hit = rowi == ie
        es.append(ie)
        gv.append(jnp.sum(jnp.where(hit, scores, 0.0), axis=0, keepdims=True))
        onehot = jnp.where(hit, 1.0, onehot)
        cur = jnp.where(hit, neg, cur)
    gsum = gv[0]
    for k in range(1, TOP_K):
        gsum = gsum + gv[k]
    ranks = jnp.dot(onehot.astype(BF16), earlier, preferred_element_type=F32) + cnt_ref[...]
    for k in range(TOP_K):
        e_ref[k:k + 1, :] = es[k]
        g_ref[k:k + 1, :] = gv[k] / gsum * ROUTED_SCALE
        r_ref[k:k + 1, :] = jnp.sum(jnp.where(rowi == es[k], ranks, 0.0), axis=0, keepdims=True).astype(I32)
    cnt_ref[...] += jnp.sum(onehot, axis=1, keepdims=True)


def _route(logits_t, router_bias, *, tl=256):
    E, T = logits_t.shape
    kt = pl.BlockSpec((TOP_K, tl), lambda i: (0, i))
    return pl.pallas_call(
        functools.partial(_route_body, tl=tl),
        grid=(T // tl,),
        in_specs=[pl.BlockSpec((E, tl), lambda i: (0, i)), pl.BlockSpec((E, 1), lambda i: (0, 0))],
        out_specs=[kt, kt, kt, pl.BlockSpec((E, 1), lambda i: (0, 0))],
        out_shape=[jax.ShapeDtypeStruct((TOP_K, T), I32), jax.ShapeDtypeStruct((TOP_K, T), F32),
                   jax.ShapeDtypeStruct((TOP_K, T), I32), jax.ShapeDtypeStruct((E, 1), F32)],
        compiler_params=_cparams("arbitrary"),
        name="route",
    )(logits_t, router_bias.reshape(E, 1))


def _dest_body(e_ref, r_ref, ss_ref, d_ref):
    E = ss_ref.shape[0]
    tl = e_ref.shape[1]
    rowi = lax.broadcasted_iota(I32, (E, tl), 0)
    e = e_ref[...]
    rows = [jnp.sum(jnp.where(rowi == e[k:k + 1], ss_ref[...], 0), axis=0, keepdims=True)
            for k in range(e.shape[0])]
    d_ref[...] = jnp.concatenate(rows, axis=0) + r_ref[...]


def _dest(e_idx, rank, seg_start, *, tl=512):
    K, T = e_idx.shape
    E = seg_start.shape[0]
    kt = pl.BlockSpec((K, tl), lambda i: (0, i))
    return pl.pallas_call(
        _dest_body, grid=(T // tl,),
        in_specs=[kt, kt, pl.BlockSpec((E, 1), lambda i: (0, 0))],
        out_specs=kt, out_shape=jax.ShapeDtypeStruct((K, T), I32),
        compiler_params=_cparams("parallel"), name="dest",
    )(e_idx, rank, seg_start.reshape(E, 1))


def _pack_rows(v):
    w = v.shape[1] // 2
    lo = pltpu.bitcast(v[:, :w].astype(BF16).astype(F32), U32) >> 16
    hi = pltpu.bitcast(v[:, w:].astype(BF16).astype(F32), U32) & jnp.uint32(0xFFFF0000)
    return lo | hi


def _unpack_rows(p):
    lo = pltpu.bitcast(p << 16, F32)
    hi = pltpu.bitcast(p & jnp.uint32(0xFFFF0000), F32)
    return lo, hi


def _sc_scatter_rows(x, idx3, n_rows):
    from jax.experimental.pallas import tpu_sc as plsc
    n_chunks, K, CH = idx3.shape
    W = x.shape[1]
    info = plsc.get_sparse_core_info()
    NC, n_workers = info.num_cores, info.num_cores * info.num_subcores
    per_w = n_chunks // n_workers
    assert per_w * n_workers == n_chunks and per_w % 2 == 0

    def body(x_hbm, idx_hbm, out_hbm, idx_v, rows_v, sem_l, sem_s):
        c0 = (lax.axis_index("s") * NC + lax.axis_index("c")) * per_w

        def load(cc, b):
            return pltpu.make_async_copy(x_hbm.at[pl.ds(pl.multiple_of((c0 + cc) * CH, CH), CH)], rows_v.at[b],
                                         sem_l.at[b])

        def scatter(b, k):
            return pltpu.make_async_copy(rows_v.at[b], out_hbm.at[idx_v.at[b, k]], sem_s.at[b])

        pltpu.sync_copy(idx_hbm.at[c0], idx_v.at[0])
        load(0, 0).start()

        @pl.loop(0, per_w, step=2)
        def _(c):
            for b in range(2):
                cc = c + b
                load(cc, b).wait()
                for k in range(K):
                    scatter(b, k).start()

                @pl.when(cc >= 1)
                def _():
                    for k in range(K):
                        scatter(1 - b, k).wait()

                @pl.when(cc + 1 < per_w)
                def _():
                    pltpu.sync_copy(idx_hbm.at[c0 + cc + 1], idx_v.at[1 - b])
                    load(cc + 1, 1 - b).start()

        for k in range(K):
            scatter(1, k).wait()

    return pl.kernel(
        body, mesh=plsc.VectorSubcoreMesh(core_axis_name="c", subcore_axis_name="s"),
        out_type=jax.ShapeDtypeStruct((n_rows, W), x.dtype),
        scratch_types=[pltpu.VMEM((2, K, CH), I32), pltpu.VMEM((2, CH, W), x.dtype),
                       pltpu.SemaphoreType.DMA((2,)), pltpu.SemaphoreType.DMA((2,))],
    )(x, idx3)


def _pad_fill_body(ps_ref, pc_ref, nu_ref, xs_in, xs_hbm, zrow, sem_z, sem_c, sem_b, sem_t):
    del xs_in
    i = pl.program_id(0)
    n = pl.num_programs(0)
    R = zrow.shape[0]

    def pad_copy(row):
        return pltpu.make_async_copy(zrow.at[pl.ds(0, 1), :], xs_hbm.at[pl.ds(row, 1), :], sem_z)

    zrow[...] = jnp.zeros_like(zrow)

    n_blk = xs_hbm.shape[0] // R
    tail_per_step = -(-n_blk // n)

    def tail(j, c, wait):
        blk = i * tail_per_step + j

        @pl.when(jnp.logical_and(blk >= nu_ref[0], blk < n_blk))
        def _():
            cp = pltpu.make_async_copy(zrow, xs_hbm.at[pl.ds(pl.multiple_of(blk * R, R), R), :], sem_t)
            cp.wait() if wait else cp.start()

        return c

    lax.fori_loop(0, tail_per_step, functools.partial(tail, wait=False), 0)

    E = ps_ref.shape[0]
    per_step = -(-E // n)

    def pads(j, c, wait):
        e = jnp.minimum(i * per_step + j, E - 1)
        cnt = jnp.where(i * per_step + j < E, pc_ref[e], 0)
        start = ps_ref[e]
        n_single = jnp.minimum(cnt, (-start) & (SUBLANES - 1))
        start8 = start + n_single

        def one(r, c2):
            cp = pad_copy(start + r)
            cp.wait() if wait else cp.start()
            return c2

        rem8 = lax.shift_right_logical(cnt - n_single, 3)
        n_eight = jnp.minimum(rem8, lax.shift_right_logical(-start8, 3) & (SUBLANES - 1))
        start64 = start8 + n_eight * SUBLANES
        big = SUBLANES * SUBLANES

        def eight(r, c2):
            row = pl.multiple_of(start8 + r * SUBLANES, SUBLANES)
            cp = pltpu.make_async_copy(zrow.at[pl.ds(0, SUBLANES), :], xs_hbm.at[pl.ds(row, SUBLANES), :], sem_c)
            cp.wait() if wait else cp.start()
            return c2

        def sixty_four(r, c2):
            row = pl.multiple_of(start64 + r * big, big)
            cp = pltpu.make_async_copy(zrow.at[pl.ds(0, big), :], xs_hbm.at[pl.ds(row, big), :], sem_b)
            cp.wait() if wait else cp.start()
            return c2

        c = lax.fori_loop(0, n_single, one, c)
        c = lax.fori_loop(0, n_eight, eight, c)
        return lax.fori_loop(0, lax.shift_right_logical(rem8 - n_eight, 3), sixty_four, c)

    lax.fori_loop(0, per_step, functools.partial(pads, wait=False), 0)
    lax.fori_loop(0, per_step, functools.partial(pads, wait=True), 0)
    lax.fori_loop(0, tail_per_step, functools.partial(tail, wait=True), 0)


def _pad_fill(xs, pad_start, pad_cnt, n_used, *, steps=16):
    any_spec = pl.BlockSpec(memory_space=pl.ANY)
    return pl.pallas_call(
        _pad_fill_body,
        grid_spec=pltpu.PrefetchScalarGridSpec(
            num_scalar_prefetch=3, grid=(steps,),
            in_specs=[any_spec], out_specs=any_spec,
            scratch_shapes=[pltpu.VMEM((EXPERT_ROWS, xs.shape[1]), xs.dtype),
                            pltpu.SemaphoreType.DMA, pltpu.SemaphoreType.DMA, pltpu.SemaphoreType.DMA,
                            pltpu.SemaphoreType.DMA]),
        out_shape=jax.ShapeDtypeStruct(xs.shape, xs.dtype),
        input_output_aliases={3: 0},
        compiler_params=_cparams("arbitrary"),
        name="pad_fill",
    )(pad_start, pad_cnt, n_used, xs)


def _experts_body(fb_ref, xs_hbm, wg_ref, wu_ref, wd_ref, ys_hbm, xbuf, ybuf, wgu_s, wd_s, sem_x, sem_y):
    e = pl.program_id(0)
    n_e = pl.num_programs(0)
    de = wg_ref.shape[2]
    G = EXPERT_GROUP
    sh = G.bit_length() - 1
    R = xbuf.shape[1] // G
    n_blk = xs_hbm.shape[0] // R
    n_used = fb_ref[n_e]
    n_pairs = lax.shift_right_logical(n_used + (G - 1), sh)

    def pair_rows(ref, p):
        return ref.at[pl.ds(pl.multiple_of(p * (G * R), G * R), G * R), :]

    def x_copy(p, slot):
        return pltpu.make_async_copy(pair_rows(xs_hbm, p), xbuf.at[slot], sem_x.at[slot])

    def y_copy(p, slot):
        return pltpu.make_async_copy(ybuf.at[slot], pair_rows(ys_hbm, p), sem_y.at[slot])

    @pl.when(jnp.logical_and(e == 0, n_used > 0))
    def _():
        x_copy(0, 0).start(priority=1)

    wgu_s[:, :de] = wg_ref[0].astype(BF16)
    wgu_s[:, de:] = wu_ref[0].astype(BF16)
    wd_s[...] = wd_ref[0].astype(BF16)

    def block(g, c):
        p = lax.shift_right_logical(g, sh)
        half = g & (G - 1)
        slot = p & 1

        @pl.when(half == 0)
        def _():
            x_copy(p, slot).wait()

            @pl.when(p + 1 < n_pairs)
            def _():
                x_copy(p + 1, 1 - slot).start(priority=1)

            @pl.when(p >= 2)
            def _():
                y_copy(p - 2, slot).wait()

        r0 = pl.multiple_of(half * R, R)
        lo, hi = _unpack_rows(xbuf[slot, pl.ds(r0, R), :])
        xb = jnp.concatenate([lo.astype(BF16), hi.astype(BF16)], axis=1)
        h = jnp.dot(xb, wgu_s[...], preferred_element_type=F32)
        hg = h[:, :de]
        act = (hg * _sigmoid(hg) * h[:, de:]).astype(BF16)
        ybuf[slot, pl.ds(r0, R), :] = _pack_rows(jnp.dot(act, wd_s[...], preferred_element_type=F32))

        @pl.when(half == G - 1)
        def _():
            y_copy(p, slot).start(priority=1)

        return c

    lax.fori_loop(fb_ref[e], fb_ref[e + 1], block, 0)

    @pl.when(e == n_e - 1)
    def _():
        @pl.when((n_used & (G - 1)) != 0)
        def _():
            p = lax.shift_right_logical(n_used, sh)
            y_copy(p, p & 1).start(priority=1)

        for back in (2, 1):
            p = n_pairs - back

            @pl.when(p >= 0)
            def _():
                y_copy(p, p & 1).wait()

        ybuf[0] = jnp.zeros(ybuf.shape[1:], ybuf.dtype)

        def tail(g, c, wait):
            cp = pltpu.make_async_copy(ybuf.at[0, pl.ds(0, R), :],
                                       ys_hbm.at[pl.ds(pl.multiple_of(g * R, R), R), :], sem_y.at[0])
            cp.wait() if wait else cp.start()
            return c

        lax.fori_loop(n_used, n_blk, functools.partial(tail, wait=False), 0)
        lax.fori_loop(n_used, n_blk, functools.partial(tail, wait=True), 0)


def _experts(xs, first_blk, wg, wu, wd):
    n_rows, W = xs.shape
    R = EXPERT_ROWS
    E, D, de = wg.shape
    G = EXPERT_GROUP
    assert (n_rows // R) % G == 0 and G & (G - 1) == 0
    any_spec = pl.BlockSpec(memory_space=pl.ANY)
    wmap = lambda e, fb: (e, 0, 0)
    return pl.pallas_call(
        _experts_body,
        grid_spec=pltpu.PrefetchScalarGridSpec(
            num_scalar_prefetch=1, grid=(E,),
            in_specs=[any_spec, pl.BlockSpec((1, D, de), wmap), pl.BlockSpec((1, D, de), wmap),
                      pl.BlockSpec((1, de, D), wmap)],
            out_specs=any_spec,
            scratch_shapes=[pltpu.VMEM((2, G * R, W), U32), pltpu.VMEM((2, G * R, W), U32),
                            pltpu.VMEM((D, 2 * de), BF16), pltpu.VMEM((de, D), BF16),
                            pltpu.SemaphoreType.DMA((2,)), pltpu.SemaphoreType.DMA((2,))]),
        out_shape=jax.ShapeDtypeStruct((n_rows, W), U32),
        compiler_params=_cparams("arbitrary"),
        name="experts",
    )(first_blk, xs, wg, wu, wd)


def _sc_gather_rows(table, idx2):
    from jax.experimental.pallas import tpu_sc as plsc
    n_chunks, CH = idx2.shape
    W = table.shape[1]
    info = plsc.get_sparse_core_info()
    NC, n_workers = info.num_cores, info.num_cores * info.num_subcores
    per_w = n_chunks // n_workers
    assert per_w * n_workers == n_chunks and per_w % 2 == 0

    def body(table_hbm, idx_hbm, out_hbm, idx_v, rows_v, sem):
        c0 = (lax.axis_index("s") * NC + lax.axis_index("c")) * per_w

        def gather(b):
            return pltpu.make_async_copy(table_hbm.at[idx_v.at[b]], rows_v.at[b], sem.at[b])

        pltpu.sync_copy(idx_hbm.at[c0], idx_v.at[0])
        gather(0).start()

        @pl.loop(0, per_w, step=2)
        def _(c):
            for b in range(2):
                cc = c + b

                @pl.when(cc + 1 < per_w)
                def _():
                    pltpu.sync_copy(idx_hbm.at[c0 + cc + 1], idx_v.at[1 - b])
                    gather(1 - b).start()

                gather(b).wait()
                pltpu.sync_copy(rows_v.at[b], out_hbm.at[pl.ds(pl.multiple_of((c0 + cc) * CH, CH), CH)])

    return pl.kernel(
        body, mesh=plsc.VectorSubcoreMesh(core_axis_name="c", subcore_axis_name="s"),
        out_type=jax.ShapeDtypeStruct((n_chunks * CH, W), table.dtype),
        scratch_types=[pltpu.VMEM((2, CH), I32), pltpu.VMEM((2, CH, W), table.dtype),
                       pltpu.SemaphoreType.DMA((2,))],
    )(table, idx2)


def _combine_body(yg_ref, gate_ref, x1_ref, wgu_ref, wd_ref, lg_ref, lb_ref, *rest, alpha):
    o_ref = rest[-1]
    x1 = x1_ref[...]
    ds_ = wd_ref.shape[0]
    h = jnp.dot(x1.astype(BF16), wgu_ref[...], preferred_element_type=F32)
    hg = h[:, :ds_]
    act = (hg * _sigmoid(hg) * h[:, ds_:]).astype(BF16)
    shared = jnp.dot(act, wd_ref[...], preferred_element_type=F32)

    g = gate_ref[...]
    lo_acc = hi_acc = None
    for k in range(yg_ref.shape[0]):
        lo, hi = _unpack_rows(yg_ref[k])
        gk = g[:, k:k + 1]
        lo_acc = gk * lo if k == 0 else lo_acc + gk * lo
        hi_acc = gk * hi if k == 0 else hi_acc + gk * hi
    routed = jnp.concatenate([lo_acc, hi_acc], axis=1)

    z = alpha * x1 + (routed + shared)
    mu = jnp.mean(z, axis=-1, keepdims=True)
    zc = z - mu
    var = jnp.mean(zc * zc, axis=-1, keepdims=True)
    o_ref[...] = zc * lax.rsqrt(var + LN_EPS) * lg_ref[...] + lb_ref[...]


def _combine(yg, gate_tk, x1, wgu, wd, ln_g, ln_b, out_prev, part, *, alpha, tc):
    T, D = x1.shape
    K, Tp, W = yg.shape
    off = part * (Tp // tc)
    full = lambda a: pl.BlockSpec(a.shape, lambda i: (0,) * a.ndim)
    args = [yg, gate_tk, x1, wgu, wd, ln_g, ln_b]
    in_specs = [pl.BlockSpec((K, tc, W), lambda i: (0, i, 0)), pl.BlockSpec((tc, K), lambda i: (off + i, 0)),
                pl.BlockSpec((tc, D), lambda i: (off + i, 0)), full(wgu), full(wd), full(ln_g), full(ln_b)]
    aliases = {}
    if out_prev is not None:
        args.append(out_prev)
        in_specs.append(pl.BlockSpec(memory_space=pl.ANY))
        aliases = {len(args) - 1: 0}
    return pl.pallas_call(
        functools.partial(_combine_body, alpha=alpha),
        grid=(Tp // tc,),
        in_specs=in_specs,
        out_specs=pl.BlockSpec((tc, D), lambda i: (off + i, 0)),
        out_shape=jax.ShapeDtypeStruct((T, D), F32),
        input_output_aliases=aliases,
        compiler_params=_cparams("parallel"),
        name="combine",
    )(*args)


def _mixer_ln1(x, w_in, b_in, conv_w, conv_b, w_rg_a, b_rg_a, w_rg_i, b_rg_i, lru_lambda,
               w_proj_rnn, w_proj_att, rel_bias, w_out, ln1_g, ln1_b, w_router, alpha):
    B, S, D = x.shape
    T = B * S
    d_rnn = conv_w.shape[-1]
    gw = HEADS_PER_GROUP * HEAD_DIM
    d_att = gw * len(DILATED_GROUPS)
    a0 = 2 * d_rnn
    a1 = a0 + 3 * d_att
    head = lambda j, g: slice(a0 + j * d_att + g * gw, a0 + j * d_att + (g + 1) * gw)
    plain = [g for g, (_, d) in enumerate(DILATED_GROUPS) if d == 1]
    dilated = [g for g, (_, d) in enumerate(DILATED_GROUPS) if d > 1]
    order = ([slice(0, a0), slice(a1, None)] + [head(j, g) for g in plain for j in range(3)]
             + [head(j, g) for g in dilated for j in (1, 2, 0)])
    perm = lambda w: jnp.concatenate([w[..., s] for s in order], axis=-1)
    w_p = perm(w_in).astype(BF16)
    b_p = perm(b_in).reshape(1, -1)
    qkv0 = a0 + 2 * D
    n_main = qkv0 + 3 * gw * len(plain)
    x2 = x.reshape(T, D)

    proj, *qkv_dil = _in_proj(x2, w_p, b_p, n_main, tuple(DILATED_GROUPS[g][1] for g in dilated))
    proj3 = proj.reshape(B, S, n_main)
    y_rnn = _rnn(proj3, conv_w, conv_b, w_rg_a, b_rg_a, w_rg_i, b_rg_i, lru_lambda)

    os_, lses = [], []
    for g, (window, d) in enumerate(DILATED_GROUPS):
        blk = window // d
        bias = _attn_bias(rel_bias[:, g * HEADS_PER_GROUP:(g + 1) * HEADS_PER_GROUP], window, d)
        if d == 1:
            c0 = (qkv0 + 3 * gw * plain.index(g)) // gw
            o, lse = _attn_group(proj3, bias, blk=blk, d=1, gw=gw, q_blk=c0, k_blk=c0 + 1, v_blk=c0 + 2,
                                 row_blk=n_main // gw)
        else:
            o, lse = _attn_dilated(qkv_dil[dilated.index(g)], bias, B, blk=blk, d=d, gw=gw)
        os_.append(o.reshape(T, gw))
        lses.append(lse.reshape(T, gw))

    return _merge(y_rnn.reshape(T, d_rnn), os_, lses, proj, a0 // (2 * D), x2,
                  w_proj_rnn.astype(BF16), w_proj_att.astype(BF16), w_out.astype(BF16),
                  ln1_g.reshape(1, D), ln1_b.reshape(1, D), w_router.T, alpha=alpha)


def _moe_ln2(x1, x1p, logits_t, router_bias, w_exp_gate, w_exp_up, w_exp_down,
             w_sh_gate, w_sh_up, w_sh_down, ln2_g, ln2_b, alpha):
    T, D = x1.shape
    E = logits_t.shape[0]
    R = EXPERT_ROWS
    n_blk = T * TOP_K // R + E
    n_rows = n_blk * R

    e_idx, gate, rank, counts = _route(logits_t, router_bias)
    counts = counts.reshape(E).astype(I32)
    padded = (counts + R - 1) // R * R
    pad_end = jnp.cumsum(padded)
    seg_start = pad_end - padded
    n_used = (pad_end[-1:] // R).astype(I32)
    first_blk = jnp.concatenate([jnp.zeros((1,), I32), (pad_end // R).astype(I32)])
    dest = _dest(e_idx, rank, seg_start)
    tl = ROW_TILE
    ch = SC_GATHER_CHUNK
    dest_c = dest.reshape(TOP_K, T // ch, ch).transpose(1, 0, 2)

    xs = _sc_scatter_rows(x1p, dest_c, n_rows)
    xs = _pad_fill(xs, seg_start + counts, padded - counts, n_used)
    ys = _experts(xs, first_blk, w_exp_gate, w_exp_up, w_exp_down)
    wgu = jnp.concatenate([w_sh_gate, w_sh_up], axis=-1).astype(BF16)
    gate_tk = gate.T
    wd_sh = w_sh_down.astype(BF16)
    tp = T // COMBINE_PARTS
    out = None
    for p in range(COMBINE_PARTS):
        idx = dest[:, p * tp:(p + 1) * tp].reshape(-1, ch)
        yg = _sc_gather_rows(ys, idx).reshape(TOP_K, tp, ys.shape[1])
        out = _combine(yg, gate_tk, x1, wgu, wd_sh, ln2_g.reshape(1, D), ln2_b.reshape(1, D), out, p,
                       alpha=alpha, tc=tl)
    return out


def kernel(x, w_in, b_in, conv_w, conv_b, w_rg_a, b_rg_a, w_rg_i, b_rg_i, lru_lambda,
           w_proj_rnn, w_proj_att, rel_bias, w_out, ln1_g, ln1_b, w_router, router_bias,
           w_exp_gate, w_exp_up, w_exp_down, w_sh_gate, w_sh_up, w_sh_down, ln2_g, ln2_b):
    B, S, D = x.shape
    depth = w_in.shape[0]
    alpha = (2 * depth) ** 0.25
    for i in range(depth):
        x1, x1p, logits_t = _mixer_ln1(
            x, w_in[i], b_in[i], conv_w[i], conv_b[i], w_rg_a[i], b_rg_a[i], w_rg_i[i], b_rg_i[i],
            lru_lambda[i], w_proj_rnn[i], w_proj_att[i], rel_bias, w_out[i], ln1_g[i], ln1_b[i],
            w_router[i], alpha)
        out = _moe_ln2(x1, x1p, logits_t, router_bias[i], w_exp_gate[i], w_exp_up[i], w_exp_down[i],
                       w_sh_gate[i], w_sh_up[i], w_sh_down[i], ln2_g[i], ln2_b[i], alpha)
        x = out.reshape(B, S, D)
    return x
```
